```python
import math
import jax, jax.numpy as jnp
from jax import lax
import numpy as np

D_MODEL = 1024
BATCH = 32
SEQ = 256
DEPTH = 4
DEC_BATCH = 4
DEC_SEQ = 2048
PAST_LEN = 512

GRID_W = 64
SSM_WIDTH = 512
SSM_GROUP = 16
SSM_GROUPS = SSM_WIDTH // SSM_GROUP
SSM_STATE = 64
N_HEADS = 8
QK_NOPE = 64
QK_ROPE = 32
QK_HEAD = QK_NOPE + QK_ROPE
V_HEAD = 64
Q_LORA = 256
KV_LORA = 128
ATTN_WIDTH = N_HEADS * V_HEAD
ROPE_AXIS = QK_ROPE // 2
ROPE_BASE = 10000.0
Q_BLOCK = 128
IN_COLS = SSM_WIDTH + Q_LORA + KV_LORA + QK_ROPE + 2 * D_MODEL
IN_CUTS = [SSM_WIDTH, SSM_WIDTH + Q_LORA, SSM_WIDTH + Q_LORA + KV_LORA,
           SSM_WIDTH + Q_LORA + KV_LORA + QK_ROPE, SSM_WIDTH + Q_LORA + KV_LORA + QK_ROPE + D_MODEL]
N_EXPERTS = 16
N_GROUPS = 4
EXPERTS_PER_GROUP = N_EXPERTS // N_GROUPS
TOP_K = 2
D_EXPERT = 256
EPS = 1e-6

kernel_name = "hybrid_s5_mla_moe_diffusion_step"


def _rmsnorm(x, g):
    xf = x.astype(jnp.float32)
    y = xf * lax.rsqrt(jnp.mean(xf * xf, axis=-1, keepdims=True) + EPS)
    return (y * g.astype(jnp.float32)).astype(x.dtype)


def _axial_angles(n_tokens):
    rows = n_tokens // GRID_W
    row = jnp.repeat(jnp.arange(rows, dtype=jnp.float32), GRID_W)
    col = jnp.tile(jnp.arange(GRID_W, dtype=jnp.float32), rows)
    inv = 1.0 / (ROPE_BASE ** (jnp.arange(ROPE_AXIS // 2, dtype=jnp.float32) * 2.0 / ROPE_AXIS))
    return row[:, None] * inv, col[:, None] * inv


def _rotate(x, ang):
    cos = jnp.cos(ang)[None, :, None, :].astype(x.dtype)
    sin = jnp.sin(ang)[None, :, None, :].astype(x.dtype)
    x1, x2 = jnp.split(x, 2, axis=-1)
    return jnp.concatenate([x1 * cos - x2 * sin, x1 * sin + x2 * cos], axis=-1)


def _rope_heads(t, ang_row, ang_col):
    pe = t[..., QK_NOPE:]
    pe = jnp.concatenate([_rotate(pe[..., :ROPE_AXIS], ang_row), _rotate(pe[..., ROPE_AXIS:], ang_col)], axis=-1)
    return jnp.concatenate([t[..., :QK_NOPE], pe], axis=-1)


def _zoh(a_re, a_im, log_dt, b_re, b_im):
    dt = jnp.exp(log_dt.astype(jnp.float32))[:, None]
    a_re = a_re.astype(jnp.float32)
    a_im = a_im.astype(jnp.float32)
    mag = jnp.exp(a_re * dt)
    abar_re = mag * jnp.cos(a_im * dt)
    abar_im = mag * jnp.sin(a_im * dt)
    den = a_re * a_re + a_im * a_im
    nr, ni = abar_re - 1.0, abar_im
    coef_re = (nr * a_re + ni * a_im) / den
    coef_im = (ni * a_re - nr * a_im) / den
    b_re = b_re.astype(jnp.float32)
    b_im = b_im.astype(jnp.float32)
    bb_re = coef_re[..., None] * b_re - coef_im[..., None] * b_im
    bb_im = coef_re[..., None] * b_im + coef_im[..., None] * b_re
    return abar_re, abar_im, bb_re, bb_im


def _combine(e1, e2):
    a1r, a1i, b1r, b1i = e1
    a2r, a2i, b2r, b2i = e2
    return (a2r * a1r - a2i * a1i, a2r * a1i + a2i * a1r,
            a2r * b1r - a2i * b1i + b2r, a2r * b1i + a2i * b1r + b2i)


def _scan_dir(u, a_re, a_im, log_dt, b_re, b_im, h0_re, h0_im):
    abr, abi, bbr, bbi = _zoh(a_re, a_im, log_dt, b_re, b_im)
    bu_re = jnp.einsum("blgc,gpc->blgp", u, bbr)
    bu_im = jnp.einsum("blgc,gpc->blgp", u, bbi)
    h0_re = h0_re.astype(jnp.float32)
    h0_im = h0_im.astype(jnp.float32)
    bu_re = bu_re.at[:, 0].add(abr * h0_re - abi * h0_im)
    bu_im = bu_im.at[:, 0].add(abr * h0_im + abi * h0_re)
    a_r = jnp.broadcast_to(abr, bu_re.shape)
    a_i = jnp.broadcast_to(abi, bu_re.shape)
    _, _, s_re, s_im = lax.associative_scan(_combine, (a_r, a_i, bu_re, bu_im), axis=1)
    return s_re, s_im


def _ssm_branch(u, lp, h0_re, h0_im):
    bsz, n, _ = u.shape
    u32 = u.astype(jnp.float32)
    ug = u32.reshape(bsz, n, SSM_GROUPS, SSM_GROUP)
    sf_re, sf_im = _scan_dir(ug, lp["a_re"][0], lp["a_im"][0], lp["log_dt"][0], lp["b_re"][0], lp["b_im"][0],
                             h0_re[:, 0], h0_im[:, 0])
    sb_re, sb_im = _scan_dir(ug[:, ::-1], lp["a_re"][1], lp["a_im"][1], lp["log_dt"][1], lp["b_re"][1], lp["b_im"][1],
                             h0_re[:, 1], h0_im[:, 1])
    fin_re = jnp.stack([sf_re[:, -1], sb_re[:, -1]], axis=1)
    fin_im = jnp.stack([sf_im[:, -1], sb_im[:, -1]], axis=1)
    s_re = sf_re + sb_re[:, ::-1]
    s_im = sf_im + sb_im[:, ::-1]
    c_re = lp["c_re"].astype(jnp.float32)
    c_im = lp["c_im"].astype(jnp.float32)
    y = jnp.einsum("blgp,gcp->blgc", s_re, c_re) - jnp.einsum("blgp,gcp->blgc", s_im, c_im)
    y = y.reshape(bsz, n, SSM_WIDTH) + lp["d"].astype(jnp.float32) * u32
    y = jax.nn.gelu(y)
    y = y * jax.nn.sigmoid(y @ lp["w_glu"].astype(jnp.float32) + lp["b_glu"].astype(jnp.float32))
    out = (y.astype(u.dtype) @ lp["w_ssm_proj"]).astype(u.dtype)
    return out, fin_re, fin_im


def _project(h, lp):
    bsz, n, _ = h.shape
    z = h @ lp["w_in"]
    u, q_a, ckv, krope, g_a, g_b = jnp.split(z, IN_CUTS, axis=-1)
    q = (_rmsnorm(q_a, lp["g_qa"]) @ lp["w_qb"]).reshape(bsz, n, N_HEADS, QK_HEAD)
    q = _rmsnorm(q, lp["g_q"])
    ckv = _rmsnorm(ckv, lp["g_kva"])
    return u, q, ckv, krope, g_a, g_b


def _keys(ckv, krope, lp):
    bsz, nk, _ = ckv.shape
    kv = (ckv @ lp["w_kvb"]).reshape(bsz, nk, N_HEADS, QK_NOPE + V_HEAD)
    k_nope, v = jnp.split(kv, [QK_NOPE], axis=-1)
    k_pe = jnp.broadcast_to(krope[:, :, None, :].astype(k_nope.dtype), (bsz, nk, N_HEADS, QK_ROPE))
    k = _rmsnorm(jnp.concatenate([k_nope, k_pe], axis=-1), lp["g_k"])
    return k, v


def _attend(q, k, v):
    bsz, nq, nh, dh = q.shape
    nb = nq // Q_BLOCK
    qb = q.reshape(bsz, nb, Q_BLOCK, nh, dh).swapaxes(0, 1)
    k32 = k.astype(jnp.float32)
    scale = 1.0 / math.sqrt(dh)

    def block(qi):
        s = jnp.einsum("bqhd,bkhd->bhqk", qi.astype(jnp.float32), k32) * scale
        p = jax.nn.softmax(s, axis=-1)
        return jnp.einsum("bhqk,bkhv->bqhv", p.astype(v.dtype), v)

    o = lax.map(block, qb)
    return o.swapaxes(0, 1).reshape(bsz, nq, nh * v.shape[-1])


def _moe(h, w_router, router_bias, w_gate, w_up, w_down):
    bsz, n, d = h.shape
    t = h.reshape(-1, d)
    aff = jax.nn.sigmoid((t @ w_router).astype(jnp.float32))
    sel = aff + router_bias.astype(jnp.float32)
    grp = sel.reshape(-1, N_GROUPS, EXPERTS_PER_GROUP)
    grp_score = lax.top_k(grp, TOP_K)[0].sum(-1)
    g_idx = jnp.argmax(grp_score, axis=-1)
    in_grp = (jnp.arange(N_EXPERTS) // EXPERTS_PER_GROUP)[None, :] == g_idx[:, None]
    masked = jnp.where(in_grp, sel, -jnp.inf)
    _, e_idx = lax.top_k(masked, TOP_K)
    w_sel = jnp.take_along_axis(aff, e_idx, axis=-1)
    w_sel = w_sel / jnp.sum(w_sel, axis=-1, keepdims=True)
    gates = jnp.sum(jax.nn.one_hot(e_idx, N_EXPERTS, dtype=jnp.float32) * w_sel[..., None], axis=1)
    hg = jnp.einsum("td,edf->tef", t, w_gate)
    hu = jnp.einsum("td,edf->tef", t, w_up)
    act = jax.nn.silu(hg) * hu * gates[..., None].astype(hg.dtype)
    out = jnp.einsum("tef,efd->td", act, w_down)
    return out.reshape(bsz, n, d).astype(h.dtype)


def _merge_and_ffn(x, a_out, attn, g_a, g_b, mods, lp, w_router, router_bias):
    _, _, gate1, shift2, scale2, gate2 = mods
    b_out = attn @ lp["w_attn_proj"]
    m = jax.nn.sigmoid(g_a) * a_out + jax.nn.sigmoid(g_b) * b_out
    x = x + gate1 * (m @ lp["w_o"])
    h2 = _rmsnorm(x, lp["g_ffn"]) * (1.0 + scale2) + shift2
    return x + gate2 * _moe(h2, w_router, router_bias, lp["w_gate"], lp["w_up"], lp["w_down"])


def _context_layer(x, mod, lp, w_router, router_bias):
    mods = jnp.split(mod, 6, axis=-1)
    h = _rmsnorm(x, lp["g_mix"]) * (1.0 + mods[1]) + mods[0]
    u, q, ckv, krope, g_a, g_b = _project(h, lp)
    zeros = jnp.zeros((x.shape[0], 2, SSM_GROUPS, SSM_STATE), jnp.float32)
    a_out, fin_re, fin_im = _ssm_branch(u, lp, zeros, zeros)
    k, v = _keys(ckv, krope, lp)
    attn = _attend(q, k, v)
    x = _merge_and_ffn(x, a_out, attn, g_a, g_b, mods, lp, w_router, router_bias)
    return x, ckv, krope, fin_re, fin_im


def _latent_layer(x, mod, lp, w_router, router_bias, ckv_ctx, krope_ctx, h0_re, h0_im, ang_row, ang_col):
    mods = jnp.split(mod, 6, axis=-1)
    h = _rmsnorm(x, lp["g_mix"]) * (1.0 + mods[1]) + mods[0]
    u, q, ckv, krope, g_a, g_b = _project(h, lp)
    a_out, _, _ = _ssm_branch(u, lp, h0_re, h0_im)
    q = _rope_heads(q, ang_row, ang_col)
    k_lat, v_lat = _keys(ckv, krope, lp)
    k_lat = _rope_heads(k_lat, ang_row, ang_col)
    k_ctx, v_ctx = _keys(ckv_ctx.astype(ckv.dtype), krope_ctx, lp)
    k = jnp.concatenate([k_ctx, k_lat], axis=1)
    v = jnp.concatenate([v_ctx, v_lat], axis=1)
    attn = _attend(q, k, v)
    return _merge_and_ffn(x, a_out, attn, g_a, g_b, mods, lp, w_router, router_bias)


def setup_inputs(seed: int = 0) -> dict:
    key = jax.random.key(seed)
    ks = iter(jax.random.split(key, 48))

    def nrm(shape, s):
        return jax.random.normal(next(ks), shape, jnp.float32) * s

    def gain(shape):
        return 1.0 + nrm(shape, 0.05)

    L, G, P, C = DEPTH, SSM_GROUPS, SSM_STATE, SSM_GROUP
    a_im_base = jnp.pi * jnp.arange(P, dtype=jnp.float32)
    return {
        "x_prompt": nrm((BATCH, SEQ, D_MODEL), 1.0),
        "x_sample": nrm((DEC_BATCH, DEC_SEQ, D_MODEL), 1.0),
        "c": nrm((DEC_BATCH, D_MODEL), 1.0),
        "cache_ckv": nrm((DEC_BATCH, DEPTH, PAST_LEN, KV_LORA), 1.0),
        "cache_krope": nrm((DEC_BATCH, DEPTH, PAST_LEN, QK_ROPE), 1.0),
        "state_ssm_re": nrm((DEC_BATCH, DEPTH, 2, G, P), 0.3),
        "state_ssm_im": nrm((DEC_BATCH, DEPTH, 2, G, P), 0.3),
        "c_ctx": nrm((D_MODEL,), 1.0),
        "w_ada": nrm((L, D_MODEL, 6 * D_MODEL), 0.5 * D_MODEL ** -0.5),
        "b_ada": nrm((L, 6 * D_MODEL), 0.02),
        "g_mix": gain((L, D_MODEL)),
        "w_in": nrm((L, D_MODEL, IN_COLS), D_MODEL ** -0.5),
        "g_qa": gain((L, Q_LORA)),
        "w_qb": nrm((L, Q_LORA, N_HEADS * QK_HEAD), Q_LORA ** -0.5),
        "g_q": gain((L, QK_HEAD)),
        "g_kva": gain((L, KV_LORA)),
        "w_kvb": nrm((L, KV_LORA, N_HEADS * (QK_NOPE + V_HEAD)), KV_LORA ** -0.5),
        "g_k": gain((L, QK_HEAD)),
        "w_attn_proj": nrm((L, ATTN_WIDTH, D_MODEL), ATTN_WIDTH ** -0.5),
        "ssm_a_re": -0.5 + nrm((L, 2, G, P), 0.01),
        "ssm_a_im": a_im_base + nrm((L, 2, G, P), 0.01),
        "ssm_log_dt": jax.random.uniform(next(ks), (L, 2, G), jnp.float32, math.log(0.001), math.log(0.1)),
        "ssm_b_re": nrm((L, 2, G, P, C), C ** -0.5),
        "ssm_b_im": nrm((L, 2, G, P, C), C ** -0.5),
        "ssm_c_re": nrm((L, G, C, P), P ** -0.5),
        "ssm_c_im": nrm((L, G, C, P), P ** -0.5),
        "ssm_d": nrm((L, SSM_WIDTH), 1.0),
        "ssm_w_glu": nrm((L, SSM_WIDTH, SSM_WIDTH), SSM_WIDTH ** -0.5),
        "ssm_b_glu": nrm((L, SSM_WIDTH), 0.02),
        "ssm_w_proj": nrm((L, SSM_WIDTH, D_MODEL), SSM_WIDTH ** -0.5),
        "w_o": nrm((L, D_MODEL, D_MODEL), D_MODEL ** -0.5),
        "g_ffn": gain((L, D_MODEL)),
        "w_router": nrm((D_MODEL, N_EXPERTS), D_MODEL ** -0.5),
        "router_bias": nrm((N_EXPERTS,), 0.01),
        "w_gate": nrm((L, N_EXPERTS, D_MODEL, D_EXPERT), D_MODEL ** -0.5),
        "w_up": nrm((L, N_EXPERTS, D_MODEL, D_EXPERT), D_MODEL ** -0.5),
        "w_down": nrm((L, N_EXPERTS, D_EXPERT, D_MODEL), D_EXPERT ** -0.5),
    }


def reference(x_prompt, x_sample, c, cache_ckv, cache_krope, state_ssm_re, state_ssm_im, c_ctx,
              w_ada, b_ada, g_mix, w_in, g_qa, w_qb, g_q, g_kva, w_kvb, g_k, w_attn_proj,
              ssm_a_re, ssm_a_im, ssm_log_dt, ssm_b_re, ssm_b_im, ssm_c_re, ssm_c_im, ssm_d,
              ssm_w_glu, ssm_b_glu, ssm_w_proj, w_o, g_ffn, w_router, router_bias,
              w_gate, w_up, w_down):
    ang_row, ang_col = _axial_angles(x_sample.shape[1])
    y_p = x_prompt
    y_s = x_sample
    ckvs, kropes, s_res, s_ims = [], [], [], []
    for l in range(DEPTH):
        lp = {
            "g_mix": g_mix[l], "w_in": w_in[l], "g_qa": g_qa[l], "w_qb": w_qb[l], "g_q": g_q[l],
            "g_kva": g_kva[l], "w_kvb": w_kvb[l], "g_k": g_k[l], "w_attn_proj": w_attn_proj[l],
            "a_re": ssm_a_re[l], "a_im": ssm_a_im[l], "log_dt": ssm_log_dt[l],
            "b_re": ssm_b_re[l], "b_im": ssm_b_im[l], "c_re": ssm_c_re[l], "c_im": ssm_c_im[l],
            "d": ssm_d[l], "w_glu": ssm_w_glu[l], "b_glu": ssm_b_glu[l], "w_ssm_proj": ssm_w_proj[l],
            "w_o": w_o[l], "g_ffn": g_ffn[l], "w_gate": w_gate[l], "w_up": w_up[l], "w_down": w_down[l],
        }
        mod_ctx = jax.nn.silu(c_ctx) @ w_ada[l] + b_ada[l]
        mod_lat = (jax.nn.silu(c) @ w_ada[l] + b_ada[l])[:, None, :]
        y_p, ckv_l, krope_l, fin_re, fin_im = _context_layer(y_p, mod_ctx, lp, w_router, router_bias)
        ckvs.append(ckv_l)
        kropes.append(krope_l)
        s_res.append(fin_re)
        s_ims.append(fin_im)
        y_s = _latent_layer(y_s, mod_lat, lp, w_router, router_bias, cache_ckv[:, l], cache_krope[:, l],
                            state_ssm_re[:, l], state_ssm_im[:, l], ang_row, ang_col)
    new_ckv = jnp.stack(ckvs, axis=1)
    new_krope = jnp.stack(kropes, axis=1)
    new_ssm_re = jnp.stack(s_res, axis=1)
    new_ssm_im = jnp.stack(s_ims, axis=1)
    return (y_p, y_s, new_ckv, new_krope, new_ssm_re, new_ssm_im)
```

```python
import functools
import math

import jax
import jax.numpy as jnp
import numpy as np
from jax import lax
from jax.experimental import pallas as pl
from jax.experimental.pallas import tpu as pltpu

F32 = jnp.float32
BF16 = jnp.bfloat16

D_MODEL = 1024
DEPTH = 4
GRID_W = 64
SSM_WIDTH = 512
SSM_GROUP = 16
SSM_GROUPS = 32
SSM_STATE = 64
N_HEADS = 8
QK_NOPE = 64
QK_ROPE = 32
QK_HEAD = 96
V_HEAD = 64
Q_LORA = 256
KV_LORA = 128
ATTN_WIDTH = 512
ROPE_AXIS = 16
ROPE_BASE = 10000.0
N_EXPERTS = 16
N_GROUPS = 4
EXPERTS_PER_GROUP = 4
D_EXPERT = 256
EPS = 1e-6

LANES = 128
HEAD_PAD = LANES
QK_PAD = N_HEADS * HEAD_PAD
CHUNK = 16
CW = CHUNK * SSM_GROUP
TOK_TILE = 512
VMEM_LIMIT = 48 * 1024 * 1024


def _params(sem, vmem=VMEM_LIMIT):
    return pltpu.CompilerParams(dimension_semantics=sem, vmem_limit_bytes=vmem)


def _sigmoid(x):
    return 1.0 / (1.0 + jnp.exp(-x))


def _rms(x, n):
    return x * lax.rsqrt(jnp.sum(x * x, axis=-1, keepdims=True) * (1.0 / n) + EPS)


def _mods_kernel(c_ref, w_ref, b_ref, o_ref):
    c = c_ref[...]
    cs = (c * _sigmoid(c)).astype(BF16)
    o_ref[...] = jnp.dot(cs, w_ref[...].astype(BF16), preferred_element_type=F32) + b_ref[...]


def _mods(cvec, w_ada, b_ada):
    nb = 1536
    return pl.pallas_call(
        _mods_kernel,
        grid=(DEPTH, 6 * D_MODEL // nb),
        in_specs=[
            pl.BlockSpec((8, D_MODEL), lambda l, j: (0, 0)),
            pl.BlockSpec((None, D_MODEL, nb), lambda l, j: (l, 0, j)),
            pl.BlockSpec((None, 1, nb), lambda l, j: (l, 0, j)),
        ],
        out_specs=pl.BlockSpec((None, 8, nb), lambda l, j: (l, 0, j)),
        out_shape=jax.ShapeDtypeStruct((DEPTH, 8, 6 * D_MODEL), F32),
        compiler_params=_params(("arbitrary", "arbitrary")),
        name="mods",
    )(cvec, w_ada, b_ada.reshape(DEPTH, 1, 6 * D_MODEL))


def _head_norm_rope(t, gain, rope):
    outs = []
    for h in range(N_HEADS):
        th = t[:, h * HEAD_PAD:(h + 1) * HEAD_PAD]
        y = _rms(th, QK_HEAD) * gain
        if rope is not None:
            cos, sin_a, sin_b = rope
            half = ROPE_AXIS // 2
            y = y * cos + pltpu.roll(y, HEAD_PAD - half, 1) * sin_a + pltpu.roll(y, half, 1) * sin_b
        outs.append(y)
    return jnp.concatenate(outs, axis=1)


def _keys_from(ckvn, krp, wk_ref, gk, rope):
    kv = (jnp.dot(ckvn.astype(BF16), wk_ref[0:KV_LORA, :], preferred_element_type=F32)
          + jnp.dot(krp.astype(BF16), wk_ref[KV_LORA:KV_LORA + krp.shape[1], :],
                    preferred_element_type=F32))
    k = _head_norm_rope(kv[:, :QK_PAD], gk, rope)
    return k, kv[:, QK_PAD:]


def _inproj_kernel(use_rope, x_ref, mod_ref, gmix_ref, wa_ref, wga_ref, wgb_ref, gqa_ref, wqb_ref, gq_ref,
                   gkva_ref, wk_ref, gk_ref, cos_ref, sa_ref, sb_ref,
                   u_ref, q_ref, k_ref, v_ref, sga_ref, sgb_ref, ckv_ref, kr_ref):
    x = x_ref[...]
    mod = mod_ref[...]
    shift1 = mod[:, 0:D_MODEL]
    scale1 = mod[:, D_MODEL:2 * D_MODEL]
    h = _rms(x, D_MODEL) * gmix_ref[...]
    h = (h * (1.0 + scale1) + shift1).astype(BF16)
    za = jnp.dot(h, wa_ref[...], preferred_element_type=F32)
    sga_ref[...] = _sigmoid(jnp.dot(h, wga_ref[...], preferred_element_type=F32)).astype(BF16)
    sgb_ref[...] = _sigmoid(jnp.dot(h, wgb_ref[...], preferred_element_type=F32)).astype(BF16)
    u_ref[...] = za[:, :SSM_WIDTH].astype(BF16)
    qa = za[:, SSM_WIDTH:SSM_WIDTH + Q_LORA]
    ckv = za[:, SSM_WIDTH + Q_LORA:SSM_WIDTH + Q_LORA + KV_LORA]
    krp = za[:, SSM_WIDTH + Q_LORA + KV_LORA:]
    rope = (cos_ref[...], sa_ref[...], sb_ref[...]) if use_rope else None
    qn = (_rms(qa, Q_LORA) * gqa_ref[...]).astype(BF16)
    qr = jnp.dot(qn, wqb_ref[...], preferred_element_type=F32)
    q = _head_norm_rope(qr, gq_ref[...], rope)
    q_ref[...] = (q * (1.0 / math.sqrt(QK_HEAD))).astype(BF16)
    ckvn = _rms(ckv, KV_LORA) * gkva_ref[...]
    ckv_ref[...] = ckvn
    kr_ref[...] = krp[:, :QK_ROPE]
    k, v = _keys_from(ckvn, krp, wk_ref, gk_ref[...], rope)
    k_ref[...] = k.astype(BF16)
    v_ref[...] = v.astype(BF16)


def _inproj(x, mods, l, lat, wts, tabs):
    n = x.shape[0]
    tt = TOK_TILE
    per_seq = 2048 // tt
    if lat:
        mod_map = lambda i: (l, 1 + i // per_seq, 0, 0)
        tab_map = lambda i: (i % per_seq, 0)
    else:
        mod_map = lambda i: (l, 0, 0, 0)
        tab_map = lambda i: (0, 0)
    row = lambda i: (i, 0)
    lw = lambda *shape: pl.BlockSpec((None,) + shape, lambda i: (l,) + (0,) * len(shape))
    tab = pl.BlockSpec((tt, HEAD_PAD), tab_map)
    out_shapes = (
        jax.ShapeDtypeStruct((n, SSM_WIDTH), BF16),
        jax.ShapeDtypeStruct((n, QK_PAD), BF16),
        jax.ShapeDtypeStruct((n, QK_PAD), BF16),
        jax.ShapeDtypeStruct((n, ATTN_WIDTH), BF16),
        jax.ShapeDtypeStruct((n, D_MODEL), BF16),
        jax.ShapeDtypeStruct((n, D_MODEL), BF16),
        jax.ShapeDtypeStruct((n, KV_LORA), F32),
        jax.ShapeDtypeStruct((n, QK_ROPE), F32),
    )
    return pl.pallas_call(
        functools.partial(_inproj_kernel, lat),
        grid=(n // tt,),
        in_specs=[
            pl.BlockSpec((tt, D_MODEL), row),
            pl.BlockSpec((None, None, 1, 6 * D_MODEL), mod_map),
            lw(1, D_MODEL), lw(D_MODEL, 1024), lw(D_MODEL, D_MODEL), lw(D_MODEL, D_MODEL),
            lw(1, Q_LORA), lw(Q_LORA, QK_PAD), lw(1, HEAD_PAD), lw(1, KV_LORA),
            lw(2 * KV_LORA, QK_PAD + ATTN_WIDTH), lw(1, HEAD_PAD),
            tab, tab, tab,
        ],
        out_specs=[pl.BlockSpec((tt, s.shape[1]), row) for s in out_shapes],
        out_shape=out_shapes,
        compiler_params=_params(("parallel",)),
        name="inproj_lat" if lat else "inproj_ctx",
    )(x, mods, wts["g_mix"], wts["w_a"], wts["w_ga"], wts["w_gb"], wts["g_qa"], wts["w_qb"], wts["g_q"],
      wts["g_kva"], wts["w_k"], wts["g_k"], *tabs)


def _cachekv_kernel(ckv_ref, kr_ref, wk_ref, gk_ref, k_ref, v_ref):
    k, v = _keys_from(ckv_ref[...], kr_ref[...], wk_ref, gk_ref[...], None)
    k_ref[...] = k.astype(BF16)
    v_ref[...] = v.astype(BF16)


def _cachekv(cache_ckv, cache_krope, wts):
    nb, _, past, _ = cache_ckv.shape
    return pl.pallas_call(
        _cachekv_kernel,
        grid=(DEPTH, nb),
        in_specs=[
            pl.BlockSpec((None, None, past, KV_LORA), lambda l, b: (b, l, 0, 0)),
            pl.BlockSpec((None, None, past, QK_ROPE), lambda l, b: (b, l, 0, 0)),
            pl.BlockSpec((None, 2 * KV_LORA, QK_PAD + ATTN_WIDTH), lambda l, b: (l, 0, 0)),
            pl.BlockSpec((None, 1, HEAD_PAD), lambda l, b: (l, 0, 0)),
        ],
        out_specs=[
            pl.BlockSpec((None, None, past, QK_PAD), lambda l, b: (l, b, 0, 0)),
            pl.BlockSpec((None, None, past, ATTN_WIDTH), lambda l, b: (l, b, 0, 0)),
        ],
        out_shape=(jax.ShapeDtypeStruct((DEPTH, nb, past, QK_PAD), BF16),
                   jax.ShapeDtypeStruct((DEPTH, nb, past, ATTN_WIDTH), BF16)),
        compiler_params=_params(("arbitrary", "arbitrary")),
        name="cache_kv",
    )(cache_ckv, cache_krope, wts["w_k"], wts["g_k"])


def _attn_heads(q_ref, segs, o_ref):
    for h in range(N_HEADS):
        qh = q_ref[:, h * HEAD_PAD:(h + 1) * HEAD_PAD]
        scores = [lax.dot_general(qh, k_ref[:, h * HEAD_PAD:(h + 1) * HEAD_PAD], (((1,), (1,)), ((), ())),
                                  preferred_element_type=F32) for k_ref, _ in segs]
        m = scores[0].max(axis=-1, keepdims=True)
        for s in scores[1:]:
            m = jnp.maximum(m, s.max(axis=-1, keepdims=True))
        den = None
        acc = None
        for s, (_, v_ref) in zip(scores, segs):
            p = jnp.exp(s - m)
            d = p.sum(axis=-1, keepdims=True)
            a = jnp.dot(p.astype(BF16), v_ref[:, h * V_HEAD:(h + 1) * V_HEAD], preferred_element_type=F32)
            den = d if den is None else den + d
            acc = a if acc is None else acc + a
        o_ref[:, h * V_HEAD:(h + 1) * V_HEAD] = (acc / den).astype(o_ref.dtype)


def _attn_ctx_kernel(q_ref, k_ref, v_ref, o_ref):
    _attn_heads(q_ref, [(k_ref, v_ref)], o_ref)


def _attn_lat_kernel(q_ref, kc_ref, vc_ref, kl_ref, vl_ref, o_ref):
    _attn_heads(q_ref, [(kc_ref, vc_ref), (kl_ref, vl_ref)], o_ref)


def _attn_ctx(q, k, v, seq):
    n = q.shape[0]
    blk = lambda w: pl.BlockSpec((seq, w), lambda b: (b, 0))
    return pl.pallas_call(
        _attn_ctx_kernel,
        grid=(n // seq,),
        in_specs=[blk(QK_PAD), blk(QK_PAD), blk(ATTN_WIDTH)],
        out_specs=blk(ATTN_WIDTH),
        out_shape=jax.ShapeDtypeStruct((n, ATTN_WIDTH), BF16),
        compiler_params=_params(("parallel",)),
        name="attn_ctx",
    )(q, k, v)


def _attn_lat(q, k, v, kc, vc, l, seq, tq):
    n = q.shape[0]
    per = seq // tq
    past = kc.shape[2]
    return pl.pallas_call(
        _attn_lat_kernel,
        grid=(n // seq, per),
        in_specs=[
            pl.BlockSpec((tq, QK_PAD), lambda b, j: (b * per + j, 0)),
            pl.BlockSpec((None, None, past, QK_PAD), lambda b, j: (l, b, 0, 0)),
            pl.BlockSpec((None, None, past, ATTN_WIDTH), lambda b, j: (l, b, 0, 0)),
            pl.BlockSpec((seq, QK_PAD), lambda b, j: (b, 0)),
            pl.BlockSpec((seq, ATTN_WIDTH), lambda b, j: (b, 0)),
        ],
        out_specs=pl.BlockSpec((tq, ATTN_WIDTH), lambda b, j: (b * per + j, 0)),
        out_shape=jax.ShapeDtypeStruct((n, ATTN_WIDTH), BF16),
        compiler_params=_params(("parallel", "arbitrary")),
        name="attn_lat",
    )(q, kc, vc, k, v)


def _cmul(ar, ai, br, bi):
    return ar * br - ai * bi, ar * bi + ai * br


def _outer_rows(pw, m):
    return (pw[:, None, :] * m[None, :, :]).reshape(CW, SSM_STATE)


def _ssm_prep_kernel(are_ref, aim_ref, ldt_ref, bre_ref, bim_ref, cre_ref, cim_ref,
                     mext_ref, et_ref, ft_ref, a16_ref):
    et_ref[...] = jnp.zeros(et_ref.shape, F32)
    ft_ref[...] = jnp.zeros(ft_ref.shape, F32)
    steps = lax.broadcasted_iota(jnp.int32, (CHUNK, SSM_STATE), 0).astype(F32)
    for gi in range(2):
        lanes = slice(gi * SSM_STATE, (gi + 1) * SSM_STATE)
        rows = slice(gi * CW, (gi + 1) * CW)
        cre = cre_ref[gi]
        cim = cim_ref[gi]
        m_dir = []
        for d in range(2):
            a_re = are_ref[d, gi]
            a_im = aim_ref[d, gi]
            dt = jnp.exp(ldt_ref[d, gi])
            mag = jnp.exp(a_re * dt)
            abr = mag * jnp.cos(a_im * dt)
            abi = mag * jnp.sin(a_im * dt)
            den = a_re * a_re + a_im * a_im
            nr = abr - 1.0
            coef_re = (nr * a_re + abi * a_im) / den
            coef_im = (abi * a_re - nr * a_im) / den
            bbr, bbi = _cmul(coef_re, coef_im, bre_ref[d, gi], bim_ref[d, gi])

            def power(nsteps):
                pm = jnp.exp(nsteps * (a_re * dt))
                return pm * jnp.cos(nsteps * (a_im * dt)), pm * jnp.sin(nsteps * (a_im * dt))

            to_end = (CHUNK - 1.0 - steps) if d == 0 else steps
            from_start = (steps + 1.0) if d == 0 else (CHUNK - steps)
            lag = steps if d == 0 else (CHUNK - 1.0 - steps)
            pr, pi = power(to_end)
            er = _outer_rows(pr, bbr) - _outer_rows(pi, bbi)
            ei = _outer_rows(pr, bbi) + _outer_rows(pi, bbr)
            et_ref[2 * d, rows, lanes] = er
            et_ref[2 * d + 1, rows, lanes] = ei
            pr, pi = power(from_start)
            ft_ref[2 * d, rows, lanes] = _outer_rows(pr, cre) - _outer_rows(pi, cim)
            ft_ref[2 * d + 1, rows, lanes] = -(_outer_rows(pr, cim) + _outer_rows(pi, cre))
            pr, pi = power(lag)
            xr = _outer_rows(pr, bbr) - _outer_rows(pi, bbi)
            xi = _outer_rows(pr, bbi) + _outer_rows(pi, bbr)
            nt = (((1,), (1,)), ((), ()))
            m_dir.append(
                lax.dot_general(xr, cre, nt, preferred_element_type=F32, precision=lax.Precision.HIGHEST)
                - lax.dot_general(xi, cim, nt, preferred_element_type=F32, precision=lax.Precision.HIGHEST))
            p16r, p16i = power(jnp.full((1, SSM_STATE), float(CHUNK), F32))
            a16_ref[2 * d:2 * d + 1, lanes] = p16r
            a16_ref[2 * d + 1:2 * d + 2, lanes] = p16i
        mf, mb = m_dir
        last = CW - SSM_GROUP
        mext_ref[gi, 0:last, :] = mb[0:last]
        mext_ref[gi, last:CW, :] = mb[last:CW] + mf[0:SSM_GROUP]
        mext_ref[gi, CW:2 * CW - SSM_GROUP, :] = mf[SSM_GROUP:CW]
        mext_ref[gi, 2 * CW - SSM_GROUP:2 * CW, :] = jnp.zeros((SSM_GROUP, SSM_GROUP), F32)


def _ssm_prep(a_re, a_im, log_dt, b_re, b_im, c_re, c_im):
    g2 = SSM_GROUPS // 2
    a_spec = pl.BlockSpec((None, 2, 2, 1, SSM_STATE), lambda l, g: (l, 0, g, 0, 0))
    b_spec = pl.BlockSpec((None, 2, 2, SSM_GROUP, SSM_STATE), lambda l, g: (l, 0, g, 0, 0))
    c_spec = pl.BlockSpec((None, 2, SSM_GROUP, SSM_STATE), lambda l, g: (l, g, 0, 0))
    out_shapes = (
        jax.ShapeDtypeStruct((DEPTH, SSM_GROUPS, 2 * CW, SSM_GROUP), F32),
        jax.ShapeDtypeStruct((DEPTH, g2, 4, 2 * CW, 2 * SSM_STATE), F32),
        jax.ShapeDtypeStruct((DEPTH, g2, 4, 2 * CW, 2 * SSM_STATE), F32),
        jax.ShapeDtypeStruct((DEPTH, g2, 4, 2 * SSM_STATE), F32),
    )
    return pl.pallas_call(
        _ssm_prep_kernel,
        grid=(DEPTH, g2),
        in_specs=[a_spec, a_spec,
                  pl.BlockSpec((None, 2, 2, 1, 1), lambda l, g: (l, 0, g, 0, 0)),
                  b_spec, b_spec, c_spec, c_spec],
        out_specs=[
            pl.BlockSpec((None, 2, 2 * CW, SSM_GROUP), lambda l, g: (l, g, 0, 0)),
            pl.BlockSpec((None, None, 4, 2 * CW, 2 * SSM_STATE), lambda l, g: (l, g, 0, 0, 0)),
            pl.BlockSpec((None, None, 4, 2 * CW, 2 * SSM_STATE), lambda l, g: (l, g, 0, 0, 0)),
            pl.BlockSpec((None, None, 4, 2 * SSM_STATE), lambda l, g: (l, g, 0, 0)),
        ],
        out_shape=out_shapes,
        compiler_params=_params(("arbitrary", "arbitrary")),
        name="ssm_prep",
    )(a_re.reshape(DEPTH, 2, SSM_GROUPS, 1, SSM_STATE), a_im.reshape(DEPTH, 2, SSM_GROUPS, 1, SSM_STATE),
      log_dt.reshape(DEPTH, 2, SSM_GROUPS, 1, 1),
      jnp.swapaxes(b_re, -1, -2), jnp.swapaxes(b_im, -1, -2), c_re, c_im)


def _toeplitz(mext):
    m5 = mext.reshape(DEPTH, SSM_GROUPS, 2 * CHUNK, SSM_GROUP, SSM_GROUP)
    j = np.arange(CHUNK)
    idx = j[None, :] - j[:, None] + (CHUNK - 1)
    t = m5[:, :, idx]
    return jnp.transpose(t, (0, 1, 2, 4, 3, 5)).reshape(DEPTH, SSM_GROUPS, CW, CW)


def _ssm_in_kernel(u_ref, t_ref, et_ref, y_ref, e0_ref, e1_ref, e2_ref, e3_ref):
    e_refs = (e0_ref, e1_ref, e2_ref, e3_ref)
    u0 = u_ref[0]
    u1 = u_ref[1]
    y_ref[0] = jnp.dot(u0, t_ref[0], preferred_element_type=F32)
    y_ref[1] = jnp.dot(u1, t_ref[1], preferred_element_type=F32)
    for x in range(4):
        e_refs[x][...] = (jnp.dot(u0, et_ref[x, 0:CW, :], preferred_element_type=F32)
                          + jnp.dot(u1, et_ref[x, CW:2 * CW, :], preferred_element_type=F32))


def _ssm_in(ug, tmat, et, l):
    r = ug.shape[1]
    g2 = SSM_GROUPS // 2
    e_shape = jax.ShapeDtypeStruct((r, SSM_GROUPS * SSM_STATE), F32)
    e_spec = pl.BlockSpec((r, 2 * SSM_STATE), lambda g: (0, g))
    return pl.pallas_call(
        _ssm_in_kernel,
        grid=(g2,),
        in_specs=[
            pl.BlockSpec((2, r, CW), lambda g: (g, 0, 0)),
            pl.BlockSpec((None, 2, CW, CW), lambda g: (l, g, 0, 0)),
            pl.BlockSpec((None, None, 4, 2 * CW, 2 * SSM_STATE), lambda g: (l, g, 0, 0, 0)),
        ],
        out_specs=[pl.BlockSpec((2, r, CW), lambda g: (g, 0, 0)), e_spec, e_spec, e_spec, e_spec],
        out_shape=(jax.ShapeDtypeStruct((SSM_GROUPS, r, CW), F32), e_shape, e_shape, e_shape, e_shape),
        compiler_params=_params(("parallel",)),
        name="ssm_in",
    )(ug, tmat, et)


def _ssm_scan_kernel(nchunk, rb, efr_ref, efi_ref, ebr_ref, ebi_ref, a16_ref, h0fr_ref, h0fi_ref, h0br_ref,
                     h0bi_ref, hfr_ref, hfi_ref, hbr_ref, hbi_ref, fin_ref):
    def run(er_ref, ei_ref, hr_ref, hi_ref, ar, ai, h0r, h0i, backward):
        def body(i, carry):
            hr, hi = carry
            k = (nchunk - 1 - i) if backward else i
            rows = pl.ds(pl.multiple_of(k * rb, rb), rb)
            hr_ref[rows, :] = hr.astype(hr_ref.dtype)
            hi_ref[rows, :] = hi.astype(hi_ref.dtype)
            nr = ar * hr - ai * hi + er_ref[rows, :]
            ni = ar * hi + ai * hr + ei_ref[rows, :]
            return nr, ni
        return lax.fori_loop(0, nchunk, body, (h0r, h0i))

    ffr, ffi = run(efr_ref, efi_ref, hfr_ref, hfi_ref, a16_ref[0:1, :], a16_ref[1:2, :],
                   h0fr_ref[...], h0fi_ref[...], False)
    fbr, fbi = run(ebr_ref, ebi_ref, hbr_ref, hbi_ref, a16_ref[2:3, :], a16_ref[3:4, :],
                   h0br_ref[...], h0bi_ref[...], True)
    fin_ref[0] = ffr
    fin_ref[1] = ffi
    fin_ref[2] = fbr
    fin_ref[3] = fbi


def _ssm_scan(es, a16, h0s, nchunk, rb):
    r = es[0].shape[0]
    width = SSM_GROUPS * SSM_STATE
    lb = 256
    big = pl.BlockSpec((r, lb), lambda j: (0, j))
    small = pl.BlockSpec((rb, lb), lambda j: (0, j))
    h_shape = jax.ShapeDtypeStruct((r, width), F32)
    return pl.pallas_call(
        functools.partial(_ssm_scan_kernel, nchunk, rb),
        grid=(width // lb,),
        in_specs=[big, big, big, big, pl.BlockSpec((4, lb), lambda j: (0, j)), small, small, small, small],
        out_specs=[big, big, big, big, pl.BlockSpec((4, rb, lb), lambda j: (0, 0, j))],
        out_shape=(h_shape, h_shape, h_shape, h_shape, jax.ShapeDtypeStruct((4, rb, width), F32)),
        compiler_params=_params(("parallel",)),
        name="ssm_scan",
    )(*es, a16, *h0s)


def _ssm_out_kernel(y_ref, h0_ref, h1_ref, h2_ref, h3_ref, ft_ref, o_ref):
    nt = (((1,), (1,)), ((), ()))
    acc = None
    for x, h_ref in enumerate((h0_ref, h1_ref, h2_ref, h3_ref)):
        part = lax.dot_general(h_ref[...].astype(BF16), ft_ref[x], nt, preferred_element_type=F32)
        acc = part if acc is None else acc + part
    o_ref[0] = y_ref[0] + acc[:, :CW]
    o_ref[1] = y_ref[1] + acc[:, CW:]


def _ssm_out(y_intra, hs, ft, l):
    r = y_intra.shape[1]
    g2 = SSM_GROUPS // 2
    h_spec = pl.BlockSpec((r, 2 * SSM_STATE), lambda g: (0, g))
    y_spec = pl.BlockSpec((2, r, CW), lambda g: (g, 0, 0))
    return pl.pallas_call(
        _ssm_out_kernel,
        grid=(g2,),
        in_specs=[y_spec, h_spec, h_spec, h_spec, h_spec,
                  pl.BlockSpec((None, None, 4, 2 * CW, 2 * SSM_STATE), lambda g: (l, g, 0, 0, 0))],
        out_specs=y_spec,
        out_shape=jax.ShapeDtypeStruct((SSM_GROUPS, r, CW), F32),
        compiler_params=_params(("parallel",)),
        name="ssm_out",
    )(y_intra, *hs, ft)


def _ssm_mix(u, nb, seq, h0_re, h0_im, prep, l):
    tmat, et, ft, a16 = prep
    nchunk = seq // CHUNK
    rb = max(nb, 8)
    ug = u.reshape(nb, nchunk, CHUNK, SSM_GROUPS, SSM_GROUP)
    ug = jnp.transpose(ug, (3, 1, 0, 2, 4))
    if rb != nb:
        ug = jnp.pad(ug, ((0, 0), (0, 0), (0, rb - nb), (0, 0), (0, 0)))
    ug = ug.reshape(SSM_GROUPS, nchunk * rb, CW)
    y_intra, *es = _ssm_in(ug, tmat, et, l)
    width = SSM_GROUPS * SSM_STATE

    def h0_rows(h, d):
        h = h[:, d].reshape(nb, width).astype(F32)
        return jnp.pad(h, ((0, rb - nb), (0, 0))) if rb != nb else h

    h0s = (h0_rows(h0_re, 0), h0_rows(h0_im, 0), h0_rows(h0_re, 1), h0_rows(h0_im, 1))
    a16_l = jnp.transpose(a16[l], (1, 0, 2)).reshape(4, width)
    *hs, fin = _ssm_scan(es, a16_l, h0s, nchunk, rb)
    y = _ssm_out(y_intra, hs, ft, l)
    y = y.reshape(SSM_GROUPS, nchunk, rb, CHUNK, SSM_GROUP)[:, :, :nb]
    y = jnp.transpose(y, (2, 1, 3, 0, 4)).reshape(nb * seq, SSM_WIDTH)
    fin = fin[:, :nb].reshape(2, 2, nb, SSM_GROUPS, SSM_STATE)
    return y, fin


def _gelu_tanh(x):
    return 0.5 * x * (1.0 + jnp.tanh(math.sqrt(2.0 / math.pi) * (x + 0.044715 * (x * x * x))))


def _route(aff, bias):
    a = [aff[e:e + 1, :] for e in range(N_EXPERTS)]
    s = [a[e] + bias[e:e + 1, :] for e in range(N_EXPERTS)]
    keep = []
    for e in range(N_EXPERTS):
        g0 = (e // EXPERTS_PER_GROUP) * EXPERTS_PER_GROUP
        rank = None
        for j in range(g0, g0 + EXPERTS_PER_GROUP):
            if j == e:
                continue
            beats = (s[j] >= s[e]) if j < e else (s[j] > s[e])
            r = jnp.where(beats, 1.0, 0.0)
            rank = r if rank is None else rank + r
        keep.append(rank < 1.5)
    score = []
    for g in range(N_GROUPS):
        tot = None
        for e in range(g * EXPERTS_PER_GROUP, (g + 1) * EXPERTS_PER_GROUP):
            v = jnp.where(keep[e], s[e], 0.0)
            tot = v if tot is None else tot + v
        score.append(tot)
    gates = []
    for g in range(N_GROUPS):
        lost = None
        for j in range(N_GROUPS):
            if j == g:
                continue
            beats = (score[j] >= score[g]) if j < g else (score[j] > score[g])
            r = jnp.where(beats, 1.0, 0.0)
            lost = r if lost is None else lost + r
        chosen = lost < 0.5
        w = [jnp.where(keep[e], a[e], 0.0) for e in range(g * EXPERTS_PER_GROUP, (g + 1) * EXPERTS_PER_GROUP)]
        tot = w[0] + w[1] + w[2] + w[3]
        for v in w:
            gates.append(jnp.where(chosen, v / tot, 0.0))
    return jnp.concatenate(gates, axis=0)


def _merge_kernel(x_ref, y_ref, u_ref, at_ref, sga_ref, sgb_ref, mod_ref, d_ref, wglu_ref, bglu_ref, wsp_ref,
                  wap_ref, wo_ref, gffn_ref, wr_ref, rb_ref, xo_ref, h2_ref, gt_ref):
    mod = mod_ref[...]
    gate1 = mod[:, 2 * D_MODEL:3 * D_MODEL]
    shift2 = mod[:, 3 * D_MODEL:4 * D_MODEL]
    scale2 = mod[:, 4 * D_MODEL:5 * D_MODEL]
    y = _gelu_tanh(y_ref[...] + d_ref[...] * u_ref[...].astype(F32))
    glu = jnp.dot(y.astype(BF16), wglu_ref[...], preferred_element_type=F32) + bglu_ref[...]
    y = (y * _sigmoid(glu)).astype(BF16)
    a_out = jnp.dot(y, wsp_ref[...], preferred_element_type=F32)
    b_out = jnp.dot(at_ref[...], wap_ref[...], preferred_element_type=F32)
    m = (sga_ref[...].astype(F32) * a_out + sgb_ref[...].astype(F32) * b_out).astype(BF16)
    x = x_ref[...] + gate1 * jnp.dot(m, wo_ref[...], preferred_element_type=F32)
    xo_ref[...] = x
    h2 = (_rms(x, D_MODEL) * gffn_ref[...]) * (1.0 + scale2) + shift2
    h2_ref[...] = h2.astype(BF16)
    logits = lax.dot_general(wr_ref[...], h2, (((1,), (1,)), ((), ())), preferred_element_type=F32,
                             precision=lax.Precision.HIGHEST)
    gates = _route(_sigmoid(logits), rb_ref[...])
    pad = jnp.zeros((LANES - N_EXPERTS, gates.shape[1]), F32)
    gt_ref[...] = jnp.concatenate([gates, pad], axis=0).T


def _merge(x, y, u, attn, sga, sgb, mods, l, lat, wts):
    n = x.shape[0]
    tt = TOK_TILE
    per_seq = 2048 // tt
    mod_map = (lambda i: (l, 1 + i // per_seq, 0, 0)) if lat else (lambda i: (l, 0, 0, 0))
    row = lambda i: (i, 0)
    lw = lambda *shape: pl.BlockSpec((None,) + shape, lambda i: (l,) + (0,) * len(shape))
    return pl.pallas_call(
        _merge_kernel,
        grid=(n // tt,),
        in_specs=[
            pl.BlockSpec((tt, D_MODEL), row), pl.BlockSpec((tt, SSM_WIDTH), row), pl.BlockSpec((tt, SSM_WIDTH), row),
            pl.BlockSpec((tt, ATTN_WIDTH), row), pl.BlockSpec((tt, D_MODEL), row), pl.BlockSpec((tt, D_MODEL), row),
            pl.BlockSpec((None, None, 1, 6 * D_MODEL), mod_map),
            lw(1, SSM_WIDTH), lw(SSM_WIDTH, SSM_WIDTH), lw(1, SSM_WIDTH), lw(SSM_WIDTH, D_MODEL),
            lw(ATTN_WIDTH, D_MODEL), lw(D_MODEL, D_MODEL), lw(1, D_MODEL),
            pl.BlockSpec((N_EXPERTS, D_MODEL), lambda i: (0, 0)),
            pl.BlockSpec((N_EXPERTS, 1), lambda i: (0, 0)),
        ],
        out_specs=[pl.BlockSpec((tt, D_MODEL), row), pl.BlockSpec((tt, D_MODEL), row),
                   pl.BlockSpec((tt, LANES), row)],
        out_shape=(jax.ShapeDtypeStruct((n, D_MODEL), F32), jax.ShapeDtypeStruct((n, D_MODEL), BF16),
                   jax.ShapeDtypeStruct((n, LANES), F32)),
        compiler_params=_params(("parallel",)),
        name="merge_lat" if lat else "merge_ctx",
    )(x, y, u, attn, sga, sgb, mods, wts["ssm_d"], wts["w_glu"], wts["b_glu"], wts["w_ssm_proj"],
      wts["w_attn_proj"], wts["w_o"], wts["g_ffn"], wts["w_router_t"], wts["router_bias"])


def _moe_kernel(x_ref, h_ref, gt_ref, mod_ref, wgu_ref, wd_ref, o_ref, acc_ref):
    e = pl.program_id(1)

    @pl.when(e == 0)
    def _():
        acc_ref[...] = jnp.zeros(acc_ref.shape, F32)

    gu = jnp.dot(h_ref[...], wgu_ref[...], preferred_element_type=F32)
    hg = gu[:, :D_EXPERT]
    lane = lax.broadcasted_iota(jnp.int32, gt_ref.shape, 1)
    gate = jnp.sum(jnp.where(lane == e, gt_ref[...], 0.0), axis=-1, keepdims=True)
    act = (hg * _sigmoid(hg)) * gu[:, D_EXPERT:] * gate
    acc_ref[...] += jnp.dot(act.astype(BF16), wd_ref[...], preferred_element_type=F32)

    @pl.when(e == N_EXPERTS - 1)
    def _():
        gate2 = mod_ref[:, 5 * D_MODEL:6 * D_MODEL]
        o_ref[...] = x_ref[...] + gate2 * acc_ref[...]


def _moe(x, h2, gates, mods, l, lat, wts):
    n = x.shape[0]
    tt = TOK_TILE
    per_seq = 2048 // tt
    mod_map = (lambda i, e: (l, 1 + i // per_seq, 0, 0)) if lat else (lambda i, e: (l, 0, 0, 0))
    row = lambda i, e: (i, 0)
    return pl.pallas_call(
        _moe_kernel,
        grid=(n // tt, N_EXPERTS),
        in_specs=[
            pl.BlockSpec((tt, D_MODEL), row), pl.BlockSpec((tt, D_MODEL), row), pl.BlockSpec((tt, LANES), row),
            pl.BlockSpec((None, None, 1, 6 * D_MODEL), mod_map),
            pl.BlockSpec((None, None, D_MODEL, 2 * D_EXPERT), lambda i, e: (l, e, 0, 0)),
            pl.BlockSpec((None, None, D_EXPERT, D_MODEL), lambda i, e: (l, e, 0, 0)),
        ],
        out_specs=pl.BlockSpec((tt, D_MODEL), row),
        out_shape=jax.ShapeDtypeStruct((n, D_MODEL), F32),
        scratch_shapes=[pltpu.VMEM((tt, D_MODEL), F32)],
        compiler_params=_params(("parallel", "arbitrary")),
        name="moe_lat" if lat else "moe_ctx",
    )(x, h2, gates, mods, wts["w_gu"], wts["w_down"])


def _pad_heads(w, width):
    lead = w.shape[:-1]
    w = w.reshape(lead + (N_HEADS, width))
    w = jnp.pad(w, [(0, 0)] * len(lead) + [(0, 0), (0, HEAD_PAD - width)])
    return w.reshape(lead + (N_HEADS * HEAD_PAD,))


def _layout_weights(w_in, g_qa, w_qb, g_q, g_kva, w_kvb, g_k, g_mix, w_attn_proj, ssm_d, ssm_w_glu, ssm_b_glu,
                    ssm_w_proj, w_o, g_ffn, w_router, router_bias, w_gate, w_up, w_down):
    n_a = SSM_WIDTH + Q_LORA + KV_LORA + QK_ROPE
    w_a = jnp.pad(w_in[:, :, :n_a], ((0, 0), (0, 0), (0, 1024 - n_a)))
    kv = w_kvb.reshape(DEPTH, KV_LORA, N_HEADS, QK_NOPE + V_HEAD)
    k_cols = _pad_heads(kv[..., :QK_NOPE].reshape(DEPTH, KV_LORA, N_HEADS * QK_NOPE), QK_NOPE)
    v_cols = kv[..., QK_NOPE:].reshape(DEPTH, KV_LORA, ATTN_WIDTH)
    place = np.zeros((KV_LORA, QK_PAD + ATTN_WIDTH), np.float32)
    for h in range(N_HEADS):
        for r in range(QK_ROPE):
            place[r, h * HEAD_PAD + QK_NOPE + r] = 1.0
    w_k = jnp.concatenate([jnp.concatenate([k_cols, v_cols], axis=-1),
                           jnp.broadcast_to(jnp.asarray(place), (DEPTH,) + place.shape)], axis=1)
    pad_gain = lambda g: jnp.pad(g, ((0, 0), (0, HEAD_PAD - QK_HEAD))).reshape(DEPTH, 1, HEAD_PAD)
    vec = lambda g: g.reshape(DEPTH, 1, -1)
    return {
        "g_mix": vec(g_mix), "w_a": w_a.astype(BF16),
        "w_ga": w_in[:, :, n_a:n_a + D_MODEL].astype(BF16), "w_gb": w_in[:, :, n_a + D_MODEL:].astype(BF16),
        "g_qa": vec(g_qa), "w_qb": _pad_heads(w_qb, QK_HEAD).astype(BF16), "g_q": pad_gain(g_q),
        "g_kva": vec(g_kva), "w_k": w_k.astype(BF16), "g_k": pad_gain(g_k),
        "ssm_d": vec(ssm_d), "w_glu": ssm_w_glu.astype(BF16), "b_glu": vec(ssm_b_glu),
        "w_ssm_proj": ssm_w_proj.astype(BF16), "w_attn_proj": w_attn_proj.astype(BF16), "w_o": w_o.astype(BF16),
        "g_ffn": vec(g_ffn), "w_router_t": w_router.T, "router_bias": router_bias.reshape(N_EXPERTS, 1),
        "w_gu": jnp.concatenate([w_gate, w_up], axis=-1).astype(BF16), "w_down": w_down.astype(BF16),
    }


def _rope_tables(n_tokens):
    pos = np.arange(n_tokens)
    inv = 1.0 / (ROPE_BASE ** (np.arange(ROPE_AXIS // 2, dtype=np.float32) * 2.0 / ROPE_AXIS))
    ang = np.zeros((n_tokens, HEAD_PAD), np.float32)
    half = ROPE_AXIS // 2
    row = (pos // GRID_W).astype(np.float32)[:, None] * inv
    col = (pos % GRID_W).astype(np.float32)[:, None] * inv
    first = np.zeros((HEAD_PAD,), bool)
    second = np.zeros((HEAD_PAD,), bool)
    for base, a in ((QK_NOPE, row), (QK_NOPE + ROPE_AXIS, col)):
        ang[:, base:base + half] = a
        ang[:, base + half:base + 2 * half] = a
        first[base:base + half] = True
        second[base + half:base + 2 * half] = True
    ang = jnp.asarray(ang)
    rot = jnp.asarray(first | second)
    cos = jnp.where(rot, jnp.cos(ang), 1.0)
    sin = jnp.sin(ang)
    return cos, jnp.where(jnp.asarray(first), -sin, 0.0), jnp.where(jnp.asarray(second), sin, 0.0)


def kernel(x_prompt, x_sample, c, cache_ckv, cache_krope, state_ssm_re, state_ssm_im, c_ctx, w_ada, b_ada, g_mix, w_in, g_qa, w_qb, g_q, g_kva, w_kvb, g_k, w_attn_proj, ssm_a_re, ssm_a_im, ssm_log_dt, ssm_b_re, ssm_b_im, ssm_c_re, ssm_c_im, ssm_d, ssm_w_glu, ssm_b_glu, ssm_w_proj, w_o, g_ffn, w_router, router_bias, w_gate, w_up, w_down):
    nb_c, seq_c, _ = x_prompt.shape
    nb_l, seq_l, _ = x_sample.shape
    wts = _layout_weights(w_in, g_qa, w_qb, g_q, g_kva, w_kvb, g_k, g_mix, w_attn_proj, ssm_d, ssm_w_glu,
                          ssm_b_glu, ssm_w_proj, w_o, g_ffn, w_router, router_bias, w_gate, w_up, w_down)
    cvec = jnp.concatenate([c_ctx[None, :], c, jnp.zeros((8 - 1 - nb_l, D_MODEL), F32)], axis=0)
    mods = _mods(cvec, w_ada, b_ada).reshape(DEPTH, 8, 1, 6 * D_MODEL)
    mext, et, ft, a16 = _ssm_prep(ssm_a_re, ssm_a_im, ssm_log_dt, ssm_b_re, ssm_b_im, ssm_c_re, ssm_c_im)
    prep = (_toeplitz(mext).astype(BF16), et.astype(BF16), ft.astype(BF16), a16)
    tabs = _rope_tables(seq_l)
    kc, vc = _cachekv(cache_ckv, cache_krope, wts)

    xp = x_prompt.reshape(nb_c * seq_c, D_MODEL)
    xs = x_sample.reshape(nb_l * seq_l, D_MODEL)
    zeros_c = jnp.zeros((nb_c, 2, SSM_GROUPS, SSM_STATE), F32)
    ckvs, kropes, fins = [], [], []
    for l in range(DEPTH):
        u, q, k, v, sga, sgb, ckv, krope = _inproj(xp, mods, l, False, wts, tabs)
        ckvs.append(ckv)
        kropes.append(krope)
        y, fin = _ssm_mix(u, nb_c, seq_c, zeros_c, zeros_c, prep, l)
        fins.append(fin)
        attn = _attn_ctx(q, k, v, seq_c)
        xp, h2, gates = _merge(xp, y, u, attn, sga, sgb, mods, l, False, wts)
        xp = _moe(xp, h2, gates, mods, l, False, wts)

        u, q, k, v, sga, sgb, _, _ = _inproj(xs, mods, l, True, wts, tabs)
        y, _ = _ssm_mix(u, nb_l, seq_l, state_ssm_re[:, l], state_ssm_im[:, l], prep, l)
        attn = _attn_lat(q, k, v, kc, vc, l, seq_l, 256)
        xs, h2, gates = _merge(xs, y, u, attn, sga, sgb, mods, l, True, wts)
        xs = _moe(xs, h2, gates, mods, l, True, wts)

    new_ckv = jnp.stack(ckvs, axis=0).reshape(DEPTH, nb_c, seq_c, KV_LORA).transpose(1, 0, 2, 3)
    new_krope = jnp.stack(kropes, axis=0).reshape(DEPTH, nb_c, seq_c, QK_ROPE).transpose(1, 0, 2, 3)
    fin = jnp.stack(fins, axis=0)
    new_re = jnp.transpose(fin[:, :, 0], (2, 0, 1, 3, 4))
    new_im = jnp.transpose(fin[:, :, 1], (2, 0, 1, 3, 4))
    return (xp.reshape(nb_c, seq_c, D_MODEL), xs.reshape(nb_l, seq_l, D_MODEL), new_ckv, new_krope, new_re, new_im)
```

```python
import functools
import math

import jax
import jax.numpy as jnp
import numpy as np
from jax import lax
from jax.experimental import pallas as pl
from jax.experimental.pallas import tpu as pltpu

F32 = jnp.float32
BF16 = jnp.bfloat16

D_MODEL = 1024
DEPTH = 4
GRID_W = 64
SSM_WIDTH = 512
SSM_GROUP = 16
SSM_GROUPS = 32
SSM_STATE = 64
N_HEADS = 8
QK_NOPE = 64
QK_ROPE = 32
QK_HEAD = 96
V_HEAD = 64
Q_LORA = 256
KV_LORA = 128
ATTN_WIDTH = 512
ROPE_AXIS = 16
ROPE_BASE = 10000.0
N_EXPERTS = 16
N_GROUPS = 4
EXPERTS_PER_GROUP = 4
D_EXPERT = 256
EPS = 1e-6

LANES = 128
HEAD_PAD = LANES
QK_PAD = N_HEADS * HEAD_PAD
CHUNK = 16
CW = CHUNK * SSM_GROUP
TOK_TILE = 512
TILE_CHUNKS = TOK_TILE // CHUNK
LANE_TILES = SSM_WIDTH // LANES
GROUPS_PER_TILE = LANES // SSM_GROUP
VMEM_LIMIT = 48 * 1024 * 1024


def _params(sem, vmem=VMEM_LIMIT):
    return pltpu.CompilerParams(dimension_semantics=sem, vmem_limit_bytes=vmem)


def _sigmoid(x):
    return 1.0 / (1.0 + jnp.exp(-x))


def _rms(x, n):
    return x * lax.rsqrt(jnp.sum(x * x, axis=-1, keepdims=True) * (1.0 / n) + EPS)


def _mods_kernel(c_ref, w_ref, b_ref, o_ref):
    c = c_ref[...]
    cs = (c * _sigmoid(c)).astype(BF16)
    o_ref[...] = jnp.dot(cs, w_ref[...].astype(BF16), preferred_element_type=F32) + b_ref[...]


def _mods(cvec, w_ada, b_ada):
    nb = 1536
    return pl.pallas_call(
        _mods_kernel,
        grid=(DEPTH, 6 * D_MODEL // nb),
        in_specs=[
            pl.BlockSpec((8, D_MODEL), lambda l, j: (0, 0)),
            pl.BlockSpec((None, D_MODEL, nb), lambda l, j: (l, 0, j)),
            pl.BlockSpec((None, 1, nb), lambda l, j: (l, 0, j)),
        ],
        out_specs=pl.BlockSpec((None, 8, nb), lambda l, j: (l, 0, j)),
        out_shape=jax.ShapeDtypeStruct((DEPTH, 8, 6 * D_MODEL), F32),
        compiler_params=_params(("arbitrary", "arbitrary")),
        name="mods",
    )(cvec, w_ada, b_ada.reshape(DEPTH, 1, 6 * D_MODEL))


def _slot_perm():
    perm = np.zeros((SSM_GROUPS, CW), np.int32)
    for g in range(SSM_GROUPS):
        for s in range(CHUNK):
            slot = 8 * (s // 8) + (g + s) % 8
            perm[g, SSM_GROUP * slot:SSM_GROUP * (slot + 1)] = SSM_GROUP * s + np.arange(SSM_GROUP)
    return perm


def _lane_segment():
    return lax.broadcasted_iota(jnp.int32, (TILE_CHUNKS, LANES), 1) // SSM_GROUP


def _to_chunk_rows(tok_ref, ug_ref):
    seg = _lane_segment()
    for j in range(LANE_TILES):
        for half in range(CHUNK // 8):
            rolled = []
            for r in range(8):
                v = tok_ref[j, pl.ds(8 * half + r, TILE_CHUNKS, stride=CHUNK), :]
                rolled.append(pltpu.roll(v, SSM_GROUP * r, 1) if r else v)
            for gg in range(GROUPS_PER_TILE):
                acc = rolled[(0 - gg) % 8]
                for m in range(1, 8):
                    acc = jnp.where(seg == m, rolled[(m - gg) % 8], acc)
                ug_ref[GROUPS_PER_TILE * j + gg, :, LANES * half:LANES * (half + 1)] = acc.astype(ug_ref.dtype)


def _from_chunk_rows(yg_ref, tok_ref):
    seg = _lane_segment()
    for j in range(LANE_TILES):
        for half in range(CHUNK // 8):
            ys = [yg_ref[GROUPS_PER_TILE * j + gg, :, LANES * half:LANES * (half + 1)]
                  for gg in range(GROUPS_PER_TILE)]
            for r in range(8):
                acc = ys[(0 - r) % 8]
                for m in range(1, 8):
                    acc = jnp.where(seg == m, ys[(m - r) % 8], acc)
                out = pltpu.roll(acc, LANES - SSM_GROUP * r, 1) if r else acc
                tok_ref[j, pl.ds(8 * half + r, TILE_CHUNKS, stride=CHUNK), :] = out


def _head_norm_rope(t, gain, swapped):
    outs = []
    for h in range(N_HEADS):
        lanes = slice(h * HEAD_PAD, (h + 1) * HEAD_PAD)
        th = t[:, lanes]
        scale = lax.rsqrt(jnp.sum(th * th, axis=-1, keepdims=True) * (1.0 / QK_HEAD) + EPS)
        y = th * scale * gain
        if swapped is not None:
            t_sw, gain_sw, cos, sin = swapped
            y = y * cos + (t_sw[:, lanes] * scale * gain_sw) * sin
        outs.append(y)
    return jnp.concatenate(outs, axis=1)


def _kv_dot(ckvn, krp, wk_ref):
    return (jnp.dot(ckvn, wk_ref[0:KV_LORA, :], preferred_element_type=F32)
            + jnp.dot(krp, wk_ref[KV_LORA:KV_LORA + krp.shape[1], :], preferred_element_type=F32))


def _inproj_kernel(use_rope, x_ref, mod_ref, gmix_ref, wa_ref, wga_ref, wgb_ref, gqa_ref, wqb_ref, wqbs_ref, gq_ref,
                   gqs_ref, gkva_ref, wk_ref, wks_ref, gk_ref, gks_ref, cos_ref, sin_ref,
                   u_ref, ug_ref, q_ref, k_ref, v_ref, sga_ref, sgb_ref, ckv_ref, kr_ref, tok_ref):
    x = x_ref[...]
    mod = mod_ref[...]
    shift1 = mod[:, 0:D_MODEL]
    scale1 = mod[:, D_MODEL:2 * D_MODEL]
    h = _rms(x, D_MODEL) * gmix_ref[...]
    h = (h * (1.0 + scale1) + shift1).astype(BF16)
    za = jnp.dot(h, wa_ref[...], preferred_element_type=F32)
    sga_ref[...] = _sigmoid(jnp.dot(h, wga_ref[...], preferred_element_type=F32)).astype(BF16)
    sgb_ref[...] = _sigmoid(jnp.dot(h, wgb_ref[...], preferred_element_type=F32)).astype(BF16)
    u_ref[...] = za[:, :SSM_WIDTH].astype(BF16)
    for j in range(LANE_TILES):
        tok_ref[j] = za[:, LANES * j:LANES * (j + 1)]
    _to_chunk_rows(tok_ref, ug_ref)
    qa = za[:, SSM_WIDTH:SSM_WIDTH + Q_LORA]
    ckv = za[:, SSM_WIDTH + Q_LORA:SSM_WIDTH + Q_LORA + KV_LORA]
    krp = za[:, SSM_WIDTH + Q_LORA + KV_LORA:]
    qn = (_rms(qa, Q_LORA) * gqa_ref[...]).astype(BF16)
    qr = jnp.dot(qn, wqb_ref[...], preferred_element_type=F32)
    ckvn = _rms(ckv, KV_LORA) * gkva_ref[...]
    ckv_ref[...] = ckvn
    kr_ref[...] = krp[:, :QK_ROPE]
    ckvn = ckvn.astype(BF16)
    krp = krp.astype(BF16)
    kv = _kv_dot(ckvn, krp, wk_ref)
    q_sw = k_sw = None
    if use_rope:
        cos = cos_ref[...]
        sin = sin_ref[...]
        q_sw = (jnp.dot(qn, wqbs_ref[...], preferred_element_type=F32), gqs_ref[...], cos, sin)
        k_sw = (_kv_dot(ckvn, krp, wks_ref), gks_ref[...], cos, sin)
    q = _head_norm_rope(qr, gq_ref[...], q_sw)
    q_ref[...] = (q * (1.0 / math.sqrt(QK_HEAD))).astype(BF16)
    k_ref[...] = _head_norm_rope(kv[:, :QK_PAD], gk_ref[...], k_sw).astype(BF16)
    v_ref[...] = kv[:, QK_PAD:].astype(BF16)


def _inproj(x, mods, l, lat, wts, tabs):
    n = x.shape[0]
    tt = TOK_TILE
    per_seq = 2048 // tt
    if lat:
        mod_map = lambda i: (l, 1 + i // per_seq, 0, 0)
        tab_map = lambda i: (i % per_seq, 0)
    else:
        mod_map = lambda i: (l, 0, 0, 0)
        tab_map = lambda i: (0, 0)
    row = lambda i: (i, 0)
    lw = lambda *shape: pl.BlockSpec((None,) + shape, lambda i: (l,) + (0,) * len(shape))
    tab = pl.BlockSpec((tt, HEAD_PAD), tab_map)
    row_shapes = (
        jax.ShapeDtypeStruct((n, SSM_WIDTH), BF16),
        jax.ShapeDtypeStruct((n, QK_PAD), BF16),
        jax.ShapeDtypeStruct((n, QK_PAD), BF16),
        jax.ShapeDtypeStruct((n, ATTN_WIDTH), BF16),
        jax.ShapeDtypeStruct((n, D_MODEL), BF16),
        jax.ShapeDtypeStruct((n, D_MODEL), BF16),
        jax.ShapeDtypeStruct((n, KV_LORA), F32),
        jax.ShapeDtypeStruct((n, QK_ROPE), F32),
    )
    row_specs = [pl.BlockSpec((tt, s.shape[1]), row) for s in row_shapes]
    ug_shape = jax.ShapeDtypeStruct((SSM_GROUPS, n // CHUNK, CW), BF16)
    ug_spec = pl.BlockSpec((SSM_GROUPS, TILE_CHUNKS, CW), lambda i: (0, i, 0))
    outs = pl.pallas_call(
        functools.partial(_inproj_kernel, lat),
        grid=(n // tt,),
        in_specs=[
            pl.BlockSpec((tt, D_MODEL), row),
            pl.BlockSpec((None, None, 1, 6 * D_MODEL), mod_map),
            lw(1, D_MODEL), lw(D_MODEL, 1024), lw(D_MODEL, D_MODEL), lw(D_MODEL, D_MODEL),
            lw(1, Q_LORA), lw(Q_LORA, QK_PAD), lw(Q_LORA, QK_PAD), lw(1, HEAD_PAD), lw(1, HEAD_PAD), lw(1, KV_LORA),
            lw(2 * KV_LORA, QK_PAD + ATTN_WIDTH), lw(2 * KV_LORA, QK_PAD), lw(1, HEAD_PAD), lw(1, HEAD_PAD),
            tab, tab,
        ],
        out_specs=row_specs[:1] + [ug_spec] + row_specs[1:],
        out_shape=row_shapes[:1] + (ug_shape,) + row_shapes[1:],
        scratch_shapes=[pltpu.VMEM((LANE_TILES, tt, LANES), F32)],
        compiler_params=_params(("parallel",)),
        name="inproj_lat" if lat else "inproj_ctx",
    )(x, mods, wts["g_mix"], wts["w_a"], wts["w_ga"], wts["w_gb"], wts["g_qa"], wts["w_qb"], wts["w_qb_sw"],
      wts["g_q"], wts["g_q_sw"], wts["g_kva"], wts["w_k"], wts["w_k_sw"], wts["g_k"], wts["g_k_sw"], *tabs)
    return outs


def _cachekv_kernel(ckv_ref, kr_ref, wk_ref, gk_ref, k_ref, v_ref):
    kv = _kv_dot(ckv_ref[...].astype(BF16), kr_ref[...].astype(BF16), wk_ref)
    k_ref[...] = _head_norm_rope(kv[:, :QK_PAD], gk_ref[...], None).astype(BF16)
    v_ref[...] = kv[:, QK_PAD:].astype(BF16)


def _cachekv(cache_ckv, cache_krope, wts):
    nb, _, past, _ = cache_ckv.shape
    return pl.pallas_call(
        _cachekv_kernel,
        grid=(DEPTH, nb),
        in_specs=[
            pl.BlockSpec((None, None, past, KV_LORA), lambda l, b: (b, l, 0, 0)),
            pl.BlockSpec((None, None, past, QK_ROPE), lambda l, b: (b, l, 0, 0)),
            pl.BlockSpec((None, 2 * KV_LORA, QK_PAD + ATTN_WIDTH), lambda l, b: (l, 0, 0)),
            pl.BlockSpec((None, 1, HEAD_PAD), lambda l, b: (l, 0, 0)),
        ],
        out_specs=[
            pl.BlockSpec((None, None, past, QK_PAD), lambda l, b: (l, b, 0, 0)),
            pl.BlockSpec((None, None, past, ATTN_WIDTH), lambda l, b: (l, b, 0, 0)),
        ],
        out_shape=(jax.ShapeDtypeStruct((DEPTH, nb, past, QK_PAD), BF16),
                   jax.ShapeDtypeStruct((DEPTH, nb, past, ATTN_WIDTH), BF16)),
        compiler_params=_params(("arbitrary", "arbitrary")),
        name="cache_kv",
    )(cache_ckv, cache_krope, wts["w_k"], wts["g_k"])


def _attn_heads(q_ref, segs, o_ref):
    for h in range(N_HEADS):
        qh = q_ref[:, h * HEAD_PAD:(h + 1) * HEAD_PAD]
        scores = [lax.dot_general(qh, k_ref[:, h * HEAD_PAD:(h + 1) * HEAD_PAD], (((1,), (1,)), ((), ())),
                                  preferred_element_type=F32) for k_ref, _ in segs]
        m = scores[0].max(axis=-1, keepdims=True)
        for s in scores[1:]:
            m = jnp.maximum(m, s.max(axis=-1, keepdims=True))
        den = None
        acc = None
        for s, (_, v_ref) in zip(scores, segs):
            p = jnp.exp(s - m)
            d = p.sum(axis=-1, keepdims=True)
            a = jnp.dot(p.astype(BF16), v_ref[:, h * V_HEAD:(h + 1) * V_HEAD], preferred_element_type=F32)
            den = d if den is None else den + d
            acc = a if acc is None else acc + a
        o_ref[:, h * V_HEAD:(h + 1) * V_HEAD] = (acc / den).astype(o_ref.dtype)


def _attn_ctx_kernel(q_ref, k_ref, v_ref, o_ref):
    _attn_heads(q_ref, [(k_ref, v_ref)], o_ref)


def _attn_lat_kernel(q_ref, kc_ref, vc_ref, kl_ref, vl_ref, o_ref):
    _attn_heads(q_ref, [(kc_ref, vc_ref), (kl_ref, vl_ref)], o_ref)


def _attn_ctx(q, k, v, seq):
    n = q.shape[0]
    blk = lambda w: pl.BlockSpec((seq, w), lambda b: (b, 0))
    return pl.pallas_call(
        _attn_ctx_kernel,
        grid=(n // seq,),
        in_specs=[blk(QK_PAD), blk(QK_PAD), blk(ATTN_WIDTH)],
        out_specs=blk(ATTN_WIDTH),
        out_shape=jax.ShapeDtypeStruct((n, ATTN_WIDTH), BF16),
        compiler_params=_params(("parallel",)),
        name="attn_ctx",
    )(q, k, v)


def _attn_lat(q, k, v, kc, vc, l, seq, tq):
    n = q.shape[0]
    per = seq // tq
    past = kc.shape[2]
    return pl.pallas_call(
        _attn_lat_kernel,
        grid=(n // seq, per),
        in_specs=[
            pl.BlockSpec((tq, QK_PAD), lambda b, j: (b * per + j, 0)),
            pl.BlockSpec((None, None, past, QK_PAD), lambda b, j: (l, b, 0, 0)),
            pl.BlockSpec((None, None, past, ATTN_WIDTH), lambda b, j: (l, b, 0, 0)),
            pl.BlockSpec((seq, QK_PAD), lambda b, j: (b, 0)),
            pl.BlockSpec((seq, ATTN_WIDTH), lambda b, j: (b, 0)),
        ],
        out_specs=pl.BlockSpec((tq, ATTN_WIDTH), lambda b, j: (b * per + j, 0)),
        out_shape=jax.ShapeDtypeStruct((n, ATTN_WIDTH), BF16),
        compiler_params=_params(("parallel", "arbitrary")),
        name="attn_lat",
    )(q, kc, vc, k, v)


def _cmul(ar, ai, br, bi):
    return ar * br - ai * bi, ar * bi + ai * br


def _outer_rows(pw, m):
    return (pw[:, None, :] * m[None, :, :]).reshape(CW, SSM_STATE)


def _ssm_prep_kernel(are_ref, aim_ref, ldt_ref, bre_ref, bim_ref, cre_ref, cim_ref,
                     mext_ref, et_ref, ft_ref, a16_ref):
    et_ref[...] = jnp.zeros(et_ref.shape, F32)
    ft_ref[...] = jnp.zeros(ft_ref.shape, F32)
    steps = lax.broadcasted_iota(jnp.int32, (CHUNK, SSM_STATE), 0).astype(F32)
    for gi in range(2):
        lanes = slice(gi * SSM_STATE, (gi + 1) * SSM_STATE)
        rows = slice(gi * CW, (gi + 1) * CW)
        cre = cre_ref[gi]
        cim = cim_ref[gi]
        m_dir = []
        for d in range(2):
            a_re = are_ref[d, gi]
            a_im = aim_ref[d, gi]
            dt = jnp.exp(ldt_ref[d, gi])
            mag = jnp.exp(a_re * dt)
            abr = mag * jnp.cos(a_im * dt)
            abi = mag * jnp.sin(a_im * dt)
            den = a_re * a_re + a_im * a_im
            nr = abr - 1.0
            coef_re = (nr * a_re + abi * a_im) / den
            coef_im = (abi * a_re - nr * a_im) / den
            bbr, bbi = _cmul(coef_re, coef_im, bre_ref[d, gi], bim_ref[d, gi])

            def power(nsteps):
                pm = jnp.exp(nsteps * (a_re * dt))
                return pm * jnp.cos(nsteps * (a_im * dt)), pm * jnp.sin(nsteps * (a_im * dt))

            to_end = (CHUNK - 1.0 - steps) if d == 0 else steps
            from_start = (steps + 1.0) if d == 0 else (CHUNK - steps)
            lag = steps if d == 0 else (CHUNK - 1.0 - steps)
            pr, pi = power(to_end)
            er = _outer_rows(pr, bbr) - _outer_rows(pi, bbi)
            ei = _outer_rows(pr, bbi) + _outer_rows(pi, bbr)
            et_ref[2 * d, rows, lanes] = er
            et_ref[2 * d + 1, rows, lanes] = ei
            pr, pi = power(from_start)
            ft_ref[2 * d, rows, lanes] = _outer_rows(pr, cre) - _outer_rows(pi, cim)
            ft_ref[2 * d + 1, rows, lanes] = -(_outer_rows(pr, cim) + _outer_rows(pi, cre))
            pr, pi = power(lag)
            xr = _outer_rows(pr, bbr) - _outer_rows(pi, bbi)
            xi = _outer_rows(pr, bbi) + _outer_rows(pi, bbr)
            nt = (((1,), (1,)), ((), ()))
            m_dir.append(
                lax.dot_general(xr, cre, nt, preferred_element_type=F32, precision=lax.Precision.HIGHEST)
                - lax.dot_general(xi, cim, nt, preferred_element_type=F32, precision=lax.Precision.HIGHEST))
            p16r, p16i = power(jnp.full((1, SSM_STATE), float(CHUNK), F32))
            a16_ref[2 * d:2 * d + 1, lanes] = p16r
            a16_ref[2 * d + 1:2 * d + 2, lanes] = p16i
        mf, mb = m_dir
        last = CW - SSM_GROUP
        mext_ref[gi, 0:last, :] = mb[0:last]
        mext_ref[gi, last:CW, :] = mb[last:CW] + mf[0:SSM_GROUP]
        mext_ref[gi, CW:2 * CW - SSM_GROUP, :] = mf[SSM_GROUP:CW]
        mext_ref[gi, 2 * CW - SSM_GROUP:2 * CW, :] = jnp.zeros((SSM_GROUP, SSM_GROUP), F32)


def _ssm_prep(a_re, a_im, log_dt, b_re, b_im, c_re, c_im):
    g2 = SSM_GROUPS // 2
    a_spec = pl.BlockSpec((None, 2, 2, 1, SSM_STATE), lambda l, g: (l, 0, g, 0, 0))
    b_spec = pl.BlockSpec((None, 2, 2, SSM_GROUP, SSM_STATE), lambda l, g: (l, 0, g, 0, 0))
    c_spec = pl.BlockSpec((None, 2, SSM_GROUP, SSM_STATE), lambda l, g: (l, g, 0, 0))
    out_shapes = (
        jax.ShapeDtypeStruct((DEPTH, SSM_GROUPS, 2 * CW, SSM_GROUP), F32),
        jax.ShapeDtypeStruct((DEPTH, g2, 4, 2 * CW, 2 * SSM_STATE), F32),
        jax.ShapeDtypeStruct((DEPTH, g2, 4, 2 * CW, 2 * SSM_STATE), F32),
        jax.ShapeDtypeStruct((DEPTH, g2, 4, 2 * SSM_STATE), F32),
    )
    return pl.pallas_call(
        _ssm_prep_kernel,
        grid=(DEPTH, g2),
        in_specs=[a_spec, a_spec,
                  pl.BlockSpec((None, 2, 2, 1, 1), lambda l, g: (l, 0, g, 0, 0)),
                  b_spec, b_spec, c_spec, c_spec],
        out_specs=[
            pl.BlockSpec((None, 2, 2 * CW, SSM_GROUP), lambda l, g: (l, g, 0, 0)),
            pl.BlockSpec((None, None, 4, 2 * CW, 2 * SSM_STATE), lambda l, g: (l, g, 0, 0, 0)),
            pl.BlockSpec((None, None, 4, 2 * CW, 2 * SSM_STATE), lambda l, g: (l, g, 0, 0, 0)),
            pl.BlockSpec((None, None, 4, 2 * SSM_STATE), lambda l, g: (l, g, 0, 0)),
        ],
        out_shape=out_shapes,
        compiler_params=_params(("arbitrary", "arbitrary")),
        name="ssm_prep",
    )(a_re.reshape(DEPTH, 2, SSM_GROUPS, 1, SSM_STATE), a_im.reshape(DEPTH, 2, SSM_GROUPS, 1, SSM_STATE),
      log_dt.reshape(DEPTH, 2, SSM_GROUPS, 1, 1),
      jnp.swapaxes(b_re, -1, -2), jnp.swapaxes(b_im, -1, -2), c_re, c_im)


def _toeplitz(mext):
    m5 = mext.reshape(DEPTH, SSM_GROUPS, 2 * CHUNK, SSM_GROUP, SSM_GROUP)
    j = np.arange(CHUNK)
    idx = j[None, :] - j[:, None] + (CHUNK - 1)
    t = m5[:, :, idx]
    return jnp.transpose(t, (0, 1, 2, 4, 3, 5)).reshape(DEPTH, SSM_GROUPS, CW, CW)


def _chunk_matrices(mext, et, ft):
    perm = jnp.asarray(_slot_perm())
    t = _toeplitz(mext)
    t = jnp.take_along_axis(t, perm[None, :, :, None], axis=2)
    t = jnp.take_along_axis(t, perm[None, :, None, :], axis=3)
    pair_perm = perm.reshape(SSM_GROUPS // 2, 1, 2, CW, 1)

    def shuffle(m):
        m6 = m.reshape(DEPTH, SSM_GROUPS // 2, 4, 2, CW, 2 * SSM_STATE)
        return jnp.take_along_axis(m6, pair_perm[None], axis=4).reshape(m.shape)

    return t.astype(BF16), shuffle(et).astype(BF16), shuffle(ft).astype(BF16)


def _ssm_in_kernel(u_ref, t_ref, et_ref, y_ref, e0_ref, e1_ref, e2_ref, e3_ref):
    e_refs = (e0_ref, e1_ref, e2_ref, e3_ref)
    u0 = u_ref[0]
    u1 = u_ref[1]
    y_ref[0] = jnp.dot(u0, t_ref[0], preferred_element_type=F32)
    y_ref[1] = jnp.dot(u1, t_ref[1], preferred_element_type=F32)
    for x in range(4):
        e_refs[x][...] = (jnp.dot(u0, et_ref[x, 0:CW, :], preferred_element_type=F32)
                          + jnp.dot(u1, et_ref[x, CW:2 * CW, :], preferred_element_type=F32))


def _ssm_in(ug, tmat, et, l):
    r = ug.shape[1]
    g2 = SSM_GROUPS // 2
    e_shape = jax.ShapeDtypeStruct((r, SSM_GROUPS * SSM_STATE), F32)
    e_spec = pl.BlockSpec((r, 2 * SSM_STATE), lambda g: (0, g))
    return pl.pallas_call(
        _ssm_in_kernel,
        grid=(g2,),
        in_specs=[
            pl.BlockSpec((2, r, CW), lambda g: (g, 0, 0)),
            pl.BlockSpec((None, 2, CW, CW), lambda g: (l, g, 0, 0)),
            pl.BlockSpec((None, None, 4, 2 * CW, 2 * SSM_STATE), lambda g: (l, g, 0, 0, 0)),
        ],
        out_specs=[pl.BlockSpec((2, r, CW), lambda g: (g, 0, 0)), e_spec, e_spec, e_spec, e_spec],
        out_shape=(jax.ShapeDtypeStruct((SSM_GROUPS, r, CW), F32), e_shape, e_shape, e_shape, e_shape),
        compiler_params=_params(("parallel",)),
        name="ssm_in",
    )(ug, tmat, et)


def _ssm_scan_kernel(nchunk, nb, efr_ref, efi_ref, ebr_ref, ebi_ref, a16_ref, h0fr_ref, h0fi_ref, h0br_ref,
                     h0bi_ref, hfr_ref, hfi_ref, hbr_ref, hbi_ref, fin_ref):
    afr, afi, abr, abi = (a16_ref[i:i + 1, :] for i in range(4))

    def body(i, carry):
        fr, fi, br, bi = carry
        kf = pl.ds(i, nb, stride=nchunk)
        kb = pl.ds(nchunk - 1 - i, nb, stride=nchunk)
        hfr_ref[kf, :] = fr
        hfi_ref[kf, :] = fi
        hbr_ref[kb, :] = br
        hbi_ref[kb, :] = bi
        return (afr * fr - afi * fi + efr_ref[kf, :], afr * fi + afi * fr + efi_ref[kf, :],
                abr * br - abi * bi + ebr_ref[kb, :], abr * bi + abi * br + ebi_ref[kb, :])

    fin = lax.fori_loop(0, nchunk, body, (h0fr_ref[...], h0fi_ref[...], h0br_ref[...], h0bi_ref[...]))
    for i in range(4):
        fin_ref[i] = fin[i]


def _ssm_scan(es, a16, h0s, nchunk, nb):
    r = es[0].shape[0]
    width = SSM_GROUPS * SSM_STATE
    big = pl.BlockSpec((r, LANES), lambda j: (0, j))
    small = pl.BlockSpec((nb, LANES), lambda j: (0, j))
    h_shape = jax.ShapeDtypeStruct((r, width), F32)
    return pl.pallas_call(
        functools.partial(_ssm_scan_kernel, nchunk, nb),
        grid=(width // LANES,),
        in_specs=[big, big, big, big, pl.BlockSpec((4, LANES), lambda j: (0, j)), small, small, small, small],
        out_specs=[big, big, big, big, pl.BlockSpec((4, nb, LANES), lambda j: (0, 0, j))],
        out_shape=(h_shape, h_shape, h_shape, h_shape, jax.ShapeDtypeStruct((4, nb, width), F32)),
        compiler_params=_params(("parallel",)),
        name="ssm_scan",
    )(*es, a16, *h0s)


def _ssm_out_kernel(y_ref, h0_ref, h1_ref, h2_ref, h3_ref, ft_ref, o_ref):
    nt = (((1,), (1,)), ((), ()))
    acc = None
    for x, h_ref in enumerate((h0_ref, h1_ref, h2_ref, h3_ref)):
        part = lax.dot_general(h_ref[...].astype(BF16), ft_ref[x], nt, preferred_element_type=F32)
        acc = part if acc is None else acc + part
    o_ref[0] = y_ref[0] + acc[:, :CW]
    o_ref[1] = y_ref[1] + acc[:, CW:]


def _ssm_out(y_intra, hs, ft, l):
    r = y_intra.shape[1]
    g2 = SSM_GROUPS // 2
    h_spec = pl.BlockSpec((r, 2 * SSM_STATE), lambda g: (0, g))
    y_spec = pl.BlockSpec((2, r, CW), lambda g: (g, 0, 0))
    return pl.pallas_call(
        _ssm_out_kernel,
        grid=(g2,),
        in_specs=[y_spec, h_spec, h_spec, h_spec, h_spec,
                  pl.BlockSpec((None, None, 4, 2 * CW, 2 * SSM_STATE), lambda g: (l, g, 0, 0, 0))],
        out_specs=y_spec,
        out_shape=jax.ShapeDtypeStruct((SSM_GROUPS, r, CW), F32),
        compiler_params=_params(("parallel",)),
        name="ssm_out",
    )(y_intra, *hs, ft)


def _ssm_mix(ug, nb, seq, h0_re, h0_im, prep, l):
    tmat, et, ft, a16 = prep
    nchunk = seq // CHUNK
    y_intra, *es = _ssm_in(ug, tmat, et, l)
    width = SSM_GROUPS * SSM_STATE
    h0s = tuple(h[:, d].reshape(nb, width).astype(F32) for d in range(2) for h in (h0_re, h0_im))
    a16_l = jnp.transpose(a16[l], (1, 0, 2)).reshape(4, width)
    *hs, fin = _ssm_scan(es, a16_l, h0s, nchunk, nb)
    y = _ssm_out(y_intra, hs, ft, l)
    fin = fin.reshape(2, 2, nb, SSM_GROUPS, SSM_STATE)
    return y, fin


def _gelu_tanh(x):
    return 0.5 * x * (1.0 + jnp.tanh(math.sqrt(2.0 / math.pi) * (x + 0.044715 * (x * x * x))))


def _route(aff, bias):
    a = [aff[e:e + 1, :] for e in range(N_EXPERTS)]
    s = [a[e] + bias[e:e + 1, :] for e in range(N_EXPERTS)]
    keep = []
    for e in range(N_EXPERTS):
        g0 = (e // EXPERTS_PER_GROUP) * EXPERTS_PER_GROUP
        rank = None
        for j in range(g0, g0 + EXPERTS_PER_GROUP):
            if j == e:
                continue
            beats = (s[j] >= s[e]) if j < e else (s[j] > s[e])
            r = jnp.where(beats, 1.0, 0.0)
            rank = r if rank is None else rank + r
        keep.append(rank < 1.5)
    score = []
    for g in range(N_GROUPS):
        tot = None
        for e in range(g * EXPERTS_PER_GROUP, (g + 1) * EXPERTS_PER_GROUP):
            v = jnp.where(keep[e], s[e], 0.0)
            tot = v if tot is None else tot + v
        score.append(tot)
    gates = []
    for g in range(N_GROUPS):
        lost = None
        for j in range(N_GROUPS):
            if j == g:
                continue
            beats = (score[j] >= score[g]) if j < g else (score[j] > score[g])
            r = jnp.where(beats, 1.0, 0.0)
            lost = r if lost is None else lost + r
        chosen = lost < 0.5
        w = [jnp.where(keep[e], a[e], 0.0) for e in range(g * EXPERTS_PER_GROUP, (g + 1) * EXPERTS_PER_GROUP)]
        tot = w[0] + w[1] + w[2] + w[3]
        for v in w:
            gates.append(jnp.where(chosen, v / tot, 0.0))
    return jnp.concatenate(gates, axis=0)


def _merge_kernel(x_ref, yg_ref, u_ref, at_ref, sga_ref, sgb_ref, mod_ref, d_ref, wglu_ref, bglu_ref, wsp_ref,
                  wap_ref, wo_ref, gffn_ref, wr_ref, rb_ref, xo_ref, h2_ref, gt_ref, tok_ref):
    mod = mod_ref[...]
    gate1 = mod[:, 2 * D_MODEL:3 * D_MODEL]
    shift2 = mod[:, 3 * D_MODEL:4 * D_MODEL]
    scale2 = mod[:, 4 * D_MODEL:5 * D_MODEL]
    _from_chunk_rows(yg_ref, tok_ref)
    ys = jnp.concatenate([tok_ref[j] for j in range(LANE_TILES)], axis=1)
    y = _gelu_tanh(ys + d_ref[...] * u_ref[...].astype(F32))
    glu = jnp.dot(y.astype(BF16), wglu_ref[...], preferred_element_type=F32) + bglu_ref[...]
    y = (y * _sigmoid(glu)).astype(BF16)
    a_out = jnp.dot(y, wsp_ref[...], preferred_element_type=F32)
    b_out = jnp.dot(at_ref[...], wap_ref[...], preferred_element_type=F32)
    m = (sga_ref[...].astype(F32) * a_out + sgb_ref[...].astype(F32) * b_out).astype(BF16)
    x = x_ref[...] + gate1 * jnp.dot(m, wo_ref[...], preferred_element_type=F32)
    xo_ref[...] = x
    h2 = (_rms(x, D_MODEL) * gffn_ref[...]) * (1.0 + scale2) + shift2
    h2_ref[...] = h2.astype(BF16)
    logits = lax.dot_general(wr_ref[...], h2, (((1,), (1,)), ((), ())), preferred_element_type=F32,
                             precision=lax.Precision.HIGHEST)
    gates = _route(_sigmoid(logits), rb_ref[...])
    pad = jnp.zeros((LANES - N_EXPERTS, gates.shape[1]), F32)
    gt_ref[...] = jnp.concatenate([gates, pad], axis=0).T


def _merge(x, yg, u, attn, sga, sgb, mods, l, lat, wts):
    n = x.shape[0]
    tt = TOK_TILE
    per_seq = 2048 // tt
    mod_map = (lambda i: (l, 1 + i // per_seq, 0, 0)) if lat else (lambda i: (l, 0, 0, 0))
    row = lambda i: (i, 0)
    lw = lambda *shape: pl.BlockSpec((None,) + shape, lambda i: (l,) + (0,) * len(shape))
    return pl.pallas_call(
        _merge_kernel,
        grid=(n // tt,),
        in_specs=[
            pl.BlockSpec((tt, D_MODEL), row),
            pl.BlockSpec((SSM_GROUPS, TILE_CHUNKS, CW), lambda i: (0, i, 0)),
            pl.BlockSpec((tt, SSM_WIDTH), row),
            pl.BlockSpec((tt, ATTN_WIDTH), row), pl.BlockSpec((tt, D_MODEL), row), pl.BlockSpec((tt, D_MODEL), row),
            pl.BlockSpec((None, None, 1, 6 * D_MODEL), mod_map),
            lw(1, SSM_WIDTH), lw(SSM_WIDTH, SSM_WIDTH), lw(1, SSM_WIDTH), lw(SSM_WIDTH, D_MODEL),
            lw(ATTN_WIDTH, D_MODEL), lw(D_MODEL, D_MODEL), lw(1, D_MODEL),
            pl.BlockSpec((N_EXPERTS, D_MODEL), lambda i: (0, 0)),
            pl.BlockSpec((N_EXPERTS, 1), lambda i: (0, 0)),
        ],
        out_specs=[pl.BlockSpec((tt, D_MODEL), row), pl.BlockSpec((tt, D_MODEL), row),
                   pl.BlockSpec((tt, LANES), row)],
        out_shape=(jax.ShapeDtypeStruct((n, D_MODEL), F32), jax.ShapeDtypeStruct((n, D_MODEL), BF16),
                   jax.ShapeDtypeStruct((n, LANES), F32)),
        scratch_shapes=[pltpu.VMEM((LANE_TILES, tt, LANES), F32)],
        compiler_params=_params(("parallel",)),
        name="merge_lat" if lat else "merge_ctx",
    )(x, yg, u, attn, sga, sgb, mods, wts["ssm_d"], wts["w_glu"], wts["b_glu"], wts["w_ssm_proj"],
      wts["w_attn_proj"], wts["w_o"], wts["g_ffn"], wts["w_router_t"], wts["router_bias"])


def _moe_kernel(x_ref, h_ref, gt_ref, mod_ref, wgu_ref, wd_ref, o_ref, acc_ref):
    e = pl.program_id(1)

    @pl.when(e == 0)
    def _():
        acc_ref[...] = jnp.zeros(acc_ref.shape, F32)

    gu = jnp.dot(h_ref[...], wgu_ref[...], preferred_element_type=F32)
    hg = gu[:, :D_EXPERT]
    lane = lax.broadcasted_iota(jnp.int32, gt_ref.shape, 1)
    gate = jnp.sum(jnp.where(lane == e, gt_ref[...], 0.0), axis=-1, keepdims=True)
    act = (hg * _sigmoid(hg)) * gu[:, D_EXPERT:] * gate
    acc_ref[...] += jnp.dot(act.astype(BF16), wd_ref[...], preferred_element_type=F32)

    @pl.when(e == N_EXPERTS - 1)
    def _():
        gate2 = mod_ref[:, 5 * D_MODEL:6 * D_MODEL]
        o_ref[...] = x_ref[...] + gate2 * acc_ref[...]


def _moe(x, h2, gates, mods, l, lat, wts):
    n = x.shape[0]
    tt = TOK_TILE
    per_seq = 2048 // tt
    mod_map = (lambda i, e: (l, 1 + i // per_seq, 0, 0)) if lat else (lambda i, e: (l, 0, 0, 0))
    row = lambda i, e: (i, 0)
    return pl.pallas_call(
        _moe_kernel,
        grid=(n // tt, N_EXPERTS),
        in_specs=[
            pl.BlockSpec((tt, D_MODEL), row), pl.BlockSpec((tt, D_MODEL), row), pl.BlockSpec((tt, LANES), row),
            pl.BlockSpec((None, None, 1, 6 * D_MODEL), mod_map),
            pl.BlockSpec((None, None, D_MODEL, 2 * D_EXPERT), lambda i, e: (l, e, 0, 0)),
            pl.BlockSpec((None, None, D_EXPERT, D_MODEL), lambda i, e: (l, e, 0, 0)),
        ],
        out_specs=pl.BlockSpec((tt, D_MODEL), row),
        out_shape=jax.ShapeDtypeStruct((n, D_MODEL), F32),
        scratch_shapes=[pltpu.VMEM((tt, D_MODEL), F32)],
        compiler_params=_params(("parallel", "arbitrary")),
        name="moe_lat" if lat else "moe_ctx",
    )(x, h2, gates, mods, wts["w_gu"], wts["w_down"])


def _pad_heads(w, width):
    lead = w.shape[:-1]
    w = w.reshape(lead + (N_HEADS, width))
    w = jnp.pad(w, [(0, 0)] * len(lead) + [(0, 0), (0, HEAD_PAD - width)])
    return w.reshape(lead + (N_HEADS * HEAD_PAD,))


def _rope_partner():
    half = ROPE_AXIS // 2
    p = np.arange(HEAD_PAD)
    for base in (QK_NOPE, QK_NOPE + ROPE_AXIS):
        p[base:base + half] = np.arange(base + half, base + 2 * half)
        p[base + half:base + 2 * half] = np.arange(base, base + half)
    return p


def _layout_weights(w_in, g_qa, w_qb, g_q, g_kva, w_kvb, g_k, g_mix, w_attn_proj, ssm_d, ssm_w_glu, ssm_b_glu,
                    ssm_w_proj, w_o, g_ffn, w_router, router_bias, w_gate, w_up, w_down):
    n_a = SSM_WIDTH + Q_LORA + KV_LORA + QK_ROPE
    w_a = jnp.pad(w_in[:, :, :n_a], ((0, 0), (0, 0), (0, 1024 - n_a)))
    kv = w_kvb.reshape(DEPTH, KV_LORA, N_HEADS, QK_NOPE + V_HEAD)
    k_cols = _pad_heads(kv[..., :QK_NOPE].reshape(DEPTH, KV_LORA, N_HEADS * QK_NOPE), QK_NOPE)
    v_cols = kv[..., QK_NOPE:].reshape(DEPTH, KV_LORA, ATTN_WIDTH)
    place = np.zeros((KV_LORA, QK_PAD + ATTN_WIDTH), np.float32)
    for h in range(N_HEADS):
        for r in range(QK_ROPE):
            place[r, h * HEAD_PAD + QK_NOPE + r] = 1.0
    w_k = jnp.concatenate([jnp.concatenate([k_cols, v_cols], axis=-1),
                           jnp.broadcast_to(jnp.asarray(place), (DEPTH,) + place.shape)], axis=1)
    partner = _rope_partner()
    all_partner = (np.arange(N_HEADS)[:, None] * HEAD_PAD + partner[None, :]).reshape(-1)
    w_qb_pad = _pad_heads(w_qb, QK_HEAD)
    pad_gain = lambda g: jnp.pad(g, ((0, 0), (0, HEAD_PAD - QK_HEAD))).reshape(DEPTH, 1, HEAD_PAD)
    vec = lambda g: g.reshape(DEPTH, 1, -1)
    return {
        "g_mix": vec(g_mix), "w_a": w_a.astype(BF16),
        "w_ga": w_in[:, :, n_a:n_a + D_MODEL].astype(BF16), "w_gb": w_in[:, :, n_a + D_MODEL:].astype(BF16),
        "g_qa": vec(g_qa), "w_qb": w_qb_pad.astype(BF16), "w_qb_sw": w_qb_pad[:, :, all_partner].astype(BF16),
        "g_q": pad_gain(g_q), "g_q_sw": pad_gain(g_q)[:, :, partner],
        "g_kva": vec(g_kva), "w_k": w_k.astype(BF16), "w_k_sw": w_k[:, :, all_partner].astype(BF16),
        "g_k": pad_gain(g_k), "g_k_sw": pad_gain(g_k)[:, :, partner],
        "ssm_d": vec(ssm_d), "w_glu": ssm_w_glu.astype(BF16), "b_glu": vec(ssm_b_glu),
        "w_ssm_proj": ssm_w_proj.astype(BF16), "w_attn_proj": w_attn_proj.astype(BF16), "w_o": w_o.astype(BF16),
        "g_ffn": vec(g_ffn), "w_router_t": w_router.T, "router_bias": router_bias.reshape(N_EXPERTS, 1),
        "w_gu": jnp.concatenate([w_gate, w_up], axis=-1).astype(BF16), "w_down": w_down.astype(BF16),
    }


def _rope_tables(n_tokens):
    pos = np.arange(n_tokens)
    inv = 1.0 / (ROPE_BASE ** (np.arange(ROPE_AXIS // 2, dtype=np.float32) * 2.0 / ROPE_AXIS))
    ang = np.zeros((n_tokens, HEAD_PAD), np.float32)
    half = ROPE_AXIS // 2
    row = (pos // GRID_W).astype(np.float32)[:, None] * inv
    col = (pos % GRID_W).astype(np.float32)[:, None] * inv
    first = np.zeros((HEAD_PAD,), bool)
    second = np.zeros((HEAD_PAD,), bool)
    for base, a in ((QK_NOPE, row), (QK_NOPE + ROPE_AXIS, col)):
        ang[:, base:base + half] = a
        ang[:, base + half:base + 2 * half] = a
        first[base:base + half] = True
        second[base + half:base + 2 * half] = True
    ang = jnp.asarray(ang)
    cos = jnp.where(jnp.asarray(first | second), jnp.cos(ang), 1.0)
    sin = jnp.sin(ang)
    return cos, jnp.where(jnp.asarray(first), -sin, jnp.where(jnp.asarray(second), sin, 0.0))


def kernel(x_prompt, x_sample, c, cache_ckv, cache_krope, state_ssm_re, state_ssm_im, c_ctx, w_ada, b_ada, g_mix, w_in, g_qa, w_qb, g_q, g_kva, w_kvb, g_k, w_attn_proj, ssm_a_re, ssm_a_im, ssm_log_dt, ssm_b_re, ssm_b_im, ssm_c_re, ssm_c_im, ssm_d, ssm_w_glu, ssm_b_glu, ssm_w_proj, w_o, g_ffn, w_router, router_bias, w_gate, w_up, w_down):
    nb_c, seq_c, _ = x_prompt.shape
    nb_l, seq_l, _ = x_sample.shape
    wts = _layout_weights(w_in, g_qa, w_qb, g_q, g_kva, w_kvb, g_k, g_mix, w_attn_proj, ssm_d, ssm_w_glu,
                          ssm_b_glu, ssm_w_proj, w_o, g_ffn, w_router, router_bias, w_gate, w_up, w_down)
    cvec = jnp.concatenate([c_ctx[None, :], c, jnp.zeros((8 - 1 - nb_l, D_MODEL), F32)], axis=0)
    mods = _mods(cvec, w_ada, b_ada).reshape(DEPTH, 8, 1, 6 * D_MODEL)
    mext, et, ft, a16 = _ssm_prep(ssm_a_re, ssm_a_im, ssm_log_dt, ssm_b_re, ssm_b_im, ssm_c_re, ssm_c_im)
    prep = _chunk_matrices(mext, et, ft) + (a16,)
    tabs = _rope_tables(seq_l)
    kc, vc = _cachekv(cache_ckv, cache_krope, wts)

    xp = x_prompt.reshape(nb_c * seq_c, D_MODEL)
    xs = x_sample.reshape(nb_l * seq_l, D_MODEL)
    zeros_c = jnp.zeros((nb_c, 2, SSM_GROUPS, SSM_STATE), F32)
    ckvs, kropes, fins = [], [], []
    for l in range(DEPTH):
        u, ug, q, k, v, sga, sgb, ckv, krope = _inproj(xp, mods, l, False, wts, tabs)
        ckvs.append(ckv)
        kropes.append(krope)
        yg, fin = _ssm_mix(ug, nb_c, seq_c, zeros_c, zeros_c, prep, l)
        fins.append(fin)
        attn = _attn_ctx(q, k, v, seq_c)
        xp, h2, gates = _merge(xp, yg, u, attn, sga, sgb, mods, l, False, wts)
        xp = _moe(xp, h2, gates, mods, l, False, wts)

        u, ug, q, k, v, sga, sgb, _, _ = _inproj(xs, mods, l, True, wts, tabs)
        yg, _ = _ssm_mix(ug, nb_l, seq_l, state_ssm_re[:, l], state_ssm_im[:, l], prep, l)
        attn = _attn_lat(q, k, v, kc, vc, l, seq_l, 256)
        xs, h2, gates = _merge(xs, yg, u, attn, sga, sgb, mods, l, True, wts)
        xs = _moe(xs, h2, gates, mods, l, True, wts)

    new_ckv = jnp.stack(ckvs, axis=0).reshape(DEPTH, nb_c, seq_c, KV_LORA).transpose(1, 0, 2, 3)
    new_krope = jnp.stack(kropes, axis=0).reshape(DEPTH, nb_c, seq_c, QK_ROPE).transpose(1, 0, 2, 3)
    fin = jnp.stack(fins, axis=0)
    new_re = jnp.transpose(fin[:, :, 0], (2, 0, 1, 3, 4))
    new_im = jnp.transpose(fin[:, :, 1], (2, 0, 1, 3, 4))
    return (xp.reshape(nb_c, seq_c, D_MODEL), xs.reshape(nb_l, seq_l, D_MODEL), new_ckv, new_krope, new_re, new_im)
```

```python
import functools
import math

import jax
import jax.numpy as jnp
import numpy as np
from jax import lax
from jax.experimental import pallas as pl
from jax.experimental.pallas import tpu as pltpu

F32 = jnp.float32
BF16 = jnp.bfloat16

D_MODEL = 1024
DEPTH = 4
GRID_W = 64
SSM_WIDTH = 512
SSM_GROUP = 16
SSM_GROUPS = 32
SSM_STATE = 64
N_HEADS = 8
QK_NOPE = 64
QK_ROPE = 32
QK_HEAD = 96
V_HEAD = 64
Q_LORA = 256
KV_LORA = 128
ATTN_WIDTH = 512
ROPE_AXIS = 16
ROPE_BASE = 10000.0
N_EXPERTS = 16
N_GROUPS = 4
EXPERTS_PER_GROUP = 4
D_EXPERT = 256
EPS = 1e-6

LANES = 128
HEAD_PAD = LANES
QK_PAD = N_HEADS * HEAD_PAD
CHUNK = 16
CW = CHUNK * SSM_GROUP
TOK_TILE = 512
TILE_CHUNKS = TOK_TILE // CHUNK
LANE_TILES = SSM_WIDTH // LANES
GROUPS_PER_TILE = LANES // SSM_GROUP
MOE_ROWS = 128
VMEM_LIMIT = 48 * 1024 * 1024
MOE_VMEM_LIMIT = 56 * 1024 * 1024


def _params(sem, vmem=VMEM_LIMIT):
    return pltpu.CompilerParams(dimension_semantics=sem, vmem_limit_bytes=vmem)


def _sigmoid(x):
    return 1.0 / (1.0 + jnp.exp(-x))


def _rms(x, n):
    return x * lax.rsqrt(jnp.sum(x * x, axis=-1, keepdims=True) * (1.0 / n) + EPS)


def _mods_kernel(c_ref, w_ref, b_ref, o_ref):
    c = c_ref[...]
    cs = (c * _sigmoid(c)).astype(BF16)
    o_ref[...] = jnp.dot(cs, w_ref[...].astype(BF16), preferred_element_type=F32) + b_ref[...]


def _mods(cvec, w_ada, b_ada):
    nb = 1536
    return pl.pallas_call(
        _mods_kernel,
        grid=(DEPTH, 6 * D_MODEL // nb),
        in_specs=[
            pl.BlockSpec((8, D_MODEL), lambda l, j: (0, 0)),
            pl.BlockSpec((None, D_MODEL, nb), lambda l, j: (l, 0, j)),
            pl.BlockSpec((None, 1, nb), lambda l, j: (l, 0, j)),
        ],
        out_specs=pl.BlockSpec((None, 8, nb), lambda l, j: (l, 0, j)),
        out_shape=jax.ShapeDtypeStruct((DEPTH, 8, 6 * D_MODEL), F32),
        compiler_params=_params(("arbitrary", "arbitrary")),
        name="mods",
    )(cvec, w_ada, b_ada.reshape(DEPTH, 1, 6 * D_MODEL))


def _slot_step(slot, g):
    return (slot & 8) + ((slot - g) & 7)


def _lane_segment():
    return lax.broadcasted_iota(jnp.int32, (TILE_CHUNKS, LANES), 1) // SSM_GROUP


def _to_chunk_rows(tok_ref, ug_ref):
    seg = _lane_segment()
    for j in range(LANE_TILES):
        for half in range(CHUNK // 8):
            rolled = []
            for r in range(8):
                v = tok_ref[j, pl.ds(8 * half + r, TILE_CHUNKS, stride=CHUNK), :]
                rolled.append(pltpu.roll(v, SSM_GROUP * r, 1) if r else v)
            for gg in range(GROUPS_PER_TILE):
                acc = rolled[(0 - gg) % 8]
                for m in range(1, 8):
                    acc = jnp.where(seg == m, rolled[(m - gg) % 8], acc)
                ug_ref[GROUPS_PER_TILE * j + gg, :, LANES * half:LANES * (half + 1)] = acc.astype(ug_ref.dtype)


def _from_chunk_rows(yg_ref, tok_ref):
    seg = _lane_segment()
    for j in range(LANE_TILES):
        for half in range(CHUNK // 8):
            ys = [yg_ref[GROUPS_PER_TILE * j + gg, :, LANES * half:LANES * (half + 1)]
                  for gg in range(GROUPS_PER_TILE)]
            for r in range(8):
                acc = ys[(0 - r) % 8]
                for m in range(1, 8):
                    acc = jnp.where(seg == m, ys[(m - r) % 8], acc)
                out = pltpu.roll(acc, LANES - SSM_GROUP * r, 1) if r else acc
                tok_ref[j, pl.ds(8 * half + r, TILE_CHUNKS, stride=CHUNK), :] = out


def _head_norm_rope(t, gain, swapped):
    outs = []
    for h in range(N_HEADS):
        lanes = slice(h * HEAD_PAD, (h + 1) * HEAD_PAD)
        th = t[:, lanes]
        scale = lax.rsqrt(jnp.sum(th * th, axis=-1, keepdims=True) * (1.0 / QK_HEAD) + EPS)
        y = th * scale * gain
        if swapped is not None:
            t_sw, gain_sw, cos, sin = swapped
            y = y * cos + (t_sw[:, lanes] * scale * gain_sw) * sin
        outs.append(y)
    return jnp.concatenate(outs, axis=1)


def _kv_dot(ckvn, krp, wk_ref):
    return (jnp.dot(ckvn, wk_ref[0:KV_LORA, :], preferred_element_type=F32)
            + jnp.dot(krp, wk_ref[KV_LORA:KV_LORA + krp.shape[1], :], preferred_element_type=F32))


def _inproj_kernel(use_rope, x_ref, mod_ref, gmix_ref, wa_ref, wga_ref, wgb_ref, gqa_ref, wqb_ref, wqbs_ref, gq_ref,
                   gqs_ref, gkva_ref, wk_ref, wks_ref, gk_ref, gks_ref, cos_ref, sin_ref,
                   u_ref, ug_ref, q_ref, k_ref, v_ref, sga_ref, sgb_ref, ckv_ref, kr_ref, tok_ref):
    x = x_ref[...]
    mod = mod_ref[...]
    shift1 = mod[:, 0:D_MODEL]
    scale1 = mod[:, D_MODEL:2 * D_MODEL]
    h = _rms(x, D_MODEL) * gmix_ref[...]
    h = (h * (1.0 + scale1) + shift1).astype(BF16)
    za = jnp.dot(h, wa_ref[...], preferred_element_type=F32)
    sga_ref[...] = _sigmoid(jnp.dot(h, wga_ref[...], preferred_element_type=F32)).astype(BF16)
    sgb_ref[...] = _sigmoid(jnp.dot(h, wgb_ref[...], preferred_element_type=F32)).astype(BF16)
    u_ref[...] = za[:, :SSM_WIDTH].astype(BF16)
    for j in range(LANE_TILES):
        tok_ref[j] = za[:, LANES * j:LANES * (j + 1)]
    _to_chunk_rows(tok_ref, ug_ref)
    qa = za[:, SSM_WIDTH:SSM_WIDTH + Q_LORA]
    ckv = za[:, SSM_WIDTH + Q_LORA:SSM_WIDTH + Q_LORA + KV_LORA]
    krp = za[:, SSM_WIDTH + Q_LORA + KV_LORA:]
    qn = (_rms(qa, Q_LORA) * gqa_ref[...]).astype(BF16)
    qr = jnp.dot(qn, wqb_ref[...], preferred_element_type=F32)
    ckvn = _rms(ckv, KV_LORA) * gkva_ref[...]
    ckv_ref[...] = ckvn
    kr_ref[...] = krp[:, :QK_ROPE]
    ckvn = ckvn.astype(BF16)
    krp = krp.astype(BF16)
    kv = _kv_dot(ckvn, krp, wk_ref)
    q_sw = k_sw = None
    if use_rope:
        cos = cos_ref[...]
        sin = sin_ref[...]
        q_sw = (jnp.dot(qn, wqbs_ref[...], preferred_element_type=F32), gqs_ref[...], cos, sin)
        k_sw = (_kv_dot(ckvn, krp, wks_ref), gks_ref[...], cos, sin)
    q = _head_norm_rope(qr, gq_ref[...], q_sw)
    q_ref[...] = (q * (1.0 / math.sqrt(QK_HEAD))).astype(BF16)
    k_ref[...] = _head_norm_rope(kv[:, :QK_PAD], gk_ref[...], k_sw).astype(BF16)
    v_ref[...] = kv[:, QK_PAD:].astype(BF16)


def _inproj(x, mods, l, lat, wts, tabs):
    n = x.shape[0]
    tt = TOK_TILE
    per_seq = 2048 // tt
    if lat:
        mod_map = lambda i: (l, 1 + i // per_seq, 0, 0)
        tab_map = lambda i: (i % per_seq, 0)
    else:
        mod_map = lambda i: (l, 0, 0, 0)
        tab_map = lambda i: (0, 0)
    row = lambda i: (i, 0)
    lw = lambda *shape: pl.BlockSpec((None,) + shape, lambda i: (l,) + (0,) * len(shape))
    tab = pl.BlockSpec((tt, HEAD_PAD), tab_map)
    row_shapes = (
        jax.ShapeDtypeStruct((n, SSM_WIDTH), BF16),
        jax.ShapeDtypeStruct((n, QK_PAD), BF16),
        jax.ShapeDtypeStruct((n, QK_PAD), BF16),
        jax.ShapeDtypeStruct((n, ATTN_WIDTH), BF16),
        jax.ShapeDtypeStruct((n, D_MODEL), BF16),
        jax.ShapeDtypeStruct((n, D_MODEL), BF16),
        jax.ShapeDtypeStruct((n, KV_LORA), F32),
        jax.ShapeDtypeStruct((n, QK_ROPE), F32),
    )
    row_specs = [pl.BlockSpec((tt, s.shape[1]), row) for s in row_shapes]
    ug_shape = jax.ShapeDtypeStruct((SSM_GROUPS, n // CHUNK, CW), BF16)
    ug_spec = pl.BlockSpec((SSM_GROUPS, TILE_CHUNKS, CW), lambda i: (0, i, 0))
    outs = pl.pallas_call(
        functools.partial(_inproj_kernel, lat),
        grid=(n // tt,),
        in_specs=[
            pl.BlockSpec((tt, D_MODEL), row),
            pl.BlockSpec((None, None, 1, 6 * D_MODEL), mod_map),
            lw(1, D_MODEL), lw(D_MODEL, 1024), lw(D_MODEL, D_MODEL), lw(D_MODEL, D_MODEL),
            lw(1, Q_LORA), lw(Q_LORA, QK_PAD), lw(Q_LORA, QK_PAD), lw(1, HEAD_PAD), lw(1, HEAD_PAD), lw(1, KV_LORA),
            lw(2 * KV_LORA, QK_PAD + ATTN_WIDTH), lw(2 * KV_LORA, QK_PAD), lw(1, HEAD_PAD), lw(1, HEAD_PAD),
            tab, tab,
        ],
        out_specs=row_specs[:1] + [ug_spec] + row_specs[1:],
        out_shape=row_shapes[:1] + (ug_shape,) + row_shapes[1:],
        scratch_shapes=[pltpu.VMEM((LANE_TILES, tt, LANES), F32)],
        compiler_params=_params(("parallel",)),
        name="inproj_lat" if lat else "inproj_ctx",
    )(x, mods, wts["g_mix"], wts["w_a"], wts["w_ga"], wts["w_gb"], wts["g_qa"], wts["w_qb"], wts["w_qb_sw"],
      wts["g_q"], wts["g_q_sw"], wts["g_kva"], wts["w_k"], wts["w_k_sw"], wts["g_k"], wts["g_k_sw"], *tabs)
    return outs


def _cachekv_kernel(ckv_ref, kr_ref, wk_ref, gk_ref, k_ref, v_ref):
    kv = _kv_dot(ckv_ref[...].astype(BF16), kr_ref[...].astype(BF16), wk_ref)
    k_ref[...] = _head_norm_rope(kv[:, :QK_PAD], gk_ref[...], None).astype(BF16)
    v_ref[...] = kv[:, QK_PAD:].astype(BF16)


def _cachekv(cache_ckv, cache_krope, wts):
    nb, _, past, _ = cache_ckv.shape
    return pl.pallas_call(
        _cachekv_kernel,
        grid=(DEPTH, nb),
        in_specs=[
            pl.BlockSpec((None, None, past, KV_LORA), lambda l, b: (b, l, 0, 0)),
            pl.BlockSpec((None, None, past, QK_ROPE), lambda l, b: (b, l, 0, 0)),
            pl.BlockSpec((None, 2 * KV_LORA, QK_PAD + ATTN_WIDTH), lambda l, b: (l, 0, 0)),
            pl.BlockSpec((None, 1, HEAD_PAD), lambda l, b: (l, 0, 0)),
        ],
        out_specs=[
            pl.BlockSpec((None, None, past, QK_PAD), lambda l, b: (l, b, 0, 0)),
            pl.BlockSpec((None, None, past, ATTN_WIDTH), lambda l, b: (l, b, 0, 0)),
        ],
        out_shape=(jax.ShapeDtypeStruct((DEPTH, nb, past, QK_PAD), BF16),
                   jax.ShapeDtypeStruct((DEPTH, nb, past, ATTN_WIDTH), BF16)),
        compiler_params=_params(("arbitrary", "arbitrary")),
        name="cache_kv",
    )(cache_ckv, cache_krope, wts["w_k"], wts["g_k"])


def _attn_heads(q_ref, segs, o_ref):
    for h in range(N_HEADS):
        qh = q_ref[:, h * HEAD_PAD:(h + 1) * HEAD_PAD]
        scores = [lax.dot_general(qh, k_ref[:, h * HEAD_PAD:(h + 1) * HEAD_PAD], (((1,), (1,)), ((), ())),
                                  preferred_element_type=F32) for k_ref, _ in segs]
        m = scores[0].max(axis=-1, keepdims=True)
        for s in scores[1:]:
            m = jnp.maximum(m, s.max(axis=-1, keepdims=True))
        den = None
        acc = None
        for s, (_, v_ref) in zip(scores, segs):
            p = jnp.exp(s - m)
            d = p.sum(axis=-1, keepdims=True)
            a = jnp.dot(p.astype(BF16), v_ref[:, h * V_HEAD:(h + 1) * V_HEAD], preferred_element_type=F32)
            den = d if den is None else den + d
            acc = a if acc is None else acc + a
        o_ref[:, h * V_HEAD:(h + 1) * V_HEAD] = (acc / den).astype(o_ref.dtype)


def _attn_ctx_kernel(q_ref, k_ref, v_ref, o_ref):
    _attn_heads(q_ref, [(k_ref, v_ref)], o_ref)


def _attn_lat_kernel(q_ref, kc_ref, vc_ref, kl_ref, vl_ref, o_ref):
    _attn_heads(q_ref, [(kc_ref, vc_ref), (kl_ref, vl_ref)], o_ref)


def _attn_ctx(q, k, v, seq):
    n = q.shape[0]
    blk = lambda w: pl.BlockSpec((seq, w), lambda b: (b, 0))
    return pl.pallas_call(
        _attn_ctx_kernel,
        grid=(n // seq,),
        in_specs=[blk(QK_PAD), blk(QK_PAD), blk(ATTN_WIDTH)],
        out_specs=blk(ATTN_WIDTH),
        out_shape=jax.ShapeDtypeStruct((n, ATTN_WIDTH), BF16),
        compiler_params=_params(("parallel",)),
        name="attn_ctx",
    )(q, k, v)


def _attn_lat(q, k, v, kc, vc, l, seq, tq):
    n = q.shape[0]
    per = seq // tq
    past = kc.shape[2]
    return pl.pallas_call(
        _attn_lat_kernel,
        grid=(n // seq, per),
        in_specs=[
            pl.BlockSpec((tq, QK_PAD), lambda b, j: (b * per + j, 0)),
            pl.BlockSpec((None, None, past, QK_PAD), lambda b, j: (l, b, 0, 0)),
            pl.BlockSpec((None, None, past, ATTN_WIDTH), lambda b, j: (l, b, 0, 0)),
            pl.BlockSpec((seq, QK_PAD), lambda b, j: (b, 0)),
            pl.BlockSpec((seq, ATTN_WIDTH), lambda b, j: (b, 0)),
        ],
        out_specs=pl.BlockSpec((tq, ATTN_WIDTH), lambda b, j: (b * per + j, 0)),
        out_shape=jax.ShapeDtypeStruct((n, ATTN_WIDTH), BF16),
        compiler_params=_params(("parallel", "arbitrary")),
        name="attn_lat",
    )(q, kc, vc, k, v)


def _cmul(ar, ai, br, bi):
    return ar * br - ai * bi, ar * bi + ai * br


def _outer_rows(pw, m):
    return (pw[:, None, :] * m[None, :, :]).reshape(CW, SSM_STATE)


def _ssm_prep_kernel(are_ref, aim_ref, ldt_ref, bre_ref, bim_ref, cre_ref, cim_ref,
                     mext_ref, et_ref, ft_ref, a16_ref):
    et_ref[...] = jnp.zeros(et_ref.shape, F32)
    ft_ref[...] = jnp.zeros(ft_ref.shape, F32)
    row_id = lax.broadcasted_iota(jnp.int32, (CHUNK, SSM_STATE), 0)
    lags = row_id.astype(F32)
    for gi in range(2):
        steps = _slot_step(row_id, 2 * pl.program_id(1) + gi).astype(F32)
        lanes = slice(gi * SSM_STATE, (gi + 1) * SSM_STATE)
        rows = slice(gi * CW, (gi + 1) * CW)
        cre = cre_ref[gi]
        cim = cim_ref[gi]
        m_dir = []
        for d in range(2):
            a_re = are_ref[d, gi]
            a_im = aim_ref[d, gi]
            dt = jnp.exp(ldt_ref[d, gi])
            mag = jnp.exp(a_re * dt)
            abr = mag * jnp.cos(a_im * dt)
            abi = mag * jnp.sin(a_im * dt)
            den = a_re * a_re + a_im * a_im
            nr = abr - 1.0
            coef_re = (nr * a_re + abi * a_im) / den
            coef_im = (abi * a_re - nr * a_im) / den
            bbr, bbi = _cmul(coef_re, coef_im, bre_ref[d, gi], bim_ref[d, gi])

            def power(nsteps):
                pm = jnp.exp(nsteps * (a_re * dt))
                return pm * jnp.cos(nsteps * (a_im * dt)), pm * jnp.sin(nsteps * (a_im * dt))

            to_end = (CHUNK - 1.0 - steps) if d == 0 else steps
            from_start = (steps + 1.0) if d == 0 else (CHUNK - steps)
            lag = lags if d == 0 else (CHUNK - 1.0 - lags)
            pr, pi = power(to_end)
            er = _outer_rows(pr, bbr) - _outer_rows(pi, bbi)
            ei = _outer_rows(pr, bbi) + _outer_rows(pi, bbr)
            et_ref[2 * d, rows, lanes] = er
            et_ref[2 * d + 1, rows, lanes] = ei
            pr, pi = power(from_start)
            ft_ref[2 * d, rows, lanes] = _outer_rows(pr, cre) - _outer_rows(pi, cim)
            ft_ref[2 * d + 1, rows, lanes] = -(_outer_rows(pr, cim) + _outer_rows(pi, cre))
            pr, pi = power(lag)
            xr = _outer_rows(pr, bbr) - _outer_rows(pi, bbi)
            xi = _outer_rows(pr, bbi) + _outer_rows(pi, bbr)
            nt = (((1,), (1,)), ((), ()))
            m_dir.append(
                lax.dot_general(xr, cre, nt, preferred_element_type=F32, precision=lax.Precision.HIGHEST)
                - lax.dot_general(xi, cim, nt, preferred_element_type=F32, precision=lax.Precision.HIGHEST))
            p16r, p16i = power(jnp.full((1, SSM_STATE), float(CHUNK), F32))
            a16_ref[2 * d:2 * d + 1, lanes] = p16r
            a16_ref[2 * d + 1:2 * d + 2, lanes] = p16i
        mf, mb = m_dir
        last = CW - SSM_GROUP
        mext_ref[gi, 0:last, :] = mb[0:last]
        mext_ref[gi, last:CW, :] = mb[last:CW] + mf[0:SSM_GROUP]
        mext_ref[gi, CW:2 * CW - SSM_GROUP, :] = mf[SSM_GROUP:CW]
        mext_ref[gi, 2 * CW - SSM_GROUP:2 * CW, :] = jnp.zeros((SSM_GROUP, SSM_GROUP), F32)


def _ssm_prep(a_re, a_im, log_dt, b_re, b_im, c_re, c_im):
    g2 = SSM_GROUPS // 2
    a_spec = pl.BlockSpec((None, 2, 2, 1, SSM_STATE), lambda l, g: (l, 0, g, 0, 0))
    b_spec = pl.BlockSpec((None, 2, 2, SSM_GROUP, SSM_STATE), lambda l, g: (l, 0, g, 0, 0))
    c_spec = pl.BlockSpec((None, 2, SSM_GROUP, SSM_STATE), lambda l, g: (l, g, 0, 0))
    out_shapes = (
        jax.ShapeDtypeStruct((DEPTH, SSM_GROUPS, 2 * CW, SSM_GROUP), F32),
        jax.ShapeDtypeStruct((DEPTH, g2, 4, 2 * CW, 2 * SSM_STATE), F32),
        jax.ShapeDtypeStruct((DEPTH, g2, 4, 2 * CW, 2 * SSM_STATE), F32),
        jax.ShapeDtypeStruct((DEPTH, g2, 4, 2 * SSM_STATE), F32),
    )
    return pl.pallas_call(
        _ssm_prep_kernel,
        grid=(DEPTH, g2),
        in_specs=[a_spec, a_spec,
                  pl.BlockSpec((None, 2, 2, 1, 1), lambda l, g: (l, 0, g, 0, 0)),
                  b_spec, b_spec, c_spec, c_spec],
        out_specs=[
            pl.BlockSpec((None, 2, 2 * CW, SSM_GROUP), lambda l, g: (l, g, 0, 0)),
            pl.BlockSpec((None, None, 4, 2 * CW, 2 * SSM_STATE), lambda l, g: (l, g, 0, 0, 0)),
            pl.BlockSpec((None, None, 4, 2 * CW, 2 * SSM_STATE), lambda l, g: (l, g, 0, 0, 0)),
            pl.BlockSpec((None, None, 4, 2 * SSM_STATE), lambda l, g: (l, g, 0, 0)),
        ],
        out_shape=out_shapes,
        compiler_params=_params(("arbitrary", "arbitrary")),
        name="ssm_prep",
    )(a_re.reshape(DEPTH, 2, SSM_GROUPS, 1, SSM_STATE), a_im.reshape(DEPTH, 2, SSM_GROUPS, 1, SSM_STATE),
      log_dt.reshape(DEPTH, 2, SSM_GROUPS, 1, 1),
      jnp.swapaxes(b_re, -1, -2), jnp.swapaxes(b_im, -1, -2), c_re, c_im)


def _toeplitz(mext):
    m5 = mext.reshape(DEPTH, SSM_GROUPS, 2 * CHUNK, SSM_GROUP, SSM_GROUP)
    g = np.arange(SSM_GROUPS)[:, None, None]
    slot = np.arange(CHUNK)
    idx = _slot_step(slot[None, None, :], g) - _slot_step(slot[None, :, None], g) + (CHUNK - 1)
    t = m5[:, g, idx]
    return jnp.transpose(t, (0, 1, 2, 4, 3, 5)).reshape(DEPTH, SSM_GROUPS, CW, CW)


def _ssm_in_kernel(u_ref, t_ref, et_ref, y_ref, e0_ref, e1_ref, e2_ref, e3_ref):
    e_refs = (e0_ref, e1_ref, e2_ref, e3_ref)
    u0 = u_ref[0]
    u1 = u_ref[1]
    y_ref[0] = jnp.dot(u0, t_ref[0], preferred_element_type=F32)
    y_ref[1] = jnp.dot(u1, t_ref[1], preferred_element_type=F32)
    for x in range(4):
        e_refs[x][...] = (jnp.dot(u0, et_ref[x, 0:CW, :], preferred_element_type=F32)
                          + jnp.dot(u1, et_ref[x, CW:2 * CW, :], preferred_element_type=F32))


def _ssm_in(ug, tmat, et, l):
    r = ug.shape[1]
    g2 = SSM_GROUPS // 2
    e_shape = jax.ShapeDtypeStruct((r, SSM_GROUPS * SSM_STATE), F32)
    e_spec = pl.BlockSpec((r, 2 * SSM_STATE), lambda g: (0, g))
    return pl.pallas_call(
        _ssm_in_kernel,
        grid=(g2,),
        in_specs=[
            pl.BlockSpec((2, r, CW), lambda g: (g, 0, 0)),
            pl.BlockSpec((None, 2, CW, CW), lambda g: (l, g, 0, 0)),
            pl.BlockSpec((None, None, 4, 2 * CW, 2 * SSM_STATE), lambda g: (l, g, 0, 0, 0)),
        ],
        out_specs=[pl.BlockSpec((2, r, CW), lambda g: (g, 0, 0)), e_spec, e_spec, e_spec, e_spec],
        out_shape=(jax.ShapeDtypeStruct((SSM_GROUPS, r, CW), F32), e_shape, e_shape, e_shape, e_shape),
        compiler_params=_params(("parallel",)),
        name="ssm_in",
    )(ug, tmat, et)


def _ssm_scan_kernel(nchunk, nb, efr_ref, efi_ref, ebr_ref, ebi_ref, a16_ref, h0fr_ref, h0fi_ref, h0br_ref,
                     h0bi_ref, hfr_ref, hfi_ref, hbr_ref, hbi_ref, fin_ref):
    afr, afi, abr, abi = (a16_ref[i:i + 1, :] for i in range(4))

    def body(i, carry):
        fr, fi, br, bi = carry
        kf = pl.ds(i, nb, stride=nchunk)
        kb = pl.ds(nchunk - 1 - i, nb, stride=nchunk)
        hfr_ref[kf, :] = fr
        hfi_ref[kf, :] = fi
        hbr_ref[kb, :] = br
        hbi_ref[kb, :] = bi
        return (afr * fr - afi * fi + efr_ref[kf, :], afr * fi + afi * fr + efi_ref[kf, :],
                abr * br - abi * bi + ebr_ref[kb, :], abr * bi + abi * br + ebi_ref[kb, :])

    fin = lax.fori_loop(0, nchunk, body, (h0fr_ref[...], h0fi_ref[...], h0br_ref[...], h0bi_ref[...]))
    for i in range(4):
        fin_ref[i] = fin[i]


def _ssm_scan(es, a16, h0s, nchunk, nb):
    r = es[0].shape[0]
    width = SSM_GROUPS * SSM_STATE
    big = pl.BlockSpec((r, LANES), lambda j: (0, j))
    small = pl.BlockSpec((nb, LANES), lambda j: (0, j))
    h_shape = jax.ShapeDtypeStruct((r, width), F32)
    return pl.pallas_call(
        functools.partial(_ssm_scan_kernel, nchunk, nb),
        grid=(width // LANES,),
        in_specs=[big, big, big, big, pl.BlockSpec((4, LANES), lambda j: (0, j)), small, small, small, small],
        out_specs=[big, big, big, big, pl.BlockSpec((4, nb, LANES), lambda j: (0, 0, j))],
        out_shape=(h_shape, h_shape, h_shape, h_shape, jax.ShapeDtypeStruct((4, nb, width), F32)),
        compiler_params=_params(("parallel",)),
        name="ssm_scan",
    )(*es, a16, *h0s)


def _ssm_out_kernel(y_ref, h0_ref, h1_ref, h2_ref, h3_ref, ft_ref, o_ref):
    nt = (((1,), (1,)), ((), ()))
    acc = None
    for x, h_ref in enumerate((h0_ref, h1_ref, h2_ref, h3_ref)):
        part = lax.dot_general(h_ref[...].astype(BF16), ft_ref[x], nt, preferred_element_type=F32)
        acc = part if acc is None else acc + part
    o_ref[0] = y_ref[0] + acc[:, :CW]
    o_ref[1] = y_ref[1] + acc[:, CW:]


def _ssm_out(y_intra, hs, ft, l):
    r = y_intra.shape[1]
    g2 = SSM_GROUPS // 2
    h_spec = pl.BlockSpec((r, 2 * SSM_STATE), lambda g: (0, g))
    y_spec = pl.BlockSpec((2, r, CW), lambda g: (g, 0, 0))
    return pl.pallas_call(
        _ssm_out_kernel,
        grid=(g2,),
        in_specs=[y_spec, h_spec, h_spec, h_spec, h_spec,
                  pl.BlockSpec((None, None, 4, 2 * CW, 2 * SSM_STATE), lambda g: (l, g, 0, 0, 0))],
        out_specs=y_spec,
        out_shape=jax.ShapeDtypeStruct((SSM_GROUPS, r, CW), F32),
        compiler_params=_params(("parallel",)),
        name="ssm_out",
    )(y_intra, *hs, ft)


def _ssm_mix(ug, nb, seq, h0_re, h0_im, prep, l):
    tmat, et, ft, a16 = prep
    nchunk = seq // CHUNK
    y_intra, *es = _ssm_in(ug, tmat, et, l)
    width = SSM_GROUPS * SSM_STATE
    h0s = tuple(h[:, d].reshape(nb, width).astype(F32) for d in range(2) for h in (h0_re, h0_im))
    a16_l = jnp.transpose(a16[l], (1, 0, 2)).reshape(4, width)
    *hs, fin = _ssm_scan(es, a16_l, h0s, nchunk, nb)
    y = _ssm_out(y_intra, hs, ft, l)
    fin = fin.reshape(2, 2, nb, SSM_GROUPS, SSM_STATE)
    return y, fin


def _gelu_tanh(x):
    return 0.5 * x * (1.0 + jnp.tanh(math.sqrt(2.0 / math.pi) * (x + 0.044715 * (x * x * x))))


def _route(aff, bias):
    a = [aff[e:e + 1, :] for e in range(N_EXPERTS)]
    s = [a[e] + bias[e:e + 1, :] for e in range(N_EXPERTS)]
    keep = []
    for e in range(N_EXPERTS):
        g0 = (e // EXPERTS_PER_GROUP) * EXPERTS_PER_GROUP
        rank = None
        for j in range(g0, g0 + EXPERTS_PER_GROUP):
            if j == e:
                continue
            beats = (s[j] >= s[e]) if j < e else (s[j] > s[e])
            r = jnp.where(beats, 1.0, 0.0)
            rank = r if rank is None else rank + r
        keep.append(rank < 1.5)
    score = []
    for g in range(N_GROUPS):
        tot = None
        for e in range(g * EXPERTS_PER_GROUP, (g + 1) * EXPERTS_PER_GROUP):
            v = jnp.where(keep[e], s[e], 0.0)
            tot = v if tot is None else tot + v
        score.append(tot)
    gates = []
    picked = []
    for g in range(N_GROUPS):
        lost = None
        for j in range(N_GROUPS):
            if j == g:
                continue
            beats = (score[j] >= score[g]) if j < g else (score[j] > score[g])
            r = jnp.where(beats, 1.0, 0.0)
            lost = r if lost is None else lost + r
        chosen = lost < 0.5
        picked.append(jnp.where(chosen, 1.0, 0.0))
        w = [jnp.where(keep[e], a[e], 0.0) for e in range(g * EXPERTS_PER_GROUP, (g + 1) * EXPERTS_PER_GROUP)]
        tot = w[0] + w[1] + w[2] + w[3]
        for v in w:
            gates.append(jnp.where(chosen, v / tot, 0.0))
    return gates, picked


def _merge_kernel(x_ref, yg_ref, u_ref, at_ref, sga_ref, sgb_ref, mod_ref, d_ref, wglu_ref, bglu_ref, wsp_ref,
                  wap_ref, wo_ref, gffn_ref, wr_ref, rb_ref, xo_ref, hs_ref, mt_ref, ms_ref, of_ref, tok_ref):
    mod = mod_ref[...]
    gate1 = mod[:, 2 * D_MODEL:3 * D_MODEL]
    shift2 = mod[:, 3 * D_MODEL:4 * D_MODEL]
    scale2 = mod[:, 4 * D_MODEL:5 * D_MODEL]
    _from_chunk_rows(yg_ref, tok_ref)
    ys = jnp.concatenate([tok_ref[j] for j in range(LANE_TILES)], axis=1)
    y = _gelu_tanh(ys + d_ref[...] * u_ref[...].astype(F32))
    glu = jnp.dot(y.astype(BF16), wglu_ref[...], preferred_element_type=F32) + bglu_ref[...]
    y = (y * _sigmoid(glu)).astype(BF16)
    a_out = jnp.dot(y, wsp_ref[...], preferred_element_type=F32)
    b_out = jnp.dot(at_ref[...], wap_ref[...], preferred_element_type=F32)
    m = (sga_ref[...].astype(F32) * a_out + sgb_ref[...].astype(F32) * b_out).astype(BF16)
    x = x_ref[...] + gate1 * jnp.dot(m, wo_ref[...], preferred_element_type=F32)
    xo_ref[...] = x
    h2 = (_rms(x, D_MODEL) * gffn_ref[...]) * (1.0 + scale2) + shift2
    logits = lax.dot_general(wr_ref[...], h2, (((1,), (1,)), ((), ())), preferred_element_type=F32,
                             precision=lax.Precision.HIGHEST)
    gates, picked = _route(_sigmoid(logits), rb_ref[...])
    n = x.shape[0]
    onehot = jnp.concatenate(picked + [jnp.zeros((8 - N_GROUPS, n), F32)], axis=0)
    earlier = jnp.where(lax.broadcasted_iota(jnp.int32, (n, n), 0) < lax.broadcasted_iota(jnp.int32, (n, n), 1),
                        1.0, 0.0).astype(BF16)
    before = jnp.dot(onehot.astype(BF16), earlier, preferred_element_type=F32)
    count = [jnp.sum(p, axis=-1, keepdims=True) for p in picked]
    start = [jnp.zeros((1, 1), F32)]
    for g in range(N_GROUPS - 1):
        start.append(start[-1] + count[g])
    pos = sum(picked[g] * (start[g] + before[g:g + 1, :]) for g in range(N_GROUPS))
    gid = sum(float(g) * picked[g] for g in range(1, N_GROUPS))
    perm = jnp.where(lax.broadcasted_iota(jnp.int32, (n, n), 0).astype(F32) == pos, 1.0, 0.0).astype(BF16)
    hs_ref[...] = jnp.dot(perm, h2.astype(BF16), preferred_element_type=F32).astype(BF16)
    own = [sum(gates[EXPERTS_PER_GROUP * g + e] for g in range(N_GROUPS)) for e in range(EXPERTS_PER_GROUP)]
    meta = jnp.concatenate(own + [gid, pos, jnp.zeros((LANES - EXPERTS_PER_GROUP - 2, n), F32)], axis=0).T
    mt_ref[...] = meta
    hi = meta.astype(BF16)
    lo = (meta - hi.astype(F32)).astype(BF16)
    ms_ref[...] = (jnp.dot(perm, hi, preferred_element_type=F32)
                   + jnp.dot(perm, lo, preferred_element_type=F32))
    lane = lax.broadcasted_iota(jnp.int32, (8, LANES), 1)
    offs = sum(jnp.where(lane == g, start[g], 0.0) for g in range(1, N_GROUPS))
    of_ref[...] = offs.astype(jnp.int32)


def _merge(x, yg, u, attn, sga, sgb, mods, l, lat, wts):
    n = x.shape[0]
    tt = TOK_TILE
    per_seq = 2048 // tt
    mod_map = (lambda i: (l, 1 + i // per_seq, 0, 0)) if lat else (lambda i: (l, 0, 0, 0))
    row = lambda i: (i, 0)
    lw = lambda *shape: pl.BlockSpec((None,) + shape, lambda i: (l,) + (0,) * len(shape))
    return pl.pallas_call(
        _merge_kernel,
        grid=(n // tt,),
        in_specs=[
            pl.BlockSpec((tt, D_MODEL), row),
            pl.BlockSpec((SSM_GROUPS, TILE_CHUNKS, CW), lambda i: (0, i, 0)),
            pl.BlockSpec((tt, SSM_WIDTH), row),
            pl.BlockSpec((tt, ATTN_WIDTH), row), pl.BlockSpec((tt, D_MODEL), row), pl.BlockSpec((tt, D_MODEL), row),
            pl.BlockSpec((None, None, 1, 6 * D_MODEL), mod_map),
            lw(1, SSM_WIDTH), lw(SSM_WIDTH, SSM_WIDTH), lw(1, SSM_WIDTH), lw(SSM_WIDTH, D_MODEL),
            lw(ATTN_WIDTH, D_MODEL), lw(D_MODEL, D_MODEL), lw(1, D_MODEL),
            pl.BlockSpec((N_EXPERTS, D_MODEL), lambda i: (0, 0)),
            pl.BlockSpec((N_EXPERTS, 1), lambda i: (0, 0)),
        ],
        out_specs=[pl.BlockSpec((tt, D_MODEL), row), pl.BlockSpec((tt, D_MODEL), row),
                   pl.BlockSpec((tt, LANES), row), pl.BlockSpec((tt, LANES), row),
                   pl.BlockSpec((None, 8, LANES), lambda i: (i, 0, 0))],
        out_shape=(jax.ShapeDtypeStruct((n, D_MODEL), F32), jax.ShapeDtypeStruct((n, D_MODEL), BF16),
                   jax.ShapeDtypeStruct((n, LANES), F32), jax.ShapeDtypeStruct((n, LANES), F32),
                   jax.ShapeDtypeStruct((n // tt, 8, LANES), jnp.int32)),
        scratch_shapes=[pltpu.VMEM((LANE_TILES, tt, LANES), F32)],
        compiler_params=_params(("parallel",)),
        name="merge_lat" if lat else "merge_ctx",
    )(x, yg, u, attn, sga, sgb, mods, wts["ssm_d"], wts["w_glu"], wts["b_glu"], wts["w_ssm_proj"],
      wts["w_attn_proj"], wts["w_o"], wts["g_ffn"], wts["w_router_t"], wts["router_bias"])


def _moe_kernel(offs_ref, x_ref, hs_ref, ms_ref, mt_ref, mod_ref, wgu_ref, wd_ref, o_ref, acc_ref):
    tile = pl.program_id(0)
    starts = [offs_ref[N_GROUPS * tile + g] for g in range(1, N_GROUPS)]
    width = EXPERTS_PER_GROUP * D_EXPERT

    def group_of(r):
        return sum((s <= r).astype(jnp.int32) for s in starts)

    for c in range(TOK_TILE // MOE_ROWS):
        rows = slice(c * MOE_ROWS, (c + 1) * MOE_ROWS)
        acc_ref[rows, :] = jnp.zeros((MOE_ROWS, D_MODEL), F32)

        def body(g, carry):
            meta = ms_ref[rows, :]
            own = jnp.where(meta[:, EXPERTS_PER_GROUP:EXPERTS_PER_GROUP + 1] == g.astype(F32), 1.0, 0.0)
            gu = jnp.dot(hs_ref[rows, :], wgu_ref[g], preferred_element_type=F32)
            hg = gu[:, :width]
            gate = jnp.concatenate([jnp.broadcast_to(meta[:, e:e + 1] * own, (MOE_ROWS, D_EXPERT))
                                    for e in range(EXPERTS_PER_GROUP)], axis=1)
            act = (hg * _sigmoid(hg)) * gu[:, width:] * gate
            acc_ref[rows, :] += jnp.dot(act.astype(BF16), wd_ref[g], preferred_element_type=F32)
            return carry

        lax.fori_loop(group_of(c * MOE_ROWS), group_of((c + 1) * MOE_ROWS - 1) + 1, body, 0)

    slot = lax.broadcasted_iota(jnp.int32, (TOK_TILE, TOK_TILE), 1).astype(F32)
    unsort = jnp.where(mt_ref[:, EXPERTS_PER_GROUP + 1:EXPERTS_PER_GROUP + 2] == slot, 1.0, 0.0).astype(BF16)
    y = jnp.dot(unsort, acc_ref[...].astype(BF16), preferred_element_type=F32)
    o_ref[...] = x_ref[...] + mod_ref[:, 5 * D_MODEL:6 * D_MODEL] * y


def _moe(x, hs, meta_tok, meta_sorted, offs, mods, l, lat, wts):
    n = x.shape[0]
    tt = TOK_TILE
    per_seq = 2048 // tt
    mod_map = (lambda i, o: (l, 1 + i // per_seq, 0, 0)) if lat else (lambda i, o: (l, 0, 0, 0))
    row = lambda i, o: (i, 0)
    width = EXPERTS_PER_GROUP * D_EXPERT
    resident = lambda *shape: pl.BlockSpec((None,) + shape, lambda i, o: (l,) + (0,) * len(shape),
                                           pipeline_mode=pl.Buffered(1))
    grid_spec = pltpu.PrefetchScalarGridSpec(
        num_scalar_prefetch=1,
        grid=(n // tt,),
        in_specs=[
            pl.BlockSpec((tt, D_MODEL), row), pl.BlockSpec((tt, D_MODEL), row), pl.BlockSpec((tt, LANES), row),
            pl.BlockSpec((tt, LANES), row),
            pl.BlockSpec((None, None, 1, 6 * D_MODEL), mod_map),
            resident(N_GROUPS, D_MODEL, 2 * width), resident(N_GROUPS, width, D_MODEL),
        ],
        out_specs=pl.BlockSpec((tt, D_MODEL), row),
        scratch_shapes=[pltpu.VMEM((tt, D_MODEL), F32)],
    )
    return pl.pallas_call(
        _moe_kernel,
        grid_spec=grid_spec,
        out_shape=jax.ShapeDtypeStruct((n, D_MODEL), F32),
        compiler_params=_params(("arbitrary",), MOE_VMEM_LIMIT),
        name="moe_lat" if lat else "moe_ctx",
    )(offs[:, 0, :N_GROUPS].reshape(-1), x, hs, meta_sorted, meta_tok, mods, wts["w_gu"], wts["w_down"])


def _pad_heads(w, width):
    lead = w.shape[:-1]
    w = w.reshape(lead + (N_HEADS, width))
    w = jnp.pad(w, [(0, 0)] * len(lead) + [(0, 0), (0, HEAD_PAD - width)])
    return w.reshape(lead + (N_HEADS * HEAD_PAD,))


def _swap_partners(w):
    lead = w.shape[:-1]
    w = w.reshape(lead + (-1, HEAD_PAD))
    half = ROPE_AXIS // 2
    a, b = QK_NOPE, QK_NOPE + ROPE_AXIS
    parts = [w[..., :a], w[..., a + half:b], w[..., a:a + half], w[..., b + half:b + 2 * half], w[..., b:b + half],
             w[..., b + 2 * half:]]
    return jnp.concatenate(parts, axis=-1).reshape(lead + (-1,))


def _layout_weights(w_in, g_qa, w_qb, g_q, g_kva, w_kvb, g_k, g_mix, w_attn_proj, ssm_d, ssm_w_glu, ssm_b_glu,
                    ssm_w_proj, w_o, g_ffn, w_router, router_bias, w_gate, w_up, w_down):
    n_a = SSM_WIDTH + Q_LORA + KV_LORA + QK_ROPE
    w_a = jnp.pad(w_in[:, :, :n_a], ((0, 0), (0, 0), (0, 1024 - n_a)))
    kv = w_kvb.reshape(DEPTH, KV_LORA, N_HEADS, QK_NOPE + V_HEAD)
    k_cols = _pad_heads(kv[..., :QK_NOPE].reshape(DEPTH, KV_LORA, N_HEADS * QK_NOPE), QK_NOPE)
    v_cols = kv[..., QK_NOPE:].reshape(DEPTH, KV_LORA, ATTN_WIDTH)
    place = np.zeros((KV_LORA, QK_PAD + ATTN_WIDTH), np.float32)
    for h in range(N_HEADS):
        for r in range(QK_ROPE):
            place[r, h * HEAD_PAD + QK_NOPE + r] = 1.0
    w_k = jnp.concatenate([jnp.concatenate([k_cols, v_cols], axis=-1),
                           jnp.broadcast_to(jnp.asarray(place), (DEPTH,) + place.shape)], axis=1)
    w_qb_pad = _pad_heads(w_qb, QK_HEAD).astype(BF16)
    w_k = w_k.astype(BF16)
    to_groups = lambda w: jnp.transpose(w.astype(BF16).reshape(DEPTH, N_GROUPS, EXPERTS_PER_GROUP, D_MODEL, D_EXPERT),
                                        (0, 1, 3, 2, 4)).reshape(DEPTH, N_GROUPS, D_MODEL, EXPERTS_PER_GROUP * D_EXPERT)
    pad_gain = lambda g: jnp.pad(g, ((0, 0), (0, HEAD_PAD - QK_HEAD))).reshape(DEPTH, 1, HEAD_PAD)
    vec = lambda g: g.reshape(DEPTH, 1, -1)
    return {
        "g_mix": vec(g_mix), "w_a": w_a.astype(BF16),
        "w_ga": w_in[:, :, n_a:n_a + D_MODEL].astype(BF16), "w_gb": w_in[:, :, n_a + D_MODEL:].astype(BF16),
        "g_qa": vec(g_qa), "w_qb": w_qb_pad, "w_qb_sw": _swap_partners(w_qb_pad),
        "g_q": pad_gain(g_q), "g_q_sw": _swap_partners(pad_gain(g_q)),
        "g_kva": vec(g_kva), "w_k": w_k, "w_k_sw": _swap_partners(w_k[:, :, :QK_PAD]),
        "g_k": pad_gain(g_k), "g_k_sw": _swap_partners(pad_gain(g_k)),
        "ssm_d": vec(ssm_d), "w_glu": ssm_w_glu.astype(BF16), "b_glu": vec(ssm_b_glu),
        "w_ssm_proj": ssm_w_proj.astype(BF16), "w_attn_proj": w_attn_proj.astype(BF16), "w_o": w_o.astype(BF16),
        "g_ffn": vec(g_ffn), "w_router_t": w_router.T, "router_bias": router_bias.reshape(N_EXPERTS, 1),
        "w_gu": jnp.concatenate([to_groups(w_gate), to_groups(w_up)], axis=-1),
        "w_down": w_down.astype(BF16).reshape(DEPTH, N_GROUPS, EXPERTS_PER_GROUP * D_EXPERT, D_MODEL),
    }


def _rope_tables(n_tokens):
    pos = np.arange(n_tokens)
    inv = 1.0 / (ROPE_BASE ** (np.arange(ROPE_AXIS // 2, dtype=np.float32) * 2.0 / ROPE_AXIS))
    ang = np.zeros((n_tokens, HEAD_PAD), np.float32)
    half = ROPE_AXIS // 2
    row = (pos // GRID_W).astype(np.float32)[:, None] * inv
    col = (pos % GRID_W).astype(np.float32)[:, None] * inv
    first = np.zeros((HEAD_PAD,), bool)
    second = np.zeros((HEAD_PAD,), bool)
    for base, a in ((QK_NOPE, row), (QK_NOPE + ROPE_AXIS, col)):
        ang[:, base:base + half] = a
        ang[:, base + half:base + 2 * half] = a
        first[base:base + half] = True
        second[base + half:base + 2 * half] = True
    ang = jnp.asarray(ang)
    cos = jnp.where(jnp.asarray(first | second), jnp.cos(ang), 1.0)
    sin = jnp.sin(ang)
    return cos, jnp.where(jnp.asarray(first), -sin, jnp.where(jnp.asarray(second), sin, 0.0))


def kernel(x_prompt, x_sample, c, cache_ckv, cache_krope, state_ssm_re, state_ssm_im, c_ctx, w_ada, b_ada, g_mix, w_in, g_qa, w_qb, g_q, g_kva, w_kvb, g_k, w_attn_proj, ssm_a_re, ssm_a_im, ssm_log_dt, ssm_b_re, ssm_b_im, ssm_c_re, ssm_c_im, ssm_d, ssm_w_glu, ssm_b_glu, ssm_w_proj, w_o, g_ffn, w_router, router_bias, w_gate, w_up, w_down):
    nb_c, seq_c, _ = x_prompt.shape
    nb_l, seq_l, _ = x_sample.shape
    wts = _layout_weights(w_in, g_qa, w_qb, g_q, g_kva, w_kvb, g_k, g_mix, w_attn_proj, ssm_d, ssm_w_glu,
                          ssm_b_glu, ssm_w_proj, w_o, g_ffn, w_router, router_bias, w_gate, w_up, w_down)
    cvec = jnp.concatenate([c_ctx[None, :], c, jnp.zeros((8 - 1 - nb_l, D_MODEL), F32)], axis=0)
    mods = _mods(cvec, w_ada, b_ada).reshape(DEPTH, 8, 1, 6 * D_MODEL)
    mext, et, ft, a16 = _ssm_prep(ssm_a_re, ssm_a_im, ssm_log_dt, ssm_b_re, ssm_b_im, ssm_c_re, ssm_c_im)
    prep = (_toeplitz(mext).astype(BF16), et.astype(BF16), ft.astype(BF16), a16)
    tabs = _rope_tables(seq_l)
    kc, vc = _cachekv(cache_ckv, cache_krope, wts)

    xp = x_prompt.reshape(nb_c * seq_c, D_MODEL)
    xs = x_sample.reshape(nb_l * seq_l, D_MODEL)
    zeros_c = jnp.zeros((nb_c, 2, SSM_GROUPS, SSM_STATE), F32)
    ckvs, kropes, fins = [], [], []
    for l in range(DEPTH):
        u, ug, q, k, v, sga, sgb, ckv, krope = _inproj(xp, mods, l, False, wts, tabs)
        ckvs.append(ckv)
        kropes.append(krope)
        yg, fin = _ssm_mix(ug, nb_c, seq_c, zeros_c, zeros_c, prep, l)
        fins.append(fin)
        attn = _attn_ctx(q, k, v, seq_c)
        xp, *routed = _merge(xp, yg, u, attn, sga, sgb, mods, l, False, wts)
        xp = _moe(xp, *routed, mods, l, False, wts)

        u, ug, q, k, v, sga, sgb, _, _ = _inproj(xs, mods, l, True, wts, tabs)
        yg, _ = _ssm_mix(ug, nb_l, seq_l, state_ssm_re[:, l], state_ssm_im[:, l], prep, l)
        attn = _attn_lat(q, k, v, kc, vc, l, seq_l, 256)
        xs, *routed = _merge(xs, yg, u, attn, sga, sgb, mods, l, True, wts)
        xs = _moe(xs, *routed, mods, l, True, wts)

    new_ckv = jnp.stack(ckvs, axis=0).reshape(DEPTH, nb_c, seq_c, KV_LORA).transpose(1, 0, 2, 3)
    new_krope = jnp.stack(kropes, axis=0).reshape(DEPTH, nb_c, seq_c, QK_ROPE).transpose(1, 0, 2, 3)
    fin = jnp.stack(fins, axis=0)
    new_re = jnp.transpose(fin[:, :, 0], (2, 0, 1, 3, 4))
    new_im = jnp.transpose(fin[:, :, 1], (2, 0, 1, 3, 4))
    return (xp.reshape(nb_c, seq_c, D_MODEL), xs.reshape(nb_l, seq_l, D_MODEL), new_ckv, new_krope, new_re, new_im)
```

```python
import functools
import math

import jax
import jax.numpy as jnp
import numpy as np
from jax import lax
from jax.experimental import pallas as pl
from jax.experimental.pallas import tpu as pltpu

F32 = jnp.float32
BF16 = jnp.bfloat16

D_MODEL = 1024
DEPTH = 4
GRID_W = 64
SSM_WIDTH = 512
SSM_GROUP = 16
SSM_GROUPS = 32
SSM_STATE = 64
N_HEADS = 8
QK_NOPE = 64
QK_ROPE = 32
QK_HEAD = 96
V_HEAD = 64
Q_LORA = 256
KV_LORA = 128
ATTN_WIDTH = 512
ROPE_AXIS = 16
ROPE_BASE = 10000.0
N_EXPERTS = 16
N_GROUPS = 4
EXPERTS_PER_GROUP = 4
D_EXPERT = 256
EPS = 1e-6

LANES = 128
HEAD_PAD = LANES
QK_PAD = N_HEADS * HEAD_PAD
CHUNK = 16
CW = CHUNK * SSM_GROUP
TOK_TILE = 512
TILE_CHUNKS = TOK_TILE // CHUNK
LANE_TILES = SSM_WIDTH // LANES
GROUPS_PER_TILE = LANES // SSM_GROUP
Q_SCALE = math.log2(math.e) / math.sqrt(QK_HEAD)
MOE_ROWS = 128
VMEM_LIMIT = 48 * 1024 * 1024
MOE_VMEM_LIMIT = 56 * 1024 * 1024


def _params(sem, vmem=VMEM_LIMIT):
    return pltpu.CompilerParams(dimension_semantics=sem, vmem_limit_bytes=vmem)


def _sigmoid(x):
    return 1.0 / (1.0 + jnp.exp(-x))


def _rms(x, n):
    return x * lax.rsqrt(jnp.sum(x * x, axis=-1, keepdims=True) * (1.0 / n) + EPS)


def _mods_kernel(c_ref, w_ref, b_ref, o_ref):
    c = c_ref[...]
    cs = (c * _sigmoid(c)).astype(BF16)
    o_ref[...] = jnp.dot(cs, w_ref[...].astype(BF16), preferred_element_type=F32) + b_ref[...]


def _mods(cvec, w_ada, b_ada):
    nb = 1536
    return pl.pallas_call(
        _mods_kernel,
        grid=(DEPTH, 6 * D_MODEL // nb),
        in_specs=[
            pl.BlockSpec((8, D_MODEL), lambda l, j: (0, 0)),
            pl.BlockSpec((None, D_MODEL, nb), lambda l, j: (l, 0, j)),
            pl.BlockSpec((None, 1, nb), lambda l, j: (l, 0, j)),
        ],
        out_specs=pl.BlockSpec((None, 8, nb), lambda l, j: (l, 0, j)),
        out_shape=jax.ShapeDtypeStruct((DEPTH, 8, 6 * D_MODEL), F32),
        compiler_params=_params(("arbitrary", "arbitrary")),
        name="mods",
    )(cvec, w_ada, b_ada.reshape(DEPTH, 1, 6 * D_MODEL))


def _slot_step(slot, g):
    return (slot & 8) + ((slot - g) & 7)


def _lane_segment():
    return lax.broadcasted_iota(jnp.int32, (TILE_CHUNKS, LANES), 1) // SSM_GROUP


def _to_chunk_rows(tok_ref, ug_ref):
    seg = _lane_segment()
    for j in range(LANE_TILES):
        for half in range(CHUNK // 8):
            rolled = []
            for r in range(8):
                v = tok_ref[j, pl.ds(8 * half + r, TILE_CHUNKS, stride=CHUNK), :]
                rolled.append(pltpu.roll(v, SSM_GROUP * r, 1) if r else v)
            for gg in range(GROUPS_PER_TILE):
                acc = rolled[(0 - gg) % 8]
                for m in range(1, 8):
                    acc = jnp.where(seg == m, rolled[(m - gg) % 8], acc)
                ug_ref[GROUPS_PER_TILE * j + gg, :, LANES * half:LANES * (half + 1)] = acc.astype(ug_ref.dtype)


def _from_chunk_rows(yg_ref, tok_ref):
    seg = _lane_segment()
    for j in range(LANE_TILES):
        for half in range(CHUNK // 8):
            ys = [yg_ref[GROUPS_PER_TILE * j + gg, :, LANES * half:LANES * (half + 1)]
                  for gg in range(GROUPS_PER_TILE)]
            for r in range(8):
                acc = ys[(0 - r) % 8]
                for m in range(1, 8):
                    acc = jnp.where(seg == m, ys[(m - r) % 8], acc)
                out = pltpu.roll(acc, LANES - SSM_GROUP * r, 1) if r else acc
                tok_ref[j, pl.ds(8 * half + r, TILE_CHUNKS, stride=CHUNK), :] = out


def _head_norm_rope(t, gain, swapped):
    outs = []
    for h in range(N_HEADS):
        lanes = slice(h * HEAD_PAD, (h + 1) * HEAD_PAD)
        th = t[:, lanes]
        scale = lax.rsqrt(jnp.sum(th * th, axis=-1, keepdims=True) * (1.0 / QK_HEAD) + EPS)
        y = th * gain
        if swapped is not None:
            y = y + swapped[0][:, lanes] * swapped[1]
        outs.append(y * scale)
    return jnp.concatenate(outs, axis=1)


def _kv_dot(ckvn, krp, wk_ref):
    return (jnp.dot(ckvn, wk_ref[0:KV_LORA, :], preferred_element_type=F32)
            + jnp.dot(krp, wk_ref[KV_LORA:KV_LORA + krp.shape[1], :], preferred_element_type=F32))


def _inproj_kernel(use_rope, x_ref, mod_ref, gmix_ref, wa_ref, wga_ref, wgb_ref, gqa_ref, wqb_ref, wqbs_ref, gq_ref,
                   gqs_ref, gkva_ref, wk_ref, wks_ref, gk_ref, gks_ref, cos_ref, sin_ref,
                   u_ref, ug_ref, q_ref, k_ref, v_ref, sga_ref, sgb_ref, ckv_ref, kr_ref, tok_ref):
    x = x_ref[...]
    mod = mod_ref[...]
    shift1 = mod[:, 0:D_MODEL]
    scale1 = mod[:, D_MODEL:2 * D_MODEL]
    h = _rms(x, D_MODEL) * gmix_ref[...]
    h = (h * (1.0 + scale1) + shift1).astype(BF16)
    za = jnp.dot(h, wa_ref[...], preferred_element_type=F32)
    sga_ref[...] = _sigmoid(jnp.dot(h, wga_ref[...], preferred_element_type=F32)).astype(BF16)
    sgb_ref[...] = _sigmoid(jnp.dot(h, wgb_ref[...], preferred_element_type=F32)).astype(BF16)
    u_ref[...] = za[:, :SSM_WIDTH].astype(BF16)
    for j in range(LANE_TILES):
        tok_ref[j] = za[:, LANES * j:LANES * (j + 1)]
    _to_chunk_rows(tok_ref, ug_ref)
    qa = za[:, SSM_WIDTH:SSM_WIDTH + Q_LORA]
    ckv = za[:, SSM_WIDTH + Q_LORA:SSM_WIDTH + Q_LORA + KV_LORA]
    krp = za[:, SSM_WIDTH + Q_LORA + KV_LORA:]
    qn = (_rms(qa, Q_LORA) * gqa_ref[...]).astype(BF16)
    qr = jnp.dot(qn, wqb_ref[...], preferred_element_type=F32)
    ckvn = _rms(ckv, KV_LORA) * gkva_ref[...]
    ckv_ref[...] = ckvn
    kr_ref[...] = krp[:, :QK_ROPE]
    ckvn = ckvn.astype(BF16)
    krp = krp.astype(BF16)
    kv = _kv_dot(ckvn, krp, wk_ref)
    gq = gq_ref[...]
    gk = gk_ref[...]
    q_sw = k_sw = None
    if use_rope:
        cos = cos_ref[...]
        sin = sin_ref[...]
        q_sw = (jnp.dot(qn, wqbs_ref[...], preferred_element_type=F32), gqs_ref[...] * sin)
        k_sw = (_kv_dot(ckvn, krp, wks_ref), gks_ref[...] * sin)
        gq = gq * cos
        gk = gk * cos
    q_ref[...] = _head_norm_rope(qr, gq, q_sw).astype(BF16)
    k_ref[...] = _head_norm_rope(kv[:, :QK_PAD], gk, k_sw).astype(BF16)
    v_ref[...] = kv[:, QK_PAD:].astype(BF16)


def _inproj(x, mods, l, lat, wts, tabs):
    n = x.shape[0]
    tt = TOK_TILE
    per_seq = 2048 // tt
    if lat:
        mod_map = lambda i: (l, 1 + i // per_seq, 0, 0)
        tab_map = lambda i: (i % per_seq, 0)
    else:
        mod_map = lambda i: (l, 0, 0, 0)
        tab_map = lambda i: (0, 0)
    row = lambda i: (i, 0)
    lw = lambda *shape: pl.BlockSpec((None,) + shape, lambda i: (l,) + (0,) * len(shape))
    tab = pl.BlockSpec((tt, HEAD_PAD), tab_map)
    row_shapes = (
        jax.ShapeDtypeStruct((n, SSM_WIDTH), BF16),
        jax.ShapeDtypeStruct((n, QK_PAD), BF16),
        jax.ShapeDtypeStruct((n, QK_PAD), BF16),
        jax.ShapeDtypeStruct((n, ATTN_WIDTH), BF16),
        jax.ShapeDtypeStruct((n, D_MODEL), BF16),
        jax.ShapeDtypeStruct((n, D_MODEL), BF16),
        jax.ShapeDtypeStruct((n, KV_LORA), F32),
        jax.ShapeDtypeStruct((n, QK_ROPE), F32),
    )
    row_specs = [pl.BlockSpec((tt, s.shape[1]), row) for s in row_shapes]
    ug_shape = jax.ShapeDtypeStruct((SSM_GROUPS, n // CHUNK, CW), BF16)
    ug_spec = pl.BlockSpec((SSM_GROUPS, TILE_CHUNKS, CW), lambda i: (0, i, 0))
    outs = pl.pallas_call(
        functools.partial(_inproj_kernel, lat),
        grid=(n // tt,),
        in_specs=[
            pl.BlockSpec((tt, D_MODEL), row),
            pl.BlockSpec((None, None, 1, 6 * D_MODEL), mod_map),
            lw(1, D_MODEL), lw(D_MODEL, 1024), lw(D_MODEL, D_MODEL), lw(D_MODEL, D_MODEL),
            lw(1, Q_LORA), lw(Q_LORA, QK_PAD), lw(Q_LORA, QK_PAD), lw(1, HEAD_PAD), lw(1, HEAD_PAD), lw(1, KV_LORA),
            lw(2 * KV_LORA, QK_PAD + ATTN_WIDTH), lw(2 * KV_LORA, QK_PAD), lw(1, HEAD_PAD), lw(1, HEAD_PAD),
            tab, tab,
        ],
        out_specs=row_specs[:1] + [ug_spec] + row_specs[1:],
        out_shape=row_shapes[:1] + (ug_shape,) + row_shapes[1:],
        scratch_shapes=[pltpu.VMEM((LANE_TILES, tt, LANES), F32)],
        compiler_params=_params(("parallel",)),
        name="inproj_lat" if lat else "inproj_ctx",
    )(x, mods, wts["g_mix"], wts["w_a"], wts["w_ga"], wts["w_gb"], wts["g_qa"], wts["w_qb"], wts["w_qb_sw"],
      wts["g_q"], wts["g_q_sw"], wts["g_kva"], wts["w_k"], wts["w_k_sw"], wts["g_k"], wts["g_k_sw"], *tabs)
    return outs


def _cachekv_kernel(ckv_ref, kr_ref, wk_ref, gk_ref, k_ref, v_ref):
    kv = _kv_dot(ckv_ref[...].astype(BF16), kr_ref[...].astype(BF16), wk_ref)
    k_ref[...] = _head_norm_rope(kv[:, :QK_PAD], gk_ref[...], None).astype(BF16)
    v_ref[...] = kv[:, QK_PAD:].astype(BF16)


def _cachekv(cache_ckv, cache_krope, wts):
    nb, _, past, _ = cache_ckv.shape
    return pl.pallas_call(
        _cachekv_kernel,
        grid=(DEPTH, nb),
        in_specs=[
            pl.BlockSpec((None, None, past, KV_LORA), lambda l, b: (b, l, 0, 0)),
            pl.BlockSpec((None, None, past, QK_ROPE), lambda l, b: (b, l, 0, 0)),
            pl.BlockSpec((None, 2 * KV_LORA, QK_PAD + ATTN_WIDTH), lambda l, b: (l, 0, 0)),
            pl.BlockSpec((None, 1, HEAD_PAD), lambda l, b: (l, 0, 0)),
        ],
        out_specs=[
            pl.BlockSpec((None, None, past, QK_PAD), lambda l, b: (l, b, 0, 0)),
            pl.BlockSpec((None, None, past, ATTN_WIDTH), lambda l, b: (l, b, 0, 0)),
        ],
        out_shape=(jax.ShapeDtypeStruct((DEPTH, nb, past, QK_PAD), BF16),
                   jax.ShapeDtypeStruct((DEPTH, nb, past, ATTN_WIDTH), BF16)),
        compiler_params=_params(("arbitrary", "arbitrary")),
        name="cache_kv",
    )(cache_ckv, cache_krope, wts["w_k"], wts["g_k"])


def _attn_heads(q_ref, segs, o_ref):
    for h in range(N_HEADS):
        qh = q_ref[:, h * HEAD_PAD:(h + 1) * HEAD_PAD]
        scores = [lax.dot_general(qh, k_ref[:, h * HEAD_PAD:(h + 1) * HEAD_PAD], (((1,), (1,)), ((), ())),
                                  preferred_element_type=F32) for k_ref, _ in segs]
        m = scores[0].max(axis=-1, keepdims=True)
        for s in scores[1:]:
            m = jnp.maximum(m, s.max(axis=-1, keepdims=True))
        den = None
        acc = None
        for s, (_, v_ref) in zip(scores, segs):
            p = jnp.exp2(s - m)
            d = p.sum(axis=-1, keepdims=True)
            a = jnp.dot(p.astype(BF16), v_ref[:, h * V_HEAD:(h + 1) * V_HEAD], preferred_element_type=F32)
            den = d if den is None else den + d
            acc = a if acc is None else acc + a
        o_ref[:, h * V_HEAD:(h + 1) * V_HEAD] = (acc / den).astype(o_ref.dtype)


def _attn_ctx_kernel(q_ref, k_ref, v_ref, o_ref):
    _attn_heads(q_ref, [(k_ref, v_ref)], o_ref)


def _attn_lat_kernel(q_ref, kc_ref, vc_ref, kl_ref, vl_ref, o_ref):
    _attn_heads(q_ref, [(kc_ref, vc_ref), (kl_ref, vl_ref)], o_ref)


def _attn_ctx(q, k, v, seq):
    n = q.shape[0]
    blk = lambda w: pl.BlockSpec((seq, w), lambda b: (b, 0))
    return pl.pallas_call(
        _attn_ctx_kernel,
        grid=(n // seq,),
        in_specs=[blk(QK_PAD), blk(QK_PAD), blk(ATTN_WIDTH)],
        out_specs=blk(ATTN_WIDTH),
        out_shape=jax.ShapeDtypeStruct((n, ATTN_WIDTH), BF16),
        compiler_params=_params(("parallel",)),
        name="attn_ctx",
    )(q, k, v)


def _attn_lat(q, k, v, kc, vc, l, seq, tq):
    n = q.shape[0]
    per = seq // tq
    past = kc.shape[2]
    return pl.pallas_call(
        _attn_lat_kernel,
        grid=(n // seq, per),
        in_specs=[
            pl.BlockSpec((tq, QK_PAD), lambda b, j: (b * per + j, 0)),
            pl.BlockSpec((None, None, past, QK_PAD), lambda b, j: (l, b, 0, 0)),
            pl.BlockSpec((None, None, past, ATTN_WIDTH), lambda b, j: (l, b, 0, 0)),
            pl.BlockSpec((seq, QK_PAD), lambda b, j: (b, 0)),
            pl.BlockSpec((seq, ATTN_WIDTH), lambda b, j: (b, 0)),
        ],
        out_specs=pl.BlockSpec((tq, ATTN_WIDTH), lambda b, j: (b * per + j, 0)),
        out_shape=jax.ShapeDtypeStruct((n, ATTN_WIDTH), BF16),
        compiler_params=_params(("parallel", "arbitrary")),
        name="attn_lat",
    )(q, kc, vc, k, v)


def _cmul(ar, ai, br, bi):
    return ar * br - ai * bi, ar * bi + ai * br


def _outer_rows(pw, m):
    return (pw[:, None, :] * m[None, :, :]).reshape(CW, SSM_STATE)


def _ssm_prep_kernel(are_ref, aim_ref, ldt_ref, bre_ref, bim_ref, cre_ref, cim_ref,
                     t_ref, et_ref, ft_ref, a16_ref):
    et_ref[...] = jnp.zeros(et_ref.shape, et_ref.dtype)
    ft_ref[...] = jnp.zeros(ft_ref.shape, ft_ref.dtype)
    row_id = lax.broadcasted_iota(jnp.int32, (CHUNK, SSM_STATE), 0)
    lags = row_id.astype(F32)
    strip_lane = lax.broadcasted_iota(jnp.int32, (SSM_GROUP, CW), 1)
    nt = (((1,), (1,)), ((), ()))
    for gi in range(2):
        g = 2 * pl.program_id(1) + gi
        steps = _slot_step(row_id, g).astype(F32)
        lanes = slice(gi * SSM_STATE, (gi + 1) * SSM_STATE)
        rows = slice(gi * CW, (gi + 1) * CW)
        cre = cre_ref[gi]
        cim = cim_ref[gi]
        m_dir = []
        for d in range(2):
            a_re = are_ref[d, gi]
            a_im = aim_ref[d, gi]
            dt = jnp.exp(ldt_ref[d, gi])
            mag = jnp.exp(a_re * dt)
            abr = mag * jnp.cos(a_im * dt)
            abi = mag * jnp.sin(a_im * dt)
            den = a_re * a_re + a_im * a_im
            nr = abr - 1.0
            coef_re = (nr * a_re + abi * a_im) / den
            coef_im = (abi * a_re - nr * a_im) / den
            bbr, bbi = _cmul(coef_re, coef_im, bre_ref[d, gi], bim_ref[d, gi])

            def power(nsteps):
                pm = jnp.exp(nsteps * (a_re * dt))
                return pm * jnp.cos(nsteps * (a_im * dt)), pm * jnp.sin(nsteps * (a_im * dt))

            to_end = (CHUNK - 1.0 - steps) if d == 0 else steps
            from_start = (steps + 1.0) if d == 0 else (CHUNK - steps)
            lag = lags if d == 0 else (CHUNK - 1.0 - lags)
            pr, pi = power(to_end)
            er = _outer_rows(pr, bbr) - _outer_rows(pi, bbi)
            ei = _outer_rows(pr, bbi) + _outer_rows(pi, bbr)
            et_ref[2 * d, rows, lanes] = er.astype(et_ref.dtype)
            et_ref[2 * d + 1, rows, lanes] = ei.astype(et_ref.dtype)
            pr, pi = power(from_start)
            ft_ref[2 * d, rows, lanes] = (_outer_rows(pr, cre) - _outer_rows(pi, cim)).astype(ft_ref.dtype)
            ft_ref[2 * d + 1, rows, lanes] = (-(_outer_rows(pr, cim) + _outer_rows(pi, cre))).astype(ft_ref.dtype)
            pr, pi = power(lag)
            zr = _outer_rows(pr, cre) - _outer_rows(pi, cim)
            zi = _outer_rows(pi, cre) + _outer_rows(pr, cim)
            m_dir.append(
                lax.dot_general(bbr, zr, nt, preferred_element_type=F32, precision=lax.Precision.HIGHEST)
                - lax.dot_general(bbi, zi, nt, preferred_element_type=F32, precision=lax.Precision.HIGHEST))
            p16r, p16i = power(jnp.full((1, SSM_STATE), float(CHUNK), F32))
            a16_ref[2 * d:2 * d + 1, lanes] = p16r
            a16_ref[2 * d + 1:2 * d + 2, lanes] = p16i
        mf, mb = m_dir
        edge = CW - SSM_GROUP
        low = mb + pltpu.roll(jnp.where(strip_lane < SSM_GROUP, mf, 0.0), edge, 1)
        high = jnp.where(strip_lane < edge, pltpu.roll(mf, edge, 1), 0.0)
        strip = jnp.concatenate([low, high], axis=1)
        turn = SSM_GROUP * (g & 7)
        for j in range(CHUNK):
            first = SSM_GROUP * (CHUNK - 1 - j)
            win = strip[:, first:first + CW]
            win = jnp.concatenate([pltpu.roll(win[:, LANES * h:LANES * (h + 1)], turn, 1)
                                   for h in range(CW // LANES)], axis=1)
            slot = (j & 8) + ((j + g) & 7)
            t_ref[gi, pl.ds(pl.multiple_of(SSM_GROUP * slot, SSM_GROUP), SSM_GROUP), :] = win.astype(t_ref.dtype)


def _ssm_prep(a_re, a_im, log_dt, b_re, b_im, c_re, c_im):
    g2 = SSM_GROUPS // 2
    a_spec = pl.BlockSpec((None, 2, 2, 1, SSM_STATE), lambda l, g: (l, 0, g, 0, 0))
    b_spec = pl.BlockSpec((None, 2, 2, SSM_GROUP, SSM_STATE), lambda l, g: (l, 0, g, 0, 0))
    c_spec = pl.BlockSpec((None, 2, SSM_GROUP, SSM_STATE), lambda l, g: (l, g, 0, 0))
    out_shapes = (
        jax.ShapeDtypeStruct((DEPTH, SSM_GROUPS, CW, CW), BF16),
        jax.ShapeDtypeStruct((DEPTH, g2, 4, 2 * CW, 2 * SSM_STATE), BF16),
        jax.ShapeDtypeStruct((DEPTH, g2, 4, 2 * CW, 2 * SSM_STATE), BF16),
        jax.ShapeDtypeStruct((DEPTH, g2, 4, 2 * SSM_STATE), F32),
    )
    return pl.pallas_call(
        _ssm_prep_kernel,
        grid=(DEPTH, g2),
        in_specs=[a_spec, a_spec,
                  pl.BlockSpec((None, 2, 2, 1, 1), lambda l, g: (l, 0, g, 0, 0)),
                  b_spec, b_spec, c_spec, c_spec],
        out_specs=[
            pl.BlockSpec((None, 2, CW, CW), lambda l, g: (l, g, 0, 0)),
            pl.BlockSpec((None, None, 4, 2 * CW, 2 * SSM_STATE), lambda l, g: (l, g, 0, 0, 0)),
            pl.BlockSpec((None, None, 4, 2 * CW, 2 * SSM_STATE), lambda l, g: (l, g, 0, 0, 0)),
            pl.BlockSpec((None, None, 4, 2 * SSM_STATE), lambda l, g: (l, g, 0, 0)),
        ],
        out_shape=out_shapes,
        compiler_params=_params(("arbitrary", "arbitrary")),
        name="ssm_prep",
    )(a_re.reshape(DEPTH, 2, SSM_GROUPS, 1, SSM_STATE), a_im.reshape(DEPTH, 2, SSM_GROUPS, 1, SSM_STATE),
      log_dt.reshape(DEPTH, 2, SSM_GROUPS, 1, 1),
      jnp.swapaxes(b_re, -1, -2), jnp.swapaxes(b_im, -1, -2), c_re, c_im)


def _ssm_kernel(nchunk, nb, u_ref, t_ref, et_ref, ft_ref, a16_ref, h0_ref, y_ref, fin_ref, e_ref, h_ref):
    u0 = u_ref[0]
    u1 = u_ref[1]
    for x in range(4):
        e_ref[x] = (jnp.dot(u0, et_ref[x, 0:CW, :], preferred_element_type=F32)
                    + jnp.dot(u1, et_ref[x, CW:2 * CW, :], preferred_element_type=F32))
    afr, afi, abr, abi = (a16_ref[i:i + 1, :] for i in range(4))

    def body(i, carry):
        fr, fi, br, bi = carry
        kf = pl.ds(i, nb, stride=nchunk)
        kb = pl.ds(nchunk - 1 - i, nb, stride=nchunk)
        h_ref[0, kf, :] = fr
        h_ref[1, kf, :] = fi
        h_ref[2, kb, :] = br
        h_ref[3, kb, :] = bi
        return (afr * fr - afi * fi + e_ref[0, kf, :], afr * fi + afi * fr + e_ref[1, kf, :],
                abr * br - abi * bi + e_ref[2, kb, :], abr * bi + abi * br + e_ref[3, kb, :])

    fin = lax.fori_loop(0, nchunk, body, tuple(h0_ref[i] for i in range(4)))
    for i in range(4):
        fin_ref[i] = fin[i]
    nt = (((1,), (1,)), ((), ()))
    acc = None
    for x in range(4):
        part = lax.dot_general(h_ref[x].astype(BF16), ft_ref[x], nt, preferred_element_type=F32)
        acc = part if acc is None else acc + part
    y_ref[0] = jnp.dot(u0, t_ref[0], preferred_element_type=F32) + acc[:, :CW]
    y_ref[1] = jnp.dot(u1, t_ref[1], preferred_element_type=F32) + acc[:, CW:]


def _ssm_mix(ug, nb, seq, h0_re, h0_im, prep, l):
    tmat, et, ft, a16 = prep
    nchunk = seq // CHUNK
    r = ug.shape[1]
    g2 = SSM_GROUPS // 2
    width = SSM_GROUPS * SSM_STATE
    h0 = jnp.stack([h[:, d].reshape(nb, width).astype(F32) for d in range(2) for h in (h0_re, h0_im)], axis=0)
    pair = pl.BlockSpec((2, r, CW), lambda g: (g, 0, 0))
    mats = pl.BlockSpec((None, None, 4, 2 * CW, 2 * SSM_STATE), lambda g: (l, g, 0, 0, 0))
    state = pl.BlockSpec((4, nb, 2 * SSM_STATE), lambda g: (0, 0, g))
    y, fin = pl.pallas_call(
        functools.partial(_ssm_kernel, nchunk, nb),
        grid=(g2,),
        in_specs=[pair, pl.BlockSpec((None, 2, CW, CW), lambda g: (l, g, 0, 0)), mats, mats,
                  pl.BlockSpec((None, None, 4, 2 * SSM_STATE), lambda g: (l, g, 0, 0)), state],
        out_specs=[pair, state],
        out_shape=(jax.ShapeDtypeStruct((SSM_GROUPS, r, CW), F32), jax.ShapeDtypeStruct((4, nb, width), F32)),
        scratch_shapes=[pltpu.VMEM((4, r, 2 * SSM_STATE), F32), pltpu.VMEM((4, r, 2 * SSM_STATE), F32)],
        compiler_params=_params(("parallel",)),
        name="ssm",
    )(ug, tmat, et, ft, a16, h0)
    fin = fin.reshape(2, 2, nb, SSM_GROUPS, SSM_STATE)
    return y, fin


def _gelu_tanh(x):
    return 0.5 * x * (1.0 + jnp.tanh(math.sqrt(2.0 / math.pi) * (x + 0.044715 * (x * x * x))))


def _route(aff, bias):
    a = [aff[e:e + 1, :] for e in range(N_EXPERTS)]
    s = [a[e] + bias[e:e + 1, :] for e in range(N_EXPERTS)]
    keep = []
    for e in range(N_EXPERTS):
        g0 = (e // EXPERTS_PER_GROUP) * EXPERTS_PER_GROUP
        rank = None
        for j in range(g0, g0 + EXPERTS_PER_GROUP):
            if j == e:
                continue
            beats = (s[j] >= s[e]) if j < e else (s[j] > s[e])
            r = jnp.where(beats, 1.0, 0.0)
            rank = r if rank is None else rank + r
        keep.append(rank < 1.5)
    score = []
    for g in range(N_GROUPS):
        tot = None
        for e in range(g * EXPERTS_PER_GROUP, (g + 1) * EXPERTS_PER_GROUP):
            v = jnp.where(keep[e], s[e], 0.0)
            tot = v if tot is None else tot + v
        score.append(tot)
    gates = []
    picked = []
    for g in range(N_GROUPS):
        lost = None
        for j in range(N_GROUPS):
            if j == g:
                continue
            beats = (score[j] >= score[g]) if j < g else (score[j] > score[g])
            r = jnp.where(beats, 1.0, 0.0)
            lost = r if lost is None else lost + r
        chosen = lost < 0.5
        picked.append(jnp.where(chosen, 1.0, 0.0))
        w = [jnp.where(keep[e], a[e], 0.0) for e in range(g * EXPERTS_PER_GROUP, (g + 1) * EXPERTS_PER_GROUP)]
        tot = w[0] + w[1] + w[2] + w[3]
        for v in w:
            gates.append(jnp.where(chosen, v / tot, 0.0))
    return gates, picked


def _merge_kernel(x_ref, yg_ref, u_ref, at_ref, sga_ref, sgb_ref, mod_ref, d_ref, wglu_ref, bglu_ref, wsp_ref,
                  wap_ref, wo_ref, gffn_ref, wr_ref, rb_ref, xo_ref, hs_ref, mt_ref, ms_ref, of_ref, tok_ref):
    mod = mod_ref[...]
    gate1 = mod[:, 2 * D_MODEL:3 * D_MODEL]
    shift2 = mod[:, 3 * D_MODEL:4 * D_MODEL]
    scale2 = mod[:, 4 * D_MODEL:5 * D_MODEL]
    _from_chunk_rows(yg_ref, tok_ref)
    ys = jnp.concatenate([tok_ref[j] for j in range(LANE_TILES)], axis=1)
    y = _gelu_tanh(ys + d_ref[...] * u_ref[...].astype(F32))
    glu = jnp.dot(y.astype(BF16), wglu_ref[...], preferred_element_type=F32) + bglu_ref[...]
    y = (y * _sigmoid(glu)).astype(BF16)
    a_out = jnp.dot(y, wsp_ref[...], preferred_element_type=F32)
    b_out = jnp.dot(at_ref[...], wap_ref[...], preferred_element_type=F32)
    m = (sga_ref[...].astype(F32) * a_out + sgb_ref[...].astype(F32) * b_out).astype(BF16)
    x = x_ref[...] + gate1 * jnp.dot(m, wo_ref[...], preferred_element_type=F32)
    xo_ref[...] = x
    h2 = (_rms(x, D_MODEL) * gffn_ref[...]) * (1.0 + scale2) + shift2
    logits = lax.dot_general(wr_ref[...], h2, (((1,), (1,)), ((), ())), preferred_element_type=F32,
                             precision=lax.Precision.HIGHEST)
    gates, picked = _route(_sigmoid(logits), rb_ref[...])
    n = x.shape[0]
    onehot = jnp.concatenate(picked + [jnp.zeros((8 - N_GROUPS, n), F32)], axis=0)
    earlier = jnp.where(lax.broadcasted_iota(jnp.int32, (n, n), 0) < lax.broadcasted_iota(jnp.int32, (n, n), 1),
                        1.0, 0.0).astype(BF16)
    before = jnp.dot(onehot.astype(BF16), earlier, preferred_element_type=F32)
    count = [jnp.sum(p, axis=-1, keepdims=True) for p in picked]
    start = [jnp.zeros((1, 1), F32)]
    for g in range(N_GROUPS - 1):
        start.append(start[-1] + count[g])
    pos = sum(picked[g] * (start[g] + before[g:g + 1, :]) for g in range(N_GROUPS))
    gid = sum(float(g) * picked[g] for g in range(1, N_GROUPS))
    perm = jnp.where(lax.broadcasted_iota(jnp.int32, (n, n), 0).astype(F32) == pos, 1.0, 0.0).astype(BF16)
    hs_ref[...] = jnp.dot(perm, h2.astype(BF16), preferred_element_type=F32).astype(BF16)
    own = [sum(gates[EXPERTS_PER_GROUP * g + e] for g in range(N_GROUPS)) for e in range(EXPERTS_PER_GROUP)]
    meta = jnp.concatenate(own + [gid, pos, jnp.zeros((LANES - EXPERTS_PER_GROUP - 2, n), F32)], axis=0).T
    mt_ref[...] = meta
    hi = meta.astype(BF16)
    lo = (meta - hi.astype(F32)).astype(BF16)
    ms_ref[...] = (jnp.dot(perm, hi, preferred_element_type=F32)
                   + jnp.dot(perm, lo, preferred_element_type=F32))
    lane = lax.broadcasted_iota(jnp.int32, (8, LANES), 1)
    offs = sum(jnp.where(lane == g, start[g], 0.0) for g in range(1, N_GROUPS))
    of_ref[...] = offs.astype(jnp.int32)


def _merge(x, yg, u, attn, sga, sgb, mods, l, lat, wts):
    n = x.shape[0]
    tt = TOK_TILE
    per_seq = 2048 // tt
    mod_map = (lambda i: (l, 1 + i // per_seq, 0, 0)) if lat else (lambda i: (l, 0, 0, 0))
    row = lambda i: (i, 0)
    lw = lambda *shape: pl.BlockSpec((None,) + shape, lambda i: (l,) + (0,) * len(shape))
    return pl.pallas_call(
        _merge_kernel,
        grid=(n // tt,),
        in_specs=[
            pl.BlockSpec((tt, D_MODEL), row),
            pl.BlockSpec((SSM_GROUPS, TILE_CHUNKS, CW), lambda i: (0, i, 0)),
            pl.BlockSpec((tt, SSM_WIDTH), row),
            pl.BlockSpec((tt, ATTN_WIDTH), row), pl.BlockSpec((tt, D_MODEL), row), pl.BlockSpec((tt, D_MODEL), row),
            pl.BlockSpec((None, None, 1, 6 * D_MODEL), mod_map),
            lw(1, SSM_WIDTH), lw(SSM_WIDTH, SSM_WIDTH), lw(1, SSM_WIDTH), lw(SSM_WIDTH, D_MODEL),
            lw(ATTN_WIDTH, D_MODEL), lw(D_MODEL, D_MODEL), lw(1, D_MODEL),
            pl.BlockSpec((N_EXPERTS, D_MODEL), lambda i: (0, 0)),
            pl.BlockSpec((N_EXPERTS, 1), lambda i: (0, 0)),
        ],
        out_specs=[pl.BlockSpec((tt, D_MODEL), row), pl.BlockSpec((tt, D_MODEL), row),
                   pl.BlockSpec((tt, LANES), row), pl.BlockSpec((tt, LANES), row),
                   pl.BlockSpec((None, 8, LANES), lambda i: (i, 0, 0))],
        out_shape=(jax.ShapeDtypeStruct((n, D_MODEL), F32), jax.ShapeDtypeStruct((n, D_MODEL), BF16),
                   jax.ShapeDtypeStruct((n, LANES), F32), jax.ShapeDtypeStruct((n, LANES), F32),
                   jax.ShapeDtypeStruct((n // tt, 8, LANES), jnp.int32)),
        scratch_shapes=[pltpu.VMEM((LANE_TILES, tt, LANES), F32)],
        compiler_params=_params(("parallel",)),
        name="merge_lat" if lat else "merge_ctx",
    )(x, yg, u, attn, sga, sgb, mods, wts["ssm_d"], wts["w_glu"], wts["b_glu"], wts["w_ssm_proj"],
      wts["w_attn_proj"], wts["w_o"], wts["g_ffn"], wts["w_router_t"], wts["router_bias"])


def _moe_kernel(offs_ref, x_ref, hs_ref, ms_ref, mt_ref, mod_ref, wg_ref, wu_ref, wd_ref, o_ref, acc_ref):
    tile = pl.program_id(0)
    starts = [offs_ref[N_GROUPS * tile + g] for g in range(1, N_GROUPS)]

    def group_of(r):
        return sum((s <= r).astype(jnp.int32) for s in starts)

    for c in range(TOK_TILE // MOE_ROWS):
        rows = slice(c * MOE_ROWS, (c + 1) * MOE_ROWS)
        acc_ref[rows, :] = jnp.zeros((MOE_ROWS, D_MODEL), F32)

        def body(g, carry):
            meta = ms_ref[rows, :]
            own = jnp.where(meta[:, EXPERTS_PER_GROUP:EXPERTS_PER_GROUP + 1] == g.astype(F32), 1.0, 0.0)
            hs = hs_ref[rows, :]
            out = None
            for e in range(EXPERTS_PER_GROUP):
                expert = EXPERTS_PER_GROUP * g + e
                hg = jnp.dot(hs, wg_ref[expert], preferred_element_type=F32)
                hu = jnp.dot(hs, wu_ref[expert], preferred_element_type=F32)
                act = (hg * _sigmoid(hg)) * hu * (meta[:, e:e + 1] * own)
                part = jnp.dot(act.astype(BF16), wd_ref[expert], preferred_element_type=F32)
                out = part if out is None else out + part
            acc_ref[rows, :] += out
            return carry

        lax.fori_loop(group_of(c * MOE_ROWS), group_of((c + 1) * MOE_ROWS - 1) + 1, body, 0)

    slot = lax.broadcasted_iota(jnp.int32, (TOK_TILE, TOK_TILE), 1).astype(F32)
    unsort = jnp.where(mt_ref[:, EXPERTS_PER_GROUP + 1:EXPERTS_PER_GROUP + 2] == slot, 1.0, 0.0).astype(BF16)
    y = jnp.dot(unsort, acc_ref[...].astype(BF16), preferred_element_type=F32)
    o_ref[...] = x_ref[...] + mod_ref[:, 5 * D_MODEL:6 * D_MODEL] * y


def _moe(x, hs, meta_tok, meta_sorted, offs, mods, l, lat, wts):
    n = x.shape[0]
    tt = TOK_TILE
    per_seq = 2048 // tt
    mod_map = (lambda i, o: (l, 1 + i // per_seq, 0, 0)) if lat else (lambda i, o: (l, 0, 0, 0))
    row = lambda i, o: (i, 0)
    resident = lambda *shape: pl.BlockSpec((None,) + shape, lambda i, o: (l,) + (0,) * len(shape),
                                           pipeline_mode=pl.Buffered(1))
    grid_spec = pltpu.PrefetchScalarGridSpec(
        num_scalar_prefetch=1,
        grid=(n // tt,),
        in_specs=[
            pl.BlockSpec((tt, D_MODEL), row), pl.BlockSpec((tt, D_MODEL), row), pl.BlockSpec((tt, LANES), row),
            pl.BlockSpec((tt, LANES), row),
            pl.BlockSpec((None, None, 1, 6 * D_MODEL), mod_map),
            resident(N_EXPERTS, D_MODEL, D_EXPERT), resident(N_EXPERTS, D_MODEL, D_EXPERT),
            resident(N_EXPERTS, D_EXPERT, D_MODEL),
        ],
        out_specs=pl.BlockSpec((tt, D_MODEL), row),
        scratch_shapes=[pltpu.VMEM((tt, D_MODEL), F32)],
    )
    return pl.pallas_call(
        _moe_kernel,
        grid_spec=grid_spec,
        out_shape=jax.ShapeDtypeStruct((n, D_MODEL), F32),
        compiler_params=_params(("arbitrary",), MOE_VMEM_LIMIT),
        name="moe_lat" if lat else "moe_ctx",
    )(offs[:, 0, :N_GROUPS].reshape(-1), x, hs, meta_sorted, meta_tok, mods, wts["w_gate"], wts["w_up"],
      wts["w_down"])


def _pad_heads(w, width):
    lead = w.shape[:-1]
    w = w.reshape(lead + (N_HEADS, width))
    w = jnp.pad(w, [(0, 0)] * len(lead) + [(0, 0), (0, HEAD_PAD - width)])
    return w.reshape(lead + (N_HEADS * HEAD_PAD,))


def _swap_partners(w):
    lead = w.shape[:-1]
    w = w.reshape(lead + (-1, HEAD_PAD))
    half = ROPE_AXIS // 2
    a, b = QK_NOPE, QK_NOPE + ROPE_AXIS
    parts = [w[..., :a], w[..., a + half:b], w[..., a:a + half], w[..., b + half:b + 2 * half], w[..., b:b + half],
             w[..., b + 2 * half:]]
    return jnp.concatenate(parts, axis=-1).reshape(lead + (-1,))


def _layout_weights(w_in, g_qa, w_qb, g_q, g_kva, w_kvb, g_k, g_mix, w_attn_proj, ssm_d, ssm_w_glu, ssm_b_glu,
                    ssm_w_proj, w_o, g_ffn, w_router, router_bias, w_gate, w_up, w_down):
    n_a = SSM_WIDTH + Q_LORA + KV_LORA + QK_ROPE
    w_a = jnp.pad(w_in[:, :, :n_a], ((0, 0), (0, 0), (0, 1024 - n_a)))
    kv = w_kvb.reshape(DEPTH, KV_LORA, N_HEADS, QK_NOPE + V_HEAD)
    k_cols = _pad_heads(kv[..., :QK_NOPE].reshape(DEPTH, KV_LORA, N_HEADS * QK_NOPE), QK_NOPE)
    v_cols = kv[..., QK_NOPE:].reshape(DEPTH, KV_LORA, ATTN_WIDTH)
    place = np.zeros((KV_LORA, QK_PAD + ATTN_WIDTH), np.float32)
    for h in range(N_HEADS):
        for r in range(QK_ROPE):
            place[r, h * HEAD_PAD + QK_NOPE + r] = 1.0
    w_k = jnp.concatenate([jnp.concatenate([k_cols, v_cols], axis=-1),
                           jnp.broadcast_to(jnp.asarray(place), (DEPTH,) + place.shape)], axis=1)
    w_qb_pad = _pad_heads(w_qb, QK_HEAD).astype(BF16)
    w_k = w_k.astype(BF16)
    pad_gain = lambda g: jnp.pad(g, ((0, 0), (0, HEAD_PAD - QK_HEAD))).reshape(DEPTH, 1, HEAD_PAD)
    vec = lambda g: g.reshape(DEPTH, 1, -1)
    return {
        "g_mix": vec(g_mix), "w_a": w_a.astype(BF16),
        "w_ga": w_in[:, :, n_a:n_a + D_MODEL].astype(BF16), "w_gb": w_in[:, :, n_a + D_MODEL:].astype(BF16),
        "g_qa": vec(g_qa), "w_qb": w_qb_pad, "w_qb_sw": _swap_partners(w_qb_pad),
        "g_q": pad_gain(g_q) * Q_SCALE, "g_q_sw": _swap_partners(pad_gain(g_q)) * Q_SCALE,
        "g_kva": vec(g_kva), "w_k": w_k, "w_k_sw": _swap_partners(w_k[:, :, :QK_PAD]),
        "g_k": pad_gain(g_k), "g_k_sw": _swap_partners(pad_gain(g_k)),
        "ssm_d": vec(ssm_d), "w_glu": ssm_w_glu.astype(BF16), "b_glu": vec(ssm_b_glu),
        "w_ssm_proj": ssm_w_proj.astype(BF16), "w_attn_proj": w_attn_proj.astype(BF16), "w_o": w_o.astype(BF16),
        "g_ffn": vec(g_ffn), "w_router_t": w_router.T, "router_bias": router_bias.reshape(N_EXPERTS, 1),
        "w_gate": w_gate.astype(BF16), "w_up": w_up.astype(BF16), "w_down": w_down.astype(BF16),
    }


def _rope_tables(n_tokens):
    pos = np.arange(n_tokens)
    inv = 1.0 / (ROPE_BASE ** (np.arange(ROPE_AXIS // 2, dtype=np.float32) * 2.0 / ROPE_AXIS))
    ang = np.zeros((n_tokens, HEAD_PAD), np.float32)
    half = ROPE_AXIS // 2
    row = (pos // GRID_W).astype(np.float32)[:, None] * inv
    col = (pos % GRID_W).astype(np.float32)[:, None] * inv
    first = np.zeros((HEAD_PAD,), bool)
    second = np.zeros((HEAD_PAD,), bool)
    for base, a in ((QK_NOPE, row), (QK_NOPE + ROPE_AXIS, col)):
        ang[:, base:base + half] = a
        ang[:, base + half:base + 2 * half] = a
        first[base:base + half] = True
        second[base + half:base + 2 * half] = True
    ang = jnp.asarray(ang)
    cos = jnp.where(jnp.asarray(first | second), jnp.cos(ang), 1.0)
    sin = jnp.sin(ang)
    return cos, jnp.where(jnp.asarray(first), -sin, jnp.where(jnp.asarray(second), sin, 0.0))


def kernel(x_prompt, x_sample, c, cache_ckv, cache_krope, state_ssm_re, state_ssm_im, c_ctx, w_ada, b_ada, g_mix, w_in, g_qa, w_qb, g_q, g_kva, w_kvb, g_k, w_attn_proj, ssm_a_re, ssm_a_im, ssm_log_dt, ssm_b_re, ssm_b_im, ssm_c_re, ssm_c_im, ssm_d, ssm_w_glu, ssm_b_glu, ssm_w_proj, w_o, g_ffn, w_router, router_bias, w_gate, w_up, w_down):
    nb_c, seq_c, _ = x_prompt.shape
    nb_l, seq_l, _ = x_sample.shape
    wts = _layout_weights(w_in, g_qa, w_qb, g_q, g_kva, w_kvb, g_k, g_mix, w_attn_proj, ssm_d, ssm_w_glu,
                          ssm_b_glu, ssm_w_proj, w_o, g_ffn, w_router, router_bias, w_gate, w_up, w_down)
    cvec = jnp.concatenate([c_ctx[None, :], c, jnp.zeros((8 - 1 - nb_l, D_MODEL), F32)], axis=0)
    mods = _mods(cvec, w_ada, b_ada).reshape(DEPTH, 8, 1, 6 * D_MODEL)
    prep = _ssm_prep(ssm_a_re, ssm_a_im, ssm_log_dt, ssm_b_re, ssm_b_im, ssm_c_re, ssm_c_im)
    tabs = _rope_tables(seq_l)
    kc, vc = _cachekv(cache_ckv, cache_krope, wts)

    xp = x_prompt.reshape(nb_c * seq_c, D_MODEL)
    xs = x_sample.reshape(nb_l * seq_l, D_MODEL)
    zeros_c = jnp.zeros((nb_c, 2, SSM_GROUPS, SSM_STATE), F32)
    ckvs, kropes, fins = [], [], []
    for l in range(DEPTH):
        u, ug, q, k, v, sga, sgb, ckv, krope = _inproj(xp, mods, l, False, wts, tabs)
        ckvs.append(ckv)
        kropes.append(krope)
        yg, fin = _ssm_mix(ug, nb_c, seq_c, zeros_c, zeros_c, prep, l)
        fins.append(fin)
        attn = _attn_ctx(q, k, v, seq_c)
        xp, *routed = _merge(xp, yg, u, attn, sga, sgb, mods, l, False, wts)
        xp = _moe(xp, *routed, mods, l, False, wts)

        u, ug, q, k, v, sga, sgb, _, _ = _inproj(xs, mods, l, True, wts, tabs)
        yg, _ = _ssm_mix(ug, nb_l, seq_l, state_ssm_re[:, l], state_ssm_im[:, l], prep, l)
        attn = _attn_lat(q, k, v, kc, vc, l, seq_l, 256)
        xs, *routed = _merge(xs, yg, u, attn, sga, sgb, mods, l, True, wts)
        xs = _moe(xs, *routed, mods, l, True, wts)

    new_ckv = jnp.stack(ckvs, axis=0).reshape(DEPTH, nb_c, seq_c, KV_LORA).transpose(1, 0, 2, 3)
    new_krope = jnp.stack(kropes, axis=0).reshape(DEPTH, nb_c, seq_c, QK_ROPE).transpose(1, 0, 2, 3)
    fin = jnp.stack(fins, axis=0)
    new_re = jnp.transpose(fin[:, :, 0], (2, 0, 1, 3, 4))
    new_im = jnp.transpose(fin[:, :, 1], (2, 0, 1, 3, 4))
    return (xp.reshape(nb_c, seq_c, D_MODEL), xs.reshape(nb_l, seq_l, D_MODEL), new_ckv, new_krope, new_re, new_im)
```

```python
import functools
import math

import jax
import jax.numpy as jnp
import numpy as np
from jax import lax
from jax.experimental import pallas as pl
from jax.experimental.pallas import tpu as pltpu

F32 = jnp.float32
BF16 = jnp.bfloat16

D_MODEL = 1024
DEPTH = 4
GRID_W = 64
SSM_WIDTH = 512
SSM_GROUP = 16
SSM_GROUPS = 32
SSM_STATE = 64
N_HEADS = 8
QK_NOPE = 64
QK_ROPE = 32
QK_HEAD = 96
V_HEAD = 64
Q_LORA = 256
KV_LORA = 128
ATTN_WIDTH = 512
ROPE_AXIS = 16
ROPE_BASE = 10000.0
N_EXPERTS = 16
N_GROUPS = 4
EXPERTS_PER_GROUP = 4
D_EXPERT = 256
EPS = 1e-6

LANES = 128
HEAD_PAD = LANES
QK_PAD = N_HEADS * HEAD_PAD
CHUNK = 16
CW = CHUNK * SSM_GROUP
TOK_TILE = 512
TILE_CHUNKS = TOK_TILE // CHUNK
LANE_TILES = SSM_WIDTH // LANES
GROUPS_PER_TILE = LANES // SSM_GROUP
Q_SCALE = math.log2(math.e) / math.sqrt(QK_HEAD)
MOE_ROWS = 128
VMEM_LIMIT = 48 * 1024 * 1024
MOE_VMEM_LIMIT = 56 * 1024 * 1024


def _params(sem, vmem=VMEM_LIMIT):
    return pltpu.CompilerParams(dimension_semantics=sem, vmem_limit_bytes=vmem)


def _sigmoid(x):
    return 1.0 / (1.0 + jnp.exp(-x))


def _rms(x, n):
    return x * lax.rsqrt(jnp.sum(x * x, axis=-1, keepdims=True) * (1.0 / n) + EPS)


def _mods_kernel(c_ref, w_ref, b_ref, o_ref):
    c = c_ref[...]
    cs = (c * _sigmoid(c)).astype(BF16)
    o_ref[...] = jnp.dot(cs, w_ref[...].astype(BF16), preferred_element_type=F32) + b_ref[...]


def _mods(cvec, w_ada, b_ada):
    nb = 1536
    return pl.pallas_call(
        _mods_kernel,
        grid=(DEPTH, 6 * D_MODEL // nb),
        in_specs=[
            pl.BlockSpec((8, D_MODEL), lambda l, j: (0, 0)),
            pl.BlockSpec((None, D_MODEL, nb), lambda l, j: (l, 0, j)),
            pl.BlockSpec((None, 1, nb), lambda l, j: (l, 0, j)),
        ],
        out_specs=pl.BlockSpec((None, 8, nb), lambda l, j: (l, 0, j)),
        out_shape=jax.ShapeDtypeStruct((DEPTH, 8, 6 * D_MODEL), F32),
        compiler_params=_params(("arbitrary", "arbitrary")),
        name="mods",
    )(cvec, w_ada, b_ada.reshape(DEPTH, 1, 6 * D_MODEL))


def _slot_step(slot, g):
    return (slot & 8) + ((slot - g) & 7)


def _lane_segment():
    return lax.broadcasted_iota(jnp.int32, (TILE_CHUNKS, LANES), 1) // SSM_GROUP


def _to_chunk_rows(tok_ref, ug_ref):
    seg = _lane_segment()
    for j in range(LANE_TILES):
        for half in range(CHUNK // 8):
            rolled = []
            for r in range(8):
                v = tok_ref[j, pl.ds(8 * half + r, TILE_CHUNKS, stride=CHUNK), :]
                rolled.append(pltpu.roll(v, SSM_GROUP * r, 1) if r else v)
            for gg in range(GROUPS_PER_TILE):
                acc = rolled[(0 - gg) % 8]
                for m in range(1, 8):
                    acc = jnp.where(seg == m, rolled[(m - gg) % 8], acc)
                ug_ref[GROUPS_PER_TILE * j + gg, :, LANES * half:LANES * (half + 1)] = acc.astype(ug_ref.dtype)


def _from_chunk_rows(yg_ref, tok_ref):
    seg = _lane_segment()
    for j in range(LANE_TILES):
        for half in range(CHUNK // 8):
            ys = [yg_ref[GROUPS_PER_TILE * j + gg, :, LANES * half:LANES * (half + 1)]
                  for gg in range(GROUPS_PER_TILE)]
            for r in range(8):
                acc = ys[(0 - r) % 8]
                for m in range(1, 8):
                    acc = jnp.where(seg == m, ys[(m - r) % 8], acc)
                out = pltpu.roll(acc, LANES - SSM_GROUP * r, 1) if r else acc
                tok_ref[j, pl.ds(8 * half + r, TILE_CHUNKS, stride=CHUNK), :] = out


def _head_norm_rope(t, gain, swapped):
    outs = []
    for h in range(N_HEADS):
        lanes = slice(h * HEAD_PAD, (h + 1) * HEAD_PAD)
        th = t[:, lanes]
        scale = lax.rsqrt(jnp.sum(th * th, axis=-1, keepdims=True) * (1.0 / QK_HEAD) + EPS)
        y = th * gain
        if swapped is not None:
            y = y + swapped[0][:, lanes] * swapped[1]
        outs.append(y * scale)
    return jnp.concatenate(outs, axis=1)


def _with_ones(v):
    lane = lax.broadcasted_iota(jnp.int32, (1, v.shape[1]), 1)
    return jnp.where((lane & (HEAD_PAD - 1)) == V_HEAD, 1.0, v)


def _kv_dot(ckvn, krp, wk_ref):
    return (jnp.dot(ckvn, wk_ref[0:KV_LORA, :], preferred_element_type=F32)
            + jnp.dot(krp, wk_ref[KV_LORA:KV_LORA + krp.shape[1], :], preferred_element_type=F32))


def _inproj_kernel(use_rope, x_ref, mod_ref, gmix_ref, wa_ref, wga_ref, wgb_ref, gqa_ref, wqb_ref, wqbs_ref, gq_ref,
                   gqs_ref, gkva_ref, wk_ref, wks_ref, gk_ref, gks_ref, cos_ref, sin_ref,
                   u_ref, ug_ref, q_ref, k_ref, v_ref, sga_ref, sgb_ref, ckv_ref, kr_ref, tok_ref):
    x = x_ref[...]
    mod = mod_ref[...]
    shift1 = mod[:, 0:D_MODEL]
    scale1 = mod[:, D_MODEL:2 * D_MODEL]
    h = _rms(x, D_MODEL) * gmix_ref[...]
    h = (h * (1.0 + scale1) + shift1).astype(BF16)
    za = jnp.dot(h, wa_ref[...], preferred_element_type=F32)
    sga_ref[...] = _sigmoid(jnp.dot(h, wga_ref[...], preferred_element_type=F32)).astype(BF16)
    sgb_ref[...] = _sigmoid(jnp.dot(h, wgb_ref[...], preferred_element_type=F32)).astype(BF16)
    u_ref[...] = za[:, :SSM_WIDTH].astype(BF16)
    for j in range(LANE_TILES):
        tok_ref[j] = za[:, LANES * j:LANES * (j + 1)]
    _to_chunk_rows(tok_ref, ug_ref)
    qa = za[:, SSM_WIDTH:SSM_WIDTH + Q_LORA]
    ckv = za[:, SSM_WIDTH + Q_LORA:SSM_WIDTH + Q_LORA + KV_LORA]
    krp = za[:, SSM_WIDTH + Q_LORA + KV_LORA:]
    qn = (_rms(qa, Q_LORA) * gqa_ref[...]).astype(BF16)
    qr = jnp.dot(qn, wqb_ref[...], preferred_element_type=F32)
    ckvn = _rms(ckv, KV_LORA) * gkva_ref[...]
    ckv_ref[...] = ckvn
    kr_ref[...] = krp[:, :QK_ROPE]
    ckvn = ckvn.astype(BF16)
    krp = krp.astype(BF16)
    kv = _kv_dot(ckvn, krp, wk_ref)
    gq = gq_ref[...]
    gk = gk_ref[...]
    q_sw = k_sw = None
    if use_rope:
        cos = cos_ref[...]
        sin = sin_ref[...]
        q_sw = (jnp.dot(qn, wqbs_ref[...], preferred_element_type=F32), gqs_ref[...] * sin)
        k_sw = (_kv_dot(ckvn, krp, wks_ref), gks_ref[...] * sin)
        gq = gq * cos
        gk = gk * cos
    q_ref[...] = _head_norm_rope(qr, gq, q_sw).astype(BF16)
    k_ref[...] = _head_norm_rope(kv[:, :QK_PAD], gk, k_sw).astype(BF16)
    v_ref[...] = _with_ones(kv[:, QK_PAD:]).astype(BF16)


def _inproj(x, mods, l, lat, wts, tabs):
    n = x.shape[0]
    tt = TOK_TILE
    per_seq = 2048 // tt
    if lat:
        mod_map = lambda i: (l, 1 + i // per_seq, 0, 0)
        tab_map = lambda i: (i % per_seq, 0)
    else:
        mod_map = lambda i: (l, 0, 0, 0)
        tab_map = lambda i: (0, 0)
    row = lambda i: (i, 0)
    lw = lambda *shape: pl.BlockSpec((None,) + shape, lambda i: (l,) + (0,) * len(shape))
    tab = pl.BlockSpec((tt, HEAD_PAD), tab_map)
    row_shapes = (
        jax.ShapeDtypeStruct((n, SSM_WIDTH), BF16),
        jax.ShapeDtypeStruct((n, QK_PAD), BF16),
        jax.ShapeDtypeStruct((n, QK_PAD), BF16),
        jax.ShapeDtypeStruct((n, QK_PAD), BF16),
        jax.ShapeDtypeStruct((n, D_MODEL), BF16),
        jax.ShapeDtypeStruct((n, D_MODEL), BF16),
        jax.ShapeDtypeStruct((n, KV_LORA), F32),
        jax.ShapeDtypeStruct((n, QK_ROPE), F32),
    )
    row_specs = [pl.BlockSpec((tt, s.shape[1]), row) for s in row_shapes]
    ug_shape = jax.ShapeDtypeStruct((SSM_GROUPS, n // CHUNK, CW), BF16)
    ug_spec = pl.BlockSpec((SSM_GROUPS, TILE_CHUNKS, CW), lambda i: (0, i, 0))
    outs = pl.pallas_call(
        functools.partial(_inproj_kernel, lat),
        grid=(n // tt,),
        in_specs=[
            pl.BlockSpec((tt, D_MODEL), row),
            pl.BlockSpec((None, None, 1, 6 * D_MODEL), mod_map),
            lw(1, D_MODEL), lw(D_MODEL, 1024), lw(D_MODEL, D_MODEL), lw(D_MODEL, D_MODEL),
            lw(1, Q_LORA), lw(Q_LORA, QK_PAD), lw(Q_LORA, QK_PAD), lw(1, HEAD_PAD), lw(1, HEAD_PAD), lw(1, KV_LORA),
            lw(2 * KV_LORA, 2 * QK_PAD), lw(2 * KV_LORA, QK_PAD), lw(1, HEAD_PAD), lw(1, HEAD_PAD),
            tab, tab,
        ],
        out_specs=row_specs[:1] + [ug_spec] + row_specs[1:],
        out_shape=row_shapes[:1] + (ug_shape,) + row_shapes[1:],
        scratch_shapes=[pltpu.VMEM((LANE_TILES, tt, LANES), F32)],
        compiler_params=_params(("parallel",)),
        name="inproj_lat" if lat else "inproj_ctx",
    )(x, mods, wts["g_mix"], wts["w_a"], wts["w_ga"], wts["w_gb"], wts["g_qa"], wts["w_qb"], wts["w_qb_sw"],
      wts["g_q"], wts["g_q_sw"], wts["g_kva"], wts["w_k"], wts["w_k_sw"], wts["g_k"], wts["g_k_sw"], *tabs)
    return outs


def _cachekv_kernel(ckv_ref, kr_ref, wk_ref, gk_ref, k_ref, v_ref):
    kv = _kv_dot(ckv_ref[...].astype(BF16), kr_ref[...].astype(BF16), wk_ref)
    k_ref[...] = _head_norm_rope(kv[:, :QK_PAD], gk_ref[...], None).astype(BF16)
    v_ref[...] = _with_ones(kv[:, QK_PAD:]).astype(BF16)


def _cachekv(cache_ckv, cache_krope, wts):
    nb, _, past, _ = cache_ckv.shape
    return pl.pallas_call(
        _cachekv_kernel,
        grid=(DEPTH, nb),
        in_specs=[
            pl.BlockSpec((None, None, past, KV_LORA), lambda l, b: (b, l, 0, 0)),
            pl.BlockSpec((None, None, past, QK_ROPE), lambda l, b: (b, l, 0, 0)),
            pl.BlockSpec((None, 2 * KV_LORA, 2 * QK_PAD), lambda l, b: (l, 0, 0)),
            pl.BlockSpec((None, 1, HEAD_PAD), lambda l, b: (l, 0, 0)),
        ],
        out_specs=[
            pl.BlockSpec((None, None, past, QK_PAD), lambda l, b: (l, b, 0, 0)),
            pl.BlockSpec((None, None, past, QK_PAD), lambda l, b: (l, b, 0, 0)),
        ],
        out_shape=(jax.ShapeDtypeStruct((DEPTH, nb, past, QK_PAD), BF16),
                   jax.ShapeDtypeStruct((DEPTH, nb, past, QK_PAD), BF16)),
        compiler_params=_params(("arbitrary", "arbitrary")),
        name="cache_kv",
    )(cache_ckv, cache_krope, wts["w_k"], wts["g_k"])


def _attn_heads(q_ref, segs, o_ref):
    for h in range(N_HEADS):
        qh = q_ref[:, h * HEAD_PAD:(h + 1) * HEAD_PAD]
        scores = [lax.dot_general(qh, k_ref[:, h * HEAD_PAD:(h + 1) * HEAD_PAD], (((1,), (1,)), ((), ())),
                                  preferred_element_type=F32) for k_ref, _ in segs]
        m = scores[0].max(axis=-1, keepdims=True)
        for s in scores[1:]:
            m = jnp.maximum(m, s.max(axis=-1, keepdims=True))
        acc = None
        for s, (_, v_ref) in zip(scores, segs):
            p = jnp.exp2((s - m).astype(BF16))
            a = jnp.dot(p, v_ref[:, h * HEAD_PAD:(h + 1) * HEAD_PAD], preferred_element_type=F32)
            acc = a if acc is None else acc + a
        o_ref[:, h * V_HEAD:(h + 1) * V_HEAD] = (acc[:, :V_HEAD] / acc[:, V_HEAD:V_HEAD + 1]).astype(o_ref.dtype)


def _attn_ctx_kernel(q_ref, k_ref, v_ref, o_ref):
    _attn_heads(q_ref, [(k_ref, v_ref)], o_ref)


def _attn_lat_kernel(q_ref, kc_ref, vc_ref, kl_ref, vl_ref, o_ref):
    _attn_heads(q_ref, [(kc_ref, vc_ref), (kl_ref, vl_ref)], o_ref)


def _attn_ctx(q, k, v, seq):
    n = q.shape[0]
    blk = lambda w: pl.BlockSpec((seq, w), lambda b: (b, 0))
    return pl.pallas_call(
        _attn_ctx_kernel,
        grid=(n // seq,),
        in_specs=[blk(QK_PAD), blk(QK_PAD), blk(QK_PAD)],
        out_specs=blk(ATTN_WIDTH),
        out_shape=jax.ShapeDtypeStruct((n, ATTN_WIDTH), BF16),
        compiler_params=_params(("parallel",)),
        name="attn_ctx",
    )(q, k, v)


def _attn_lat(q, k, v, kc, vc, l, seq, tq):
    n = q.shape[0]
    per = seq // tq
    past = kc.shape[2]
    return pl.pallas_call(
        _attn_lat_kernel,
        grid=(n // seq, per),
        in_specs=[
            pl.BlockSpec((tq, QK_PAD), lambda b, j: (b * per + j, 0)),
            pl.BlockSpec((None, None, past, QK_PAD), lambda b, j: (l, b, 0, 0)),
            pl.BlockSpec((None, None, past, QK_PAD), lambda b, j: (l, b, 0, 0)),
            pl.BlockSpec((seq, QK_PAD), lambda b, j: (b, 0)),
            pl.BlockSpec((seq, QK_PAD), lambda b, j: (b, 0)),
        ],
        out_specs=pl.BlockSpec((tq, ATTN_WIDTH), lambda b, j: (b * per + j, 0)),
        out_shape=jax.ShapeDtypeStruct((n, ATTN_WIDTH), BF16),
        compiler_params=_params(("parallel", "arbitrary")),
        name="attn_lat",
    )(q, kc, vc, k, v)


def _cmul(ar, ai, br, bi):
    return ar * br - ai * bi, ar * bi + ai * br


def _outer_rows(pw, m):
    return (pw[:, None, :] * m[None, :, :]).reshape(CW, SSM_STATE)


def _ssm_prep_kernel(are_ref, aim_ref, ldt_ref, bre_ref, bim_ref, cre_ref, cim_ref,
                     t_ref, et_ref, ft_ref, a16_ref):
    et_ref[...] = jnp.zeros(et_ref.shape, et_ref.dtype)
    ft_ref[...] = jnp.zeros(ft_ref.shape, ft_ref.dtype)
    row_id = lax.broadcasted_iota(jnp.int32, (CHUNK, SSM_STATE), 0)
    strip_lane = lax.broadcasted_iota(jnp.int32, (SSM_GROUP, CW), 1)
    nt = (((1,), (1,)), ((), ()))
    for gi in range(2):
        g = 2 * pl.program_id(1) + gi
        steps = _slot_step(row_id, g)
        lanes = slice(gi * SSM_STATE, (gi + 1) * SSM_STATE)
        rows = slice(gi * CW, (gi + 1) * CW)
        cre = cre_ref[gi]
        cim = cim_ref[gi]
        m_dir = []
        for d in range(2):
            a_re = are_ref[d, gi]
            a_im = aim_ref[d, gi]
            dt = jnp.exp(ldt_ref[d, gi])
            mag = jnp.exp(a_re * dt)
            abr = mag * jnp.cos(a_im * dt)
            abi = mag * jnp.sin(a_im * dt)
            den = a_re * a_re + a_im * a_im
            nr = abr - 1.0
            coef_re = (nr * a_re + abi * a_im) / den
            coef_im = (abi * a_re - nr * a_im) / den
            bbr, bbi = _cmul(coef_re, coef_im, bre_ref[d, gi], bim_ref[d, gi])

            squares = [(abr, abi)]
            for _ in range(4):
                squares.append(_cmul(*squares[-1], *squares[-1]))

            def power(n):
                pr = jnp.ones(n.shape, F32)
                pi = jnp.zeros(n.shape, F32)
                for b, (sr, si) in enumerate(squares):
                    nr_, ni_ = _cmul(pr, pi, sr, si)
                    hit = (n & (1 << b)) != 0
                    pr = jnp.where(hit, nr_, pr)
                    pi = jnp.where(hit, ni_, pi)
                return pr, pi

            to_end = (CHUNK - 1 - steps) if d == 0 else steps
            from_start = (steps + 1) if d == 0 else (CHUNK - steps)
            lag = row_id if d == 0 else (CHUNK - 1 - row_id)
            pr, pi = power(to_end)
            er = _outer_rows(pr, bbr) - _outer_rows(pi, bbi)
            ei = _outer_rows(pr, bbi) + _outer_rows(pi, bbr)
            et_ref[2 * d, rows, lanes] = er.astype(et_ref.dtype)
            et_ref[2 * d + 1, rows, lanes] = ei.astype(et_ref.dtype)
            pr, pi = power(from_start)
            ft_ref[2 * d, rows, lanes] = (_outer_rows(pr, cre) - _outer_rows(pi, cim)).astype(ft_ref.dtype)
            ft_ref[2 * d + 1, rows, lanes] = (-(_outer_rows(pr, cim) + _outer_rows(pi, cre))).astype(ft_ref.dtype)
            pr, pi = power(lag)
            zr = _outer_rows(pr, cre) - _outer_rows(pi, cim)
            zi = _outer_rows(pi, cre) + _outer_rows(pr, cim)
            m_dir.append(
                lax.dot_general(bbr, zr, nt, preferred_element_type=F32, precision=lax.Precision.HIGHEST)
                - lax.dot_general(bbi, zi, nt, preferred_element_type=F32, precision=lax.Precision.HIGHEST))
            p16r, p16i = squares[4]
            a16_ref[2 * d:2 * d + 1, lanes] = p16r
            a16_ref[2 * d + 1:2 * d + 2, lanes] = p16i
        mf, mb = m_dir
        edge = CW - SSM_GROUP
        low = mb + pltpu.roll(jnp.where(strip_lane < SSM_GROUP, mf, 0.0), edge, 1)
        high = jnp.where(strip_lane < edge, pltpu.roll(mf, edge, 1), 0.0)
        strip = jnp.concatenate([low, high], axis=1)
        turn = SSM_GROUP * (g & 7)
        for j in range(CHUNK):
            first = SSM_GROUP * (CHUNK - 1 - j)
            win = strip[:, first:first + CW]
            win = jnp.concatenate([pltpu.roll(win[:, LANES * h:LANES * (h + 1)], turn, 1)
                                   for h in range(CW // LANES)], axis=1)
            slot = (j & 8) + ((j + g) & 7)
            t_ref[gi, pl.ds(pl.multiple_of(SSM_GROUP * slot, SSM_GROUP), SSM_GROUP), :] = win.astype(t_ref.dtype)


def _ssm_prep(a_re, a_im, log_dt, b_re, b_im, c_re, c_im):
    g2 = SSM_GROUPS // 2
    a_spec = pl.BlockSpec((None, 2, 2, 1, SSM_STATE), lambda l, g: (l, 0, g, 0, 0))
    b_spec = pl.BlockSpec((None, 2, 2, SSM_GROUP, SSM_STATE), lambda l, g: (l, 0, g, 0, 0))
    c_spec = pl.BlockSpec((None, 2, SSM_GROUP, SSM_STATE), lambda l, g: (l, g, 0, 0))
    out_shapes = (
        jax.ShapeDtypeStruct((DEPTH, SSM_GROUPS, CW, CW), BF16),
        jax.ShapeDtypeStruct((DEPTH, g2, 4, 2 * CW, 2 * SSM_STATE), BF16),
        jax.ShapeDtypeStruct((DEPTH, g2, 4, 2 * CW, 2 * SSM_STATE), BF16),
        jax.ShapeDtypeStruct((DEPTH, g2, 4, 2 * SSM_STATE), F32),
    )
    return pl.pallas_call(
        _ssm_prep_kernel,
        grid=(DEPTH, g2),
        in_specs=[a_spec, a_spec,
                  pl.BlockSpec((None, 2, 2, 1, 1), lambda l, g: (l, 0, g, 0, 0)),
                  b_spec, b_spec, c_spec, c_spec],
        out_specs=[
            pl.BlockSpec((None, 2, CW, CW), lambda l, g: (l, g, 0, 0)),
            pl.BlockSpec((None, None, 4, 2 * CW, 2 * SSM_STATE), lambda l, g: (l, g, 0, 0, 0)),
            pl.BlockSpec((None, None, 4, 2 * CW, 2 * SSM_STATE), lambda l, g: (l, g, 0, 0, 0)),
            pl.BlockSpec((None, None, 4, 2 * SSM_STATE), lambda l, g: (l, g, 0, 0)),
        ],
        out_shape=out_shapes,
        compiler_params=_params(("arbitrary", "arbitrary")),
        name="ssm_prep",
    )(a_re.reshape(DEPTH, 2, SSM_GROUPS, 1, SSM_STATE), a_im.reshape(DEPTH, 2, SSM_GROUPS, 1, SSM_STATE),
      log_dt.reshape(DEPTH, 2, SSM_GROUPS, 1, 1),
      jnp.swapaxes(b_re, -1, -2), jnp.swapaxes(b_im, -1, -2), c_re, c_im)


def _ssm_kernel(nchunk, nb, u_ref, t_ref, et_ref, ft_ref, a16_ref, h0_ref, y_ref, fin_ref, e_ref, h_ref):
    u0 = u_ref[0]
    u1 = u_ref[1]
    for x in range(4):
        e_ref[x] = (jnp.dot(u0, et_ref[x, 0:CW, :], preferred_element_type=F32)
                    + jnp.dot(u1, et_ref[x, CW:2 * CW, :], preferred_element_type=F32))
    afr, afi, abr, abi = (a16_ref[i:i + 1, :] for i in range(4))

    def body(i, carry):
        fr, fi, br, bi = carry
        kf = pl.ds(i, nb, stride=nchunk)
        kb = pl.ds(nchunk - 1 - i, nb, stride=nchunk)
        h_ref[0, kf, :] = fr
        h_ref[1, kf, :] = fi
        h_ref[2, kb, :] = br
        h_ref[3, kb, :] = bi
        return (afr * fr - afi * fi + e_ref[0, kf, :], afr * fi + afi * fr + e_ref[1, kf, :],
                abr * br - abi * bi + e_ref[2, kb, :], abr * bi + abi * br + e_ref[3, kb, :])

    fin = lax.fori_loop(0, nchunk, body, tuple(h0_ref[i] for i in range(4)))
    for i in range(4):
        fin_ref[i] = fin[i]
    nt = (((1,), (1,)), ((), ()))
    acc = None
    for x in range(4):
        part = lax.dot_general(h_ref[x].astype(BF16), ft_ref[x], nt, preferred_element_type=F32)
        acc = part if acc is None else acc + part
    y_ref[0] = jnp.dot(u0, t_ref[0], preferred_element_type=F32) + acc[:, :CW]
    y_ref[1] = jnp.dot(u1, t_ref[1], preferred_element_type=F32) + acc[:, CW:]


def _ssm_mix(ug, nb, seq, h0_re, h0_im, prep, l):
    tmat, et, ft, a16 = prep
    nchunk = seq // CHUNK
    r = ug.shape[1]
    g2 = SSM_GROUPS // 2
    width = SSM_GROUPS * SSM_STATE
    h0 = jnp.stack([h[:, d].reshape(nb, width).astype(F32) for d in range(2) for h in (h0_re, h0_im)], axis=0)
    pair = pl.BlockSpec((2, r, CW), lambda g: (g, 0, 0))
    mats = pl.BlockSpec((None, None, 4, 2 * CW, 2 * SSM_STATE), lambda g: (l, g, 0, 0, 0))
    state = pl.BlockSpec((4, nb, 2 * SSM_STATE), lambda g: (0, 0, g))
    y, fin = pl.pallas_call(
        functools.partial(_ssm_kernel, nchunk, nb),
        grid=(g2,),
        in_specs=[pair, pl.BlockSpec((None, 2, CW, CW), lambda g: (l, g, 0, 0)), mats, mats,
                  pl.BlockSpec((None, None, 4, 2 * SSM_STATE), lambda g: (l, g, 0, 0)), state],
        out_specs=[pair, state],
        out_shape=(jax.ShapeDtypeStruct((SSM_GROUPS, r, CW), F32), jax.ShapeDtypeStruct((4, nb, width), F32)),
        scratch_shapes=[pltpu.VMEM((4, r, 2 * SSM_STATE), F32), pltpu.VMEM((4, r, 2 * SSM_STATE), F32)],
        compiler_params=_params(("parallel",)),
        name="ssm",
    )(ug, tmat, et, ft, a16, h0)
    fin = fin.reshape(2, 2, nb, SSM_GROUPS, SSM_STATE)
    return y, fin


def _gelu_tanh(x):
    return 0.5 * x * (1.0 + jnp.tanh(math.sqrt(2.0 / math.pi) * (x + 0.044715 * (x * x * x))))


def _route(aff, bias):
    a = [aff[e:e + 1, :] for e in range(N_EXPERTS)]
    s = [a[e] + bias[e:e + 1, :] for e in range(N_EXPERTS)]
    keep = []
    for e in range(N_EXPERTS):
        g0 = (e // EXPERTS_PER_GROUP) * EXPERTS_PER_GROUP
        rank = None
        for j in range(g0, g0 + EXPERTS_PER_GROUP):
            if j == e:
                continue
            beats = (s[j] >= s[e]) if j < e else (s[j] > s[e])
            r = jnp.where(beats, 1.0, 0.0)
            rank = r if rank is None else rank + r
        keep.append(rank < 1.5)
    score = []
    for g in range(N_GROUPS):
        tot = None
        for e in range(g * EXPERTS_PER_GROUP, (g + 1) * EXPERTS_PER_GROUP):
            v = jnp.where(keep[e], s[e], 0.0)
            tot = v if tot is None else tot + v
        score.append(tot)
    gates = []
    picked = []
    for g in range(N_GROUPS):
        lost = None
        for j in range(N_GROUPS):
            if j == g:
                continue
            beats = (score[j] >= score[g]) if j < g else (score[j] > score[g])
            r = jnp.where(beats, 1.0, 0.0)
            lost = r if lost is None else lost + r
        chosen = lost < 0.5
        picked.append(jnp.where(chosen, 1.0, 0.0))
        w = [jnp.where(keep[e], a[e], 0.0) for e in range(g * EXPERTS_PER_GROUP, (g + 1) * EXPERTS_PER_GROUP)]
        tot = w[0] + w[1] + w[2] + w[3]
        for v in w:
            gates.append(jnp.where(chosen, v / tot, 0.0))
    return gates, picked


def _merge_kernel(x_ref, yg_ref, u_ref, at_ref, sga_ref, sgb_ref, mod_ref, d_ref, wglu_ref, bglu_ref, wsp_ref,
                  wap_ref, wo_ref, gffn_ref, wr_ref, rb_ref, tri_ref, ramp_ref,
                  xo_ref, hs_ref, mt_ref, ms_ref, of_ref, tok_ref):
    mod = mod_ref[...]
    gate1 = mod[:, 2 * D_MODEL:3 * D_MODEL]
    shift2 = mod[:, 3 * D_MODEL:4 * D_MODEL]
    scale2 = mod[:, 4 * D_MODEL:5 * D_MODEL]
    _from_chunk_rows(yg_ref, tok_ref)
    ys = jnp.concatenate([tok_ref[j] for j in range(LANE_TILES)], axis=1)
    y = _gelu_tanh(ys + d_ref[...] * u_ref[...].astype(F32))
    glu = jnp.dot(y.astype(BF16), wglu_ref[...], preferred_element_type=F32) + bglu_ref[...]
    y = (y * _sigmoid(glu)).astype(BF16)
    a_out = jnp.dot(y, wsp_ref[...], preferred_element_type=F32)
    b_out = jnp.dot(at_ref[...], wap_ref[...], preferred_element_type=F32)
    m = (sga_ref[...].astype(F32) * a_out + sgb_ref[...].astype(F32) * b_out).astype(BF16)
    x = x_ref[...] + gate1 * jnp.dot(m, wo_ref[...], preferred_element_type=F32)
    xo_ref[...] = x
    h2 = (_rms(x, D_MODEL) * gffn_ref[...]) * (1.0 + scale2) + shift2
    logits = lax.dot_general(wr_ref[...], h2, (((1,), (1,)), ((), ())), preferred_element_type=F32,
                             precision=lax.Precision.HIGHEST)
    gates, picked = _route(_sigmoid(logits), rb_ref[...])
    n = x.shape[0]
    onehot = jnp.concatenate(picked + [jnp.zeros((8 - N_GROUPS, n), F32)], axis=0)
    before = jnp.dot(onehot.astype(BF16), tri_ref[...], preferred_element_type=F32)
    count = [jnp.sum(p, axis=-1, keepdims=True) for p in picked]
    start = [jnp.zeros((1, 1), F32)]
    for g in range(N_GROUPS - 1):
        start.append(start[-1] + count[g])
    pos = sum(picked[g] * (start[g] + before[g:g + 1, :]) for g in range(N_GROUPS))
    gid = sum(float(g) * picked[g] for g in range(1, N_GROUPS))
    perm = jnp.where(ramp_ref[...] == pos, 1.0, 0.0).astype(BF16)
    hs_ref[...] = jnp.dot(perm, h2.astype(BF16), preferred_element_type=F32).astype(BF16)
    own = [sum(gates[EXPERTS_PER_GROUP * g + e] for g in range(N_GROUPS)) for e in range(EXPERTS_PER_GROUP)]
    meta = jnp.concatenate(own + [gid, pos, jnp.zeros((LANES - EXPERTS_PER_GROUP - 2, n), F32)], axis=0).T
    mt_ref[...] = meta
    hi = meta.astype(BF16)
    lo = (meta - hi.astype(F32)).astype(BF16)
    ms_ref[...] = (jnp.dot(perm, hi, preferred_element_type=F32)
                   + jnp.dot(perm, lo, preferred_element_type=F32))
    lane = lax.broadcasted_iota(jnp.int32, (8, LANES), 1)
    offs = sum(jnp.where(lane == g, start[g], 0.0) for g in range(1, N_GROUPS))
    of_ref[...] = offs.astype(jnp.int32)


def _merge(x, yg, u, attn, sga, sgb, mods, l, lat, wts):
    n = x.shape[0]
    tt = TOK_TILE
    per_seq = 2048 // tt
    mod_map = (lambda i: (l, 1 + i // per_seq, 0, 0)) if lat else (lambda i: (l, 0, 0, 0))
    row = lambda i: (i, 0)
    lw = lambda *shape: pl.BlockSpec((None,) + shape, lambda i: (l,) + (0,) * len(shape))
    return pl.pallas_call(
        _merge_kernel,
        grid=(n // tt,),
        in_specs=[
            pl.BlockSpec((tt, D_MODEL), row),
            pl.BlockSpec((SSM_GROUPS, TILE_CHUNKS, CW), lambda i: (0, i, 0)),
            pl.BlockSpec((tt, SSM_WIDTH), row),
            pl.BlockSpec((tt, ATTN_WIDTH), row), pl.BlockSpec((tt, D_MODEL), row), pl.BlockSpec((tt, D_MODEL), row),
            pl.BlockSpec((None, None, 1, 6 * D_MODEL), mod_map),
            lw(1, SSM_WIDTH), lw(SSM_WIDTH, SSM_WIDTH), lw(1, SSM_WIDTH), lw(SSM_WIDTH, D_MODEL),
            lw(ATTN_WIDTH, D_MODEL), lw(D_MODEL, D_MODEL), lw(1, D_MODEL),
            pl.BlockSpec((N_EXPERTS, D_MODEL), lambda i: (0, 0)),
            pl.BlockSpec((N_EXPERTS, 1), lambda i: (0, 0)),
            pl.BlockSpec((tt, tt), lambda i: (0, 0)), pl.BlockSpec((tt, tt), lambda i: (0, 0)),
        ],
        out_specs=[pl.BlockSpec((tt, D_MODEL), row), pl.BlockSpec((tt, D_MODEL), row),
                   pl.BlockSpec((tt, LANES), row), pl.BlockSpec((tt, LANES), row),
                   pl.BlockSpec((None, 8, LANES), lambda i: (i, 0, 0))],
        out_shape=(jax.ShapeDtypeStruct((n, D_MODEL), F32), jax.ShapeDtypeStruct((n, D_MODEL), BF16),
                   jax.ShapeDtypeStruct((n, LANES), F32), jax.ShapeDtypeStruct((n, LANES), F32),
                   jax.ShapeDtypeStruct((n // tt, 8, LANES), jnp.int32)),
        scratch_shapes=[pltpu.VMEM((LANE_TILES, tt, LANES), F32)],
        compiler_params=_params(("parallel",)),
        name="merge_lat" if lat else "merge_ctx",
    )(x, yg, u, attn, sga, sgb, mods, wts["ssm_d"], wts["w_glu"], wts["b_glu"], wts["w_ssm_proj"],
      wts["w_attn_proj"], wts["w_o"], wts["g_ffn"], wts["w_router_t"], wts["router_bias"],
      jnp.asarray(np.triu(np.ones((tt, tt), np.float32), 1), BF16),
      jnp.asarray(np.broadcast_to(np.arange(tt, dtype=np.float32)[:, None], (tt, tt))))


def _moe_kernel(offs_ref, x_ref, hs_ref, ms_ref, mt_ref, mod_ref, wgu_ref, wd_ref, o_ref, acc_ref):
    tile = pl.program_id(0)
    starts = [offs_ref[N_GROUPS * tile + g] for g in range(1, N_GROUPS)]
    width = EXPERTS_PER_GROUP * D_EXPERT

    def group_of(r):
        return sum((s <= r).astype(jnp.int32) for s in starts)

    for c in range(TOK_TILE // MOE_ROWS):
        rows = slice(c * MOE_ROWS, (c + 1) * MOE_ROWS)
        acc_ref[rows, :] = jnp.zeros((MOE_ROWS, D_MODEL), F32)

        def body(g, carry):
            meta = ms_ref[rows, :]
            own = jnp.where(meta[:, EXPERTS_PER_GROUP:EXPERTS_PER_GROUP + 1] == g.astype(F32), 1.0, 0.0)
            gu = jnp.dot(hs_ref[rows, :], wgu_ref[g], preferred_element_type=F32)
            hg = gu[:, :width]
            gate = jnp.concatenate([jnp.broadcast_to(meta[:, e:e + 1] * own, (MOE_ROWS, D_EXPERT))
                                    for e in range(EXPERTS_PER_GROUP)], axis=1)
            act = (hg * _sigmoid(hg)) * gu[:, width:] * gate
            acc_ref[rows, :] += jnp.dot(act.astype(BF16), wd_ref[g], preferred_element_type=F32)
            return carry

        lax.fori_loop(group_of(c * MOE_ROWS), group_of((c + 1) * MOE_ROWS - 1) + 1, body, 0)

    slot = lax.broadcasted_iota(jnp.int32, (TOK_TILE, TOK_TILE), 1).astype(F32)
    unsort = jnp.where(mt_ref[:, EXPERTS_PER_GROUP + 1:EXPERTS_PER_GROUP + 2] == slot, 1.0, 0.0).astype(BF16)
    y = jnp.dot(unsort, acc_ref[...].astype(BF16), preferred_element_type=F32)
    o_ref[...] = x_ref[...] + mod_ref[:, 5 * D_MODEL:6 * D_MODEL] * y


def _moe(x, hs, meta_tok, meta_sorted, offs, mods, l, lat, wts):
    n = x.shape[0]
    tt = TOK_TILE
    per_seq = 2048 // tt
    mod_map = (lambda i, o: (l, 1 + i // per_seq, 0, 0)) if lat else (lambda i, o: (l, 0, 0, 0))
    row = lambda i, o: (i, 0)
    width = EXPERTS_PER_GROUP * D_EXPERT
    resident = lambda *shape: pl.BlockSpec((None,) + shape, lambda i, o: (l,) + (0,) * len(shape),
                                           pipeline_mode=pl.Buffered(1))
    grid_spec = pltpu.PrefetchScalarGridSpec(
        num_scalar_prefetch=1,
        grid=(n // tt,),
        in_specs=[
            pl.BlockSpec((tt, D_MODEL), row), pl.BlockSpec((tt, D_MODEL), row), pl.BlockSpec((tt, LANES), row),
            pl.BlockSpec((tt, LANES), row),
            pl.BlockSpec((None, None, 1, 6 * D_MODEL), mod_map),
            resident(N_GROUPS, D_MODEL, 2 * width), resident(N_GROUPS, width, D_MODEL),
        ],
        out_specs=pl.BlockSpec((tt, D_MODEL), row),
        scratch_shapes=[pltpu.VMEM((tt, D_MODEL), F32)],
    )
    return pl.pallas_call(
        _moe_kernel,
        grid_spec=grid_spec,
        out_shape=jax.ShapeDtypeStruct((n, D_MODEL), F32),
        compiler_params=_params(("arbitrary",), MOE_VMEM_LIMIT),
        name="moe_lat" if lat else "moe_ctx",
    )(offs[:, 0, :N_GROUPS].reshape(-1), x, hs, meta_sorted, meta_tok, mods, wts["w_gu"], wts["w_down"])


def _pad_heads(w, width):
    lead = w.shape[:-1]
    w = w.reshape(lead + (N_HEADS, width))
    w = jnp.pad(w, [(0, 0)] * len(lead) + [(0, 0), (0, HEAD_PAD - width)])
    return w.reshape(lead + (N_HEADS * HEAD_PAD,))


def _swap_partners(w):
    lead = w.shape[:-1]
    w = w.reshape(lead + (-1, HEAD_PAD))
    half = ROPE_AXIS // 2
    a, b = QK_NOPE, QK_NOPE + ROPE_AXIS
    parts = [w[..., :a], w[..., a + half:b], w[..., a:a + half], w[..., b + half:b + 2 * half], w[..., b:b + half],
             w[..., b + 2 * half:]]
    return jnp.concatenate(parts, axis=-1).reshape(lead + (-1,))


def _layout_weights(w_in, g_qa, w_qb, g_q, g_kva, w_kvb, g_k, g_mix, w_attn_proj, ssm_d, ssm_w_glu, ssm_b_glu,
                    ssm_w_proj, w_o, g_ffn, w_router, router_bias, w_gate, w_up, w_down):
    n_a = SSM_WIDTH + Q_LORA + KV_LORA + QK_ROPE
    w_a = jnp.pad(w_in[:, :, :n_a], ((0, 0), (0, 0), (0, 1024 - n_a)))
    kv = w_kvb.reshape(DEPTH, KV_LORA, N_HEADS, QK_NOPE + V_HEAD)
    k_cols = _pad_heads(kv[..., :QK_NOPE].reshape(DEPTH, KV_LORA, N_HEADS * QK_NOPE), QK_NOPE)
    v_cols = _pad_heads(kv[..., QK_NOPE:].reshape(DEPTH, KV_LORA, ATTN_WIDTH), V_HEAD)
    place = np.zeros((KV_LORA, 2 * QK_PAD), np.float32)
    for h in range(N_HEADS):
        for r in range(QK_ROPE):
            place[r, h * HEAD_PAD + QK_NOPE + r] = 1.0
    w_k = jnp.concatenate([jnp.concatenate([k_cols, v_cols], axis=-1),
                           jnp.broadcast_to(jnp.asarray(place), (DEPTH,) + place.shape)], axis=1)
    w_qb_pad = _pad_heads(w_qb, QK_HEAD).astype(BF16)
    w_k = w_k.astype(BF16)
    to_groups = lambda w: jnp.transpose(w.astype(BF16).reshape(DEPTH, N_GROUPS, EXPERTS_PER_GROUP, D_MODEL, D_EXPERT),
                                        (0, 1, 3, 2, 4)).reshape(DEPTH, N_GROUPS, D_MODEL, EXPERTS_PER_GROUP * D_EXPERT)
    pad_gain = lambda g: jnp.pad(g, ((0, 0), (0, HEAD_PAD - QK_HEAD))).reshape(DEPTH, 1, HEAD_PAD)
    vec = lambda g: g.reshape(DEPTH, 1, -1)
    return {
        "g_mix": vec(g_mix), "w_a": w_a.astype(BF16),
        "w_ga": w_in[:, :, n_a:n_a + D_MODEL].astype(BF16), "w_gb": w_in[:, :, n_a + D_MODEL:].astype(BF16),
        "g_qa": vec(g_qa), "w_qb": w_qb_pad, "w_qb_sw": _swap_partners(w_qb_pad),
        "g_q": pad_gain(g_q) * Q_SCALE, "g_q_sw": _swap_partners(pad_gain(g_q)) * Q_SCALE,
        "g_kva": vec(g_kva), "w_k": w_k, "w_k_sw": _swap_partners(w_k[:, :, :QK_PAD]),
        "g_k": pad_gain(g_k), "g_k_sw": _swap_partners(pad_gain(g_k)),
        "ssm_d": vec(ssm_d), "w_glu": ssm_w_glu.astype(BF16), "b_glu": vec(ssm_b_glu),
        "w_ssm_proj": ssm_w_proj.astype(BF16), "w_attn_proj": w_attn_proj.astype(BF16), "w_o": w_o.astype(BF16),
        "g_ffn": vec(g_ffn), "w_router_t": w_router.T, "router_bias": router_bias.reshape(N_EXPERTS, 1),
        "w_gu": jnp.concatenate([to_groups(w_gate), to_groups(w_up)], axis=-1),
        "w_down": w_down.astype(BF16).reshape(DEPTH, N_GROUPS, EXPERTS_PER_GROUP * D_EXPERT, D_MODEL),
    }


def _rope_tables(n_tokens):
    pos = np.arange(n_tokens)
    inv = 1.0 / (ROPE_BASE ** (np.arange(ROPE_AXIS // 2, dtype=np.float32) * 2.0 / ROPE_AXIS))
    ang = np.zeros((n_tokens, HEAD_PAD), np.float32)
    half = ROPE_AXIS // 2
    row = (pos // GRID_W).astype(np.float32)[:, None] * inv
    col = (pos % GRID_W).astype(np.float32)[:, None] * inv
    first = np.zeros((HEAD_PAD,), bool)
    second = np.zeros((HEAD_PAD,), bool)
    for base, a in ((QK_NOPE, row), (QK_NOPE + ROPE_AXIS, col)):
        ang[:, base:base + half] = a
        ang[:, base + half:base + 2 * half] = a
        first[base:base + half] = True
        second[base + half:base + 2 * half] = True
    ang = jnp.asarray(ang)
    cos = jnp.where(jnp.asarray(first | second), jnp.cos(ang), 1.0)
    sin = jnp.sin(ang)
    return cos, jnp.where(jnp.asarray(first), -sin, jnp.where(jnp.asarray(second), sin, 0.0))


def kernel(x_prompt, x_sample, c, cache_ckv, cache_krope, state_ssm_re, state_ssm_im, c_ctx, w_ada, b_ada, g_mix, w_in, g_qa, w_qb, g_q, g_kva, w_kvb, g_k, w_attn_proj, ssm_a_re, ssm_a_im, ssm_log_dt, ssm_b_re, ssm_b_im, ssm_c_re, ssm_c_im, ssm_d, ssm_w_glu, ssm_b_glu, ssm_w_proj, w_o, g_ffn, w_router, router_bias, w_gate, w_up, w_down):
    nb_c, seq_c, _ = x_prompt.shape
    nb_l, seq_l, _ = x_sample.shape
    wts = _layout_weights(w_in, g_qa, w_qb, g_q, g_kva, w_kvb, g_k, g_mix, w_attn_proj, ssm_d, ssm_w_glu,
                          ssm_b_glu, ssm_w_proj, w_o, g_ffn, w_router, router_bias, w_gate, w_up, w_down)
    cvec = jnp.concatenate([c_ctx[None, :], c, jnp.zeros((8 - 1 - nb_l, D_MODEL), F32)], axis=0)
    mods = _mods(cvec, w_ada, b_ada).reshape(DEPTH, 8, 1, 6 * D_MODEL)
    prep = _ssm_prep(ssm_a_re, ssm_a_im, ssm_log_dt, ssm_b_re, ssm_b_im, ssm_c_re, ssm_c_im)
    tabs = _rope_tables(seq_l)
    kc, vc = _cachekv(cache_ckv, cache_krope, wts)

    xp = x_prompt.reshape(nb_c * seq_c, D_MODEL)
    xs = x_sample.reshape(nb_l * seq_l, D_MODEL)
    zeros_c = jnp.zeros((nb_c, 2, SSM_GROUPS, SSM_STATE), F32)
    ckvs, kropes, fins = [], [], []
    for l in range(DEPTH):
        u, ug, q, k, v, sga, sgb, ckv, krope = _inproj(xp, mods, l, False, wts, tabs)
        ckvs.append(ckv)
        kropes.append(krope)
        yg, fin = _ssm_mix(ug, nb_c, seq_c, zeros_c, zeros_c, prep, l)
        fins.append(fin)
        attn = _attn_ctx(q, k, v, seq_c)
        xp, *routed = _merge(xp, yg, u, attn, sga, sgb, mods, l, False, wts)
        xp = _moe(xp, *routed, mods, l, False, wts)

        u, ug, q, k, v, sga, sgb, _, _ = _inproj(xs, mods, l, True, wts, tabs)
        yg, _ = _ssm_mix(ug, nb_l, seq_l, state_ssm_re[:, l], state_ssm_im[:, l], prep, l)
        attn = _attn_lat(q, k, v, kc, vc, l, seq_l, 256)
        xs, *routed = _merge(xs, yg, u, attn, sga, sgb, mods, l, True, wts)
        xs = _moe(xs, *routed, mods, l, True, wts)

    new_ckv = jnp.stack(ckvs, axis=0).reshape(DEPTH, nb_c, seq_c, KV_LORA).transpose(1, 0, 2, 3)
    new_krope = jnp.stack(kropes, axis=0).reshape(DEPTH, nb_c, seq_c, QK_ROPE).transpose(1, 0, 2, 3)
    fin = jnp.stack(fins, axis=0)
    new_re = jnp.transpose(fin[:, :, 0], (2, 0, 1, 3, 4))
    new_im = jnp.transpose(fin[:, :, 1], (2, 0, 1, 3, 4))
    return (xp.reshape(nb_c, seq_c, D_MODEL), xs.reshape(nb_l, seq_l, D_MODEL), new_ckv, new_krope, new_re, new_im)
```

```python
import functools
import math

import jax
import jax.numpy as jnp
import numpy as np
from jax import lax
from jax.experimental import pallas as pl
from jax.experimental.pallas import tpu as pltpu

F32 = jnp.float32
BF16 = jnp.bfloat16

D_MODEL = 1024
DEPTH = 4
GRID_W = 64
SSM_WIDTH = 512
SSM_GROUP = 16
SSM_GROUPS = 32
SSM_STATE = 64
N_HEADS = 8
QK_NOPE = 64
QK_ROPE = 32
QK_HEAD = 96
V_HEAD = 64
Q_LORA = 256
KV_LORA = 128
ATTN_WIDTH = 512
ROPE_AXIS = 16
ROPE_BASE = 10000.0
N_EXPERTS = 16
N_GROUPS = 4
EXPERTS_PER_GROUP = 4
D_EXPERT = 256
EPS = 1e-6

LANES = 128
HEAD_PAD = LANES
QK_PAD = N_HEADS * HEAD_PAD
CHUNK = 16
CW = CHUNK * SSM_GROUP
TOK_TILE = 512
TILE_CHUNKS = TOK_TILE // CHUNK
LANE_TILES = SSM_WIDTH // LANES
GROUPS_PER_TILE = LANES // SSM_GROUP
Q_SCALE = math.log2(math.e) / math.sqrt(QK_HEAD)
MOE_ROWS = 192
VMEM_LIMIT = 48 * 1024 * 1024
MOE_VMEM_LIMIT = 56 * 1024 * 1024


def _params(sem, vmem=VMEM_LIMIT):
    return pltpu.CompilerParams(dimension_semantics=sem, vmem_limit_bytes=vmem)


def _sigmoid(x):
    return 1.0 / (1.0 + jnp.exp(-x))


def _rms(x, n):
    return x * lax.rsqrt(jnp.sum(x * x, axis=-1, keepdims=True) * (1.0 / n) + EPS)


def _mods_kernel(c_ref, w_ref, b_ref, o_ref):
    c = c_ref[...]
    cs = (c * _sigmoid(c)).astype(BF16)
    o_ref[...] = jnp.dot(cs, w_ref[...].astype(BF16), preferred_element_type=F32) + b_ref[...]


def _mods(cvec, w_ada, b_ada):
    nb = 1536
    return pl.pallas_call(
        _mods_kernel,
        grid=(DEPTH, 6 * D_MODEL // nb),
        in_specs=[
            pl.BlockSpec((8, D_MODEL), lambda l, j: (0, 0)),
            pl.BlockSpec((None, D_MODEL, nb), lambda l, j: (l, 0, j)),
            pl.BlockSpec((None, 1, nb), lambda l, j: (l, 0, j)),
        ],
        out_specs=pl.BlockSpec((None, 8, nb), lambda l, j: (l, 0, j)),
        out_shape=jax.ShapeDtypeStruct((DEPTH, 8, 6 * D_MODEL), F32),
        compiler_params=_params(("arbitrary", "arbitrary")),
        name="mods",
    )(cvec, w_ada, b_ada.reshape(DEPTH, 1, 6 * D_MODEL))


def _slot_step(slot, g):
    return (slot & 8) + ((slot - g) & 7)


def _lane_segment():
    return lax.broadcasted_iota(jnp.int32, (TILE_CHUNKS, LANES), 1) // SSM_GROUP


def _to_chunk_rows(tok_ref, ug_ref):
    seg = _lane_segment()
    for j in range(LANE_TILES):
        for half in range(CHUNK // 8):
            rolled = []
            for r in range(8):
                v = tok_ref[j, pl.ds(8 * half + r, TILE_CHUNKS, stride=CHUNK), :]
                rolled.append(pltpu.roll(v, SSM_GROUP * r, 1) if r else v)
            for gg in range(GROUPS_PER_TILE):
                acc = rolled[(0 - gg) % 8]
                for m in range(1, 8):
                    acc = jnp.where(seg == m, rolled[(m - gg) % 8], acc)
                ug_ref[GROUPS_PER_TILE * j + gg, :, LANES * half:LANES * (half + 1)] = acc.astype(ug_ref.dtype)


def _from_chunk_rows(yg_ref, tok_ref):
    seg = _lane_segment()
    for j in range(LANE_TILES):
        for half in range(CHUNK // 8):
            ys = [yg_ref[GROUPS_PER_TILE * j + gg, :, LANES * half:LANES * (half + 1)]
                  for gg in range(GROUPS_PER_TILE)]
            for r in range(8):
                acc = ys[(0 - r) % 8]
                for m in range(1, 8):
                    acc = jnp.where(seg == m, ys[(m - r) % 8], acc)
                out = pltpu.roll(acc, LANES - SSM_GROUP * r, 1) if r else acc
                tok_ref[j, pl.ds(8 * half + r, TILE_CHUNKS, stride=CHUNK), :] = out


def _head_norm_rope(t, gain, swapped):
    outs = []
    for h in range(N_HEADS):
        lanes = slice(h * HEAD_PAD, (h + 1) * HEAD_PAD)
        th = t[:, lanes]
        scale = lax.rsqrt(jnp.sum(th * th, axis=-1, keepdims=True) * (1.0 / QK_HEAD) + EPS)
        y = th * gain
        if swapped is not None:
            y = y + swapped[0][:, lanes] * swapped[1]
        outs.append(y * scale)
    return jnp.concatenate(outs, axis=1)


def _with_ones(v):
    lane = lax.broadcasted_iota(jnp.int32, (1, v.shape[1]), 1)
    return jnp.where((lane & (HEAD_PAD - 1)) == V_HEAD, 1.0, v)


def _kv_dot(ckvn, krp, wk_ref):
    return (jnp.dot(ckvn, wk_ref[0:KV_LORA, :], preferred_element_type=F32)
            + jnp.dot(krp, wk_ref[KV_LORA:KV_LORA + krp.shape[1], :], preferred_element_type=F32))


def _inproj_kernel(use_rope, x_ref, mod_ref, gmix_ref, wa_ref, wga_ref, wgb_ref, gqa_ref, wqb_ref, wqbs_ref, gq_ref,
                   gqs_ref, gkva_ref, wk_ref, wks_ref, gk_ref, gks_ref, cos_ref, sin_ref,
                   u_ref, ug_ref, q_ref, k_ref, v_ref, sga_ref, sgb_ref, ckv_ref, kr_ref, tok_ref):
    x = x_ref[...]
    mod = mod_ref[...]
    shift1 = mod[:, 0:D_MODEL]
    scale1 = mod[:, D_MODEL:2 * D_MODEL]
    h = _rms(x, D_MODEL) * gmix_ref[...]
    h = (h * (1.0 + scale1) + shift1).astype(BF16)
    za = jnp.dot(h, wa_ref[...], preferred_element_type=F32)
    sga_ref[...] = _sigmoid(jnp.dot(h, wga_ref[...], preferred_element_type=F32)).astype(BF16)
    sgb_ref[...] = _sigmoid(jnp.dot(h, wgb_ref[...], preferred_element_type=F32)).astype(BF16)
    u_ref[...] = za[:, :SSM_WIDTH].astype(BF16)
    for j in range(LANE_TILES):
        tok_ref[j] = za[:, LANES * j:LANES * (j + 1)]
    _to_chunk_rows(tok_ref, ug_ref)
    qa = za[:, SSM_WIDTH:SSM_WIDTH + Q_LORA]
    ckv = za[:, SSM_WIDTH + Q_LORA:SSM_WIDTH + Q_LORA + KV_LORA]
    krp = za[:, SSM_WIDTH + Q_LORA + KV_LORA:]
    qn = (_rms(qa, Q_LORA) * gqa_ref[...]).astype(BF16)
    qr = jnp.dot(qn, wqb_ref[...], preferred_element_type=F32)
    ckvn = _rms(ckv, KV_LORA) * gkva_ref[...]
    ckv_ref[...] = ckvn
    kr_ref[...] = krp[:, :QK_ROPE]
    ckvn = ckvn.astype(BF16)
    krp = krp.astype(BF16)
    kv = _kv_dot(ckvn, krp, wk_ref)
    gq = gq_ref[...]
    gk = gk_ref[...]
    q_sw = k_sw = None
    if use_rope:
        cos = cos_ref[...]
        sin = sin_ref[...]
        q_sw = (jnp.dot(qn, wqbs_ref[...], preferred_element_type=F32), gqs_ref[...] * sin)
        k_sw = (_kv_dot(ckvn, krp, wks_ref), gks_ref[...] * sin)
        gq = gq * cos
        gk = gk * cos
    q_ref[...] = _head_norm_rope(qr, gq, q_sw).astype(BF16)
    k_ref[...] = _head_norm_rope(kv[:, :QK_PAD], gk, k_sw).astype(BF16)
    v_ref[...] = _with_ones(kv[:, QK_PAD:]).astype(BF16)


def _inproj(x, mods, l, lat, wts, tabs):
    n = x.shape[0]
    tt = TOK_TILE
    per_seq = 2048 // tt
    if lat:
        mod_map = lambda i: (l, 1 + i // per_seq, 0, 0)
        tab_map = lambda i: (i % per_seq, 0)
    else:
        mod_map = lambda i: (l, 0, 0, 0)
        tab_map = lambda i: (0, 0)
    row = lambda i: (i, 0)
    lw = lambda *shape: pl.BlockSpec((None,) + shape, lambda i: (l,) + (0,) * len(shape))
    tab = pl.BlockSpec((tt, HEAD_PAD), tab_map)
    row_shapes = (
        jax.ShapeDtypeStruct((n, SSM_WIDTH), BF16),
        jax.ShapeDtypeStruct((n, QK_PAD), BF16),
        jax.ShapeDtypeStruct((n, QK_PAD), BF16),
        jax.ShapeDtypeStruct((n, QK_PAD), BF16),
        jax.ShapeDtypeStruct((n, D_MODEL), BF16),
        jax.ShapeDtypeStruct((n, D_MODEL), BF16),
        jax.ShapeDtypeStruct((n, KV_LORA), F32),
        jax.ShapeDtypeStruct((n, QK_ROPE), F32),
    )
    row_specs = [pl.BlockSpec((tt, s.shape[1]), row) for s in row_shapes]
    ug_shape = jax.ShapeDtypeStruct((SSM_GROUPS, n // CHUNK, CW), BF16)
    ug_spec = pl.BlockSpec((SSM_GROUPS, TILE_CHUNKS, CW), lambda i: (0, i, 0))
    outs = pl.pallas_call(
        functools.partial(_inproj_kernel, lat),
        grid=(n // tt,),
        in_specs=[
            pl.BlockSpec((tt, D_MODEL), row),
            pl.BlockSpec((None, None, 1, 6 * D_MODEL), mod_map),
            lw(1, D_MODEL), lw(D_MODEL, 1024), lw(D_MODEL, D_MODEL), lw(D_MODEL, D_MODEL),
            lw(1, Q_LORA), lw(Q_LORA, QK_PAD), lw(Q_LORA, QK_PAD), lw(1, HEAD_PAD), lw(1, HEAD_PAD), lw(1, KV_LORA),
            lw(2 * KV_LORA, 2 * QK_PAD), lw(2 * KV_LORA, QK_PAD), lw(1, HEAD_PAD), lw(1, HEAD_PAD),
            tab, tab,
        ],
        out_specs=row_specs[:1] + [ug_spec] + row_specs[1:],
        out_shape=row_shapes[:1] + (ug_shape,) + row_shapes[1:],
        scratch_shapes=[pltpu.VMEM((LANE_TILES, tt, LANES), F32)],
        compiler_params=_params(("parallel",)),
        name="inproj_lat" if lat else "inproj_ctx",
    )(x, mods, wts["g_mix"], wts["w_a"], wts["w_ga"], wts["w_gb"], wts["g_qa"], wts["w_qb"], wts["w_qb_sw"],
      wts["g_q"], wts["g_q_sw"], wts["g_kva"], wts["w_k"], wts["w_k_sw"], wts["g_k"], wts["g_k_sw"], *tabs)
    return outs


def _cachekv_kernel(ckv_ref, kr_ref, wk_ref, gk_ref, k_ref, v_ref):
    kv = _kv_dot(ckv_ref[...].astype(BF16), kr_ref[...].astype(BF16), wk_ref)
    k_ref[...] = _head_norm_rope(kv[:, :QK_PAD], gk_ref[...], None).astype(BF16)
    v_ref[...] = _with_ones(kv[:, QK_PAD:]).astype(BF16)


def _cachekv(cache_ckv, cache_krope, wts):
    nb, _, past, _ = cache_ckv.shape
    return pl.pallas_call(
        _cachekv_kernel,
        grid=(DEPTH, nb),
        in_specs=[
            pl.BlockSpec((None, None, past, KV_LORA), lambda l, b: (b, l, 0, 0)),
            pl.BlockSpec((None, None, past, QK_ROPE), lambda l, b: (b, l, 0, 0)),
            pl.BlockSpec((None, 2 * KV_LORA, 2 * QK_PAD), lambda l, b: (l, 0, 0)),
            pl.BlockSpec((None, 1, HEAD_PAD), lambda l, b: (l, 0, 0)),
        ],
        out_specs=[
            pl.BlockSpec((None, None, past, QK_PAD), lambda l, b: (l, b, 0, 0)),
            pl.BlockSpec((None, None, past, QK_PAD), lambda l, b: (l, b, 0, 0)),
        ],
        out_shape=(jax.ShapeDtypeStruct((DEPTH, nb, past, QK_PAD), BF16),
                   jax.ShapeDtypeStruct((DEPTH, nb, past, QK_PAD), BF16)),
        compiler_params=_params(("arbitrary", "arbitrary")),
        name="cache_kv",
    )(cache_ckv, cache_krope, wts["w_k"], wts["g_k"])


def _attn_heads(q_ref, segs, o_ref, den_on_mxu):
    for h in range(N_HEADS):
        qh = q_ref[:, h * HEAD_PAD:(h + 1) * HEAD_PAD]
        scores = [lax.dot_general(qh, k_ref[:, h * HEAD_PAD:(h + 1) * HEAD_PAD], (((1,), (1,)), ((), ())),
                                  preferred_element_type=F32) for k_ref, _ in segs]
        m = scores[0].max(axis=-1, keepdims=True)
        for s in scores[1:]:
            m = jnp.maximum(m, s.max(axis=-1, keepdims=True))
        acc = None
        den = None
        for s, (_, v_ref) in zip(scores, segs):
            if den_on_mxu:
                p = jnp.exp2((s - m).astype(BF16))
                a = jnp.dot(p, v_ref[:, h * HEAD_PAD:(h + 1) * HEAD_PAD], preferred_element_type=F32)
            else:
                p = jnp.exp2(s - m)
                d = p.sum(axis=-1, keepdims=True)
                den = d if den is None else den + d
                a = jnp.dot(p.astype(BF16), v_ref[:, h * HEAD_PAD:h * HEAD_PAD + V_HEAD],
                            preferred_element_type=F32)
            acc = a if acc is None else acc + a
        if den_on_mxu:
            den = acc[:, V_HEAD:V_HEAD + 1]
        o_ref[:, h * V_HEAD:(h + 1) * V_HEAD] = (acc[:, :V_HEAD] / den).astype(o_ref.dtype)


def _attn_ctx_kernel(q_ref, k_ref, v_ref, o_ref):
    for b in range(q_ref.shape[0]):
        _attn_heads(q_ref.at[b], [(k_ref.at[b], v_ref.at[b])], o_ref.at[b], False)


def _attn_lat_kernel(q_ref, kc_ref, vc_ref, kl_ref, vl_ref, o_ref):
    _attn_heads(q_ref, [(kc_ref, vc_ref), (kl_ref, vl_ref)], o_ref, True)


def _attn_ctx(q, k, v, seq):
    n = q.shape[0]
    nb = n // seq
    per_step = 2
    blk = lambda w: pl.BlockSpec((per_step, seq, w), lambda b: (b, 0, 0))
    seqs = lambda a: a.reshape(nb, seq, a.shape[1])
    return pl.pallas_call(
        _attn_ctx_kernel,
        grid=(nb // per_step,),
        in_specs=[blk(QK_PAD), blk(QK_PAD), blk(QK_PAD)],
        out_specs=blk(ATTN_WIDTH),
        out_shape=jax.ShapeDtypeStruct((nb, seq, ATTN_WIDTH), BF16),
        compiler_params=_params(("parallel",)),
        name="attn_ctx",
    )(seqs(q), seqs(k), seqs(v)).reshape(n, ATTN_WIDTH)


def _attn_lat(q, k, v, kc, vc, l, seq, tq):
    n = q.shape[0]
    per = seq // tq
    past = kc.shape[2]
    return pl.pallas_call(
        _attn_lat_kernel,
        grid=(n // seq, per),
        in_specs=[
            pl.BlockSpec((tq, QK_PAD), lambda b, j: (b * per + j, 0)),
            pl.BlockSpec((None, None, past, QK_PAD), lambda b, j: (l, b, 0, 0)),
            pl.BlockSpec((None, None, past, QK_PAD), lambda b, j: (l, b, 0, 0)),
            pl.BlockSpec((seq, QK_PAD), lambda b, j: (b, 0)),
            pl.BlockSpec((seq, QK_PAD), lambda b, j: (b, 0)),
        ],
        out_specs=pl.BlockSpec((tq, ATTN_WIDTH), lambda b, j: (b * per + j, 0)),
        out_shape=jax.ShapeDtypeStruct((n, ATTN_WIDTH), BF16),
        compiler_params=_params(("parallel", "arbitrary")),
        name="attn_lat",
    )(q, kc, vc, k, v)


def _cmul(ar, ai, br, bi):
    return ar * br - ai * bi, ar * bi + ai * br


def _outer_rows(pw, m):
    return (pw[:, None, :] * m[None, :, :]).reshape(CW, SSM_STATE)


def _ssm_prep_kernel(are_ref, aim_ref, ldt_ref, bre_ref, bim_ref, cre_ref, cim_ref,
                     t_ref, et_ref, ft_ref, a16_ref):
    et_ref[...] = jnp.zeros(et_ref.shape, et_ref.dtype)
    ft_ref[...] = jnp.zeros(ft_ref.shape, ft_ref.dtype)
    row_id = lax.broadcasted_iota(jnp.int32, (CHUNK, SSM_STATE), 0)
    strip_lane = lax.broadcasted_iota(jnp.int32, (SSM_GROUP, CW), 1)
    nt = (((1,), (1,)), ((), ()))
    for gi in range(2):
        g = 2 * pl.program_id(1) + gi
        steps = _slot_step(row_id, g)
        lanes = slice(gi * SSM_STATE, (gi + 1) * SSM_STATE)
        rows = slice(gi * CW, (gi + 1) * CW)
        cre = cre_ref[gi]
        cim = cim_ref[gi]
        m_dir = []
        for d in range(2):
            a_re = are_ref[d, gi]
            a_im = aim_ref[d, gi]
            dt = jnp.exp(ldt_ref[d, gi])
            mag = jnp.exp(a_re * dt)
            abr = mag * jnp.cos(a_im * dt)
            abi = mag * jnp.sin(a_im * dt)
            den = a_re * a_re + a_im * a_im
            nr = abr - 1.0
            coef_re = (nr * a_re + abi * a_im) / den
            coef_im = (abi * a_re - nr * a_im) / den
            bbr, bbi = _cmul(coef_re, coef_im, bre_ref[d, gi], bim_ref[d, gi])

            squares = [(abr, abi)]
            for _ in range(4):
                squares.append(_cmul(*squares[-1], *squares[-1]))

            def power(n):
                pr = jnp.ones(n.shape, F32)
                pi = jnp.zeros(n.shape, F32)
                for b, (sr, si) in enumerate(squares):
                    nr_, ni_ = _cmul(pr, pi, sr, si)
                    hit = (n & (1 << b)) != 0
                    pr = jnp.where(hit, nr_, pr)
                    pi = jnp.where(hit, ni_, pi)
                return pr, pi

            to_end = (CHUNK - 1 - steps) if d == 0 else steps
            from_start = (steps + 1) if d == 0 else (CHUNK - steps)
            lag = row_id if d == 0 else (CHUNK - 1 - row_id)
            pr, pi = power(to_end)
            er = _outer_rows(pr, bbr) - _outer_rows(pi, bbi)
            ei = _outer_rows(pr, bbi) + _outer_rows(pi, bbr)
            et_ref[2 * d, rows, lanes] = er.astype(et_ref.dtype)
            et_ref[2 * d + 1, rows, lanes] = ei.astype(et_ref.dtype)
            pr, pi = power(from_start)
            ft_ref[2 * d, rows, lanes] = (_outer_rows(pr, cre) - _outer_rows(pi, cim)).astype(ft_ref.dtype)
            ft_ref[2 * d + 1, rows, lanes] = (-(_outer_rows(pr, cim) + _outer_rows(pi, cre))).astype(ft_ref.dtype)
            pr, pi = power(lag)
            zr = _outer_rows(pr, cre) - _outer_rows(pi, cim)
            zi = _outer_rows(pi, cre) + _outer_rows(pr, cim)
            m_dir.append(
                lax.dot_general(bbr, zr, nt, preferred_element_type=F32, precision=lax.Precision.HIGHEST)
                - lax.dot_general(bbi, zi, nt, preferred_element_type=F32, precision=lax.Precision.HIGHEST))
            p16r, p16i = squares[4]
            a16_ref[2 * d:2 * d + 1, lanes] = p16r
            a16_ref[2 * d + 1:2 * d + 2, lanes] = p16i
        mf, mb = m_dir
        edge = CW - SSM_GROUP
        low = mb + pltpu.roll(jnp.where(strip_lane < SSM_GROUP, mf, 0.0), edge, 1)
        high = jnp.where(strip_lane < edge, pltpu.roll(mf, edge, 1), 0.0)
        strip = jnp.concatenate([low, high], axis=1)
        turn = SSM_GROUP * (g & 7)
        for j in range(CHUNK):
            first = SSM_GROUP * (CHUNK - 1 - j)
            win = strip[:, first:first + CW]
            win = jnp.concatenate([pltpu.roll(win[:, LANES * h:LANES * (h + 1)], turn, 1)
                                   for h in range(CW // LANES)], axis=1)
            slot = (j & 8) + ((j + g) & 7)
            t_ref[gi, pl.ds(pl.multiple_of(SSM_GROUP * slot, SSM_GROUP), SSM_GROUP), :] = win.astype(t_ref.dtype)


def _ssm_prep(a_re, a_im, log_dt, b_re, b_im, c_re, c_im):
    g2 = SSM_GROUPS // 2
    a_spec = pl.BlockSpec((None, 2, 2, 1, SSM_STATE), lambda l, g: (l, 0, g, 0, 0))
    b_spec = pl.BlockSpec((None, 2, 2, SSM_GROUP, SSM_STATE), lambda l, g: (l, 0, g, 0, 0))
    c_spec = pl.BlockSpec((None, 2, SSM_GROUP, SSM_STATE), lambda l, g: (l, g, 0, 0))
    out_shapes = (
        jax.ShapeDtypeStruct((DEPTH, SSM_GROUPS, CW, CW), BF16),
        jax.ShapeDtypeStruct((DEPTH, g2, 4, 2 * CW, 2 * SSM_STATE), BF16),
        jax.ShapeDtypeStruct((DEPTH, g2, 4, 2 * CW, 2 * SSM_STATE), BF16),
        jax.ShapeDtypeStruct((DEPTH, g2, 4, 2 * SSM_STATE), F32),
    )
    return pl.pallas_call(
        _ssm_prep_kernel,
        grid=(DEPTH, g2),
        in_specs=[a_spec, a_spec,
                  pl.BlockSpec((None, 2, 2, 1, 1), lambda l, g: (l, 0, g, 0, 0)),
                  b_spec, b_spec, c_spec, c_spec],
        out_specs=[
            pl.BlockSpec((None, 2, CW, CW), lambda l, g: (l, g, 0, 0)),
            pl.BlockSpec((None, None, 4, 2 * CW, 2 * SSM_STATE), lambda l, g: (l, g, 0, 0, 0)),
            pl.BlockSpec((None, None, 4, 2 * CW, 2 * SSM_STATE), lambda l, g: (l, g, 0, 0, 0)),
            pl.BlockSpec((None, None, 4, 2 * SSM_STATE), lambda l, g: (l, g, 0, 0)),
        ],
        out_shape=out_shapes,
        compiler_params=_params(("arbitrary", "arbitrary")),
        name="ssm_prep",
    )(a_re.reshape(DEPTH, 2, SSM_GROUPS, 1, SSM_STATE), a_im.reshape(DEPTH, 2, SSM_GROUPS, 1, SSM_STATE),
      log_dt.reshape(DEPTH, 2, SSM_GROUPS, 1, 1),
      jnp.swapaxes(b_re, -1, -2), jnp.swapaxes(b_im, -1, -2), c_re, c_im)


def _ssm_kernel(nchunk, nb, u_ref, t_ref, et_ref, ft_ref, a16_ref, h0_ref, y_ref, fin_ref, e_ref, h_ref):
    u0 = u_ref[0]
    u1 = u_ref[1]
    for x in range(4):
        e_ref[x] = (jnp.dot(u0, et_ref[x, 0:CW, :], preferred_element_type=F32)
                    + jnp.dot(u1, et_ref[x, CW:2 * CW, :], preferred_element_type=F32))
    afr, afi, abr, abi = (a16_ref[i:i + 1, :] for i in range(4))

    def body(i, carry):
        fr, fi, br, bi = carry
        kf = pl.ds(i, nb, stride=nchunk)
        kb = pl.ds(nchunk - 1 - i, nb, stride=nchunk)
        h_ref[0, kf, :] = fr
        h_ref[1, kf, :] = fi
        h_ref[2, kb, :] = br
        h_ref[3, kb, :] = bi
        return (afr * fr - afi * fi + e_ref[0, kf, :], afr * fi + afi * fr + e_ref[1, kf, :],
                abr * br - abi * bi + e_ref[2, kb, :], abr * bi + abi * br + e_ref[3, kb, :])

    fin = lax.fori_loop(0, nchunk, body, tuple(h0_ref[i] for i in range(4)), unroll=4)
    for i in range(4):
        fin_ref[i] = fin[i]
    nt = (((1,), (1,)), ((), ()))
    acc = None
    for x in range(4):
        part = lax.dot_general(h_ref[x].astype(BF16), ft_ref[x], nt, preferred_element_type=F32)
        acc = part if acc is None else acc + part
    y_ref[0] = jnp.dot(u0, t_ref[0], preferred_element_type=F32) + acc[:, :CW]
    y_ref[1] = jnp.dot(u1, t_ref[1], preferred_element_type=F32) + acc[:, CW:]


def _ssm_mix(ug, nb, seq, h0_re, h0_im, prep, l):
    tmat, et, ft, a16 = prep
    nchunk = seq // CHUNK
    r = ug.shape[1]
    g2 = SSM_GROUPS // 2
    width = SSM_GROUPS * SSM_STATE
    h0 = jnp.stack([h[:, d].reshape(nb, width).astype(F32) for d in range(2) for h in (h0_re, h0_im)], axis=0)
    pair = pl.BlockSpec((2, r, CW), lambda g: (g, 0, 0))
    mats = pl.BlockSpec((None, None, 4, 2 * CW, 2 * SSM_STATE), lambda g: (l, g, 0, 0, 0))
    state = pl.BlockSpec((4, nb, 2 * SSM_STATE), lambda g: (0, 0, g))
    y, fin = pl.pallas_call(
        functools.partial(_ssm_kernel, nchunk, nb),
        grid=(g2,),
        in_specs=[pair, pl.BlockSpec((None, 2, CW, CW), lambda g: (l, g, 0, 0)), mats, mats,
                  pl.BlockSpec((None, None, 4, 2 * SSM_STATE), lambda g: (l, g, 0, 0)), state],
        out_specs=[pair, state],
        out_shape=(jax.ShapeDtypeStruct((SSM_GROUPS, r, CW), F32), jax.ShapeDtypeStruct((4, nb, width), F32)),
        scratch_shapes=[pltpu.VMEM((4, r, 2 * SSM_STATE), F32), pltpu.VMEM((4, r, 2 * SSM_STATE), F32)],
        compiler_params=_params(("parallel",)),
        name="ssm",
    )(ug, tmat, et, ft, a16, h0)
    fin = fin.reshape(2, 2, nb, SSM_GROUPS, SSM_STATE)
    return y, fin


def _gelu_tanh(x):
    return 0.5 * x * (1.0 + jnp.tanh(math.sqrt(2.0 / math.pi) * (x + 0.044715 * (x * x * x))))


def _route(aff, bias):
    a = [aff[e:e + 1, :] for e in range(N_EXPERTS)]
    s = [a[e] + bias[e:e + 1, :] for e in range(N_EXPERTS)]
    keep = []
    for e in range(N_EXPERTS):
        g0 = (e // EXPERTS_PER_GROUP) * EXPERTS_PER_GROUP
        rank = None
        for j in range(g0, g0 + EXPERTS_PER_GROUP):
            if j == e:
                continue
            beats = (s[j] >= s[e]) if j < e else (s[j] > s[e])
            r = jnp.where(beats, 1.0, 0.0)
            rank = r if rank is None else rank + r
        keep.append(rank < 1.5)
    score = []
    for g in range(N_GROUPS):
        tot = None
        for e in range(g * EXPERTS_PER_GROUP, (g + 1) * EXPERTS_PER_GROUP):
            v = jnp.where(keep[e], s[e], 0.0)
            tot = v if tot is None else tot + v
        score.append(tot)
    gates = []
    picked = []
    for g in range(N_GROUPS):
        lost = None
        for j in range(N_GROUPS):
            if j == g:
                continue
            beats = (score[j] >= score[g]) if j < g else (score[j] > score[g])
            r = jnp.where(beats, 1.0, 0.0)
            lost = r if lost is None else lost + r
        chosen = lost < 0.5
        picked.append(jnp.where(chosen, 1.0, 0.0))
        w = [jnp.where(keep[e], a[e], 0.0) for e in range(g * EXPERTS_PER_GROUP, (g + 1) * EXPERTS_PER_GROUP)]
        tot = w[0] + w[1] + w[2] + w[3]
        for v in w:
            gates.append(jnp.where(chosen, v / tot, 0.0))
    return gates, picked


def _merge_kernel(x_ref, yg_ref, u_ref, at_ref, sga_ref, sgb_ref, mod_ref, d_ref, wglu_ref, bglu_ref, wsp_ref,
                  wap_ref, wo_ref, gffn_ref, wr_ref, rb_ref, tri_ref, ramp_ref,
                  xo_ref, hs_ref, mt_ref, ms_ref, of_ref, tok_ref):
    mod = mod_ref[...]
    gate1 = mod[:, 2 * D_MODEL:3 * D_MODEL]
    shift2 = mod[:, 3 * D_MODEL:4 * D_MODEL]
    scale2 = mod[:, 4 * D_MODEL:5 * D_MODEL]
    _from_chunk_rows(yg_ref, tok_ref)
    ys = jnp.concatenate([tok_ref[j] for j in range(LANE_TILES)], axis=1)
    y = _gelu_tanh(ys + d_ref[...] * u_ref[...].astype(F32))
    glu = jnp.dot(y.astype(BF16), wglu_ref[...], preferred_element_type=F32) + bglu_ref[...]
    y = (y * _sigmoid(glu)).astype(BF16)
    a_out = jnp.dot(y, wsp_ref[...], preferred_element_type=F32)
    b_out = jnp.dot(at_ref[...], wap_ref[...], preferred_element_type=F32)
    m = (sga_ref[...].astype(F32) * a_out + sgb_ref[...].astype(F32) * b_out).astype(BF16)
    x = x_ref[...] + gate1 * jnp.dot(m, wo_ref[...], preferred_element_type=F32)
    xo_ref[...] = x
    h2 = (_rms(x, D_MODEL) * gffn_ref[...]) * (1.0 + scale2) + shift2
    nt = (((1,), (1,)), ((), ()))
    h2_hi = h2.astype(BF16)
    h2_lo = (h2 - h2_hi.astype(F32)).astype(BF16)
    wr = wr_ref[...]
    wr_hi = wr.astype(BF16)
    wr_lo = (wr - wr_hi.astype(F32)).astype(BF16)
    logits = (lax.dot_general(wr_hi, h2_hi, nt, preferred_element_type=F32)
              + lax.dot_general(wr_hi, h2_lo, nt, preferred_element_type=F32)
              + lax.dot_general(wr_lo, h2_hi, nt, preferred_element_type=F32))
    gates, picked = _route(_sigmoid(logits), rb_ref[...])
    n = x.shape[0]
    onehot = jnp.concatenate(picked + [jnp.zeros((8 - N_GROUPS, n), F32)], axis=0)
    before = jnp.dot(onehot.astype(BF16), tri_ref[...], preferred_element_type=F32)
    count = [jnp.sum(p, axis=-1, keepdims=True) for p in picked]
    start = [jnp.zeros((1, 1), F32)]
    for g in range(N_GROUPS - 1):
        start.append(start[-1] + count[g])
    pos = sum(picked[g] * (start[g] + before[g:g + 1, :]) for g in range(N_GROUPS))
    gid = sum(float(g) * picked[g] for g in range(1, N_GROUPS))
    perm = jnp.where(ramp_ref[...] == pos, 1.0, 0.0).astype(BF16)
    hs_ref[...] = jnp.dot(perm, h2_hi, preferred_element_type=F32).astype(BF16)
    own = [sum(gates[EXPERTS_PER_GROUP * g + e] for g in range(N_GROUPS)) for e in range(EXPERTS_PER_GROUP)]
    meta = jnp.concatenate(own + [gid, pos, jnp.zeros((LANES - EXPERTS_PER_GROUP - 2, n), F32)], axis=0).T
    mt_ref[...] = meta
    hi = meta.astype(BF16)
    lo = (meta - hi.astype(F32)).astype(BF16)
    ms_ref[...] = (jnp.dot(perm, hi, preferred_element_type=F32)
                   + jnp.dot(perm, lo, preferred_element_type=F32))
    lane = lax.broadcasted_iota(jnp.int32, (8, LANES), 1)
    offs = sum(jnp.where(lane == g, start[g], 0.0) for g in range(1, N_GROUPS)) + jnp.where(lane == N_GROUPS, n, 0.0)
    of_ref[...] = offs.astype(jnp.int32)


def _merge(x, yg, u, attn, sga, sgb, mods, l, lat, wts):
    n = x.shape[0]
    tt = TOK_TILE
    per_seq = 2048 // tt
    mod_map = (lambda i: (l, 1 + i // per_seq, 0, 0)) if lat else (lambda i: (l, 0, 0, 0))
    row = lambda i: (i, 0)
    lw = lambda *shape: pl.BlockSpec((None,) + shape, lambda i: (l,) + (0,) * len(shape))
    return pl.pallas_call(
        _merge_kernel,
        grid=(n // tt,),
        in_specs=[
            pl.BlockSpec((tt, D_MODEL), row),
            pl.BlockSpec((SSM_GROUPS, TILE_CHUNKS, CW), lambda i: (0, i, 0)),
            pl.BlockSpec((tt, SSM_WIDTH), row),
            pl.BlockSpec((tt, ATTN_WIDTH), row), pl.BlockSpec((tt, D_MODEL), row), pl.BlockSpec((tt, D_MODEL), row),
            pl.BlockSpec((None, None, 1, 6 * D_MODEL), mod_map),
            lw(1, SSM_WIDTH), lw(SSM_WIDTH, SSM_WIDTH), lw(1, SSM_WIDTH), lw(SSM_WIDTH, D_MODEL),
            lw(ATTN_WIDTH, D_MODEL), lw(D_MODEL, D_MODEL), lw(1, D_MODEL),
            pl.BlockSpec((N_EXPERTS, D_MODEL), lambda i: (0, 0)),
            pl.BlockSpec((N_EXPERTS, 1), lambda i: (0, 0)),
            pl.BlockSpec((tt, tt), lambda i: (0, 0)), pl.BlockSpec((tt, tt), lambda i: (0, 0)),
        ],
        out_specs=[pl.BlockSpec((tt, D_MODEL), row), pl.BlockSpec((tt, D_MODEL), row),
                   pl.BlockSpec((tt, LANES), row), pl.BlockSpec((tt, LANES), row),
                   pl.BlockSpec((None, 8, LANES), lambda i: (i, 0, 0))],
        out_shape=(jax.ShapeDtypeStruct((n, D_MODEL), F32), jax.ShapeDtypeStruct((n, D_MODEL), BF16),
                   jax.ShapeDtypeStruct((n, LANES), F32), jax.ShapeDtypeStruct((n, LANES), F32),
                   jax.ShapeDtypeStruct((n // tt, 8, LANES), jnp.int32)),
        scratch_shapes=[pltpu.VMEM((LANE_TILES, tt, LANES), F32)],
        compiler_params=_params(("parallel",)),
        name="merge_lat" if lat else "merge_ctx",
    )(x, yg, u, attn, sga, sgb, mods, wts["ssm_d"], wts["w_glu"], wts["b_glu"], wts["w_ssm_proj"],
      wts["w_attn_proj"], wts["w_o"], wts["g_ffn"], wts["w_router_t"], wts["router_bias"],
      jnp.asarray(np.triu(np.ones((tt, tt), np.float32), 1), BF16),
      jnp.asarray(np.broadcast_to(np.arange(tt, dtype=np.float32)[:, None], (tt, tt))))


def _moe_kernel(offs_ref, x_ref, hs_ref, ms_ref, mt_ref, mod_ref, wg_ref, wu_ref, wd_ref, o_ref, acc_ref):
    tile = pl.program_id(0)
    acc_ref[...] = jnp.zeros(acc_ref.shape, F32)
    row_id = lax.broadcasted_iota(jnp.int32, (MOE_ROWS, 1), 0)

    def group_body(g, carry):
        seg_start = offs_ref[(N_GROUPS + 1) * tile + g]
        seg_end = offs_ref[(N_GROUPS + 1) * tile + g + 1]
        first = (seg_start >> 4) << 4
        span = jnp.where(seg_end > seg_start, seg_end - first, 0)
        n_pass = sum((span > k * MOE_ROWS).astype(jnp.int32) for k in range(pl.cdiv(TOK_TILE + 16, MOE_ROWS)))

        def pass_body(j, c):
            lo = first + MOE_ROWS * j
            s = pl.multiple_of(jnp.minimum(lo, TOK_TILE - MOE_ROWS), 16)
            rows = pl.ds(s, MOE_ROWS)
            meta = ms_ref[rows, :]
            mine = (meta[:, EXPERTS_PER_GROUP:EXPERTS_PER_GROUP + 1] == g.astype(F32)) & (row_id + s >= lo)
            own = jnp.where(mine, 1.0, 0.0)
            hs = hs_ref[rows, :]
            acts = []
            for e in range(EXPERTS_PER_GROUP):
                expert = EXPERTS_PER_GROUP * g + e
                hg = jnp.dot(hs, wg_ref[expert], preferred_element_type=F32)
                hu = jnp.dot(hs, wu_ref[expert], preferred_element_type=F32)
                acts.append(((hg * _sigmoid(hg)) * hu * (meta[:, e:e + 1] * own)).astype(BF16))
            acc_ref[rows, :] += jnp.dot(jnp.concatenate(acts, axis=1), wd_ref[g], preferred_element_type=F32)
            return c

        lax.fori_loop(0, n_pass, pass_body, 0)
        return carry

    lax.fori_loop(0, N_GROUPS, group_body, 0)

    slot = lax.broadcasted_iota(jnp.int32, (TOK_TILE, TOK_TILE), 1).astype(F32)
    unsort = jnp.where(mt_ref[:, EXPERTS_PER_GROUP + 1:EXPERTS_PER_GROUP + 2] == slot, 1.0, 0.0).astype(BF16)
    y = jnp.dot(unsort, acc_ref[...].astype(BF16), preferred_element_type=F32)
    o_ref[...] = x_ref[...] + mod_ref[:, 5 * D_MODEL:6 * D_MODEL] * y


def _moe(x, hs, meta_tok, meta_sorted, offs, mods, l, lat, wts):
    n = x.shape[0]
    tt = TOK_TILE
    per_seq = 2048 // tt
    mod_map = (lambda i, o: (l, 1 + i // per_seq, 0, 0)) if lat else (lambda i, o: (l, 0, 0, 0))
    row = lambda i, o: (i, 0)
    width = EXPERTS_PER_GROUP * D_EXPERT
    resident = lambda *shape: pl.BlockSpec((None,) + shape, lambda i, o: (l,) + (0,) * len(shape),
                                           pipeline_mode=pl.Buffered(1))
    grid_spec = pltpu.PrefetchScalarGridSpec(
        num_scalar_prefetch=1,
        grid=(n // tt,),
        in_specs=[
            pl.BlockSpec((tt, D_MODEL), row), pl.BlockSpec((tt, D_MODEL), row), pl.BlockSpec((tt, LANES), row),
            pl.BlockSpec((tt, LANES), row),
            pl.BlockSpec((None, None, 1, 6 * D_MODEL), mod_map),
            resident(N_EXPERTS, D_MODEL, D_EXPERT), resident(N_EXPERTS, D_MODEL, D_EXPERT),
            resident(N_GROUPS, width, D_MODEL),
        ],
        out_specs=pl.BlockSpec((tt, D_MODEL), row),
        scratch_shapes=[pltpu.VMEM((tt, D_MODEL), F32)],
    )
    return pl.pallas_call(
        _moe_kernel,
        grid_spec=grid_spec,
        out_shape=jax.ShapeDtypeStruct((n, D_MODEL), F32),
        compiler_params=_params(("arbitrary",), MOE_VMEM_LIMIT),
        name="moe_lat" if lat else "moe_ctx",
    )(offs[:, 0, :N_GROUPS + 1].reshape(-1), x, hs, meta_sorted, meta_tok, mods, wts["w_gate"], wts["w_up"],
      wts["w_down"])


def _pad_heads(w, width):
    lead = w.shape[:-1]
    w = w.reshape(lead + (N_HEADS, width))
    w = jnp.pad(w, [(0, 0)] * len(lead) + [(0, 0), (0, HEAD_PAD - width)])
    return w.reshape(lead + (N_HEADS * HEAD_PAD,))


def _swap_partners(w):
    lead = w.shape[:-1]
    w = w.reshape(lead + (-1, HEAD_PAD))
    half = ROPE_AXIS // 2
    a, b = QK_NOPE, QK_NOPE + ROPE_AXIS
    parts = [w[..., :a], w[..., a + half:b], w[..., a:a + half], w[..., b + half:b + 2 * half], w[..., b:b + half],
             w[..., b + 2 * half:]]
    return jnp.concatenate(parts, axis=-1).reshape(lead + (-1,))


def _layout_weights(w_in, g_qa, w_qb, g_q, g_kva, w_kvb, g_k, g_mix, w_attn_proj, ssm_d, ssm_w_glu, ssm_b_glu,
                    ssm_w_proj, w_o, g_ffn, w_router, router_bias, w_gate, w_up, w_down):
    n_a = SSM_WIDTH + Q_LORA + KV_LORA + QK_ROPE
    w_a = jnp.pad(w_in[:, :, :n_a], ((0, 0), (0, 0), (0, 1024 - n_a)))
    kv = w_kvb.reshape(DEPTH, KV_LORA, N_HEADS, QK_NOPE + V_HEAD)
    k_cols = _pad_heads(kv[..., :QK_NOPE].reshape(DEPTH, KV_LORA, N_HEADS * QK_NOPE), QK_NOPE)
    v_cols = _pad_heads(kv[..., QK_NOPE:].reshape(DEPTH, KV_LORA, ATTN_WIDTH), V_HEAD)
    place = np.zeros((KV_LORA, 2 * QK_PAD), np.float32)
    for h in range(N_HEADS):
        for r in range(QK_ROPE):
            place[r, h * HEAD_PAD + QK_NOPE + r] = 1.0
    w_k = jnp.concatenate([jnp.concatenate([k_cols, v_cols], axis=-1),
                           jnp.broadcast_to(jnp.asarray(place), (DEPTH,) + place.shape)], axis=1)
    w_qb_pad = _pad_heads(w_qb, QK_HEAD).astype(BF16)
    w_k = w_k.astype(BF16)
    pad_gain = lambda g: jnp.pad(g, ((0, 0), (0, HEAD_PAD - QK_HEAD))).reshape(DEPTH, 1, HEAD_PAD)
    vec = lambda g: g.reshape(DEPTH, 1, -1)
    return {
        "g_mix": vec(g_mix), "w_a": w_a.astype(BF16),
        "w_ga": w_in[:, :, n_a:n_a + D_MODEL].astype(BF16), "w_gb": w_in[:, :, n_a + D_MODEL:].astype(BF16),
        "g_qa": vec(g_qa), "w_qb": w_qb_pad, "w_qb_sw": _swap_partners(w_qb_pad),
        "g_q": pad_gain(g_q) * Q_SCALE, "g_q_sw": _swap_partners(pad_gain(g_q)) * Q_SCALE,
        "g_kva": vec(g_kva), "w_k": w_k, "w_k_sw": _swap_partners(w_k[:, :, :QK_PAD]),
        "g_k": pad_gain(g_k), "g_k_sw": _swap_partners(pad_gain(g_k)),
        "ssm_d": vec(ssm_d), "w_glu": ssm_w_glu.astype(BF16), "b_glu": vec(ssm_b_glu),
        "w_ssm_proj": ssm_w_proj.astype(BF16), "w_attn_proj": w_attn_proj.astype(BF16), "w_o": w_o.astype(BF16),
        "g_ffn": vec(g_ffn), "w_router_t": w_router.T, "router_bias": router_bias.reshape(N_EXPERTS, 1),
        "w_gate": w_gate.astype(BF16), "w_up": w_up.astype(BF16),
        "w_down": w_down.astype(BF16).reshape(DEPTH, N_GROUPS, EXPERTS_PER_GROUP * D_EXPERT, D_MODEL),
    }


def _rope_tables(n_tokens):
    pos = np.arange(n_tokens)
    inv = 1.0 / (ROPE_BASE ** (np.arange(ROPE_AXIS // 2, dtype=np.float32) * 2.0 / ROPE_AXIS))
    ang = np.zeros((n_tokens, HEAD_PAD), np.float32)
    half = ROPE_AXIS // 2
    row = (pos // GRID_W).astype(np.float32)[:, None] * inv
    col = (pos % GRID_W).astype(np.float32)[:, None] * inv
    first = np.zeros((HEAD_PAD,), bool)
    second = np.zeros((HEAD_PAD,), bool)
    for base, a in ((QK_NOPE, row), (QK_NOPE + ROPE_AXIS, col)):
        ang[:, base:base + half] = a
        ang[:, base + half:base + 2 * half] = a
        first[base:base + half] = True
        second[base + half:base + 2 * half] = True
    ang = jnp.asarray(ang)
    cos = jnp.where(jnp.asarray(first | second), jnp.cos(ang), 1.0)
    sin = jnp.sin(ang)
    return cos, jnp.where(jnp.asarray(first), -sin, jnp.where(jnp.asarray(second), sin, 0.0))


def kernel(x_prompt, x_sample, c, cache_ckv, cache_krope, state_ssm_re, state_ssm_im, c_ctx, w_ada, b_ada, g_mix, w_in, g_qa, w_qb, g_q, g_kva, w_kvb, g_k, w_attn_proj, ssm_a_re, ssm_a_im, ssm_log_dt, ssm_b_re, ssm_b_im, ssm_c_re, ssm_c_im, ssm_d, ssm_w_glu, ssm_b_glu, ssm_w_proj, w_o, g_ffn, w_router, router_bias, w_gate, w_up, w_down):
    nb_c, seq_c, _ = x_prompt.shape
    nb_l, seq_l, _ = x_sample.shape
    wts = _layout_weights(w_in, g_qa, w_qb, g_q, g_kva, w_kvb, g_k, g_mix, w_attn_proj, ssm_d, ssm_w_glu,
                          ssm_b_glu, ssm_w_proj, w_o, g_ffn, w_router, router_bias, w_gate, w_up, w_down)
    cvec = jnp.concatenate([c_ctx[None, :], c, jnp.zeros((8 - 1 - nb_l, D_MODEL), F32)], axis=0)
    mods = _mods(cvec, w_ada, b_ada).reshape(DEPTH, 8, 1, 6 * D_MODEL)
    prep = _ssm_prep(ssm_a_re, ssm_a_im, ssm_log_dt, ssm_b_re, ssm_b_im, ssm_c_re, ssm_c_im)
    tabs = _rope_tables(seq_l)
    kc, vc = _cachekv(cache_ckv, cache_krope, wts)

    xp = x_prompt.reshape(nb_c * seq_c, D_MODEL)
    xs = x_sample.reshape(nb_l * seq_l, D_MODEL)
    zeros_c = jnp.zeros((nb_c, 2, SSM_GROUPS, SSM_STATE), F32)
    ckvs, kropes, fins = [], [], []
    for l in range(DEPTH):
        u, ug, q, k, v, sga, sgb, ckv, krope = _inproj(xp, mods, l, False, wts, tabs)
        ckvs.append(ckv)
        kropes.append(krope)
        yg, fin = _ssm_mix(ug, nb_c, seq_c, zeros_c, zeros_c, prep, l)
        fins.append(fin)
        attn = _attn_ctx(q, k, v, seq_c)
        xp, *routed = _merge(xp, yg, u, attn, sga, sgb, mods, l, False, wts)
        xp = _moe(xp, *routed, mods, l, False, wts)

        u, ug, q, k, v, sga, sgb, _, _ = _inproj(xs, mods, l, True, wts, tabs)
        yg, _ = _ssm_mix(ug, nb_l, seq_l, state_ssm_re[:, l], state_ssm_im[:, l], prep, l)
        attn = _attn_lat(q, k, v, kc, vc, l, seq_l, 256)
        xs, *routed = _merge(xs, yg, u, attn, sga, sgb, mods, l, True, wts)
        xs = _moe(xs, *routed, mods, l, True, wts)

    new_ckv = jnp.stack(ckvs, axis=0).reshape(DEPTH, nb_c, seq_c, KV_LORA).transpose(1, 0, 2, 3)
    new_krope = jnp.stack(kropes, axis=0).reshape(DEPTH, nb_c, seq_c, QK_ROPE).transpose(1, 0, 2, 3)
    fin = jnp.stack(fins, axis=0)
    new_re = jnp.transpose(fin[:, :, 0], (2, 0, 1, 3, 4))
    new_im = jnp.transpose(fin[:, :, 1], (2, 0, 1, 3, 4))
    return (xp.reshape(nb_c, seq_c, D_MODEL), xs.reshape(nb_l, seq_l, D_MODEL), new_ckv, new_krope, new_re, new_im)
```

```python
import functools
import math

import jax
import jax.numpy as jnp
import numpy as np
from jax import lax
from jax.experimental import pallas as pl
from jax.experimental.pallas import tpu as pltpu

F32 = jnp.float32
BF16 = jnp.bfloat16

D_MODEL = 1024
DEPTH = 4
GRID_W = 64
SSM_WIDTH = 512
SSM_GROUP = 16
SSM_GROUPS = 32
SSM_STATE = 64
N_HEADS = 8
QK_NOPE = 64
QK_ROPE = 32
QK_HEAD = 96
V_HEAD = 64
Q_LORA = 256
KV_LORA = 128
ATTN_WIDTH = 512
ROPE_AXIS = 16
ROPE_BASE = 10000.0
N_EXPERTS = 16
N_GROUPS = 4
EXPERTS_PER_GROUP = 4
D_EXPERT = 256
EPS = 1e-6

LANES = 128
HEAD_PAD = LANES
QK_PAD = N_HEADS * HEAD_PAD
CHUNK = 16
CW = CHUNK * SSM_GROUP
TOK_TILE = 512
TILE_CHUNKS = TOK_TILE // CHUNK
LANE_TILES = SSM_WIDTH // LANES
GROUPS_PER_TILE = LANES // SSM_GROUP
Q_SCALE = math.log2(math.e) / math.sqrt(QK_HEAD)
MOE_ROWS = 192
VMEM_LIMIT = 48 * 1024 * 1024
MOE_VMEM_LIMIT = 56 * 1024 * 1024


def _params(sem, vmem=VMEM_LIMIT):
    return pltpu.CompilerParams(dimension_semantics=sem, vmem_limit_bytes=vmem)


def _sigmoid(x):
    return 1.0 / (1.0 + jnp.exp(-x))


def _rms(x, n):
    return x * lax.rsqrt(jnp.sum(x * x, axis=-1, keepdims=True) * (1.0 / n) + EPS)


def _mods_kernel(c_ref, w_ref, b_ref, o_ref):
    c = c_ref[...]
    cs = (c * _sigmoid(c)).astype(BF16)
    o_ref[...] = jnp.dot(cs, w_ref[...].astype(BF16), preferred_element_type=F32) + b_ref[...]


def _mods(cvec, w_ada, b_ada):
    nb = 1536
    return pl.pallas_call(
        _mods_kernel,
        grid=(DEPTH, 6 * D_MODEL // nb),
        in_specs=[
            pl.BlockSpec((8, D_MODEL), lambda l, j: (0, 0)),
            pl.BlockSpec((None, D_MODEL, nb), lambda l, j: (l, 0, j)),
            pl.BlockSpec((None, 1, nb), lambda l, j: (l, 0, j)),
        ],
        out_specs=pl.BlockSpec((None, 8, nb), lambda l, j: (l, 0, j)),
        out_shape=jax.ShapeDtypeStruct((DEPTH, 8, 6 * D_MODEL), F32),
        compiler_params=_params(("arbitrary", "arbitrary")),
        name="mods",
    )(cvec, w_ada, b_ada.reshape(DEPTH, 1, 6 * D_MODEL))


def _slot_step(slot, g):
    return (slot & 8) + ((slot - g) & 7)


def _lane_segment():
    return lax.broadcasted_iota(jnp.int32, (TILE_CHUNKS, LANES), 1) // SSM_GROUP


def _to_chunk_rows(tok_ref, ug_ref):
    seg = _lane_segment()
    for j in range(LANE_TILES):
        for half in range(CHUNK // 8):
            rolled = []
            for r in range(8):
                v = tok_ref[j, pl.ds(8 * half + r, TILE_CHUNKS, stride=CHUNK), :]
                rolled.append(pltpu.roll(v, SSM_GROUP * r, 1) if r else v)
            for gg in range(GROUPS_PER_TILE):
                acc = rolled[(0 - gg) % 8]
                for m in range(1, 8):
                    acc = jnp.where(seg == m, rolled[(m - gg) % 8], acc)
                ug_ref[GROUPS_PER_TILE * j + gg, :, LANES * half:LANES * (half + 1)] = acc.astype(ug_ref.dtype)


def _from_chunk_rows(yg_ref, tok_ref):
    seg = _lane_segment()
    for j in range(LANE_TILES):
        for half in range(CHUNK // 8):
            ys = [yg_ref[GROUPS_PER_TILE * j + gg, :, LANES * half:LANES * (half + 1)]
                  for gg in range(GROUPS_PER_TILE)]
            for r in range(8):
                acc = ys[(0 - r) % 8]
                for m in range(1, 8):
                    acc = jnp.where(seg == m, ys[(m - r) % 8], acc)
                out = pltpu.roll(acc, LANES - SSM_GROUP * r, 1) if r else acc
                tok_ref[j, pl.ds(8 * half + r, TILE_CHUNKS, stride=CHUNK), :] = out


def _head_norm_rope(t, gain, swapped):
    outs = []
    for h in range(N_HEADS):
        lanes = slice(h * HEAD_PAD, (h + 1) * HEAD_PAD)
        th = t[:, lanes]
        scale = lax.rsqrt(jnp.sum(th * th, axis=-1, keepdims=True) * (1.0 / QK_HEAD) + EPS)
        y = th * gain
        if swapped is not None:
            y = y + swapped[0][:, lanes] * swapped[1]
        outs.append(y * scale)
    return jnp.concatenate(outs, axis=1)


def _with_ones(v):
    lane = lax.broadcasted_iota(jnp.int32, (1, v.shape[1]), 1)
    return jnp.where((lane & (HEAD_PAD - 1)) == V_HEAD, 1.0, v)


def _kv_dot(ckvn, krp, wk_ref):
    return (jnp.dot(ckvn, wk_ref[0:KV_LORA, :], preferred_element_type=F32)
            + jnp.dot(krp, wk_ref[KV_LORA:KV_LORA + krp.shape[1], :], preferred_element_type=F32))


def _inproj_kernel(use_rope, x_ref, mod_ref, gmix_ref, wa_ref, wga_ref, wgb_ref, gqa_ref, wqb_ref, wqbs_ref, gq_ref,
                   gqs_ref, gkva_ref, wk_ref, wks_ref, gk_ref, gks_ref, cos_ref, sin_ref,
                   u_ref, ug_ref, q_ref, k_ref, v_ref, sga_ref, sgb_ref, ckv_ref, kr_ref, tok_ref):
    mod = mod_ref[...]
    shift1 = mod[:, 0:D_MODEL]
    gain1 = gmix_ref[...] * (1.0 + mod[:, D_MODEL:2 * D_MODEL])
    h = (_rms(x_ref[...], D_MODEL) * gain1 + shift1).astype(BF16)
    za = jnp.dot(h, wa_ref[...], preferred_element_type=F32)
    sga_ref[...] = _sigmoid(jnp.dot(h, wga_ref[...], preferred_element_type=F32)).astype(BF16)
    sgb_ref[...] = _sigmoid(jnp.dot(h, wgb_ref[...], preferred_element_type=F32)).astype(BF16)
    u_ref[...] = za[:, :SSM_WIDTH].astype(BF16)
    for j in range(LANE_TILES):
        tok_ref[j] = za[:, LANES * j:LANES * (j + 1)]
    _to_chunk_rows(tok_ref, ug_ref)
    qa = za[:, SSM_WIDTH:SSM_WIDTH + Q_LORA]
    ckv = za[:, SSM_WIDTH + Q_LORA:SSM_WIDTH + Q_LORA + KV_LORA]
    krp = za[:, SSM_WIDTH + Q_LORA + KV_LORA:]
    qn = (_rms(qa, Q_LORA) * gqa_ref[...]).astype(BF16)
    qr = jnp.dot(qn, wqb_ref[...], preferred_element_type=F32)
    ckvn = _rms(ckv, KV_LORA) * gkva_ref[...]
    ckv_ref[...] = ckvn
    kr_ref[...] = krp[:, :QK_ROPE]
    ckvn = ckvn.astype(BF16)
    krp = krp.astype(BF16)
    kv = _kv_dot(ckvn, krp, wk_ref)
    gq = gq_ref[...]
    gk = gk_ref[...]
    q_sw = k_sw = None
    if use_rope:
        cos = cos_ref[...]
        sin = sin_ref[...]
        q_sw = (jnp.dot(qn, wqbs_ref[...], preferred_element_type=F32), gqs_ref[...] * sin)
        k_sw = (_kv_dot(ckvn, krp, wks_ref), gks_ref[...] * sin)
        gq = gq * cos
        gk = gk * cos
    q_ref[...] = _head_norm_rope(qr, gq, q_sw).astype(BF16)
    k_ref[...] = _head_norm_rope(kv[:, :QK_PAD], gk, k_sw).astype(BF16)
    v_ref[...] = _with_ones(kv[:, QK_PAD:]).astype(BF16)


def _inproj(x, mods, l, lat, wts, tabs):
    n = x.shape[0]
    tt = TOK_TILE
    per_seq = 2048 // tt
    if lat:
        mod_map = lambda i: (l, 1 + i // per_seq, 0, 0)
        tab_map = lambda i: (i % per_seq, 0)
    else:
        mod_map = lambda i: (l, 0, 0, 0)
        tab_map = lambda i: (0, 0)
    row = lambda i: (i, 0)
    lw = lambda *shape: pl.BlockSpec((None,) + shape, lambda i: (l,) + (0,) * len(shape))
    tab = pl.BlockSpec((tt, HEAD_PAD), tab_map)
    row_shapes = (
        jax.ShapeDtypeStruct((n, SSM_WIDTH), BF16),
        jax.ShapeDtypeStruct((n, QK_PAD), BF16),
        jax.ShapeDtypeStruct((n, QK_PAD), BF16),
        jax.ShapeDtypeStruct((n, QK_PAD), BF16),
        jax.ShapeDtypeStruct((n, D_MODEL), BF16),
        jax.ShapeDtypeStruct((n, D_MODEL), BF16),
        jax.ShapeDtypeStruct((n, KV_LORA), F32),
        jax.ShapeDtypeStruct((n, QK_ROPE), F32),
    )
    row_specs = [pl.BlockSpec((tt, s.shape[1]), row) for s in row_shapes]
    ug_shape = jax.ShapeDtypeStruct((SSM_GROUPS, n // CHUNK, CW), BF16)
    ug_spec = pl.BlockSpec((SSM_GROUPS, TILE_CHUNKS, CW), lambda i: (0, i, 0))
    outs = pl.pallas_call(
        functools.partial(_inproj_kernel, lat),
        grid=(n // tt,),
        in_specs=[
            pl.BlockSpec((tt, D_MODEL), row),
            pl.BlockSpec((None, None, 1, 6 * D_MODEL), mod_map),
            lw(1, D_MODEL), lw(D_MODEL, 1024), lw(D_MODEL, D_MODEL), lw(D_MODEL, D_MODEL),
            lw(1, Q_LORA), lw(Q_LORA, QK_PAD), lw(Q_LORA, QK_PAD), lw(1, HEAD_PAD), lw(1, HEAD_PAD), lw(1, KV_LORA),
            lw(2 * KV_LORA, 2 * QK_PAD), lw(2 * KV_LORA, QK_PAD), lw(1, HEAD_PAD), lw(1, HEAD_PAD),
            tab, tab,
        ],
        out_specs=row_specs[:1] + [ug_spec] + row_specs[1:],
        out_shape=row_shapes[:1] + (ug_shape,) + row_shapes[1:],
        scratch_shapes=[pltpu.VMEM((LANE_TILES, tt, LANES), F32)],
        compiler_params=_params(("parallel",)),
        name="inproj_lat" if lat else "inproj_ctx",
    )(x, mods, wts["g_mix"], wts["w_a"], wts["w_ga"], wts["w_gb"], wts["g_qa"], wts["w_qb"], wts["w_qb_sw"],
      wts["g_q"], wts["g_q_sw"], wts["g_kva"], wts["w_k"], wts["w_k_sw"], wts["g_k"], wts["g_k_sw"], *tabs)
    return outs


def _cachekv_kernel(ckv_ref, kr_ref, wk_ref, gk_ref, k_ref, v_ref):
    kv = _kv_dot(ckv_ref[...].astype(BF16), kr_ref[...].astype(BF16), wk_ref)
    k_ref[...] = _head_norm_rope(kv[:, :QK_PAD], gk_ref[...], None).astype(BF16)
    v_ref[...] = _with_ones(kv[:, QK_PAD:]).astype(BF16)


def _cachekv(cache_ckv, cache_krope, wts):
    nb, _, past, _ = cache_ckv.shape
    return pl.pallas_call(
        _cachekv_kernel,
        grid=(DEPTH, nb),
        in_specs=[
            pl.BlockSpec((None, None, past, KV_LORA), lambda l, b: (b, l, 0, 0)),
            pl.BlockSpec((None, None, past, QK_ROPE), lambda l, b: (b, l, 0, 0)),
            pl.BlockSpec((None, 2 * KV_LORA, 2 * QK_PAD), lambda l, b: (l, 0, 0)),
            pl.BlockSpec((None, 1, HEAD_PAD), lambda l, b: (l, 0, 0)),
        ],
        out_specs=[
            pl.BlockSpec((None, None, past, QK_PAD), lambda l, b: (l, b, 0, 0)),
            pl.BlockSpec((None, None, past, QK_PAD), lambda l, b: (l, b, 0, 0)),
        ],
        out_shape=(jax.ShapeDtypeStruct((DEPTH, nb, past, QK_PAD), BF16),
                   jax.ShapeDtypeStruct((DEPTH, nb, past, QK_PAD), BF16)),
        compiler_params=_params(("arbitrary", "arbitrary")),
        name="cache_kv",
    )(cache_ckv, cache_krope, wts["w_k"], wts["g_k"])


def _attn_heads(q_ref, segs, o_ref, den_on_mxu):
    for h in range(N_HEADS):
        qh = q_ref[:, h * HEAD_PAD:(h + 1) * HEAD_PAD]
        scores = [lax.dot_general(qh, k_ref[:, h * HEAD_PAD:(h + 1) * HEAD_PAD], (((1,), (1,)), ((), ())),
                                  preferred_element_type=F32) for k_ref, _ in segs]
        m = scores[0].max(axis=-1, keepdims=True)
        for s in scores[1:]:
            m = jnp.maximum(m, s.max(axis=-1, keepdims=True))
        acc = None
        den = None
        for s, (_, v_ref) in zip(scores, segs):
            if den_on_mxu:
                p = jnp.exp2((s - m).astype(BF16))
                a = jnp.dot(p, v_ref[:, h * HEAD_PAD:(h + 1) * HEAD_PAD], preferred_element_type=F32)
            else:
                p = jnp.exp2(s - m)
                d = p.sum(axis=-1, keepdims=True)
                den = d if den is None else den + d
                a = jnp.dot(p.astype(BF16), v_ref[:, h * HEAD_PAD:h * HEAD_PAD + V_HEAD],
                            preferred_element_type=F32)
            acc = a if acc is None else acc + a
        if den_on_mxu:
            den = acc[:, V_HEAD:V_HEAD + 1]
        o_ref[:, h * V_HEAD:(h + 1) * V_HEAD] = (acc[:, :V_HEAD] / den).astype(o_ref.dtype)


def _attn_ctx_kernel(q_ref, k_ref, v_ref, o_ref):
    for b in range(q_ref.shape[0]):
        _attn_heads(q_ref.at[b], [(k_ref.at[b], v_ref.at[b])], o_ref.at[b], False)


def _attn_lat_kernel(q_ref, kc_ref, vc_ref, kl_ref, vl_ref, o_ref):
    _attn_heads(q_ref, [(kc_ref, vc_ref), (kl_ref, vl_ref)], o_ref, True)


def _attn_ctx(q, k, v, seq):
    n = q.shape[0]
    nb = n // seq
    per_step = 2
    blk = lambda w: pl.BlockSpec((per_step, seq, w), lambda b: (b, 0, 0))
    seqs = lambda a: a.reshape(nb, seq, a.shape[1])
    return pl.pallas_call(
        _attn_ctx_kernel,
        grid=(nb // per_step,),
        in_specs=[blk(QK_PAD), blk(QK_PAD), blk(QK_PAD)],
        out_specs=blk(ATTN_WIDTH),
        out_shape=jax.ShapeDtypeStruct((nb, seq, ATTN_WIDTH), BF16),
        compiler_params=_params(("parallel",)),
        name="attn_ctx",
    )(seqs(q), seqs(k), seqs(v)).reshape(n, ATTN_WIDTH)


def _attn_lat(q, k, v, kc, vc, l, seq, tq):
    n = q.shape[0]
    per = seq // tq
    past = kc.shape[2]
    return pl.pallas_call(
        _attn_lat_kernel,
        grid=(n // seq, per),
        in_specs=[
            pl.BlockSpec((tq, QK_PAD), lambda b, j: (b * per + j, 0)),
            pl.BlockSpec((None, None, past, QK_PAD), lambda b, j: (l, b, 0, 0)),
            pl.BlockSpec((None, None, past, QK_PAD), lambda b, j: (l, b, 0, 0)),
            pl.BlockSpec((seq, QK_PAD), lambda b, j: (b, 0), pipeline_mode=pl.Buffered(1)),
            pl.BlockSpec((seq, QK_PAD), lambda b, j: (b, 0), pipeline_mode=pl.Buffered(1)),
        ],
        out_specs=pl.BlockSpec((tq, ATTN_WIDTH), lambda b, j: (b * per + j, 0)),
        out_shape=jax.ShapeDtypeStruct((n, ATTN_WIDTH), BF16),
        compiler_params=_params(("parallel", "arbitrary"), MOE_VMEM_LIMIT),
        name="attn_lat",
    )(q, kc, vc, k, v)


def _cmul(ar, ai, br, bi):
    return ar * br - ai * bi, ar * bi + ai * br


def _outer_rows(pw, m):
    return (pw[:, None, :] * m[None, :, :]).reshape(CW, SSM_STATE)


def _ssm_prep_kernel(are_ref, aim_ref, ldt_ref, bre_ref, bim_ref, cre_ref, cim_ref,
                     t_ref, et_ref, ft_ref, a16_ref):
    et_ref[...] = jnp.zeros(et_ref.shape, et_ref.dtype)
    ft_ref[...] = jnp.zeros(ft_ref.shape, ft_ref.dtype)
    row_id = lax.broadcasted_iota(jnp.int32, (CHUNK, SSM_STATE), 0)
    strip_lane = lax.broadcasted_iota(jnp.int32, (SSM_GROUP, CW), 1)
    nt = (((1,), (1,)), ((), ()))
    for gi in range(2):
        g = 2 * pl.program_id(1) + gi
        steps = _slot_step(row_id, g)
        lanes = slice(gi * SSM_STATE, (gi + 1) * SSM_STATE)
        rows = slice(gi * CW, (gi + 1) * CW)
        cre = cre_ref[gi]
        cim = cim_ref[gi]
        m_dir = []
        for d in range(2):
            a_re = are_ref[d, gi]
            a_im = aim_ref[d, gi]
            dt = jnp.exp(ldt_ref[d, gi])
            mag = jnp.exp(a_re * dt)
            abr = mag * jnp.cos(a_im * dt)
            abi = mag * jnp.sin(a_im * dt)
            den = a_re * a_re + a_im * a_im
            nr = abr - 1.0
            coef_re = (nr * a_re + abi * a_im) / den
            coef_im = (abi * a_re - nr * a_im) / den
            bbr, bbi = _cmul(coef_re, coef_im, bre_ref[d, gi], bim_ref[d, gi])

            squares = [(abr, abi)]
            for _ in range(4):
                squares.append(_cmul(*squares[-1], *squares[-1]))

            def power(n):
                pr = jnp.ones(n.shape, F32)
                pi = jnp.zeros(n.shape, F32)
                for b, (sr, si) in enumerate(squares):
                    nr_, ni_ = _cmul(pr, pi, sr, si)
                    hit = (n & (1 << b)) != 0
                    pr = jnp.where(hit, nr_, pr)
                    pi = jnp.where(hit, ni_, pi)
                return pr, pi

            to_end = (CHUNK - 1 - steps) if d == 0 else steps
            from_start = (steps + 1) if d == 0 else (CHUNK - steps)
            lag = row_id if d == 0 else (CHUNK - 1 - row_id)
            pr, pi = power(to_end)
            er = _outer_rows(pr, bbr) - _outer_rows(pi, bbi)
            ei = _outer_rows(pr, bbi) + _outer_rows(pi, bbr)
            et_ref[2 * d, rows, lanes] = er.astype(et_ref.dtype)
            et_ref[2 * d + 1, rows, lanes] = ei.astype(et_ref.dtype)
            pr, pi = power(from_start)
            ft_ref[2 * d, rows, lanes] = (_outer_rows(pr, cre) - _outer_rows(pi, cim)).astype(ft_ref.dtype)
            ft_ref[2 * d + 1, rows, lanes] = (-(_outer_rows(pr, cim) + _outer_rows(pi, cre))).astype(ft_ref.dtype)
            pr, pi = power(lag)
            zr = _outer_rows(pr, cre) - _outer_rows(pi, cim)
            zi = _outer_rows(pi, cre) + _outer_rows(pr, cim)
            m_dir.append(
                lax.dot_general(bbr, zr, nt, preferred_element_type=F32, precision=lax.Precision.HIGHEST)
                - lax.dot_general(bbi, zi, nt, preferred_element_type=F32, precision=lax.Precision.HIGHEST))
            p16r, p16i = squares[4]
            a16_ref[2 * d:2 * d + 1, lanes] = p16r
            a16_ref[2 * d + 1:2 * d + 2, lanes] = p16i
        mf, mb = m_dir
        edge = CW - SSM_GROUP
        low = mb + pltpu.roll(jnp.where(strip_lane < SSM_GROUP, mf, 0.0), edge, 1)
        high = jnp.where(strip_lane < edge, pltpu.roll(mf, edge, 1), 0.0)
        strip = jnp.concatenate([low, high], axis=1)
        turn = SSM_GROUP * (g & 7)
        for j in range(CHUNK):
            first = SSM_GROUP * (CHUNK - 1 - j)
            win = strip[:, first:first + CW]
            win = jnp.concatenate([pltpu.roll(win[:, LANES * h:LANES * (h + 1)], turn, 1)
                                   for h in range(CW // LANES)], axis=1)
            slot = (j & 8) + ((j + g) & 7)
            t_ref[gi, pl.ds(pl.multiple_of(SSM_GROUP * slot, SSM_GROUP), SSM_GROUP), :] = win.astype(t_ref.dtype)


def _ssm_prep(a_re, a_im, log_dt, b_re, b_im, c_re, c_im):
    g2 = SSM_GROUPS // 2
    a_spec = pl.BlockSpec((None, 2, 2, 1, SSM_STATE), lambda l, g: (l, 0, g, 0, 0))
    b_spec = pl.BlockSpec((None, 2, 2, SSM_GROUP, SSM_STATE), lambda l, g: (l, 0, g, 0, 0))
    c_spec = pl.BlockSpec((None, 2, SSM_GROUP, SSM_STATE), lambda l, g: (l, g, 0, 0))
    out_shapes = (
        jax.ShapeDtypeStruct((DEPTH, SSM_GROUPS, CW, CW), BF16),
        jax.ShapeDtypeStruct((DEPTH, g2, 4, 2 * CW, 2 * SSM_STATE), BF16),
        jax.ShapeDtypeStruct((DEPTH, g2, 4, 2 * CW, 2 * SSM_STATE), BF16),
        jax.ShapeDtypeStruct((DEPTH, g2, 4, 2 * SSM_STATE), F32),
    )
    return pl.pallas_call(
        _ssm_prep_kernel,
        grid=(DEPTH, g2),
        in_specs=[a_spec, a_spec,
                  pl.BlockSpec((None, 2, 2, 1, 1), lambda l, g: (l, 0, g, 0, 0)),
                  b_spec, b_spec, c_spec, c_spec],
        out_specs=[
            pl.BlockSpec((None, 2, CW, CW), lambda l, g: (l, g, 0, 0)),
            pl.BlockSpec((None, None, 4, 2 * CW, 2 * SSM_STATE), lambda l, g: (l, g, 0, 0, 0)),
            pl.BlockSpec((None, None, 4, 2 * CW, 2 * SSM_STATE), lambda l, g: (l, g, 0, 0, 0)),
            pl.BlockSpec((None, None, 4, 2 * SSM_STATE), lambda l, g: (l, g, 0, 0)),
        ],
        out_shape=out_shapes,
        compiler_params=_params(("arbitrary", "arbitrary")),
        name="ssm_prep",
    )(a_re.reshape(DEPTH, 2, SSM_GROUPS, 1, SSM_STATE), a_im.reshape(DEPTH, 2, SSM_GROUPS, 1, SSM_STATE),
      log_dt.reshape(DEPTH, 2, SSM_GROUPS, 1, 1),
      jnp.swapaxes(b_re, -1, -2), jnp.swapaxes(b_im, -1, -2), c_re, c_im)


def _ssm_kernel(nchunk, nb, u_ref, t_ref, et_ref, ft_ref, a16_ref, h0_ref, y_ref, fin_ref, e_ref, h_ref):
    u0 = u_ref[0]
    u1 = u_ref[1]
    for x in range(4):
        e_ref[x] = (jnp.dot(u0, et_ref[x, 0:CW, :], preferred_element_type=F32)
                    + jnp.dot(u1, et_ref[x, CW:2 * CW, :], preferred_element_type=F32))
    afr, afi, abr, abi = (a16_ref[i:i + 1, :] for i in range(4))

    def body(i, carry):
        fr, fi, br, bi = carry
        kf = pl.ds(i, nb, stride=nchunk)
        kb = pl.ds(nchunk - 1 - i, nb, stride=nchunk)
        h_ref[0, kf, :] = fr
        h_ref[1, kf, :] = fi
        h_ref[2, kb, :] = br
        h_ref[3, kb, :] = bi
        return (afr * fr - afi * fi + e_ref[0, kf, :], afr * fi + afi * fr + e_ref[1, kf, :],
                abr * br - abi * bi + e_ref[2, kb, :], abr * bi + abi * br + e_ref[3, kb, :])

    fin = lax.fori_loop(0, nchunk, body, tuple(h0_ref[i] for i in range(4)), unroll=4)
    for i in range(4):
        fin_ref[i] = fin[i]
    nt = (((1,), (1,)), ((), ()))
    acc = None
    for x in range(4):
        part = lax.dot_general(h_ref[x].astype(BF16), ft_ref[x], nt, preferred_element_type=F32)
        acc = part if acc is None else acc + part
    y_ref[0] = jnp.dot(u0, t_ref[0], preferred_element_type=F32) + acc[:, :CW]
    y_ref[1] = jnp.dot(u1, t_ref[1], preferred_element_type=F32) + acc[:, CW:]


def _ssm_mix(ug, nb, seq, h0_re, h0_im, prep, l):
    tmat, et, ft, a16 = prep
    nchunk = seq // CHUNK
    r = ug.shape[1]
    g2 = SSM_GROUPS // 2
    width = SSM_GROUPS * SSM_STATE
    h0 = jnp.stack([h[:, d].reshape(nb, width).astype(F32) for d in range(2) for h in (h0_re, h0_im)], axis=0)
    pair = pl.BlockSpec((2, r, CW), lambda g: (g, 0, 0))
    mats = pl.BlockSpec((None, None, 4, 2 * CW, 2 * SSM_STATE), lambda g: (l, g, 0, 0, 0))
    state = pl.BlockSpec((4, nb, 2 * SSM_STATE), lambda g: (0, 0, g))
    y, fin = pl.pallas_call(
        functools.partial(_ssm_kernel, nchunk, nb),
        grid=(g2,),
        in_specs=[pair, pl.BlockSpec((None, 2, CW, CW), lambda g: (l, g, 0, 0)), mats, mats,
                  pl.BlockSpec((None, None, 4, 2 * SSM_STATE), lambda g: (l, g, 0, 0)), state],
        out_specs=[pair, state],
        out_shape=(jax.ShapeDtypeStruct((SSM_GROUPS, r, CW), F32), jax.ShapeDtypeStruct((4, nb, width), F32)),
        scratch_shapes=[pltpu.VMEM((4, r, 2 * SSM_STATE), F32), pltpu.VMEM((4, r, 2 * SSM_STATE), F32)],
        compiler_params=_params(("parallel",)),
        name="ssm",
    )(ug, tmat, et, ft, a16, h0)
    fin = fin.reshape(2, 2, nb, SSM_GROUPS, SSM_STATE)
    return y, fin


def _gelu_tanh(x):
    return 0.5 * x * (1.0 + jnp.tanh(math.sqrt(2.0 / math.pi) * (x + 0.044715 * (x * x * x))))


def _route(aff, bias):
    a = [aff[e:e + 1, :] for e in range(N_EXPERTS)]
    s = [a[e] + bias[e:e + 1, :] for e in range(N_EXPERTS)]
    keep = []
    for e in range(N_EXPERTS):
        g0 = (e // EXPERTS_PER_GROUP) * EXPERTS_PER_GROUP
        rank = None
        for j in range(g0, g0 + EXPERTS_PER_GROUP):
            if j == e:
                continue
            beats = (s[j] >= s[e]) if j < e else (s[j] > s[e])
            r = jnp.where(beats, 1.0, 0.0)
            rank = r if rank is None else rank + r
        keep.append(rank < 1.5)
    score = []
    for g in range(N_GROUPS):
        tot = None
        for e in range(g * EXPERTS_PER_GROUP, (g + 1) * EXPERTS_PER_GROUP):
            v = jnp.where(keep[e], s[e], 0.0)
            tot = v if tot is None else tot + v
        score.append(tot)
    gates = []
    picked = []
    for g in range(N_GROUPS):
        lost = None
        for j in range(N_GROUPS):
            if j == g:
                continue
            beats = (score[j] >= score[g]) if j < g else (score[j] > score[g])
            r = jnp.where(beats, 1.0, 0.0)
            lost = r if lost is None else lost + r
        chosen = lost < 0.5
        picked.append(jnp.where(chosen, 1.0, 0.0))
        w = [jnp.where(keep[e], a[e], 0.0) for e in range(g * EXPERTS_PER_GROUP, (g + 1) * EXPERTS_PER_GROUP)]
        tot = w[0] + w[1] + w[2] + w[3]
        for v in w:
            gates.append(jnp.where(chosen, v / tot, 0.0))
    return gates, picked


def _merge_kernel(x_ref, yg_ref, u_ref, at_ref, sga_ref, sgb_ref, mod_ref, d_ref, wglu_ref, bglu_ref, wsp_ref,
                  wap_ref, wo_ref, gffn_ref, wr_ref, rb_ref, tri_ref, ramp_ref,
                  xo_ref, hs_ref, mt_ref, ms_ref, of_ref, tok_ref):
    mod = mod_ref[...]
    gate1 = mod[:, 2 * D_MODEL:3 * D_MODEL]
    shift2 = mod[:, 3 * D_MODEL:4 * D_MODEL]
    scale2 = mod[:, 4 * D_MODEL:5 * D_MODEL]
    _from_chunk_rows(yg_ref, tok_ref)
    ys = jnp.concatenate([tok_ref[j] for j in range(LANE_TILES)], axis=1)
    y = _gelu_tanh(ys + d_ref[...] * u_ref[...].astype(F32))
    glu = jnp.dot(y.astype(BF16), wglu_ref[...], preferred_element_type=F32) + bglu_ref[...]
    y = (y * _sigmoid(glu)).astype(BF16)
    a_out = jnp.dot(y, wsp_ref[...], preferred_element_type=F32)
    b_out = jnp.dot(at_ref[...], wap_ref[...], preferred_element_type=F32)
    m = (sga_ref[...].astype(F32) * a_out + sgb_ref[...].astype(F32) * b_out).astype(BF16)
    x = x_ref[...] + gate1 * jnp.dot(m, wo_ref[...], preferred_element_type=F32)
    xo_ref[...] = x
    h2 = _rms(x, D_MODEL) * (gffn_ref[...] * (1.0 + scale2)) + shift2
    nt = (((1,), (1,)), ((), ()))
    h2_hi = h2.astype(BF16)
    h2_lo = (h2 - h2_hi.astype(F32)).astype(BF16)
    wr = wr_ref[...]
    wr_hi = wr.astype(BF16)
    wr_lo = (wr - wr_hi.astype(F32)).astype(BF16)
    logits = (lax.dot_general(wr_hi, h2_hi, nt, preferred_element_type=F32)
              + lax.dot_general(wr_hi, h2_lo, nt, preferred_element_type=F32)
              + lax.dot_general(wr_lo, h2_hi, nt, preferred_element_type=F32))
    gates, picked = _route(_sigmoid(logits), rb_ref[...])
    n = x.shape[0]
    onehot = jnp.concatenate(picked + [jnp.zeros((8 - N_GROUPS, n), F32)], axis=0)
    before = jnp.dot(onehot.astype(BF16), tri_ref[...], preferred_element_type=F32)
    count = [jnp.sum(p, axis=-1, keepdims=True) for p in picked]
    start = [jnp.zeros((1, 1), F32)]
    for g in range(N_GROUPS - 1):
        start.append(start[-1] + count[g])
    pos = sum(picked[g] * (start[g] + before[g:g + 1, :]) for g in range(N_GROUPS))
    gid = sum(float(g) * picked[g] for g in range(1, N_GROUPS))
    perm = jnp.where(ramp_ref[...] == pos, 1.0, 0.0).astype(BF16)
    hs_ref[...] = jnp.dot(perm, h2_hi, preferred_element_type=F32).astype(BF16)
    own = [sum(gates[EXPERTS_PER_GROUP * g + e] for g in range(N_GROUPS)) for e in range(EXPERTS_PER_GROUP)]
    meta = jnp.concatenate(own + [gid, pos, jnp.zeros((LANES - EXPERTS_PER_GROUP - 2, n), F32)], axis=0).T
    mt_ref[...] = meta
    hi = meta.astype(BF16)
    lo = (meta - hi.astype(F32)).astype(BF16)
    ms_ref[...] = (jnp.dot(perm, hi, preferred_element_type=F32)
                   + jnp.dot(perm, lo, preferred_element_type=F32))
    lane = lax.broadcasted_iota(jnp.int32, (8, LANES), 1)
    offs = sum(jnp.where(lane == g, start[g], 0.0) for g in range(1, N_GROUPS)) + jnp.where(lane == N_GROUPS, n, 0.0)
    of_ref[...] = offs.astype(jnp.int32)


def _merge(x, yg, u, attn, sga, sgb, mods, l, lat, wts):
    n = x.shape[0]
    tt = TOK_TILE
    per_seq = 2048 // tt
    mod_map = (lambda i: (l, 1 + i // per_seq, 0, 0)) if lat else (lambda i: (l, 0, 0, 0))
    row = lambda i: (i, 0)
    lw = lambda *shape: pl.BlockSpec((None,) + shape, lambda i: (l,) + (0,) * len(shape))
    return pl.pallas_call(
        _merge_kernel,
        grid=(n // tt,),
        in_specs=[
            pl.BlockSpec((tt, D_MODEL), row),
            pl.BlockSpec((SSM_GROUPS, TILE_CHUNKS, CW), lambda i: (0, i, 0)),
            pl.BlockSpec((tt, SSM_WIDTH), row),
            pl.BlockSpec((tt, ATTN_WIDTH), row), pl.BlockSpec((tt, D_MODEL), row), pl.BlockSpec((tt, D_MODEL), row),
            pl.BlockSpec((None, None, 1, 6 * D_MODEL), mod_map),
            lw(1, SSM_WIDTH), lw(SSM_WIDTH, SSM_WIDTH), lw(1, SSM_WIDTH), lw(SSM_WIDTH, D_MODEL),
            lw(ATTN_WIDTH, D_MODEL), lw(D_MODEL, D_MODEL), lw(1, D_MODEL),
            pl.BlockSpec((N_EXPERTS, D_MODEL), lambda i: (0, 0)),
            pl.BlockSpec((N_EXPERTS, 1), lambda i: (0, 0)),
            pl.BlockSpec((tt, tt), lambda i: (0, 0)), pl.BlockSpec((tt, tt), lambda i: (0, 0)),
        ],
        out_specs=[pl.BlockSpec((tt, D_MODEL), row), pl.BlockSpec((tt, D_MODEL), row),
                   pl.BlockSpec((tt, LANES), row), pl.BlockSpec((tt, LANES), row),
                   pl.BlockSpec((None, 8, LANES), lambda i: (i, 0, 0))],
        out_shape=(jax.ShapeDtypeStruct((n, D_MODEL), F32), jax.ShapeDtypeStruct((n, D_MODEL), BF16),
                   jax.ShapeDtypeStruct((n, LANES), F32), jax.ShapeDtypeStruct((n, LANES), F32),
                   jax.ShapeDtypeStruct((n // tt, 8, LANES), jnp.int32)),
        scratch_shapes=[pltpu.VMEM((LANE_TILES, tt, LANES), F32)],
        compiler_params=_params(("parallel",)),
        name="merge_lat" if lat else "merge_ctx",
    )(x, yg, u, attn, sga, sgb, mods, wts["ssm_d"], wts["w_glu"], wts["b_glu"], wts["w_ssm_proj"],
      wts["w_attn_proj"], wts["w_o"], wts["g_ffn"], wts["w_router_t"], wts["router_bias"],
      jnp.asarray(np.triu(np.ones((tt, tt), np.float32), 1), BF16),
      jnp.asarray(np.broadcast_to(np.arange(tt, dtype=np.float32)[:, None], (tt, tt))))


def _moe_kernel(offs_ref, x_ref, hs_ref, ms_ref, mt_ref, mod_ref, wg_ref, wu_ref, wd_ref, o_ref, acc_ref):
    tile = pl.program_id(0)
    acc_ref[...] = jnp.zeros(acc_ref.shape, F32)
    row_id = lax.broadcasted_iota(jnp.int32, (MOE_ROWS, 1), 0)

    def group_body(g, carry):
        seg_start = offs_ref[(N_GROUPS + 1) * tile + g]
        seg_end = offs_ref[(N_GROUPS + 1) * tile + g + 1]
        first = (seg_start >> 4) << 4
        span = jnp.where(seg_end > seg_start, seg_end - first, 0)
        n_pass = sum((span > k * MOE_ROWS).astype(jnp.int32) for k in range(pl.cdiv(TOK_TILE + 16, MOE_ROWS)))

        def pass_body(j, c):
            lo = first + MOE_ROWS * j
            s = pl.multiple_of(jnp.minimum(lo, TOK_TILE - MOE_ROWS), 16)
            rows = pl.ds(s, MOE_ROWS)
            meta = ms_ref[rows, :]
            group_id = lax.convert_element_type(g, F32)
            mine = (meta[:, EXPERTS_PER_GROUP:EXPERTS_PER_GROUP + 1] == group_id) & (row_id + s >= lo)
            own = jnp.where(mine, 1.0, 0.0)
            hs = hs_ref[rows, :]
            acts = []
            for e in range(EXPERTS_PER_GROUP):
                expert = EXPERTS_PER_GROUP * g + e
                hg = jnp.dot(hs, wg_ref[expert], preferred_element_type=F32)
                hu = jnp.dot(hs, wu_ref[expert], preferred_element_type=F32)
                acts.append(((hg * _sigmoid(hg)) * hu * (meta[:, e:e + 1] * own)).astype(BF16))
            acc_ref[rows, :] += jnp.dot(jnp.concatenate(acts, axis=1), wd_ref[g], preferred_element_type=F32)
            return c

        lax.fori_loop(0, n_pass, pass_body, 0)
        return carry

    lax.fori_loop(0, N_GROUPS, group_body, 0)

    slot = lax.broadcasted_iota(jnp.int32, (TOK_TILE, TOK_TILE), 1).astype(F32)
    unsort = jnp.where(mt_ref[:, EXPERTS_PER_GROUP + 1:EXPERTS_PER_GROUP + 2] == slot, 1.0, 0.0).astype(BF16)
    y = jnp.dot(unsort, acc_ref[...].astype(BF16), preferred_element_type=F32)
    o_ref[...] = x_ref[...] + mod_ref[:, 5 * D_MODEL:6 * D_MODEL] * y


def _moe(x, hs, meta_tok, meta_sorted, offs, mods, l, lat, wts):
    n = x.shape[0]
    tt = TOK_TILE
    per_seq = 2048 // tt
    mod_map = (lambda i, o: (l, 1 + i // per_seq, 0, 0)) if lat else (lambda i, o: (l, 0, 0, 0))
    row = lambda i, o: (i, 0)
    width = EXPERTS_PER_GROUP * D_EXPERT
    resident = lambda *shape: pl.BlockSpec((None,) + shape, lambda i, o: (l,) + (0,) * len(shape),
                                           pipeline_mode=pl.Buffered(1))
    grid_spec = pltpu.PrefetchScalarGridSpec(
        num_scalar_prefetch=1,
        grid=(n // tt,),
        in_specs=[
            pl.BlockSpec((tt, D_MODEL), row), pl.BlockSpec((tt, D_MODEL), row), pl.BlockSpec((tt, LANES), row),
            pl.BlockSpec((tt, LANES), row),
            pl.BlockSpec((None, None, 1, 6 * D_MODEL), mod_map),
            resident(N_EXPERTS, D_MODEL, D_EXPERT), resident(N_EXPERTS, D_MODEL, D_EXPERT),
            resident(N_GROUPS, width, D_MODEL),
        ],
        out_specs=pl.BlockSpec((tt, D_MODEL), row),
        scratch_shapes=[pltpu.VMEM((tt, D_MODEL), F32)],
    )
    return pl.pallas_call(
        _moe_kernel,
        grid_spec=grid_spec,
        out_shape=jax.ShapeDtypeStruct((n, D_MODEL), F32),
        compiler_params=_params(("arbitrary",), MOE_VMEM_LIMIT),
        name="moe_lat" if lat else "moe_ctx",
    )(offs[:, 0, :N_GROUPS + 1].reshape(-1), x, hs, meta_sorted, meta_tok, mods, wts["w_gate"], wts["w_up"],
      wts["w_down"])


def _pad_heads(w, width):
    lead = w.shape[:-1]
    w = w.reshape(lead + (N_HEADS, width))
    w = jnp.pad(w, [(0, 0)] * len(lead) + [(0, 0), (0, HEAD_PAD - width)])
    return w.reshape(lead + (N_HEADS * HEAD_PAD,))


def _swap_partners(w):
    lead = w.shape[:-1]
    w = w.reshape(lead + (-1, HEAD_PAD))
    half = ROPE_AXIS // 2
    a, b = QK_NOPE, QK_NOPE + ROPE_AXIS
    parts = [w[..., :a], w[..., a + half:b], w[..., a:a + half], w[..., b + half:b + 2 * half], w[..., b:b + half],
             w[..., b + 2 * half:]]
    return jnp.concatenate(parts, axis=-1).reshape(lead + (-1,))


def _layout_weights(w_in, g_qa, w_qb, g_q, g_kva, w_kvb, g_k, g_mix, w_attn_proj, ssm_d, ssm_w_glu, ssm_b_glu,
                    ssm_w_proj, w_o, g_ffn, w_router, router_bias, w_gate, w_up, w_down):
    n_a = SSM_WIDTH + Q_LORA + KV_LORA + QK_ROPE
    w_a = jnp.pad(w_in[:, :, :n_a], ((0, 0), (0, 0), (0, 1024 - n_a)))
    kv = w_kvb.reshape(DEPTH, KV_LORA, N_HEADS, QK_NOPE + V_HEAD)
    k_cols = _pad_heads(kv[..., :QK_NOPE].reshape(DEPTH, KV_LORA, N_HEADS * QK_NOPE), QK_NOPE)
    v_cols = _pad_heads(kv[..., QK_NOPE:].reshape(DEPTH, KV_LORA, ATTN_WIDTH), V_HEAD)
    place = np.zeros((KV_LORA, 2 * QK_PAD), np.float32)
    for h in range(N_HEADS):
        for r in range(QK_ROPE):
            place[r, h * HEAD_PAD + QK_NOPE + r] = 1.0
    w_k = jnp.concatenate([jnp.concatenate([k_cols, v_cols], axis=-1),
                           jnp.broadcast_to(jnp.asarray(place), (DEPTH,) + place.shape)], axis=1)
    w_qb_pad = _pad_heads(w_qb, QK_HEAD).astype(BF16)
    w_k = w_k.astype(BF16)
    pad_gain = lambda g: jnp.pad(g, ((0, 0), (0, HEAD_PAD - QK_HEAD))).reshape(DEPTH, 1, HEAD_PAD)
    vec = lambda g: g.reshape(DEPTH, 1, -1)
    return {
        "g_mix": vec(g_mix), "w_a": w_a.astype(BF16),
        "w_ga": w_in[:, :, n_a:n_a + D_MODEL].astype(BF16), "w_gb": w_in[:, :, n_a + D_MODEL:].astype(BF16),
        "g_qa": vec(g_qa), "w_qb": w_qb_pad, "w_qb_sw": _swap_partners(w_qb_pad),
        "g_q": pad_gain(g_q) * Q_SCALE, "g_q_sw": _swap_partners(pad_gain(g_q)) * Q_SCALE,
        "g_kva": vec(g_kva), "w_k": w_k, "w_k_sw": _swap_partners(w_k[:, :, :QK_PAD]),
        "g_k": pad_gain(g_k), "g_k_sw": _swap_partners(pad_gain(g_k)),
        "ssm_d": vec(ssm_d), "w_glu": ssm_w_glu.astype(BF16), "b_glu": vec(ssm_b_glu),
        "w_ssm_proj": ssm_w_proj.astype(BF16), "w_attn_proj": w_attn_proj.astype(BF16), "w_o": w_o.astype(BF16),
        "g_ffn": vec(g_ffn), "w_router_t": w_router.T, "router_bias": router_bias.reshape(N_EXPERTS, 1),
        "w_gate": w_gate.astype(BF16), "w_up": w_up.astype(BF16),
        "w_down": w_down.astype(BF16).reshape(DEPTH, N_GROUPS, EXPERTS_PER_GROUP * D_EXPERT, D_MODEL),
    }


def _rope_tables(n_tokens):
    pos = np.arange(n_tokens)
    inv = 1.0 / (ROPE_BASE ** (np.arange(ROPE_AXIS // 2, dtype=np.float32) * 2.0 / ROPE_AXIS))
    ang = np.zeros((n_tokens, HEAD_PAD), np.float32)
    half = ROPE_AXIS // 2
    row = (pos // GRID_W).astype(np.float32)[:, None] * inv
    col = (pos % GRID_W).astype(np.float32)[:, None] * inv
    first = np.zeros((HEAD_PAD,), bool)
    second = np.zeros((HEAD_PAD,), bool)
    for base, a in ((QK_NOPE, row), (QK_NOPE + ROPE_AXIS, col)):
        ang[:, base:base + half] = a
        ang[:, base + half:base + 2 * half] = a
        first[base:base + half] = True
        second[base + half:base + 2 * half] = True
    ang = jnp.asarray(ang)
    cos = jnp.where(jnp.asarray(first | second), jnp.cos(ang), 1.0)
    sin = jnp.sin(ang)
    return cos, jnp.where(jnp.asarray(first), -sin, jnp.where(jnp.asarray(second), sin, 0.0))


def kernel(x_prompt, x_sample, c, cache_ckv, cache_krope, state_ssm_re, state_ssm_im, c_ctx, w_ada, b_ada, g_mix, w_in, g_qa, w_qb, g_q, g_kva, w_kvb, g_k, w_attn_proj, ssm_a_re, ssm_a_im, ssm_log_dt, ssm_b_re, ssm_b_im, ssm_c_re, ssm_c_im, ssm_d, ssm_w_glu, ssm_b_glu, ssm_w_proj, w_o, g_ffn, w_router, router_bias, w_gate, w_up, w_down):
    nb_c, seq_c, _ = x_prompt.shape
    nb_l, seq_l, _ = x_sample.shape
    wts = _layout_weights(w_in, g_qa, w_qb, g_q, g_kva, w_kvb, g_k, g_mix, w_attn_proj, ssm_d, ssm_w_glu,
                          ssm_b_glu, ssm_w_proj, w_o, g_ffn, w_router, router_bias, w_gate, w_up, w_down)
    cvec = jnp.concatenate([c_ctx[None, :], c, jnp.zeros((8 - 1 - nb_l, D_MODEL), F32)], axis=0)
    mods = _mods(cvec, w_ada, b_ada).reshape(DEPTH, 8, 1, 6 * D_MODEL)
    prep = _ssm_prep(ssm_a_re, ssm_a_im, ssm_log_dt, ssm_b_re, ssm_b_im, ssm_c_re, ssm_c_im)
    tabs = _rope_tables(seq_l)
    kc, vc = _cachekv(cache_ckv, cache_krope, wts)

    xp = x_prompt.reshape(nb_c * seq_c, D_MODEL)
    xs = x_sample.reshape(nb_l * seq_l, D_MODEL)
    zeros_c = jnp.zeros((nb_c, 2, SSM_GROUPS, SSM_STATE), F32)
    ckvs, kropes, fins = [], [], []
    for l in range(DEPTH):
        u, ug, q, k, v, sga, sgb, ckv, krope = _inproj(xp, mods, l, False, wts, tabs)
        ckvs.append(ckv)
        kropes.append(krope)
        yg, fin = _ssm_mix(ug, nb_c, seq_c, zeros_c, zeros_c, prep, l)
        fins.append(fin)
        attn = _attn_ctx(q, k, v, seq_c)
        xp, *routed = _merge(xp, yg, u, attn, sga, sgb, mods, l, False, wts)
        xp = _moe(xp, *routed, mods, l, False, wts)

        u, ug, q, k, v, sga, sgb, _, _ = _inproj(xs, mods, l, True, wts, tabs)
        yg, _ = _ssm_mix(ug, nb_l, seq_l, state_ssm_re[:, l], state_ssm_im[:, l], prep, l)
        attn = _attn_lat(q, k, v, kc, vc, l, seq_l, TOK_TILE)
        xs, *routed = _merge(xs, yg, u, attn, sga, sgb, mods, l, True, wts)
        xs = _moe(xs, *routed, mods, l, True, wts)

    new_ckv = jnp.stack(ckvs, axis=0).reshape(DEPTH, nb_c, seq_c, KV_LORA).transpose(1, 0, 2, 3)
    new_krope = jnp.stack(kropes, axis=0).reshape(DEPTH, nb_c, seq_c, QK_ROPE).transpose(1, 0, 2, 3)
    fin = jnp.stack(fins, axis=0)
    new_re = jnp.transpose(fin[:, :, 0], (2, 0, 1, 3, 4))
    new_im = jnp.transpose(fin[:, :, 1], (2, 0, 1, 3, 4))
    return (xp.reshape(nb_c, seq_c, D_MODEL), xs.reshape(nb_l, seq_l, D_MODEL), new_ckv, new_krope, new_re, new_im)
```

```python
import functools
import math

import jax
import jax.numpy as jnp
import numpy as np
from jax import lax
from jax.experimental import pallas as pl
from jax.experimental.pallas import tpu as pltpu

F32 = jnp.float32
BF16 = jnp.bfloat16

D_MODEL = 1024
DEPTH = 4
GRID_W = 64
SSM_WIDTH = 512
SSM_GROUP = 16
SSM_GROUPS = 32
SSM_STATE = 64
N_HEADS = 8
QK_NOPE = 64
QK_ROPE = 32
QK_HEAD = 96
V_HEAD = 64
Q_LORA = 256
KV_LORA = 128
ATTN_WIDTH = 512
ROPE_AXIS = 16
ROPE_BASE = 10000.0
N_EXPERTS = 16
N_GROUPS = 4
EXPERTS_PER_GROUP = 4
D_EXPERT = 256
EPS = 1e-6

LANES = 128
HEAD_PAD = LANES
QK_PAD = N_HEADS * HEAD_PAD
CHUNK = 16
CW = CHUNK * SSM_GROUP
TOK_TILE = 512
TILE_CHUNKS = TOK_TILE // CHUNK
LANE_TILES = SSM_WIDTH // LANES
GROUPS_PER_TILE = LANES // SSM_GROUP
Q_SCALE = math.log2(math.e) / math.sqrt(QK_HEAD)
MOE_ROWS = 160
VMEM_LIMIT = 48 * 1024 * 1024
MOE_VMEM_LIMIT = 56 * 1024 * 1024


def _params(sem, vmem=VMEM_LIMIT):
    return pltpu.CompilerParams(dimension_semantics=sem, vmem_limit_bytes=vmem)


def _sigmoid(x):
    return 1.0 / (1.0 + jnp.exp(-x))


def _rms(x, n):
    return x * lax.rsqrt(jnp.sum(x * x, axis=-1, keepdims=True) * (1.0 / n) + EPS)


def _mods_kernel(c_ref, w_ref, b_ref, o_ref):
    c = c_ref[...]
    cs = (c * _sigmoid(c)).astype(BF16)
    o_ref[...] = jnp.dot(cs, w_ref[...].astype(BF16), preferred_element_type=F32) + b_ref[...]


def _mods(cvec, w_ada, b_ada):
    nb = 1536
    return pl.pallas_call(
        _mods_kernel,
        grid=(DEPTH, 6 * D_MODEL // nb),
        in_specs=[
            pl.BlockSpec((8, D_MODEL), lambda l, j: (0, 0)),
            pl.BlockSpec((None, D_MODEL, nb), lambda l, j: (l, 0, j)),
            pl.BlockSpec((None, 1, nb), lambda l, j: (l, 0, j)),
        ],
        out_specs=pl.BlockSpec((None, 8, nb), lambda l, j: (l, 0, j)),
        out_shape=jax.ShapeDtypeStruct((DEPTH, 8, 6 * D_MODEL), F32),
        compiler_params=_params(("arbitrary", "arbitrary")),
        name="mods",
    )(cvec, w_ada, b_ada.reshape(DEPTH, 1, 6 * D_MODEL))


def _slot_step(slot, g):
    return (slot & 8) + ((slot - g) & 7)


def _lane_segment():
    return lax.broadcasted_iota(jnp.int32, (TILE_CHUNKS, LANES), 1) // SSM_GROUP


def _to_chunk_rows(tok_ref, ug_ref):
    seg = _lane_segment()
    for j in range(LANE_TILES):
        for half in range(CHUNK // 8):
            rolled = []
            for r in range(8):
                v = tok_ref[j, pl.ds(8 * half + r, TILE_CHUNKS, stride=CHUNK), :]
                rolled.append(pltpu.roll(v, SSM_GROUP * r, 1) if r else v)
            for gg in range(GROUPS_PER_TILE):
                acc = rolled[(0 - gg) % 8]
                for m in range(1, 8):
                    acc = jnp.where(seg == m, rolled[(m - gg) % 8], acc)
                ug_ref[GROUPS_PER_TILE * j + gg, :, LANES * half:LANES * (half + 1)] = acc.astype(ug_ref.dtype)


def _from_chunk_rows(yg_ref, tok_ref):
    seg = _lane_segment()
    for j in range(LANE_TILES):
        for half in range(CHUNK // 8):
            ys = [yg_ref[GROUPS_PER_TILE * j + gg, :, LANES * half:LANES * (half + 1)]
                  for gg in range(GROUPS_PER_TILE)]
            for r in range(8):
                acc = ys[(0 - r) % 8]
                for m in range(1, 8):
                    acc = jnp.where(seg == m, ys[(m - r) % 8], acc)
                out = pltpu.roll(acc, LANES - SSM_GROUP * r, 1) if r else acc
                tok_ref[j, pl.ds(8 * half + r, TILE_CHUNKS, stride=CHUNK), :] = out


def _head_norm_rope(t, gain, swapped):
    outs = []
    for h in range(N_HEADS):
        lanes = slice(h * HEAD_PAD, (h + 1) * HEAD_PAD)
        th = t[:, lanes]
        scale = lax.rsqrt(jnp.sum(th * th, axis=-1, keepdims=True) * (1.0 / QK_HEAD) + EPS)
        y = th * gain
        if swapped is not None:
            y = y + swapped[0][:, lanes] * swapped[1]
        outs.append(y * scale)
    return jnp.concatenate(outs, axis=1)


def _with_ones(v):
    lane = lax.broadcasted_iota(jnp.int32, (1, v.shape[1]), 1)
    return jnp.where((lane & (HEAD_PAD - 1)) == V_HEAD, 1.0, v)


def _kv_dot(ckvn, krp, wk_ref):
    return (jnp.dot(ckvn, wk_ref[0:KV_LORA, :], preferred_element_type=F32)
            + jnp.dot(krp, wk_ref[KV_LORA:KV_LORA + krp.shape[1], :], preferred_element_type=F32))


def _inproj_kernel(use_rope, x_ref, mod_ref, gmix_ref, wa_ref, wga_ref, wgb_ref, gqa_ref, wqb_ref, wqbs_ref, gq_ref,
                   gqs_ref, gkva_ref, wk_ref, wks_ref, gk_ref, gks_ref, cos_ref, sin_ref,
                   u_ref, ug_ref, q_ref, k_ref, v_ref, sga_ref, sgb_ref, ckv_ref, kr_ref, tok_ref):
    mod = mod_ref[...]
    shift1 = mod[:, 0:D_MODEL]
    gain1 = gmix_ref[...] * (1.0 + mod[:, D_MODEL:2 * D_MODEL])
    h = (_rms(x_ref[...], D_MODEL) * gain1 + shift1).astype(BF16)
    za = jnp.dot(h, wa_ref[...], preferred_element_type=F32)
    sga_ref[...] = _sigmoid(jnp.dot(h, wga_ref[...], preferred_element_type=F32)).astype(BF16)
    sgb_ref[...] = _sigmoid(jnp.dot(h, wgb_ref[...], preferred_element_type=F32)).astype(BF16)
    u_ref[...] = za[:, :SSM_WIDTH].astype(BF16)
    for j in range(LANE_TILES):
        tok_ref[j] = za[:, LANES * j:LANES * (j + 1)]
    _to_chunk_rows(tok_ref, ug_ref)
    qa = za[:, SSM_WIDTH:SSM_WIDTH + Q_LORA]
    ckv = za[:, SSM_WIDTH + Q_LORA:SSM_WIDTH + Q_LORA + KV_LORA]
    krp = za[:, SSM_WIDTH + Q_LORA + KV_LORA:]
    qn = (_rms(qa, Q_LORA) * gqa_ref[...]).astype(BF16)
    qr = jnp.dot(qn, wqb_ref[...], preferred_element_type=F32)
    ckvn = _rms(ckv, KV_LORA) * gkva_ref[...]
    ckv_ref[...] = ckvn
    kr_ref[...] = krp[:, :QK_ROPE]
    ckvn = ckvn.astype(BF16)
    krp = krp.astype(BF16)
    kv = _kv_dot(ckvn, krp, wk_ref)
    gq = gq_ref[...]
    gk = gk_ref[...]
    q_sw = k_sw = None
    if use_rope:
        cos = cos_ref[...]
        sin = sin_ref[...]
        q_sw = (jnp.dot(qn, wqbs_ref[...], preferred_element_type=F32), gqs_ref[...] * sin)
        k_sw = (_kv_dot(ckvn, krp, wks_ref), gks_ref[...] * sin)
        gq = gq * cos
        gk = gk * cos
    q_ref[...] = _head_norm_rope(qr, gq, q_sw).astype(BF16)
    k_ref[...] = _head_norm_rope(kv[:, :QK_PAD], gk, k_sw).astype(BF16)
    v_ref[...] = _with_ones(kv[:, QK_PAD:]).astype(BF16)


def _inproj(x, mods, l, lat, wts, tabs):
    n = x.shape[0]
    tt = TOK_TILE
    per_seq = 2048 // tt
    if lat:
        mod_map = lambda i: (l, 1 + i // per_seq, 0, 0)
        tab_map = lambda i: (i % per_seq, 0)
    else:
        mod_map = lambda i: (l, 0, 0, 0)
        tab_map = lambda i: (0, 0)
    row = lambda i: (i, 0)
    lw = lambda *shape: pl.BlockSpec((None,) + shape, lambda i: (l,) + (0,) * len(shape))
    tab = pl.BlockSpec((tt, HEAD_PAD), tab_map)
    row_shapes = (
        jax.ShapeDtypeStruct((n, SSM_WIDTH), BF16),
        jax.ShapeDtypeStruct((n, QK_PAD), BF16),
        jax.ShapeDtypeStruct((n, QK_PAD), BF16),
        jax.ShapeDtypeStruct((n, QK_PAD), BF16),
        jax.ShapeDtypeStruct((n, D_MODEL), BF16),
        jax.ShapeDtypeStruct((n, D_MODEL), BF16),
        jax.ShapeDtypeStruct((n, KV_LORA), F32),
        jax.ShapeDtypeStruct((n, QK_ROPE), F32),
    )
    row_specs = [pl.BlockSpec((tt, s.shape[1]), row) for s in row_shapes]
    ug_shape = jax.ShapeDtypeStruct((SSM_GROUPS, n // CHUNK, CW), BF16)
    ug_spec = pl.BlockSpec((SSM_GROUPS, TILE_CHUNKS, CW), lambda i: (0, i, 0))
    outs = pl.pallas_call(
        functools.partial(_inproj_kernel, lat),
        grid=(n // tt,),
        in_specs=[
            pl.BlockSpec((tt, D_MODEL), row),
            pl.BlockSpec((None, None, 1, 6 * D_MODEL), mod_map),
            lw(1, D_MODEL), lw(D_MODEL, 1024), lw(D_MODEL, D_MODEL), lw(D_MODEL, D_MODEL),
            lw(1, Q_LORA), lw(Q_LORA, QK_PAD), lw(Q_LORA, QK_PAD), lw(1, HEAD_PAD), lw(1, HEAD_PAD), lw(1, KV_LORA),
            lw(2 * KV_LORA, 2 * QK_PAD), lw(2 * KV_LORA, QK_PAD), lw(1, HEAD_PAD), lw(1, HEAD_PAD),
            tab, tab,
        ],
        out_specs=row_specs[:1] + [ug_spec] + row_specs[1:],
        out_shape=row_shapes[:1] + (ug_shape,) + row_shapes[1:],
        scratch_shapes=[pltpu.VMEM((LANE_TILES, tt, LANES), F32)],
        compiler_params=_params(("parallel",)),
        name="inproj_lat" if lat else "inproj_ctx",
    )(x, mods, wts["g_mix"], wts["w_a"], wts["w_ga"], wts["w_gb"], wts["g_qa"], wts["w_qb"], wts["w_qb_sw"],
      wts["g_q"], wts["g_q_sw"], wts["g_kva"], wts["w_k"], wts["w_k_sw"], wts["g_k"], wts["g_k_sw"], *tabs)
    return outs


def _cachekv_kernel(ckv_ref, kr_ref, wk_ref, gk_ref, k_ref, v_ref):
    kv = _kv_dot(ckv_ref[...].astype(BF16), kr_ref[...].astype(BF16), wk_ref)
    k_ref[...] = _head_norm_rope(kv[:, :QK_PAD], gk_ref[...], None).astype(BF16)
    v_ref[...] = _with_ones(kv[:, QK_PAD:]).astype(BF16)


def _cachekv(cache_ckv, cache_krope, wts):
    nb, _, past, _ = cache_ckv.shape
    return pl.pallas_call(
        _cachekv_kernel,
        grid=(DEPTH, nb),
        in_specs=[
            pl.BlockSpec((None, None, past, KV_LORA), lambda l, b: (b, l, 0, 0)),
            pl.BlockSpec((None, None, past, QK_ROPE), lambda l, b: (b, l, 0, 0)),
            pl.BlockSpec((None, 2 * KV_LORA, 2 * QK_PAD), lambda l, b: (l, 0, 0)),
            pl.BlockSpec((None, 1, HEAD_PAD), lambda l, b: (l, 0, 0)),
        ],
        out_specs=[
            pl.BlockSpec((None, None, past, QK_PAD), lambda l, b: (l, b, 0, 0)),
            pl.BlockSpec((None, None, past, QK_PAD), lambda l, b: (l, b, 0, 0)),
        ],
        out_shape=(jax.ShapeDtypeStruct((DEPTH, nb, past, QK_PAD), BF16),
                   jax.ShapeDtypeStruct((DEPTH, nb, past, QK_PAD), BF16)),
        compiler_params=_params(("arbitrary", "arbitrary")),
        name="cache_kv",
    )(cache_ckv, cache_krope, wts["w_k"], wts["g_k"])


def _attn_heads(q_ref, segs, o_ref, den_on_mxu):
    for h in range(N_HEADS):
        qh = q_ref[:, h * HEAD_PAD:(h + 1) * HEAD_PAD]
        scores = [lax.dot_general(qh, k_ref[:, h * HEAD_PAD:(h + 1) * HEAD_PAD], (((1,), (1,)), ((), ())),
                                  preferred_element_type=F32) for k_ref, _ in segs]
        m = scores[0].max(axis=-1, keepdims=True)
        for s in scores[1:]:
            m = jnp.maximum(m, s.max(axis=-1, keepdims=True))
        acc = None
        den = None
        for s, (_, v_ref) in zip(scores, segs):
            if den_on_mxu:
                p = jnp.exp2((s - m).astype(BF16))
                a = jnp.dot(p, v_ref[:, h * HEAD_PAD:(h + 1) * HEAD_PAD], preferred_element_type=F32)
            else:
                p = jnp.exp2(s - m)
                d = p.sum(axis=-1, keepdims=True)
                den = d if den is None else den + d
                a = jnp.dot(p.astype(BF16), v_ref[:, h * HEAD_PAD:h * HEAD_PAD + V_HEAD],
                            preferred_element_type=F32)
            acc = a if acc is None else acc + a
        if den_on_mxu:
            den = acc[:, V_HEAD:V_HEAD + 1]
        o_ref[:, h * V_HEAD:(h + 1) * V_HEAD] = (acc[:, :V_HEAD] / den).astype(o_ref.dtype)


def _attn_ctx_kernel(q_ref, k_ref, v_ref, o_ref):
    for b in range(q_ref.shape[0]):
        _attn_heads(q_ref.at[b], [(k_ref.at[b], v_ref.at[b])], o_ref.at[b], False)


def _attn_lat_kernel(q_ref, kc_ref, vc_ref, kl_ref, vl_ref, o_ref):
    _attn_heads(q_ref, [(kc_ref, vc_ref), (kl_ref, vl_ref)], o_ref, True)


def _attn_ctx(q, k, v, seq):
    n = q.shape[0]
    nb = n // seq
    per_step = 2
    blk = lambda w: pl.BlockSpec((per_step, seq, w), lambda b: (b, 0, 0))
    seqs = lambda a: a.reshape(nb, seq, a.shape[1])
    return pl.pallas_call(
        _attn_ctx_kernel,
        grid=(nb // per_step,),
        in_specs=[blk(QK_PAD), blk(QK_PAD), blk(QK_PAD)],
        out_specs=blk(ATTN_WIDTH),
        out_shape=jax.ShapeDtypeStruct((nb, seq, ATTN_WIDTH), BF16),
        compiler_params=_params(("parallel",)),
        name="attn_ctx",
    )(seqs(q), seqs(k), seqs(v)).reshape(n, ATTN_WIDTH)


def _attn_lat(q, k, v, kc, vc, l, seq, tq):
    n = q.shape[0]
    per = seq // tq
    past = kc.shape[2]
    return pl.pallas_call(
        _attn_lat_kernel,
        grid=(n // seq, per),
        in_specs=[
            pl.BlockSpec((tq, QK_PAD), lambda b, j: (b * per + j, 0)),
            pl.BlockSpec((None, None, past, QK_PAD), lambda b, j: (l, b, 0, 0)),
            pl.BlockSpec((None, None, past, QK_PAD), lambda b, j: (l, b, 0, 0)),
            pl.BlockSpec((seq, QK_PAD), lambda b, j: (b, 0), pipeline_mode=pl.Buffered(1)),
            pl.BlockSpec((seq, QK_PAD), lambda b, j: (b, 0), pipeline_mode=pl.Buffered(1)),
        ],
        out_specs=pl.BlockSpec((tq, ATTN_WIDTH), lambda b, j: (b * per + j, 0)),
        out_shape=jax.ShapeDtypeStruct((n, ATTN_WIDTH), BF16),
        compiler_params=_params(("parallel", "arbitrary"), MOE_VMEM_LIMIT),
        name="attn_lat",
    )(q, kc, vc, k, v)


def _cmul(ar, ai, br, bi):
    return ar * br - ai * bi, ar * bi + ai * br


def _outer_rows(pw, m):
    return (pw[:, None, :] * m[None, :, :]).reshape(CW, SSM_STATE)


def _ssm_prep_kernel(are_ref, aim_ref, ldt_ref, bre_ref, bim_ref, cre_ref, cim_ref,
                     t_ref, et_ref, ft_ref, a16_ref):
    et_ref[...] = jnp.zeros(et_ref.shape, et_ref.dtype)
    ft_ref[...] = jnp.zeros(ft_ref.shape, ft_ref.dtype)
    row_id = lax.broadcasted_iota(jnp.int32, (CHUNK, SSM_STATE), 0)
    strip_lane = lax.broadcasted_iota(jnp.int32, (SSM_GROUP, CW), 1)
    nt = (((1,), (1,)), ((), ()))
    for gi in range(2):
        g = 2 * pl.program_id(1) + gi
        steps = _slot_step(row_id, g)
        lanes = slice(gi * SSM_STATE, (gi + 1) * SSM_STATE)
        rows = slice(gi * CW, (gi + 1) * CW)
        cre = cre_ref[gi]
        cim = cim_ref[gi]
        m_dir = []
        for d in range(2):
            a_re = are_ref[d, gi]
            a_im = aim_ref[d, gi]
            dt = jnp.exp(ldt_ref[d, gi])
            mag = jnp.exp(a_re * dt)
            abr = mag * jnp.cos(a_im * dt)
            abi = mag * jnp.sin(a_im * dt)
            den = a_re * a_re + a_im * a_im
            nr = abr - 1.0
            coef_re = (nr * a_re + abi * a_im) / den
            coef_im = (abi * a_re - nr * a_im) / den
            bbr, bbi = _cmul(coef_re, coef_im, bre_ref[d, gi], bim_ref[d, gi])

            squares = [(abr, abi)]
            for _ in range(4):
                squares.append(_cmul(*squares[-1], *squares[-1]))

            def power(n):
                pr = jnp.ones(n.shape, F32)
                pi = jnp.zeros(n.shape, F32)
                for b, (sr, si) in enumerate(squares):
                    nr_, ni_ = _cmul(pr, pi, sr, si)
                    hit = (n & (1 << b)) != 0
                    pr = jnp.where(hit, nr_, pr)
                    pi = jnp.where(hit, ni_, pi)
                return pr, pi

            to_end = (CHUNK - 1 - steps) if d == 0 else steps
            from_start = (steps + 1) if d == 0 else (CHUNK - steps)
            lag = row_id if d == 0 else (CHUNK - 1 - row_id)
            pr, pi = power(to_end)
            er = _outer_rows(pr, bbr) - _outer_rows(pi, bbi)
            ei = _outer_rows(pr, bbi) + _outer_rows(pi, bbr)
            et_ref[2 * d, rows, lanes] = er.astype(et_ref.dtype)
            et_ref[2 * d + 1, rows, lanes] = ei.astype(et_ref.dtype)
            pr, pi = power(from_start)
            ft_ref[2 * d, rows, lanes] = (_outer_rows(pr, cre) - _outer_rows(pi, cim)).astype(ft_ref.dtype)
            ft_ref[2 * d + 1, rows, lanes] = (-(_outer_rows(pr, cim) + _outer_rows(pi, cre))).astype(ft_ref.dtype)
            pr, pi = power(lag)
            zr = _outer_rows(pr, cre) - _outer_rows(pi, cim)
            zi = _outer_rows(pi, cre) + _outer_rows(pr, cim)
            m_dir.append(
                lax.dot_general(bbr, zr, nt, preferred_element_type=F32, precision=lax.Precision.HIGHEST)
                - lax.dot_general(bbi, zi, nt, preferred_element_type=F32, precision=lax.Precision.HIGHEST))
            p16r, p16i = squares[4]
            a16_ref[2 * d:2 * d + 1, lanes] = p16r
            a16_ref[2 * d + 1:2 * d + 2, lanes] = p16i
        mf, mb = m_dir
        edge = CW - SSM_GROUP
        low = mb + pltpu.roll(jnp.where(strip_lane < SSM_GROUP, mf, 0.0), edge, 1)
        high = jnp.where(strip_lane < edge, pltpu.roll(mf, edge, 1), 0.0)
        strip = jnp.concatenate([low, high], axis=1)
        turn = SSM_GROUP * (g & 7)
        for j in range(CHUNK):
            first = SSM_GROUP * (CHUNK - 1 - j)
            win = strip[:, first:first + CW]
            win = jnp.concatenate([pltpu.roll(win[:, LANES * h:LANES * (h + 1)], turn, 1)
                                   for h in range(CW // LANES)], axis=1)
            slot = (j & 8) + ((j + g) & 7)
            t_ref[gi, pl.ds(pl.multiple_of(SSM_GROUP * slot, SSM_GROUP), SSM_GROUP), :] = win.astype(t_ref.dtype)


def _ssm_prep(a_re, a_im, log_dt, b_re, b_im, c_re, c_im):
    g2 = SSM_GROUPS // 2
    a_spec = pl.BlockSpec((None, 2, 2, 1, SSM_STATE), lambda l, g: (l, 0, g, 0, 0))
    b_spec = pl.BlockSpec((None, 2, 2, SSM_GROUP, SSM_STATE), lambda l, g: (l, 0, g, 0, 0))
    c_spec = pl.BlockSpec((None, 2, SSM_GROUP, SSM_STATE), lambda l, g: (l, g, 0, 0))
    out_shapes = (
        jax.ShapeDtypeStruct((DEPTH, SSM_GROUPS, CW, CW), BF16),
        jax.ShapeDtypeStruct((DEPTH, g2, 4, 2 * CW, 2 * SSM_STATE), BF16),
        jax.ShapeDtypeStruct((DEPTH, g2, 4, 2 * CW, 2 * SSM_STATE), BF16),
        jax.ShapeDtypeStruct((DEPTH, g2, 4, 2 * SSM_STATE), F32),
    )
    return pl.pallas_call(
        _ssm_prep_kernel,
        grid=(DEPTH, g2),
        in_specs=[a_spec, a_spec,
                  pl.BlockSpec((None, 2, 2, 1, 1), lambda l, g: (l, 0, g, 0, 0)),
                  b_spec, b_spec, c_spec, c_spec],
        out_specs=[
            pl.BlockSpec((None, 2, CW, CW), lambda l, g: (l, g, 0, 0)),
            pl.BlockSpec((None, None, 4, 2 * CW, 2 * SSM_STATE), lambda l, g: (l, g, 0, 0, 0)),
            pl.BlockSpec((None, None, 4, 2 * CW, 2 * SSM_STATE), lambda l, g: (l, g, 0, 0, 0)),
            pl.BlockSpec((None, None, 4, 2 * SSM_STATE), lambda l, g: (l, g, 0, 0)),
        ],
        out_shape=out_shapes,
        compiler_params=_params(("arbitrary", "arbitrary")),
        name="ssm_prep",
    )(a_re.reshape(DEPTH, 2, SSM_GROUPS, 1, SSM_STATE), a_im.reshape(DEPTH, 2, SSM_GROUPS, 1, SSM_STATE),
      log_dt.reshape(DEPTH, 2, SSM_GROUPS, 1, 1),
      jnp.swapaxes(b_re, -1, -2), jnp.swapaxes(b_im, -1, -2), c_re, c_im)


def _ssm_kernel(nchunk, nb, u_ref, t_ref, et_ref, ft_ref, a16_ref, h0_ref, y_ref, fin_ref, e_ref, h_ref):
    u0 = u_ref[0]
    u1 = u_ref[1]
    e_all = jnp.dot(jnp.concatenate([u0, u1], axis=1), jnp.concatenate([et_ref[x] for x in range(4)], axis=1),
                    preferred_element_type=F32)
    for x in range(4):
        e_ref[x] = e_all[:, LANES * x:LANES * (x + 1)]
    afr, afi, abr, abi = (a16_ref[i:i + 1, :] for i in range(4))

    def body(i, carry):
        fr, fi, br, bi = carry
        kf = pl.ds(i, nb, stride=nchunk)
        kb = pl.ds(nchunk - 1 - i, nb, stride=nchunk)
        h_ref[0, kf, :] = fr
        h_ref[1, kf, :] = fi
        h_ref[2, kb, :] = br
        h_ref[3, kb, :] = bi
        return (afr * fr - afi * fi + e_ref[0, kf, :], afr * fi + afi * fr + e_ref[1, kf, :],
                abr * br - abi * bi + e_ref[2, kb, :], abr * bi + abi * br + e_ref[3, kb, :])

    fin = lax.fori_loop(0, nchunk, body, tuple(h0_ref[i] for i in range(4)), unroll=4)
    for i in range(4):
        fin_ref[i] = fin[i]
    nt = (((1,), (1,)), ((), ()))
    acc = lax.dot_general(jnp.concatenate([h_ref[x].astype(BF16) for x in range(4)], axis=1),
                          jnp.concatenate([ft_ref[x] for x in range(4)], axis=1), nt,
                          preferred_element_type=F32)
    y_ref[0] = jnp.dot(u0, t_ref[0], preferred_element_type=F32) + acc[:, :CW]
    y_ref[1] = jnp.dot(u1, t_ref[1], preferred_element_type=F32) + acc[:, CW:]


def _ssm_mix(ug, nb, seq, h0_re, h0_im, prep, l):
    tmat, et, ft, a16 = prep
    nchunk = seq // CHUNK
    r = ug.shape[1]
    g2 = SSM_GROUPS // 2
    width = SSM_GROUPS * SSM_STATE
    h0 = jnp.stack([h[:, d].reshape(nb, width).astype(F32) for d in range(2) for h in (h0_re, h0_im)], axis=0)
    pair = pl.BlockSpec((2, r, CW), lambda g: (g, 0, 0))
    mats = pl.BlockSpec((None, None, 4, 2 * CW, 2 * SSM_STATE), lambda g: (l, g, 0, 0, 0))
    state = pl.BlockSpec((4, nb, 2 * SSM_STATE), lambda g: (0, 0, g))
    y, fin = pl.pallas_call(
        functools.partial(_ssm_kernel, nchunk, nb),
        grid=(g2,),
        in_specs=[pair, pl.BlockSpec((None, 2, CW, CW), lambda g: (l, g, 0, 0)), mats, mats,
                  pl.BlockSpec((None, None, 4, 2 * SSM_STATE), lambda g: (l, g, 0, 0)), state],
        out_specs=[pair, state],
        out_shape=(jax.ShapeDtypeStruct((SSM_GROUPS, r, CW), F32), jax.ShapeDtypeStruct((4, nb, width), F32)),
        scratch_shapes=[pltpu.VMEM((4, r, 2 * SSM_STATE), F32), pltpu.VMEM((4, r, 2 * SSM_STATE), F32)],
        compiler_params=_params(("parallel",)),
        name="ssm",
    )(ug, tmat, et, ft, a16, h0)
    fin = fin.reshape(2, 2, nb, SSM_GROUPS, SSM_STATE)
    return y, fin


def _gelu_tanh(x):
    return 0.5 * x * (1.0 + jnp.tanh(math.sqrt(2.0 / math.pi) * (x + 0.044715 * (x * x * x))))


def _route(aff, bias):
    a = [aff[e:e + 1, :] for e in range(N_EXPERTS)]
    s = [a[e] + bias[e:e + 1, :] for e in range(N_EXPERTS)]
    keep = []
    for e in range(N_EXPERTS):
        g0 = (e // EXPERTS_PER_GROUP) * EXPERTS_PER_GROUP
        rank = None
        for j in range(g0, g0 + EXPERTS_PER_GROUP):
            if j == e:
                continue
            beats = (s[j] >= s[e]) if j < e else (s[j] > s[e])
            r = jnp.where(beats, 1.0, 0.0)
            rank = r if rank is None else rank + r
        keep.append(rank < 1.5)
    score = []
    for g in range(N_GROUPS):
        tot = None
        for e in range(g * EXPERTS_PER_GROUP, (g + 1) * EXPERTS_PER_GROUP):
            v = jnp.where(keep[e], s[e], 0.0)
            tot = v if tot is None else tot + v
        score.append(tot)
    gates = []
    picked = []
    for g in range(N_GROUPS):
        lost = None
        for j in range(N_GROUPS):
            if j == g:
                continue
            beats = (score[j] >= score[g]) if j < g else (score[j] > score[g])
            r = jnp.where(beats, 1.0, 0.0)
            lost = r if lost is None else lost + r
        chosen = lost < 0.5
        picked.append(jnp.where(chosen, 1.0, 0.0))
        w = [jnp.where(keep[e], a[e], 0.0) for e in range(g * EXPERTS_PER_GROUP, (g + 1) * EXPERTS_PER_GROUP)]
        tot = w[0] + w[1] + w[2] + w[3]
        for v in w:
            gates.append(jnp.where(chosen, v / tot, 0.0))
    return gates, picked


def _merge_kernel(x_ref, yg_ref, u_ref, at_ref, sga_ref, sgb_ref, mod_ref, d_ref, wglu_ref, bglu_ref, wsp_ref,
                  wap_ref, wo_ref, gffn_ref, wr_ref, rb_ref, tri_ref, ramp_ref,
                  xo_ref, hs_ref, mt_ref, ms_ref, of_ref, tok_ref):
    mod = mod_ref[...]
    gate1 = mod[:, 2 * D_MODEL:3 * D_MODEL]
    shift2 = mod[:, 3 * D_MODEL:4 * D_MODEL]
    scale2 = mod[:, 4 * D_MODEL:5 * D_MODEL]
    _from_chunk_rows(yg_ref, tok_ref)
    ys = jnp.concatenate([tok_ref[j] for j in range(LANE_TILES)], axis=1)
    y = _gelu_tanh(ys + d_ref[...] * u_ref[...].astype(F32))
    glu = jnp.dot(y.astype(BF16), wglu_ref[...], preferred_element_type=F32) + bglu_ref[...]
    y = (y * _sigmoid(glu)).astype(BF16)
    a_out = jnp.dot(y, wsp_ref[...], preferred_element_type=F32)
    b_out = jnp.dot(at_ref[...], wap_ref[...], preferred_element_type=F32)
    m = (sga_ref[...].astype(F32) * a_out + sgb_ref[...].astype(F32) * b_out).astype(BF16)
    x = x_ref[...] + gate1 * jnp.dot(m, wo_ref[...], preferred_element_type=F32)
    xo_ref[...] = x
    h2 = _rms(x, D_MODEL) * (gffn_ref[...] * (1.0 + scale2)) + shift2
    nt = (((1,), (1,)), ((), ()))
    h2_hi = h2.astype(BF16)
    h2_lo = (h2 - h2_hi.astype(F32)).astype(BF16)
    wr = wr_ref[...]
    wr_hi = wr.astype(BF16)
    wr_lo = (wr - wr_hi.astype(F32)).astype(BF16)
    logits = (lax.dot_general(wr_hi, h2_hi, nt, preferred_element_type=F32)
              + lax.dot_general(wr_hi, h2_lo, nt, preferred_element_type=F32)
              + lax.dot_general(wr_lo, h2_hi, nt, preferred_element_type=F32))
    gates, picked = _route(_sigmoid(logits), rb_ref[...])
    n = x.shape[0]
    onehot = jnp.concatenate(picked + [jnp.zeros((8 - N_GROUPS, n), F32)], axis=0)
    before = jnp.dot(onehot.astype(BF16), tri_ref[...], preferred_element_type=F32)
    count = [jnp.sum(p, axis=-1, keepdims=True) for p in picked]
    start = [jnp.zeros((1, 1), F32)]
    for g in range(N_GROUPS - 1):
        start.append(start[-1] + count[g])
    pos = sum(picked[g] * (start[g] + before[g:g + 1, :]) for g in range(N_GROUPS))
    gid = sum(float(g) * picked[g] for g in range(1, N_GROUPS))
    perm = jnp.where(ramp_ref[...] == pos, 1.0, 0.0).astype(BF16)
    hs_ref[...] = jnp.dot(perm, h2_hi, preferred_element_type=F32).astype(BF16)
    own = [sum(gates[EXPERTS_PER_GROUP * g + e] for g in range(N_GROUPS)) for e in range(EXPERTS_PER_GROUP)]
    meta = jnp.concatenate(own + [gid, pos, jnp.zeros((LANES - EXPERTS_PER_GROUP - 2, n), F32)], axis=0).T
    mt_ref[...] = meta
    hi = meta.astype(BF16)
    lo = (meta - hi.astype(F32)).astype(BF16)
    ms_ref[...] = (jnp.dot(perm, hi, preferred_element_type=F32)
                   + jnp.dot(perm, lo, preferred_element_type=F32))
    lane = lax.broadcasted_iota(jnp.int32, (8, LANES), 1)
    offs = sum(jnp.where(lane == g, start[g], 0.0) for g in range(1, N_GROUPS)) + jnp.where(lane == N_GROUPS, n, 0.0)
    of_ref[...] = offs.astype(jnp.int32)


def _merge(x, yg, u, attn, sga, sgb, mods, l, lat, wts):
    n = x.shape[0]
    tt = TOK_TILE
    per_seq = 2048 // tt
    mod_map = (lambda i: (l, 1 + i // per_seq, 0, 0)) if lat else (lambda i: (l, 0, 0, 0))
    row = lambda i: (i, 0)
    lw = lambda *shape: pl.BlockSpec((None,) + shape, lambda i: (l,) + (0,) * len(shape))
    return pl.pallas_call(
        _merge_kernel,
        grid=(n // tt,),
        in_specs=[
            pl.BlockSpec((tt, D_MODEL), row),
            pl.BlockSpec((SSM_GROUPS, TILE_CHUNKS, CW), lambda i: (0, i, 0)),
            pl.BlockSpec((tt, SSM_WIDTH), row),
            pl.BlockSpec((tt, ATTN_WIDTH), row), pl.BlockSpec((tt, D_MODEL), row), pl.BlockSpec((tt, D_MODEL), row),
            pl.BlockSpec((None, None, 1, 6 * D_MODEL), mod_map),
            lw(1, SSM_WIDTH), lw(SSM_WIDTH, SSM_WIDTH), lw(1, SSM_WIDTH), lw(SSM_WIDTH, D_MODEL),
            lw(ATTN_WIDTH, D_MODEL), lw(D_MODEL, D_MODEL), lw(1, D_MODEL),
            pl.BlockSpec((N_EXPERTS, D_MODEL), lambda i: (0, 0)),
            pl.BlockSpec((N_EXPERTS, 1), lambda i: (0, 0)),
            pl.BlockSpec((tt, tt), lambda i: (0, 0)), pl.BlockSpec((tt, tt), lambda i: (0, 0)),
        ],
        out_specs=[pl.BlockSpec((tt, D_MODEL), row), pl.BlockSpec((tt, D_MODEL), row),
                   pl.BlockSpec((tt, LANES), row), pl.BlockSpec((tt, LANES), row),
                   pl.BlockSpec((None, 8, LANES), lambda i: (i, 0, 0))],
        out_shape=(jax.ShapeDtypeStruct((n, D_MODEL), F32), jax.ShapeDtypeStruct((n, D_MODEL), BF16),
                   jax.ShapeDtypeStruct((n, LANES), F32), jax.ShapeDtypeStruct((n, LANES), F32),
                   jax.ShapeDtypeStruct((n // tt, 8, LANES), jnp.int32)),
        scratch_shapes=[pltpu.VMEM((LANE_TILES, tt, LANES), F32)],
        compiler_params=_params(("parallel",)),
        name="merge_lat" if lat else "merge_ctx",
    )(x, yg, u, attn, sga, sgb, mods, wts["ssm_d"], wts["w_glu"], wts["b_glu"], wts["w_ssm_proj"],
      wts["w_attn_proj"], wts["w_o"], wts["g_ffn"], wts["w_router_t"], wts["router_bias"],
      jnp.asarray(np.triu(np.ones((tt, tt), np.float32), 1), BF16),
      jnp.asarray(np.broadcast_to(np.arange(tt, dtype=np.float32)[:, None], (tt, tt))))


def _moe_kernel(offs_ref, x_ref, hs_ref, ms_ref, mt_ref, mod_ref, wg_ref, wu_ref, wd_ref, o_ref, acc_ref):
    tile = pl.program_id(0)
    acc_ref[...] = jnp.zeros(acc_ref.shape, F32)
    row_id = lax.broadcasted_iota(jnp.int32, (MOE_ROWS, 1), 0)

    def group_body(g, carry):
        seg_start = offs_ref[(N_GROUPS + 1) * tile + g]
        seg_end = offs_ref[(N_GROUPS + 1) * tile + g + 1]
        first = (seg_start >> 4) << 4
        span = jnp.where(seg_end > seg_start, seg_end - first, 0)
        n_pass = sum((span > k * MOE_ROWS).astype(jnp.int32) for k in range(pl.cdiv(TOK_TILE + 16, MOE_ROWS)))

        def pass_body(j, c):
            lo = first + MOE_ROWS * j
            s = pl.multiple_of(jnp.minimum(lo, TOK_TILE - MOE_ROWS), 16)
            rows = pl.ds(s, MOE_ROWS)
            meta = ms_ref[rows, :]
            group_id = lax.convert_element_type(g, F32)
            mine = (meta[:, EXPERTS_PER_GROUP:EXPERTS_PER_GROUP + 1] == group_id) & (row_id + s >= lo)
            own = jnp.where(mine, 1.0, 0.0)
            hs = hs_ref[rows, :]
            acts = []
            for e in range(EXPERTS_PER_GROUP):
                expert = EXPERTS_PER_GROUP * g + e
                hg = jnp.dot(hs, wg_ref[expert], preferred_element_type=F32)
                hu = jnp.dot(hs, wu_ref[expert], preferred_element_type=F32)
                acts.append(((hg * _sigmoid(hg)) * hu * (meta[:, e:e + 1] * own)).astype(BF16))
            acc_ref[rows, :] += jnp.dot(jnp.concatenate(acts, axis=1), wd_ref[g], preferred_element_type=F32)
            return c

        lax.fori_loop(0, n_pass, pass_body, 0)
        return carry

    lax.fori_loop(0, N_GROUPS, group_body, 0)

    slot = lax.broadcasted_iota(jnp.int32, (TOK_TILE, TOK_TILE), 1).astype(F32)
    unsort = jnp.where(mt_ref[:, EXPERTS_PER_GROUP + 1:EXPERTS_PER_GROUP + 2] == slot, 1.0, 0.0).astype(BF16)
    y = jnp.dot(unsort, acc_ref[...].astype(BF16), preferred_element_type=F32)
    o_ref[...] = x_ref[...] + mod_ref[:, 5 * D_MODEL:6 * D_MODEL] * y


def _moe(x, hs, meta_tok, meta_sorted, offs, mods, l, lat, wts):
    n = x.shape[0]
    tt = TOK_TILE
    per_seq = 2048 // tt
    mod_map = (lambda i, o: (l, 1 + i // per_seq, 0, 0)) if lat else (lambda i, o: (l, 0, 0, 0))
    row = lambda i, o: (i, 0)
    width = EXPERTS_PER_GROUP * D_EXPERT
    resident = lambda *shape: pl.BlockSpec((None,) + shape, lambda i, o: (l,) + (0,) * len(shape),
                                           pipeline_mode=pl.Buffered(1))
    grid_spec = pltpu.PrefetchScalarGridSpec(
        num_scalar_prefetch=1,
        grid=(n // tt,),
        in_specs=[
            pl.BlockSpec((tt, D_MODEL), row), pl.BlockSpec((tt, D_MODEL), row), pl.BlockSpec((tt, LANES), row),
            pl.BlockSpec((tt, LANES), row),
            pl.BlockSpec((None, None, 1, 6 * D_MODEL), mod_map),
            resident(N_EXPERTS, D_MODEL, D_EXPERT), resident(N_EXPERTS, D_MODEL, D_EXPERT),
            resident(N_GROUPS, width, D_MODEL),
        ],
        out_specs=pl.BlockSpec((tt, D_MODEL), row),
        scratch_shapes=[pltpu.VMEM((tt, D_MODEL), F32)],
    )
    return pl.pallas_call(
        _moe_kernel,
        grid_spec=grid_spec,
        out_shape=jax.ShapeDtypeStruct((n, D_MODEL), F32),
        compiler_params=_params(("arbitrary",), MOE_VMEM_LIMIT),
        name="moe_lat" if lat else "moe_ctx",
    )(offs[:, 0, :N_GROUPS + 1].reshape(-1), x, hs, meta_sorted, meta_tok, mods, wts["w_gate"], wts["w_up"],
      wts["w_down"])


def _pad_heads(w, width):
    lead = w.shape[:-1]
    w = w.reshape(lead + (N_HEADS, width))
    w = jnp.pad(w, [(0, 0)] * len(lead) + [(0, 0), (0, HEAD_PAD - width)])
    return w.reshape(lead + (N_HEADS * HEAD_PAD,))


def _swap_partners(w):
    lead = w.shape[:-1]
    w = w.reshape(lead + (-1, HEAD_PAD))
    half = ROPE_AXIS // 2
    a, b = QK_NOPE, QK_NOPE + ROPE_AXIS
    parts = [w[..., :a], w[..., a + half:b], w[..., a:a + half], w[..., b + half:b + 2 * half], w[..., b:b + half],
             w[..., b + 2 * half:]]
    return jnp.concatenate(parts, axis=-1).reshape(lead + (-1,))


def _layout_weights(w_in, g_qa, w_qb, g_q, g_kva, w_kvb, g_k, g_mix, w_attn_proj, ssm_d, ssm_w_glu, ssm_b_glu,
                    ssm_w_proj, w_o, g_ffn, w_router, router_bias, w_gate, w_up, w_down):
    n_a = SSM_WIDTH + Q_LORA + KV_LORA + QK_ROPE
    w_a = jnp.pad(w_in[:, :, :n_a], ((0, 0), (0, 0), (0, 1024 - n_a)))
    kv = w_kvb.reshape(DEPTH, KV_LORA, N_HEADS, QK_NOPE + V_HEAD)
    k_cols = _pad_heads(kv[..., :QK_NOPE].reshape(DEPTH, KV_LORA, N_HEADS * QK_NOPE), QK_NOPE)
    v_cols = _pad_heads(kv[..., QK_NOPE:].reshape(DEPTH, KV_LORA, ATTN_WIDTH), V_HEAD)
    place = np.zeros((KV_LORA, 2 * QK_PAD), np.float32)
    for h in range(N_HEADS):
        for r in range(QK_ROPE):
            place[r, h * HEAD_PAD + QK_NOPE + r] = 1.0
    w_k = jnp.concatenate([jnp.concatenate([k_cols, v_cols], axis=-1),
                           jnp.broadcast_to(jnp.asarray(place), (DEPTH,) + place.shape)], axis=1)
    w_qb_pad = _pad_heads(w_qb, QK_HEAD).astype(BF16)
    w_k = w_k.astype(BF16)
    pad_gain = lambda g: jnp.pad(g, ((0, 0), (0, HEAD_PAD - QK_HEAD))).reshape(DEPTH, 1, HEAD_PAD)
    vec = lambda g: g.reshape(DEPTH, 1, -1)
    return {
        "g_mix": vec(g_mix), "w_a": w_a.astype(BF16),
        "w_ga": w_in[:, :, n_a:n_a + D_MODEL].astype(BF16), "w_gb": w_in[:, :, n_a + D_MODEL:].astype(BF16),
        "g_qa": vec(g_qa), "w_qb": w_qb_pad, "w_qb_sw": _swap_partners(w_qb_pad),
        "g_q": pad_gain(g_q) * Q_SCALE, "g_q_sw": _swap_partners(pad_gain(g_q)) * Q_SCALE,
        "g_kva": vec(g_kva), "w_k": w_k, "w_k_sw": _swap_partners(w_k[:, :, :QK_PAD]),
        "g_k": pad_gain(g_k), "g_k_sw": _swap_partners(pad_gain(g_k)),
        "ssm_d": vec(ssm_d), "w_glu": ssm_w_glu.astype(BF16), "b_glu": vec(ssm_b_glu),
        "w_ssm_proj": ssm_w_proj.astype(BF16), "w_attn_proj": w_attn_proj.astype(BF16), "w_o": w_o.astype(BF16),
        "g_ffn": vec(g_ffn), "w_router_t": w_router.T, "router_bias": router_bias.reshape(N_EXPERTS, 1),
        "w_gate": w_gate.astype(BF16), "w_up": w_up.astype(BF16),
        "w_down": w_down.astype(BF16).reshape(DEPTH, N_GROUPS, EXPERTS_PER_GROUP * D_EXPERT, D_MODEL),
    }


def _rope_tables(n_tokens):
    pos = np.arange(n_tokens)
    inv = 1.0 / (ROPE_BASE ** (np.arange(ROPE_AXIS // 2, dtype=np.float32) * 2.0 / ROPE_AXIS))
    ang = np.zeros((n_tokens, HEAD_PAD), np.float32)
    half = ROPE_AXIS // 2
    row = (pos // GRID_W).astype(np.float32)[:, None] * inv
    col = (pos % GRID_W).astype(np.float32)[:, None] * inv
    first = np.zeros((HEAD_PAD,), bool)
    second = np.zeros((HEAD_PAD,), bool)
    for base, a in ((QK_NOPE, row), (QK_NOPE + ROPE_AXIS, col)):
        ang[:, base:base + half] = a
        ang[:, base + half:base + 2 * half] = a
        first[base:base + half] = True
        second[base + half:base + 2 * half] = True
    ang = jnp.asarray(ang)
    cos = jnp.where(jnp.asarray(first | second), jnp.cos(ang), 1.0)
    sin = jnp.sin(ang)
    return cos, jnp.where(jnp.asarray(first), -sin, jnp.where(jnp.asarray(second), sin, 0.0))


def kernel(x_prompt, x_sample, c, cache_ckv, cache_krope, state_ssm_re, state_ssm_im, c_ctx, w_ada, b_ada, g_mix, w_in, g_qa, w_qb, g_q, g_kva, w_kvb, g_k, w_attn_proj, ssm_a_re, ssm_a_im, ssm_log_dt, ssm_b_re, ssm_b_im, ssm_c_re, ssm_c_im, ssm_d, ssm_w_glu, ssm_b_glu, ssm_w_proj, w_o, g_ffn, w_router, router_bias, w_gate, w_up, w_down):
    nb_c, seq_c, _ = x_prompt.shape
    nb_l, seq_l, _ = x_sample.shape
    wts = _layout_weights(w_in, g_qa, w_qb, g_q, g_kva, w_kvb, g_k, g_mix, w_attn_proj, ssm_d, ssm_w_glu,
                          ssm_b_glu, ssm_w_proj, w_o, g_ffn, w_router, router_bias, w_gate, w_up, w_down)
    cvec = jnp.concatenate([c_ctx[None, :], c, jnp.zeros((8 - 1 - nb_l, D_MODEL), F32)], axis=0)
    mods = _mods(cvec, w_ada, b_ada).reshape(DEPTH, 8, 1, 6 * D_MODEL)
    prep = _ssm_prep(ssm_a_re, ssm_a_im, ssm_log_dt, ssm_b_re, ssm_b_im, ssm_c_re, ssm_c_im)
    tabs = _rope_tables(seq_l)
    kc, vc = _cachekv(cache_ckv, cache_krope, wts)

    xp = x_prompt.reshape(nb_c * seq_c, D_MODEL)
    xs = x_sample.reshape(nb_l * seq_l, D_MODEL)
    zeros_c = jnp.zeros((nb_c, 2, SSM_GROUPS, SSM_STATE), F32)
    ckvs, kropes, fins = [], [], []
    for l in range(DEPTH):
        u, ug, q, k, v, sga, sgb, ckv, krope = _inproj(xp, mods, l, False, wts, tabs)
        ckvs.append(ckv)
        kropes.append(krope)
        yg, fin = _ssm_mix(ug, nb_c, seq_c, zeros_c, zeros_c, prep, l)
        fins.append(fin)
        attn = _attn_ctx(q, k, v, seq_c)
        xp, *routed = _merge(xp, yg, u, attn, sga, sgb, mods, l, False, wts)
        xp = _moe(xp, *routed, mods, l, False, wts)

        u, ug, q, k, v, sga, sgb, _, _ = _inproj(xs, mods, l, True, wts, tabs)
        yg, _ = _ssm_mix(ug, nb_l, seq_l, state_ssm_re[:, l], state_ssm_im[:, l], prep, l)
        attn = _attn_lat(q, k, v, kc, vc, l, seq_l, TOK_TILE)
        xs, *routed = _merge(xs, yg, u, attn, sga, sgb, mods, l, True, wts)
        xs = _moe(xs, *routed, mods, l, True, wts)

    new_ckv = jnp.stack(ckvs, axis=0).reshape(DEPTH, nb_c, seq_c, KV_LORA).transpose(1, 0, 2, 3)
    new_krope = jnp.stack(kropes, axis=0).reshape(DEPTH, nb_c, seq_c, QK_ROPE).transpose(1, 0, 2, 3)
    fin = jnp.stack(fins, axis=0)
    new_re = jnp.transpose(fin[:, :, 0], (2, 0, 1, 3, 4))
    new_im = jnp.transpose(fin[:, :, 1], (2, 0, 1, 3, 4))
    return (xp.reshape(nb_c, seq_c, D_MODEL), xs.reshape(nb_l, seq_l, D_MODEL), new_ckv, new_krope, new_re, new_im)
```

```python
import functools
import math

import jax
import jax.numpy as jnp
import numpy as np
from jax import lax
from jax.experimental import pallas as pl
from jax.experimental.pallas import tpu as pltpu

F32 = jnp.float32
BF16 = jnp.bfloat16

D_MODEL = 1024
DEPTH = 4
GRID_W = 64
SSM_WIDTH = 512
SSM_GROUP = 16
SSM_GROUPS = 32
SSM_STATE = 64
N_HEADS = 8
QK_NOPE = 64
QK_ROPE = 32
QK_HEAD = 96
V_HEAD = 64
Q_LORA = 256
KV_LORA = 128
ATTN_WIDTH = 512
ROPE_AXIS = 16
ROPE_BASE = 10000.0
N_EXPERTS = 16
N_GROUPS = 4
EXPERTS_PER_GROUP = 4
D_EXPERT = 256
EPS = 1e-6

LANES = 128
HEAD_PAD = LANES
QK_PAD = N_HEADS * HEAD_PAD
CHUNK = 16
CW = CHUNK * SSM_GROUP
TOK_TILE = 512
TILE_CHUNKS = TOK_TILE // CHUNK
LANE_TILES = SSM_WIDTH // LANES
GROUPS_PER_TILE = LANES // SSM_GROUP
Q_SCALE = math.log2(math.e) / math.sqrt(QK_HEAD)
MOE_PASS_ROWS = (128, 160, 192, 256)
VMEM_LIMIT = 48 * 1024 * 1024
MOE_VMEM_LIMIT = 56 * 1024 * 1024


def _params(sem, vmem=VMEM_LIMIT):
    return pltpu.CompilerParams(dimension_semantics=sem, vmem_limit_bytes=vmem)


def _sigmoid(x):
    return 1.0 / (1.0 + jnp.exp(-x))


def _rms(x, n):
    return x * lax.rsqrt(jnp.sum(x * x, axis=-1, keepdims=True) * (1.0 / n) + EPS)


def _mods_kernel(c_ref, w_ref, b_ref, o_ref):
    c = c_ref[...]
    cs = (c * _sigmoid(c)).astype(BF16)
    o_ref[...] = jnp.dot(cs, w_ref[...].astype(BF16), preferred_element_type=F32) + b_ref[...]


def _mods(cvec, w_ada, b_ada):
    nb = 1536
    return pl.pallas_call(
        _mods_kernel,
        grid=(DEPTH, 6 * D_MODEL // nb),
        in_specs=[
            pl.BlockSpec((8, D_MODEL), lambda l, j: (0, 0)),
            pl.BlockSpec((None, D_MODEL, nb), lambda l, j: (l, 0, j)),
            pl.BlockSpec((None, 1, nb), lambda l, j: (l, 0, j)),
        ],
        out_specs=pl.BlockSpec((None, 8, nb), lambda l, j: (l, 0, j)),
        out_shape=jax.ShapeDtypeStruct((DEPTH, 8, 6 * D_MODEL), F32),
        compiler_params=_params(("arbitrary", "arbitrary")),
        name="mods",
    )(cvec, w_ada, b_ada.reshape(DEPTH, 1, 6 * D_MODEL))


def _slot_step(slot, g):
    return (slot & 8) + ((slot - g) & 7)


def _lane_segment():
    return lax.broadcasted_iota(jnp.int32, (TILE_CHUNKS, LANES), 1) // SSM_GROUP


def _to_chunk_rows(tok_ref, ug_ref):
    seg = _lane_segment()
    for j in range(LANE_TILES):
        for half in range(CHUNK // 8):
            rolled = []
            for r in range(8):
                v = tok_ref[j, pl.ds(8 * half + r, TILE_CHUNKS, stride=CHUNK), :]
                rolled.append(pltpu.roll(v, SSM_GROUP * r, 1) if r else v)
            for gg in range(GROUPS_PER_TILE):
                acc = rolled[(0 - gg) % 8]
                for m in range(1, 8):
                    acc = jnp.where(seg == m, rolled[(m - gg) % 8], acc)
                ug_ref[GROUPS_PER_TILE * j + gg, :, LANES * half:LANES * (half + 1)] = acc.astype(ug_ref.dtype)


def _from_chunk_rows(yg_ref, tok_ref):
    seg = _lane_segment()
    for j in range(LANE_TILES):
        for half in range(CHUNK // 8):
            ys = [yg_ref[GROUPS_PER_TILE * j + gg, :, LANES * half:LANES * (half + 1)]
                  for gg in range(GROUPS_PER_TILE)]
            for r in range(8):
                acc = ys[(0 - r) % 8]
                for m in range(1, 8):
                    acc = jnp.where(seg == m, ys[(m - r) % 8], acc)
                out = pltpu.roll(acc, LANES - SSM_GROUP * r, 1) if r else acc
                tok_ref[j, pl.ds(8 * half + r, TILE_CHUNKS, stride=CHUNK), :] = out


def _head_norm_rope(t, gain, swapped):
    outs = []
    for h in range(N_HEADS):
        lanes = slice(h * HEAD_PAD, (h + 1) * HEAD_PAD)
        th = t[:, lanes]
        scale = lax.rsqrt(jnp.sum(th * th, axis=-1, keepdims=True) * (1.0 / QK_HEAD) + EPS)
        y = th * gain
        if swapped is not None:
            y = y + swapped[0][:, lanes] * swapped[1]
        outs.append(y * scale)
    return jnp.concatenate(outs, axis=1)


def _with_ones(v):
    lane = lax.broadcasted_iota(jnp.int32, (1, v.shape[1]), 1)
    return jnp.where((lane & (HEAD_PAD - 1)) == V_HEAD, 1.0, v)


def _kv_dot(ckvn, krp, wk_ref):
    return (jnp.dot(ckvn, wk_ref[0:KV_LORA, :], preferred_element_type=F32)
            + jnp.dot(krp, wk_ref[KV_LORA:KV_LORA + krp.shape[1], :], preferred_element_type=F32))


def _inproj_kernel(use_rope, x_ref, mod_ref, gmix_ref, wa_ref, wga_ref, wgb_ref, gqa_ref, wqb_ref, wqbs_ref, gq_ref,
                   gqs_ref, gkva_ref, wk_ref, wks_ref, gk_ref, gks_ref, cos_ref, sin_ref,
                   u_ref, ug_ref, q_ref, k_ref, v_ref, sga_ref, sgb_ref, ckv_ref, kr_ref, tok_ref):
    mod = mod_ref[...]
    shift1 = mod[:, 0:D_MODEL]
    gain1 = gmix_ref[...] * (1.0 + mod[:, D_MODEL:2 * D_MODEL])
    h = (_rms(x_ref[...], D_MODEL) * gain1 + shift1).astype(BF16)
    za = jnp.dot(h, wa_ref[...], preferred_element_type=F32)
    sga_ref[...] = _sigmoid(jnp.dot(h, wga_ref[...], preferred_element_type=F32)).astype(BF16)
    sgb_ref[...] = _sigmoid(jnp.dot(h, wgb_ref[...], preferred_element_type=F32)).astype(BF16)
    u_ref[...] = za[:, :SSM_WIDTH].astype(BF16)
    for j in range(LANE_TILES):
        tok_ref[j] = za[:, LANES * j:LANES * (j + 1)]
    _to_chunk_rows(tok_ref, ug_ref)
    qa = za[:, SSM_WIDTH:SSM_WIDTH + Q_LORA]
    ckv = za[:, SSM_WIDTH + Q_LORA:SSM_WIDTH + Q_LORA + KV_LORA]
    krp = za[:, SSM_WIDTH + Q_LORA + KV_LORA:]
    qn = (_rms(qa, Q_LORA) * gqa_ref[...]).astype(BF16)
    qr = jnp.dot(qn, wqb_ref[...], preferred_element_type=F32)
    ckvn = _rms(ckv, KV_LORA) * gkva_ref[...]
    ckv_ref[...] = ckvn
    kr_ref[...] = krp[:, :QK_ROPE]
    ckvn = ckvn.astype(BF16)
    krp = krp.astype(BF16)
    kv = _kv_dot(ckvn, krp, wk_ref)
    gq = gq_ref[...]
    gk = gk_ref[...]
    q_sw = k_sw = None
    if use_rope:
        cos = cos_ref[...]
        sin = sin_ref[...]
        q_sw = (jnp.dot(qn, wqbs_ref[...], preferred_element_type=F32), gqs_ref[...] * sin)
        k_sw = (_kv_dot(ckvn, krp, wks_ref), gks_ref[...] * sin)
        gq = gq * cos
        gk = gk * cos
    q_ref[...] = _head_norm_rope(qr, gq, q_sw).astype(BF16)
    k_ref[...] = _head_norm_rope(kv[:, :QK_PAD], gk, k_sw).astype(BF16)
    v_ref[...] = _with_ones(kv[:, QK_PAD:]).astype(BF16)


def _inproj(x, mods, l, lat, wts, tabs):
    n = x.shape[0]
    tt = TOK_TILE
    per_seq = 2048 // tt
    if lat:
        mod_map = lambda i: (l, 1 + i // per_seq, 0, 0)
        tab_map = lambda i: (i % per_seq, 0)
    else:
        mod_map = lambda i: (l, 0, 0, 0)
        tab_map = lambda i: (0, 0)
    row = lambda i: (i, 0)
    lw = lambda *shape: pl.BlockSpec((None,) + shape, lambda i: (l,) + (0,) * len(shape))
    tab = pl.BlockSpec((tt, HEAD_PAD), tab_map)
    row_shapes = (
        jax.ShapeDtypeStruct((n, SSM_WIDTH), BF16),
        jax.ShapeDtypeStruct((n, QK_PAD), BF16),
        jax.ShapeDtypeStruct((n, QK_PAD), BF16),
        jax.ShapeDtypeStruct((n, QK_PAD), BF16),
        jax.ShapeDtypeStruct((n, D_MODEL), BF16),
        jax.ShapeDtypeStruct((n, D_MODEL), BF16),
        jax.ShapeDtypeStruct((n, KV_LORA), F32),
        jax.ShapeDtypeStruct((n, QK_ROPE), F32),
    )
    row_specs = [pl.BlockSpec((tt, s.shape[1]), row) for s in row_shapes]
    ug_shape = jax.ShapeDtypeStruct((SSM_GROUPS, n // CHUNK, CW), BF16)
    ug_spec = pl.BlockSpec((SSM_GROUPS, TILE_CHUNKS, CW), lambda i: (0, i, 0))
    outs = pl.pallas_call(
        functools.partial(_inproj_kernel, lat),
        grid=(n // tt,),
        in_specs=[
            pl.BlockSpec((tt, D_MODEL), row),
            pl.BlockSpec((None, None, 1, 6 * D_MODEL), mod_map),
            lw(1, D_MODEL), lw(D_MODEL, 1024), lw(D_MODEL, D_MODEL), lw(D_MODEL, D_MODEL),
            lw(1, Q_LORA), lw(Q_LORA, QK_PAD), lw(Q_LORA, QK_PAD), lw(1, HEAD_PAD), lw(1, HEAD_PAD), lw(1, KV_LORA),
            lw(2 * KV_LORA, 2 * QK_PAD), lw(2 * KV_LORA, QK_PAD), lw(1, HEAD_PAD), lw(1, HEAD_PAD),
            tab, tab,
        ],
        out_specs=row_specs[:1] + [ug_spec] + row_specs[1:],
        out_shape=row_shapes[:1] + (ug_shape,) + row_shapes[1:],
        scratch_shapes=[pltpu.VMEM((LANE_TILES, tt, LANES), F32)],
        compiler_params=_params(("parallel",)),
        name="inproj_lat" if lat else "inproj_ctx",
    )(x, mods, wts["g_mix"], wts["w_a"], wts["w_ga"], wts["w_gb"], wts["g_qa"], wts["w_qb"], wts["w_qb_sw"],
      wts["g_q"], wts["g_q_sw"], wts["g_kva"], wts["w_k"], wts["w_k_sw"], wts["g_k"], wts["g_k_sw"], *tabs)
    return outs


def _cachekv_kernel(ckv_ref, kr_ref, wk_ref, gk_ref, k_ref, v_ref):
    kv = _kv_dot(ckv_ref[...].astype(BF16), kr_ref[...].astype(BF16), wk_ref)
    k_ref[...] = _head_norm_rope(kv[:, :QK_PAD], gk_ref[...], None).astype(BF16)
    v_ref[...] = _with_ones(kv[:, QK_PAD:]).astype(BF16)


def _cachekv(cache_ckv, cache_krope, wts):
    nb, _, past, _ = cache_ckv.shape
    return pl.pallas_call(
        _cachekv_kernel,
        grid=(DEPTH, nb),
        in_specs=[
            pl.BlockSpec((None, None, past, KV_LORA), lambda l, b: (b, l, 0, 0)),
            pl.BlockSpec((None, None, past, QK_ROPE), lambda l, b: (b, l, 0, 0)),
            pl.BlockSpec((None, 2 * KV_LORA, 2 * QK_PAD), lambda l, b: (l, 0, 0)),
            pl.BlockSpec((None, 1, HEAD_PAD), lambda l, b: (l, 0, 0)),
        ],
        out_specs=[
            pl.BlockSpec((None, None, past, QK_PAD), lambda l, b: (l, b, 0, 0)),
            pl.BlockSpec((None, None, past, QK_PAD), lambda l, b: (l, b, 0, 0)),
        ],
        out_shape=(jax.ShapeDtypeStruct((DEPTH, nb, past, QK_PAD), BF16),
                   jax.ShapeDtypeStruct((DEPTH, nb, past, QK_PAD), BF16)),
        compiler_params=_params(("arbitrary", "arbitrary")),
        name="cache_kv",
    )(cache_ckv, cache_krope, wts["w_k"], wts["g_k"])


def _attn_heads(q_ref, segs, o_ref, den_on_mxu):
    for h in range(N_HEADS):
        qh = q_ref[:, h * HEAD_PAD:(h + 1) * HEAD_PAD]
        scores = [lax.dot_general(qh, k_ref[:, h * HEAD_PAD:(h + 1) * HEAD_PAD], (((1,), (1,)), ((), ())),
                                  preferred_element_type=F32) for k_ref, _ in segs]
        m = scores[0].max(axis=-1, keepdims=True)
        for s in scores[1:]:
            m = jnp.maximum(m, s.max(axis=-1, keepdims=True))
        acc = None
        den = None
        for s, (_, v_ref) in zip(scores, segs):
            if den_on_mxu:
                p = jnp.exp2((s - m).astype(BF16))
                a = jnp.dot(p, v_ref[:, h * HEAD_PAD:(h + 1) * HEAD_PAD], preferred_element_type=F32)
            else:
                p = jnp.exp2(s - m)
                d = p.sum(axis=-1, keepdims=True)
                den = d if den is None else den + d
                a = jnp.dot(p.astype(BF16), v_ref[:, h * HEAD_PAD:h * HEAD_PAD + V_HEAD],
                            preferred_element_type=F32)
            acc = a if acc is None else acc + a
        if den_on_mxu:
            den = acc[:, V_HEAD:V_HEAD + 1]
        o_ref[:, h * V_HEAD:(h + 1) * V_HEAD] = (acc[:, :V_HEAD] / den).astype(o_ref.dtype)


def _attn_ctx_kernel(q_ref, k_ref, v_ref, o_ref):
    for b in range(q_ref.shape[0]):
        _attn_heads(q_ref.at[b], [(k_ref.at[b], v_ref.at[b])], o_ref.at[b], False)


def _attn_lat_kernel(q_ref, kc_ref, vc_ref, kl_ref, vl_ref, o_ref):
    _attn_heads(q_ref, [(kc_ref, vc_ref), (kl_ref, vl_ref)], o_ref, True)


def _attn_ctx(q, k, v, seq):
    n = q.shape[0]
    nb = n // seq
    per_step = 2
    blk = lambda w: pl.BlockSpec((per_step, seq, w), lambda b: (b, 0, 0))
    seqs = lambda a: a.reshape(nb, seq, a.shape[1])
    return pl.pallas_call(
        _attn_ctx_kernel,
        grid=(nb // per_step,),
        in_specs=[blk(QK_PAD), blk(QK_PAD), blk(QK_PAD)],
        out_specs=blk(ATTN_WIDTH),
        out_shape=jax.ShapeDtypeStruct((nb, seq, ATTN_WIDTH), BF16),
        compiler_params=_params(("parallel",)),
        name="attn_ctx",
    )(seqs(q), seqs(k), seqs(v)).reshape(n, ATTN_WIDTH)


def _attn_lat(q, k, v, kc, vc, l, seq, tq):
    n = q.shape[0]
    per = seq // tq
    past = kc.shape[2]
    return pl.pallas_call(
        _attn_lat_kernel,
        grid=(n // seq, per),
        in_specs=[
            pl.BlockSpec((tq, QK_PAD), lambda b, j: (b * per + j, 0)),
            pl.BlockSpec((None, None, past, QK_PAD), lambda b, j: (l, b, 0, 0)),
            pl.BlockSpec((None, None, past, QK_PAD), lambda b, j: (l, b, 0, 0)),
            pl.BlockSpec((seq, QK_PAD), lambda b, j: (b, 0), pipeline_mode=pl.Buffered(1)),
            pl.BlockSpec((seq, QK_PAD), lambda b, j: (b, 0), pipeline_mode=pl.Buffered(1)),
        ],
        out_specs=pl.BlockSpec((tq, ATTN_WIDTH), lambda b, j: (b * per + j, 0)),
        out_shape=jax.ShapeDtypeStruct((n, ATTN_WIDTH), BF16),
        compiler_params=_params(("parallel", "arbitrary"), MOE_VMEM_LIMIT),
        name="attn_lat",
    )(q, kc, vc, k, v)


def _cmul(ar, ai, br, bi):
    return ar * br - ai * bi, ar * bi + ai * br


def _outer_rows(pw, m):
    return (pw[:, None, :] * m[None, :, :]).reshape(CW, SSM_STATE)


def _ssm_prep_kernel(are_ref, aim_ref, ldt_ref, bre_ref, bim_ref, cre_ref, cim_ref,
                     t_ref, et_ref, ft_ref, a16_ref):
    et_ref[...] = jnp.zeros(et_ref.shape, et_ref.dtype)
    ft_ref[...] = jnp.zeros(ft_ref.shape, ft_ref.dtype)
    row_id = lax.broadcasted_iota(jnp.int32, (CHUNK, SSM_STATE), 0)
    strip_lane = lax.broadcasted_iota(jnp.int32, (SSM_GROUP, CW), 1)
    nt = (((1,), (1,)), ((), ()))
    for gi in range(2):
        g = 2 * pl.program_id(1) + gi
        steps = _slot_step(row_id, g)
        lanes = slice(gi * SSM_STATE, (gi + 1) * SSM_STATE)
        rows = slice(gi * CW, (gi + 1) * CW)
        cre = cre_ref[gi]
        cim = cim_ref[gi]
        m_dir = []
        for d in range(2):
            a_re = are_ref[d, gi]
            a_im = aim_ref[d, gi]
            dt = jnp.exp(ldt_ref[d, gi])
            mag = jnp.exp(a_re * dt)
            abr = mag * jnp.cos(a_im * dt)
            abi = mag * jnp.sin(a_im * dt)
            den = a_re * a_re + a_im * a_im
            nr = abr - 1.0
            coef_re = (nr * a_re + abi * a_im) / den
            coef_im = (abi * a_re - nr * a_im) / den
            bbr, bbi = _cmul(coef_re, coef_im, bre_ref[d, gi], bim_ref[d, gi])

            squares = [(abr, abi)]
            for _ in range(4):
                squares.append(_cmul(*squares[-1], *squares[-1]))

            def power(n):
                pr = jnp.ones(n.shape, F32)
                pi = jnp.zeros(n.shape, F32)
                for b, (sr, si) in enumerate(squares):
                    nr_, ni_ = _cmul(pr, pi, sr, si)
                    hit = (n & (1 << b)) != 0
                    pr = jnp.where(hit, nr_, pr)
                    pi = jnp.where(hit, ni_, pi)
                return pr, pi

            to_end = (CHUNK - 1 - steps) if d == 0 else steps
            from_start = (steps + 1) if d == 0 else (CHUNK - steps)
            lag = row_id if d == 0 else (CHUNK - 1 - row_id)
            pr, pi = power(to_end)
            er = _outer_rows(pr, bbr) - _outer_rows(pi, bbi)
            ei = _outer_rows(pr, bbi) + _outer_rows(pi, bbr)
            et_ref[2 * d, rows, lanes] = er.astype(et_ref.dtype)
            et_ref[2 * d + 1, rows, lanes] = ei.astype(et_ref.dtype)
            pr, pi = power(from_start)
            ft_ref[2 * d, rows, lanes] = (_outer_rows(pr, cre) - _outer_rows(pi, cim)).astype(ft_ref.dtype)
            ft_ref[2 * d + 1, rows, lanes] = (-(_outer_rows(pr, cim) + _outer_rows(pi, cre))).astype(ft_ref.dtype)
            pr, pi = power(lag)
            zr = _outer_rows(pr, cre) - _outer_rows(pi, cim)
            zi = _outer_rows(pi, cre) + _outer_rows(pr, cim)
            m_dir.append(
                lax.dot_general(bbr, zr, nt, preferred_element_type=F32, precision=lax.Precision.HIGHEST)
                - lax.dot_general(bbi, zi, nt, preferred_element_type=F32, precision=lax.Precision.HIGHEST))
            p16r, p16i = squares[4]
            a16_ref[2 * d:2 * d + 1, lanes] = p16r
            a16_ref[2 * d + 1:2 * d + 2, lanes] = p16i
        mf, mb = m_dir
        edge = CW - SSM_GROUP
        low = mb + pltpu.roll(jnp.where(strip_lane < SSM_GROUP, mf, 0.0), edge, 1)
        high = jnp.where(strip_lane < edge, pltpu.roll(mf, edge, 1), 0.0)
        strip = jnp.concatenate([low, high], axis=1)
        turn = SSM_GROUP * (g & 7)
        for j in range(CHUNK):
            first = SSM_GROUP * (CHUNK - 1 - j)
            win = strip[:, first:first + CW]
            win = jnp.concatenate([pltpu.roll(win[:, LANES * h:LANES * (h + 1)], turn, 1)
                                   for h in range(CW // LANES)], axis=1)
            slot = (j & 8) + ((j + g) & 7)
            t_ref[gi, pl.ds(pl.multiple_of(SSM_GROUP * slot, SSM_GROUP), SSM_GROUP), :] = win.astype(t_ref.dtype)


def _ssm_prep(a_re, a_im, log_dt, b_re, b_im, c_re, c_im):
    g2 = SSM_GROUPS // 2
    a_spec = pl.BlockSpec((None, 2, 2, 1, SSM_STATE), lambda l, g: (l, 0, g, 0, 0))
    b_spec = pl.BlockSpec((None, 2, 2, SSM_GROUP, SSM_STATE), lambda l, g: (l, 0, g, 0, 0))
    c_spec = pl.BlockSpec((None, 2, SSM_GROUP, SSM_STATE), lambda l, g: (l, g, 0, 0))
    out_shapes = (
        jax.ShapeDtypeStruct((DEPTH, SSM_GROUPS, CW, CW), BF16),
        jax.ShapeDtypeStruct((DEPTH, g2, 4, 2 * CW, 2 * SSM_STATE), BF16),
        jax.ShapeDtypeStruct((DEPTH, g2, 4, 2 * CW, 2 * SSM_STATE), BF16),
        jax.ShapeDtypeStruct((DEPTH, g2, 4, 2 * SSM_STATE), F32),
    )
    return pl.pallas_call(
        _ssm_prep_kernel,
        grid=(DEPTH, g2),
        in_specs=[a_spec, a_spec,
                  pl.BlockSpec((None, 2, 2, 1, 1), lambda l, g: (l, 0, g, 0, 0)),
                  b_spec, b_spec, c_spec, c_spec],
        out_specs=[
            pl.BlockSpec((None, 2, CW, CW), lambda l, g: (l, g, 0, 0)),
            pl.BlockSpec((None, None, 4, 2 * CW, 2 * SSM_STATE), lambda l, g: (l, g, 0, 0, 0)),
            pl.BlockSpec((None, None, 4, 2 * CW, 2 * SSM_STATE), lambda l, g: (l, g, 0, 0, 0)),
            pl.BlockSpec((None, None, 4, 2 * SSM_STATE), lambda l, g: (l, g, 0, 0)),
        ],
        out_shape=out_shapes,
        compiler_params=_params(("arbitrary", "arbitrary")),
        name="ssm_prep",
    )(a_re.reshape(DEPTH, 2, SSM_GROUPS, 1, SSM_STATE), a_im.reshape(DEPTH, 2, SSM_GROUPS, 1, SSM_STATE),
      log_dt.reshape(DEPTH, 2, SSM_GROUPS, 1, 1),
      jnp.swapaxes(b_re, -1, -2), jnp.swapaxes(b_im, -1, -2), c_re, c_im)


def _ssm_kernel(nchunk, nb, u_ref, t_ref, et_ref, ft_ref, a16_ref, h0_ref, y_ref, fin_ref, e_ref, h_ref):
    u0 = u_ref[0]
    u1 = u_ref[1]
    e_all = jnp.dot(jnp.concatenate([u0, u1], axis=1), jnp.concatenate([et_ref[x] for x in range(4)], axis=1),
                    preferred_element_type=F32)
    for x in range(4):
        e_ref[x] = e_all[:, LANES * x:LANES * (x + 1)]
    afr, afi, abr, abi = (a16_ref[i:i + 1, :] for i in range(4))

    def body(i, carry):
        fr, fi, br, bi = carry
        kf = pl.ds(i, nb, stride=nchunk)
        kb = pl.ds(nchunk - 1 - i, nb, stride=nchunk)
        h_ref[0, kf, :] = fr
        h_ref[1, kf, :] = fi
        h_ref[2, kb, :] = br
        h_ref[3, kb, :] = bi
        return (afr * fr - afi * fi + e_ref[0, kf, :], afr * fi + afi * fr + e_ref[1, kf, :],
                abr * br - abi * bi + e_ref[2, kb, :], abr * bi + abi * br + e_ref[3, kb, :])

    fin = lax.fori_loop(0, nchunk, body, tuple(h0_ref[i] for i in range(4)), unroll=4)
    for i in range(4):
        fin_ref[i] = fin[i]
    nt = (((1,), (1,)), ((), ()))
    acc = lax.dot_general(jnp.concatenate([h_ref[x].astype(BF16) for x in range(4)], axis=1),
                          jnp.concatenate([ft_ref[x] for x in range(4)], axis=1), nt,
                          preferred_element_type=F32)
    y_ref[0] = jnp.dot(u0, t_ref[0], preferred_element_type=F32) + acc[:, :CW]
    y_ref[1] = jnp.dot(u1, t_ref[1], preferred_element_type=F32) + acc[:, CW:]


def _ssm_mix(ug, nb, seq, h0_re, h0_im, prep, l):
    tmat, et, ft, a16 = prep
    nchunk = seq // CHUNK
    r = ug.shape[1]
    g2 = SSM_GROUPS // 2
    width = SSM_GROUPS * SSM_STATE
    h0 = jnp.stack([h[:, d].reshape(nb, width).astype(F32) for d in range(2) for h in (h0_re, h0_im)], axis=0)
    pair = pl.BlockSpec((2, r, CW), lambda g: (g, 0, 0))
    mats = pl.BlockSpec((None, None, 4, 2 * CW, 2 * SSM_STATE), lambda g: (l, g, 0, 0, 0))
    state = pl.BlockSpec((4, nb, 2 * SSM_STATE), lambda g: (0, 0, g))
    y, fin = pl.pallas_call(
        functools.partial(_ssm_kernel, nchunk, nb),
        grid=(g2,),
        in_specs=[pair, pl.BlockSpec((None, 2, CW, CW), lambda g: (l, g, 0, 0)), mats, mats,
                  pl.BlockSpec((None, None, 4, 2 * SSM_STATE), lambda g: (l, g, 0, 0)), state],
        out_specs=[pair, state],
        out_shape=(jax.ShapeDtypeStruct((SSM_GROUPS, r, CW), F32), jax.ShapeDtypeStruct((4, nb, width), F32)),
        scratch_shapes=[pltpu.VMEM((4, r, 2 * SSM_STATE), F32), pltpu.VMEM((4, r, 2 * SSM_STATE), F32)],
        compiler_params=_params(("parallel",)),
        name="ssm",
    )(ug, tmat, et, ft, a16, h0)
    fin = fin.reshape(2, 2, nb, SSM_GROUPS, SSM_STATE)
    return y, fin


def _gelu_tanh(x):
    return 0.5 * x * (1.0 + jnp.tanh(math.sqrt(2.0 / math.pi) * (x + 0.044715 * (x * x * x))))


def _route(aff, bias):
    a = [aff[e:e + 1, :] for e in range(N_EXPERTS)]
    s = [a[e] + bias[e:e + 1, :] for e in range(N_EXPERTS)]
    keep = []
    for e in range(N_EXPERTS):
        g0 = (e // EXPERTS_PER_GROUP) * EXPERTS_PER_GROUP
        rank = None
        for j in range(g0, g0 + EXPERTS_PER_GROUP):
            if j == e:
                continue
            beats = (s[j] >= s[e]) if j < e else (s[j] > s[e])
            r = jnp.where(beats, 1.0, 0.0)
            rank = r if rank is None else rank + r
        keep.append(rank < 1.5)
    score = []
    for g in range(N_GROUPS):
        tot = None
        for e in range(g * EXPERTS_PER_GROUP, (g + 1) * EXPERTS_PER_GROUP):
            v = jnp.where(keep[e], s[e], 0.0)
            tot = v if tot is None else tot + v
        score.append(tot)
    gates = []
    picked = []
    for g in range(N_GROUPS):
        lost = None
        for j in range(N_GROUPS):
            if j == g:
                continue
            beats = (score[j] >= score[g]) if j < g else (score[j] > score[g])
            r = jnp.where(beats, 1.0, 0.0)
            lost = r if lost is None else lost + r
        chosen = lost < 0.5
        picked.append(jnp.where(chosen, 1.0, 0.0))
        w = [jnp.where(keep[e], a[e], 0.0) for e in range(g * EXPERTS_PER_GROUP, (g + 1) * EXPERTS_PER_GROUP)]
        tot = w[0] + w[1] + w[2] + w[3]
        for v in w:
            gates.append(jnp.where(chosen, v / tot, 0.0))
    return gates, picked


def _merge_kernel(x_ref, yg_ref, u_ref, at_ref, sga_ref, sgb_ref, mod_ref, d_ref, wglu_ref, bglu_ref, wsp_ref,
                  wap_ref, wo_ref, gffn_ref, wr_ref, rb_ref, tri_ref, ramp_ref,
                  xo_ref, hs_ref, mt_ref, ms_ref, of_ref, tok_ref):
    mod = mod_ref[...]
    gate1 = mod[:, 2 * D_MODEL:3 * D_MODEL]
    shift2 = mod[:, 3 * D_MODEL:4 * D_MODEL]
    scale2 = mod[:, 4 * D_MODEL:5 * D_MODEL]
    _from_chunk_rows(yg_ref, tok_ref)
    ys = jnp.concatenate([tok_ref[j] for j in range(LANE_TILES)], axis=1)
    y = _gelu_tanh(ys + d_ref[...] * u_ref[...].astype(F32))
    glu = jnp.dot(y.astype(BF16), wglu_ref[...], preferred_element_type=F32) + bglu_ref[...]
    y = (y * _sigmoid(glu)).astype(BF16)
    a_out = jnp.dot(y, wsp_ref[...], preferred_element_type=F32)
    b_out = jnp.dot(at_ref[...], wap_ref[...], preferred_element_type=F32)
    m = (sga_ref[...].astype(F32) * a_out + sgb_ref[...].astype(F32) * b_out).astype(BF16)
    x = x_ref[...] + gate1 * jnp.dot(m, wo_ref[...], preferred_element_type=F32)
    xo_ref[...] = x
    h2 = _rms(x, D_MODEL) * (gffn_ref[...] * (1.0 + scale2)) + shift2
    nt = (((1,), (1,)), ((), ()))
    h2_hi = h2.astype(BF16)
    h2_lo = (h2 - h2_hi.astype(F32)).astype(BF16)
    wr = wr_ref[...]
    wr_hi = wr.astype(BF16)
    wr_lo = (wr - wr_hi.astype(F32)).astype(BF16)
    logits = (lax.dot_general(wr_hi, h2_hi, nt, preferred_element_type=F32)
              + lax.dot_general(wr_hi, h2_lo, nt, preferred_element_type=F32)
              + lax.dot_general(wr_lo, h2_hi, nt, preferred_element_type=F32))
    gates, picked = _route(_sigmoid(logits), rb_ref[...])
    n = x.shape[0]
    onehot = jnp.concatenate(picked + [jnp.zeros((8 - N_GROUPS, n), F32)], axis=0)
    before = jnp.dot(onehot.astype(BF16), tri_ref[...], preferred_element_type=F32)
    count = [jnp.sum(p, axis=-1, keepdims=True) for p in picked]
    start = [jnp.zeros((1, 1), F32)]
    for g in range(N_GROUPS - 1):
        start.append(start[-1] + count[g])
    pos = sum(picked[g] * (start[g] + before[g:g + 1, :]) for g in range(N_GROUPS))
    gid = sum(float(g) * picked[g] for g in range(1, N_GROUPS))
    perm = jnp.where(ramp_ref[...] == pos, 1.0, 0.0).astype(BF16)
    hs_ref[...] = jnp.dot(perm, h2_hi, preferred_element_type=F32).astype(BF16)
    own = [sum(gates[EXPERTS_PER_GROUP * g + e] for g in range(N_GROUPS)) for e in range(EXPERTS_PER_GROUP)]
    meta = jnp.concatenate(own + [gid, pos, jnp.zeros((LANES - EXPERTS_PER_GROUP - 2, n), F32)], axis=0).T
    mt_ref[...] = meta
    hi = meta.astype(BF16)
    lo = (meta - hi.astype(F32)).astype(BF16)
    ms_ref[...] = (jnp.dot(perm, hi, preferred_element_type=F32)
                   + jnp.dot(perm, lo, preferred_element_type=F32))
    lane = lax.broadcasted_iota(jnp.int32, (8, LANES), 1)
    offs = sum(jnp.where(lane == g, start[g], 0.0) for g in range(1, N_GROUPS)) + jnp.where(lane == N_GROUPS, n, 0.0)
    of_ref[...] = offs.astype(jnp.int32)


def _merge(x, yg, u, attn, sga, sgb, mods, l, lat, wts):
    n = x.shape[0]
    tt = TOK_TILE
    per_seq = 2048 // tt
    mod_map = (lambda i: (l, 1 + i // per_seq, 0, 0)) if lat else (lambda i: (l, 0, 0, 0))
    row = lambda i: (i, 0)
    lw = lambda *shape: pl.BlockSpec((None,) + shape, lambda i: (l,) + (0,) * len(shape))
    return pl.pallas_call(
        _merge_kernel,
        grid=(n // tt,),
        in_specs=[
            pl.BlockSpec((tt, D_MODEL), row),
            pl.BlockSpec((SSM_GROUPS, TILE_CHUNKS, CW), lambda i: (0, i, 0)),
            pl.BlockSpec((tt, SSM_WIDTH), row),
            pl.BlockSpec((tt, ATTN_WIDTH), row), pl.BlockSpec((tt, D_MODEL), row), pl.BlockSpec((tt, D_MODEL), row),
            pl.BlockSpec((None, None, 1, 6 * D_MODEL), mod_map),
            lw(1, SSM_WIDTH), lw(SSM_WIDTH, SSM_WIDTH), lw(1, SSM_WIDTH), lw(SSM_WIDTH, D_MODEL),
            lw(ATTN_WIDTH, D_MODEL), lw(D_MODEL, D_MODEL), lw(1, D_MODEL),
            pl.BlockSpec((N_EXPERTS, D_MODEL), lambda i: (0, 0)),
            pl.BlockSpec((N_EXPERTS, 1), lambda i: (0, 0)),
            pl.BlockSpec((tt, tt), lambda i: (0, 0)), pl.BlockSpec((tt, tt), lambda i: (0, 0)),
        ],
        out_specs=[pl.BlockSpec((tt, D_MODEL), row), pl.BlockSpec((tt, D_MODEL), row),
                   pl.BlockSpec((tt, LANES), row), pl.BlockSpec((tt, LANES), row),
                   pl.BlockSpec((None, 8, LANES), lambda i: (i, 0, 0))],
        out_shape=(jax.ShapeDtypeStruct((n, D_MODEL), F32), jax.ShapeDtypeStruct((n, D_MODEL), BF16),
                   jax.ShapeDtypeStruct((n, LANES), F32), jax.ShapeDtypeStruct((n, LANES), F32),
                   jax.ShapeDtypeStruct((n // tt, 8, LANES), jnp.int32)),
        scratch_shapes=[pltpu.VMEM((LANE_TILES, tt, LANES), F32)],
        compiler_params=_params(("parallel",)),
        name="merge_lat" if lat else "merge_ctx",
    )(x, yg, u, attn, sga, sgb, mods, wts["ssm_d"], wts["w_glu"], wts["b_glu"], wts["w_ssm_proj"],
      wts["w_attn_proj"], wts["w_o"], wts["g_ffn"], wts["w_router_t"], wts["router_bias"],
      jnp.asarray(np.triu(np.ones((tt, tt), np.float32), 1), BF16),
      jnp.asarray(np.broadcast_to(np.arange(tt, dtype=np.float32)[:, None], (tt, tt))))


def _moe_kernel(offs_ref, x_ref, hs_ref, ms_ref, mt_ref, mod_ref, wg_ref, wu_ref, wd_ref, o_ref, acc_ref):
    tile = pl.program_id(0)
    acc_ref[...] = jnp.zeros(acc_ref.shape, F32)

    def run_passes(height, g, first, span):
        row_id = lax.broadcasted_iota(jnp.int32, (height, 1), 0)
        n_pass = sum((span > k * height).astype(jnp.int32) for k in range(pl.cdiv(TOK_TILE + 16, height)))

        def pass_body(j, c):
            lo = first + height * j
            s = pl.multiple_of(jnp.minimum(lo, TOK_TILE - height), 16)
            rows = pl.ds(s, height)
            meta = ms_ref[rows, :]
            group_id = lax.convert_element_type(g, F32)
            mine = (meta[:, EXPERTS_PER_GROUP:EXPERTS_PER_GROUP + 1] == group_id) & (row_id + s >= lo)
            own = jnp.where(mine, 1.0, 0.0)
            hs = hs_ref[rows, :]
            acts = []
            for e in range(EXPERTS_PER_GROUP):
                expert = EXPERTS_PER_GROUP * g + e
                hg = jnp.dot(hs, wg_ref[expert], preferred_element_type=F32)
                hu = jnp.dot(hs, wu_ref[expert], preferred_element_type=F32)
                acts.append(((hg * _sigmoid(hg)) * hu * (meta[:, e:e + 1] * own)).astype(BF16))
            acc_ref[rows, :] += jnp.dot(jnp.concatenate(acts, axis=1), wd_ref[g], preferred_element_type=F32)
            return c

        lax.fori_loop(0, n_pass, pass_body, 0)

    def group_body(g, carry):
        seg_start = offs_ref[(N_GROUPS + 1) * tile + g]
        seg_end = offs_ref[(N_GROUPS + 1) * tile + g + 1]
        first = (seg_start >> 4) << 4
        span = jnp.where(seg_end > seg_start, seg_end - first, 0)
        choice = sum((span > height).astype(jnp.int32) for height in MOE_PASS_ROWS[:-1])
        lax.switch(choice, [functools.partial(run_passes, height, g, first, span) for height in MOE_PASS_ROWS])
        return carry

    lax.fori_loop(0, N_GROUPS, group_body, 0)

    slot = lax.broadcasted_iota(jnp.int32, (TOK_TILE, TOK_TILE), 1).astype(F32)
    unsort = jnp.where(mt_ref[:, EXPERTS_PER_GROUP + 1:EXPERTS_PER_GROUP + 2] == slot, 1.0, 0.0).astype(BF16)
    y = jnp.dot(unsort, acc_ref[...].astype(BF16), preferred_element_type=F32)
    o_ref[...] = x_ref[...] + mod_ref[:, 5 * D_MODEL:6 * D_MODEL] * y


def _moe(x, hs, meta_tok, meta_sorted, offs, mods, l, lat, wts):
    n = x.shape[0]
    tt = TOK_TILE
    per_seq = 2048 // tt
    mod_map = (lambda i, o: (l, 1 + i // per_seq, 0, 0)) if lat else (lambda i, o: (l, 0, 0, 0))
    row = lambda i, o: (i, 0)
    width = EXPERTS_PER_GROUP * D_EXPERT
    resident = lambda *shape: pl.BlockSpec((None,) + shape, lambda i, o: (l,) + (0,) * len(shape),
                                           pipeline_mode=pl.Buffered(1))
    grid_spec = pltpu.PrefetchScalarGridSpec(
        num_scalar_prefetch=1,
        grid=(n // tt,),
        in_specs=[
            pl.BlockSpec((tt, D_MODEL), row), pl.BlockSpec((tt, D_MODEL), row), pl.BlockSpec((tt, LANES), row),
            pl.BlockSpec((tt, LANES), row),
            pl.BlockSpec((None, None, 1, 6 * D_MODEL), mod_map),
            resident(N_EXPERTS, D_MODEL, D_EXPERT), resident(N_EXPERTS, D_MODEL, D_EXPERT),
            resident(N_GROUPS, width, D_MODEL),
        ],
        out_specs=pl.BlockSpec((tt, D_MODEL), row),
        scratch_shapes=[pltpu.VMEM((tt, D_MODEL), F32)],
    )
    return pl.pallas_call(
        _moe_kernel,
        grid_spec=grid_spec,
        out_shape=jax.ShapeDtypeStruct((n, D_MODEL), F32),
        compiler_params=_params(("arbitrary",), MOE_VMEM_LIMIT),
        name="moe_lat" if lat else "moe_ctx",
    )(offs[:, 0, :N_GROUPS + 1].reshape(-1), x, hs, meta_sorted, meta_tok, mods, wts["w_gate"], wts["w_up"],
      wts["w_down"])


def _pad_heads(w, width):
    lead = w.shape[:-1]
    w = w.reshape(lead + (N_HEADS, width))
    w = jnp.pad(w, [(0, 0)] * len(lead) + [(0, 0), (0, HEAD_PAD - width)])
    return w.reshape(lead + (N_HEADS * HEAD_PAD,))


def _swap_partners(w):
    lead = w.shape[:-1]
    w = w.reshape(lead + (-1, HEAD_PAD))
    half = ROPE_AXIS // 2
    a, b = QK_NOPE, QK_NOPE + ROPE_AXIS
    parts = [w[..., :a], w[..., a + half:b], w[..., a:a + half], w[..., b + half:b + 2 * half], w[..., b:b + half],
             w[..., b + 2 * half:]]
    return jnp.concatenate(parts, axis=-1).reshape(lead + (-1,))


def _layout_weights(w_in, g_qa, w_qb, g_q, g_kva, w_kvb, g_k, g_mix, w_attn_proj, ssm_d, ssm_w_glu, ssm_b_glu,
                    ssm_w_proj, w_o, g_ffn, w_router, router_bias, w_gate, w_up, w_down):
    n_a = SSM_WIDTH + Q_LORA + KV_LORA + QK_ROPE
    w_a = jnp.pad(w_in[:, :, :n_a], ((0, 0), (0, 0), (0, 1024 - n_a)))
    kv = w_kvb.reshape(DEPTH, KV_LORA, N_HEADS, QK_NOPE + V_HEAD)
    k_cols = _pad_heads(kv[..., :QK_NOPE].reshape(DEPTH, KV_LORA, N_HEADS * QK_NOPE), QK_NOPE)
    v_cols = _pad_heads(kv[..., QK_NOPE:].reshape(DEPTH, KV_LORA, ATTN_WIDTH), V_HEAD)
    place = np.zeros((KV_LORA, 2 * QK_PAD), np.float32)
    for h in range(N_HEADS):
        for r in range(QK_ROPE):
            place[r, h * HEAD_PAD + QK_NOPE + r] = 1.0
    w_k = jnp.concatenate([jnp.concatenate([k_cols, v_cols], axis=-1),
                           jnp.broadcast_to(jnp.asarray(place), (DEPTH,) + place.shape)], axis=1)
    w_qb_pad = _pad_heads(w_qb, QK_HEAD).astype(BF16)
    w_k = w_k.astype(BF16)
    pad_gain = lambda g: jnp.pad(g, ((0, 0), (0, HEAD_PAD - QK_HEAD))).reshape(DEPTH, 1, HEAD_PAD)
    vec = lambda g: g.reshape(DEPTH, 1, -1)
    return {
        "g_mix": vec(g_mix), "w_a": w_a.astype(BF16),
        "w_ga": w_in[:, :, n_a:n_a + D_MODEL].astype(BF16), "w_gb": w_in[:, :, n_a + D_MODEL:].astype(BF16),
        "g_qa": vec(g_qa), "w_qb": w_qb_pad, "w_qb_sw": _swap_partners(w_qb_pad),
        "g_q": pad_gain(g_q) * Q_SCALE, "g_q_sw": _swap_partners(pad_gain(g_q)) * Q_SCALE,
        "g_kva": vec(g_kva), "w_k": w_k, "w_k_sw": _swap_partners(w_k[:, :, :QK_PAD]),
        "g_k": pad_gain(g_k), "g_k_sw": _swap_partners(pad_gain(g_k)),
        "ssm_d": vec(ssm_d), "w_glu": ssm_w_glu.astype(BF16), "b_glu": vec(ssm_b_glu),
        "w_ssm_proj": ssm_w_proj.astype(BF16), "w_attn_proj": w_attn_proj.astype(BF16), "w_o": w_o.astype(BF16),
        "g_ffn": vec(g_ffn), "w_router_t": w_router.T, "router_bias": router_bias.reshape(N_EXPERTS, 1),
        "w_gate": w_gate.astype(BF16), "w_up": w_up.astype(BF16),
        "w_down": w_down.astype(BF16).reshape(DEPTH, N_GROUPS, EXPERTS_PER_GROUP * D_EXPERT, D_MODEL),
    }


def _rope_tables(n_tokens):
    pos = np.arange(n_tokens)
    inv = 1.0 / (ROPE_BASE ** (np.arange(ROPE_AXIS // 2, dtype=np.float32) * 2.0 / ROPE_AXIS))
    ang = np.zeros((n_tokens, HEAD_PAD), np.float32)
    half = ROPE_AXIS // 2
    row = (pos // GRID_W).astype(np.float32)[:, None] * inv
    col = (pos % GRID_W).astype(np.float32)[:, None] * inv
    first = np.zeros((HEAD_PAD,), bool)
    second = np.zeros((HEAD_PAD,), bool)
    for base, a in ((QK_NOPE, row), (QK_NOPE + ROPE_AXIS, col)):
        ang[:, base:base + half] = a
        ang[:, base + half:base + 2 * half] = a
        first[base:base + half] = True
        second[base + half:base + 2 * half] = True
    ang = jnp.asarray(ang)
    cos = jnp.where(jnp.asarray(first | second), jnp.cos(ang), 1.0)
    sin = jnp.sin(ang)
    return cos, jnp.where(jnp.asarray(first), -sin, jnp.where(jnp.asarray(second), sin, 0.0))


def kernel(x_prompt, x_sample, c, cache_ckv, cache_krope, state_ssm_re, state_ssm_im, c_ctx, w_ada, b_ada, g_mix, w_in, g_qa, w_qb, g_q, g_kva, w_kvb, g_k, w_attn_proj, ssm_a_re, ssm_a_im, ssm_log_dt, ssm_b_re, ssm_b_im, ssm_c_re, ssm_c_im, ssm_d, ssm_w_glu, ssm_b_glu, ssm_w_proj, w_o, g_ffn, w_router, router_bias, w_gate, w_up, w_down):
    nb_c, seq_c, _ = x_prompt.shape
    nb_l, seq_l, _ = x_sample.shape
    wts = _layout_weights(w_in, g_qa, w_qb, g_q, g_kva, w_kvb, g_k, g_mix, w_attn_proj, ssm_d, ssm_w_glu,
                          ssm_b_glu, ssm_w_proj, w_o, g_ffn, w_router, router_bias, w_gate, w_up, w_down)
    cvec = jnp.concatenate([c_ctx[None, :], c, jnp.zeros((8 - 1 - nb_l, D_MODEL), F32)], axis=0)
    mods = _mods(cvec, w_ada, b_ada).reshape(DEPTH, 8, 1, 6 * D_MODEL)
    prep = _ssm_prep(ssm_a_re, ssm_a_im, ssm_log_dt, ssm_b_re, ssm_b_im, ssm_c_re, ssm_c_im)
    tabs = _rope_tables(seq_l)
    kc, vc = _cachekv(cache_ckv, cache_krope, wts)

    xp = x_prompt.reshape(nb_c * seq_c, D_MODEL)
    xs = x_sample.reshape(nb_l * seq_l, D_MODEL)
    zeros_c = jnp.zeros((nb_c, 2, SSM_GROUPS, SSM_STATE), F32)
    ckvs, kropes, fins = [], [], []
    for l in range(DEPTH):
        u, ug, q, k, v, sga, sgb, ckv, krope = _inproj(xp, mods, l, False, wts, tabs)
        ckvs.append(ckv)
        kropes.append(krope)
        yg, fin = _ssm_mix(ug, nb_c, seq_c, zeros_c, zeros_c, prep, l)
        fins.append(fin)
        attn = _attn_ctx(q, k, v, seq_c)
        xp, *routed = _merge(xp, yg, u, attn, sga, sgb, mods, l, False, wts)
        xp = _moe(xp, *routed, mods, l, False, wts)

        u, ug, q, k, v, sga, sgb, _, _ = _inproj(xs, mods, l, True, wts, tabs)
        yg, _ = _ssm_mix(ug, nb_l, seq_l, state_ssm_re[:, l], state_ssm_im[:, l], prep, l)
        attn = _attn_lat(q, k, v, kc, vc, l, seq_l, TOK_TILE)
        xs, *routed = _merge(xs, yg, u, attn, sga, sgb, mods, l, True, wts)
        xs = _moe(xs, *routed, mods, l, True, wts)

    new_ckv = jnp.stack(ckvs, axis=0).reshape(DEPTH, nb_c, seq_c, KV_LORA).transpose(1, 0, 2, 3)
    new_krope = jnp.stack(kropes, axis=0).reshape(DEPTH, nb_c, seq_c, QK_ROPE).transpose(1, 0, 2, 3)
    fin = jnp.stack(fins, axis=0)
    new_re = jnp.transpose(fin[:, :, 0], (2, 0, 1, 3, 4))
    new_im = jnp.transpose(fin[:, :, 1], (2, 0, 1, 3, 4))
    return (xp.reshape(nb_c, seq_c, D_MODEL), xs.reshape(nb_l, seq_l, D_MODEL), new_ckv, new_krope, new_re, new_im)
```

```python
import functools
import math

import jax
import jax.numpy as jnp
import numpy as np
from jax import lax
from jax.experimental import pallas as pl
from jax.experimental.pallas import tpu as pltpu

F32 = jnp.float32
BF16 = jnp.bfloat16

D_MODEL = 1024
DEPTH = 4
GRID_W = 64
SSM_WIDTH = 512
SSM_GROUP = 16
SSM_GROUPS = 32
SSM_STATE = 64
N_HEADS = 8
QK_NOPE = 64
QK_ROPE = 32
QK_HEAD = 96
V_HEAD = 64
Q_LORA = 256
KV_LORA = 128
ATTN_WIDTH = 512
ROPE_AXIS = 16
ROPE_BASE = 10000.0
N_EXPERTS = 16
N_GROUPS = 4
EXPERTS_PER_GROUP = 4
D_EXPERT = 256
EPS = 1e-6

LANES = 128
HEAD_PAD = LANES
QK_PAD = N_HEADS * HEAD_PAD
CHUNK = 16
CW = CHUNK * SSM_GROUP
TOK_TILE = 512
TILE_CHUNKS = TOK_TILE // CHUNK
LANE_TILES = SSM_WIDTH // LANES
GROUPS_PER_TILE = LANES // SSM_GROUP
Q_SCALE = math.log2(math.e) / math.sqrt(QK_HEAD)
MOE_PASS_ROWS = (128, 160, 192, 224, 256)
VMEM_LIMIT = 48 * 1024 * 1024
MOE_VMEM_LIMIT = 56 * 1024 * 1024


def _params(sem, vmem=VMEM_LIMIT):
    return pltpu.CompilerParams(dimension_semantics=sem, vmem_limit_bytes=vmem)


def _sigmoid(x):
    return 0.5 * jnp.tanh(0.5 * x) + 0.5


def _rms(x, n):
    return x * lax.rsqrt(jnp.sum(x * x, axis=-1, keepdims=True) * (1.0 / n) + EPS)


def _mods_kernel(c_ref, w_ref, b_ref, o_ref):
    c = c_ref[...]
    cs = (c * _sigmoid(c)).astype(BF16)
    o_ref[...] = jnp.dot(cs, w_ref[...].astype(BF16), preferred_element_type=F32) + b_ref[...]


def _mods(cvec, w_ada, b_ada):
    nb = 1536
    return pl.pallas_call(
        _mods_kernel,
        grid=(DEPTH, 6 * D_MODEL // nb),
        in_specs=[
            pl.BlockSpec((8, D_MODEL), lambda l, j: (0, 0)),
            pl.BlockSpec((None, D_MODEL, nb), lambda l, j: (l, 0, j)),
            pl.BlockSpec((None, 1, nb), lambda l, j: (l, 0, j)),
        ],
        out_specs=pl.BlockSpec((None, 8, nb), lambda l, j: (l, 0, j)),
        out_shape=jax.ShapeDtypeStruct((DEPTH, 8, 6 * D_MODEL), F32),
        compiler_params=_params(("arbitrary", "arbitrary")),
        name="mods",
    )(cvec, w_ada, b_ada.reshape(DEPTH, 1, 6 * D_MODEL))


def _slot_step(slot, g):
    return (slot & 8) + ((slot - g) & 7)


def _lane_segment():
    return lax.broadcasted_iota(jnp.int32, (TILE_CHUNKS, LANES), 1) // SSM_GROUP


def _to_chunk_rows(tok_ref, ug_ref):
    seg = _lane_segment()
    for j in range(LANE_TILES):
        for half in range(CHUNK // 8):
            rolled = []
            for r in range(8):
                v = tok_ref[j, pl.ds(8 * half + r, TILE_CHUNKS, stride=CHUNK), :]
                rolled.append(pltpu.roll(v, SSM_GROUP * r, 1) if r else v)
            for gg in range(GROUPS_PER_TILE):
                acc = rolled[(0 - gg) % 8]
                for m in range(1, 8):
                    acc = jnp.where(seg == m, rolled[(m - gg) % 8], acc)
                ug_ref[GROUPS_PER_TILE * j + gg, :, LANES * half:LANES * (half + 1)] = acc.astype(ug_ref.dtype)


def _from_chunk_rows(yg_ref, tok_ref):
    seg = _lane_segment()
    for j in range(LANE_TILES):
        for half in range(CHUNK // 8):
            ys = [yg_ref[GROUPS_PER_TILE * j + gg, :, LANES * half:LANES * (half + 1)]
                  for gg in range(GROUPS_PER_TILE)]
            for r in range(8):
                acc = ys[(0 - r) % 8]
                for m in range(1, 8):
                    acc = jnp.where(seg == m, ys[(m - r) % 8], acc)
                out = pltpu.roll(acc, LANES - SSM_GROUP * r, 1) if r else acc
                tok_ref[j, pl.ds(8 * half + r, TILE_CHUNKS, stride=CHUNK), :] = out


def _head_norm_rope(t, gain, swapped):
    outs = []
    for h in range(N_HEADS):
        lanes = slice(h * HEAD_PAD, (h + 1) * HEAD_PAD)
        th = t[:, lanes]
        scale = lax.rsqrt(jnp.sum(th * th, axis=-1, keepdims=True) * (1.0 / QK_HEAD) + EPS)
        y = th * gain
        if swapped is not None:
            y = y + swapped[0][:, lanes] * swapped[1]
        outs.append(y * scale)
    return jnp.concatenate(outs, axis=1)


def _with_ones(v):
    lane = lax.broadcasted_iota(jnp.int32, (1, v.shape[1]), 1)
    return jnp.where((lane & (HEAD_PAD - 1)) == V_HEAD, 1.0, v)


def _kv_dot(ckvn, krp, wk_ref):
    return (jnp.dot(ckvn, wk_ref[0:KV_LORA, :], preferred_element_type=F32)
            + jnp.dot(krp, wk_ref[KV_LORA:KV_LORA + krp.shape[1], :], preferred_element_type=F32))


def _inproj_kernel(use_rope, x_ref, mod_ref, gmix_ref, wa_ref, wga_ref, wgb_ref, gqa_ref, wqb_ref, wqbs_ref, gq_ref,
                   gqs_ref, gkva_ref, wk_ref, wks_ref, gk_ref, gks_ref, cos_ref, sin_ref,
                   u_ref, ug_ref, q_ref, k_ref, v_ref, sga_ref, sgb_ref, ckv_ref, kr_ref, tok_ref):
    mod = mod_ref[...]
    shift1 = mod[:, 0:D_MODEL]
    gain1 = gmix_ref[...] * (1.0 + mod[:, D_MODEL:2 * D_MODEL])
    h = (_rms(x_ref[...], D_MODEL) * gain1 + shift1).astype(BF16)
    za = jnp.dot(h, wa_ref[...], preferred_element_type=F32)
    sga_ref[...] = _sigmoid(jnp.dot(h, wga_ref[...], preferred_element_type=F32)).astype(BF16)
    sgb_ref[...] = _sigmoid(jnp.dot(h, wgb_ref[...], preferred_element_type=F32)).astype(BF16)
    u_ref[...] = za[:, :SSM_WIDTH].astype(BF16)
    for j in range(LANE_TILES):
        tok_ref[j] = za[:, LANES * j:LANES * (j + 1)]
    _to_chunk_rows(tok_ref, ug_ref)
    qa = za[:, SSM_WIDTH:SSM_WIDTH + Q_LORA]
    ckv = za[:, SSM_WIDTH + Q_LORA:SSM_WIDTH + Q_LORA + KV_LORA]
    krp = za[:, SSM_WIDTH + Q_LORA + KV_LORA:]
    qn = (_rms(qa, Q_LORA) * gqa_ref[...]).astype(BF16)
    qr = jnp.dot(qn, wqb_ref[...], preferred_element_type=F32)
    ckvn = _rms(ckv, KV_LORA) * gkva_ref[...]
    ckv_ref[...] = ckvn
    kr_ref[...] = krp[:, :QK_ROPE]
    ckvn = ckvn.astype(BF16)
    krp = krp.astype(BF16)
    kv = _kv_dot(ckvn, krp, wk_ref)
    gq = gq_ref[...]
    gk = gk_ref[...]
    q_sw = k_sw = None
    if use_rope:
        cos = cos_ref[...]
        sin = sin_ref[...]
        q_sw = (jnp.dot(qn, wqbs_ref[...], preferred_element_type=F32), gqs_ref[...] * sin)
        k_sw = (_kv_dot(ckvn, krp, wks_ref), gks_ref[...] * sin)
        gq = gq * cos
        gk = gk * cos
    q_ref[...] = _head_norm_rope(qr, gq, q_sw).astype(BF16)
    k_ref[...] = _head_norm_rope(kv[:, :QK_PAD], gk, k_sw).astype(BF16)
    v_ref[...] = _with_ones(kv[:, QK_PAD:]).astype(BF16)


def _inproj(x, mods, l, lat, wts, tabs):
    n = x.shape[0]
    tt = TOK_TILE
    per_seq = 2048 // tt
    if lat:
        mod_map = lambda i: (l, 1 + i // per_seq, 0, 0)
        tab_map = lambda i: (i % per_seq, 0)
    else:
        mod_map = lambda i: (l, 0, 0, 0)
        tab_map = lambda i: (0, 0)
    row = lambda i: (i, 0)
    lw = lambda *shape: pl.BlockSpec((None,) + shape, lambda i: (l,) + (0,) * len(shape))
    tab = pl.BlockSpec((tt, HEAD_PAD), tab_map)
    row_shapes = (
        jax.ShapeDtypeStruct((n, SSM_WIDTH), BF16),
        jax.ShapeDtypeStruct((n, QK_PAD), BF16),
        jax.ShapeDtypeStruct((n, QK_PAD), BF16),
        jax.ShapeDtypeStruct((n, QK_PAD), BF16),
        jax.ShapeDtypeStruct((n, D_MODEL), BF16),
        jax.ShapeDtypeStruct((n, D_MODEL), BF16),
        jax.ShapeDtypeStruct((n, KV_LORA), F32),
        jax.ShapeDtypeStruct((n, QK_ROPE), F32),
    )
    row_specs = [pl.BlockSpec((tt, s.shape[1]), row) for s in row_shapes]
    ug_shape = jax.ShapeDtypeStruct((SSM_GROUPS, n // CHUNK, CW), BF16)
    ug_spec = pl.BlockSpec((SSM_GROUPS, TILE_CHUNKS, CW), lambda i: (0, i, 0))
    outs = pl.pallas_call(
        functools.partial(_inproj_kernel, lat),
        grid=(n // tt,),
        in_specs=[
            pl.BlockSpec((tt, D_MODEL), row),
            pl.BlockSpec((None, None, 1, 6 * D_MODEL), mod_map),
            lw(1, D_MODEL), lw(D_MODEL, 1024), lw(D_MODEL, D_MODEL), lw(D_MODEL, D_MODEL),
            lw(1, Q_LORA), lw(Q_LORA, QK_PAD), lw(Q_LORA, QK_PAD), lw(1, HEAD_PAD), lw(1, HEAD_PAD), lw(1, KV_LORA),
            lw(2 * KV_LORA, 2 * QK_PAD), lw(2 * KV_LORA, QK_PAD), lw(1, HEAD_PAD), lw(1, HEAD_PAD),
            tab, tab,
        ],
        out_specs=row_specs[:1] + [ug_spec] + row_specs[1:],
        out_shape=row_shapes[:1] + (ug_shape,) + row_shapes[1:],
        scratch_shapes=[pltpu.VMEM((LANE_TILES, tt, LANES), F32)],
        compiler_params=_params(("parallel",)),
        name="inproj_lat" if lat else "inproj_ctx",
    )(x, mods, wts["g_mix"], wts["w_a"], wts["w_ga"], wts["w_gb"], wts["g_qa"], wts["w_qb"], wts["w_qb_sw"],
      wts["g_q"], wts["g_q_sw"], wts["g_kva"], wts["w_k"], wts["w_k_sw"], wts["g_k"], wts["g_k_sw"], *tabs)
    return outs


def _cachekv_kernel(ckv_ref, kr_ref, wk_ref, gk_ref, k_ref, v_ref):
    kv = _kv_dot(ckv_ref[...].astype(BF16), kr_ref[...].astype(BF16), wk_ref)
    k_ref[...] = _head_norm_rope(kv[:, :QK_PAD], gk_ref[...], None).astype(BF16)
    v_ref[...] = _with_ones(kv[:, QK_PAD:]).astype(BF16)


def _cachekv(cache_ckv, cache_krope, wts):
    nb, _, past, _ = cache_ckv.shape
    return pl.pallas_call(
        _cachekv_kernel,
        grid=(DEPTH, nb),
        in_specs=[
            pl.BlockSpec((None, None, past, KV_LORA), lambda l, b: (b, l, 0, 0)),
            pl.BlockSpec((None, None, past, QK_ROPE), lambda l, b: (b, l, 0, 0)),
            pl.BlockSpec((None, 2 * KV_LORA, 2 * QK_PAD), lambda l, b: (l, 0, 0)),
            pl.BlockSpec((None, 1, HEAD_PAD), lambda l, b: (l, 0, 0)),
        ],
        out_specs=[
            pl.BlockSpec((None, None, past, QK_PAD), lambda l, b: (l, b, 0, 0)),
            pl.BlockSpec((None, None, past, QK_PAD), lambda l, b: (l, b, 0, 0)),
        ],
        out_shape=(jax.ShapeDtypeStruct((DEPTH, nb, past, QK_PAD), BF16),
                   jax.ShapeDtypeStruct((DEPTH, nb, past, QK_PAD), BF16)),
        compiler_params=_params(("arbitrary", "arbitrary")),
        name="cache_kv",
    )(cache_ckv, cache_krope, wts["w_k"], wts["g_k"])


def _attn_heads(q_ref, segs, o_ref, den_on_mxu):
    for h in range(N_HEADS):
        qh = q_ref[:, h * HEAD_PAD:(h + 1) * HEAD_PAD]
        scores = [lax.dot_general(qh, k_ref[:, h * HEAD_PAD:(h + 1) * HEAD_PAD], (((1,), (1,)), ((), ())),
                                  preferred_element_type=F32) for k_ref, _ in segs]
        m = scores[0].max(axis=-1, keepdims=True)
        for s in scores[1:]:
            m = jnp.maximum(m, s.max(axis=-1, keepdims=True))
        acc = None
        den = None
        for s, (_, v_ref) in zip(scores, segs):
            if den_on_mxu:
                p = jnp.exp2((s - m).astype(BF16))
                a = jnp.dot(p, v_ref[:, h * HEAD_PAD:(h + 1) * HEAD_PAD], preferred_element_type=F32)
            else:
                p = jnp.exp2(s - m)
                d = p.sum(axis=-1, keepdims=True)
                den = d if den is None else den + d
                a = jnp.dot(p.astype(BF16), v_ref[:, h * HEAD_PAD:h * HEAD_PAD + V_HEAD],
                            preferred_element_type=F32)
            acc = a if acc is None else acc + a
        if den_on_mxu:
            den = acc[:, V_HEAD:V_HEAD + 1]
        o_ref[:, h * V_HEAD:(h + 1) * V_HEAD] = (acc[:, :V_HEAD] / den).astype(o_ref.dtype)


def _attn_ctx_kernel(q_ref, k_ref, v_ref, o_ref):
    for b in range(q_ref.shape[0]):
        _attn_heads(q_ref.at[b], [(k_ref.at[b], v_ref.at[b])], o_ref.at[b], False)


def _attn_lat_kernel(q_ref, kc_ref, vc_ref, kl_ref, vl_ref, o_ref):
    _attn_heads(q_ref, [(kc_ref, vc_ref), (kl_ref, vl_ref)], o_ref, True)


def _attn_ctx(q, k, v, seq):
    n = q.shape[0]
    nb = n // seq
    per_step = 2
    blk = lambda w: pl.BlockSpec((per_step, seq, w), lambda b: (b, 0, 0))
    seqs = lambda a: a.reshape(nb, seq, a.shape[1])
    return pl.pallas_call(
        _attn_ctx_kernel,
        grid=(nb // per_step,),
        in_specs=[blk(QK_PAD), blk(QK_PAD), blk(QK_PAD)],
        out_specs=blk(ATTN_WIDTH),
        out_shape=jax.ShapeDtypeStruct((nb, seq, ATTN_WIDTH), BF16),
        compiler_params=_params(("parallel",)),
        name="attn_ctx",
    )(seqs(q), seqs(k), seqs(v)).reshape(n, ATTN_WIDTH)


def _attn_lat(q, k, v, kc, vc, l, seq, tq):
    n = q.shape[0]
    per = seq // tq
    past = kc.shape[2]
    return pl.pallas_call(
        _attn_lat_kernel,
        grid=(n // seq, per),
        in_specs=[
            pl.BlockSpec((tq, QK_PAD), lambda b, j: (b * per + j, 0)),
            pl.BlockSpec((None, None, past, QK_PAD), lambda b, j: (l, b, 0, 0)),
            pl.BlockSpec((None, None, past, QK_PAD), lambda b, j: (l, b, 0, 0)),
            pl.BlockSpec((seq, QK_PAD), lambda b, j: (b, 0), pipeline_mode=pl.Buffered(1)),
            pl.BlockSpec((seq, QK_PAD), lambda b, j: (b, 0), pipeline_mode=pl.Buffered(1)),
        ],
        out_specs=pl.BlockSpec((tq, ATTN_WIDTH), lambda b, j: (b * per + j, 0)),
        out_shape=jax.ShapeDtypeStruct((n, ATTN_WIDTH), BF16),
        compiler_params=_params(("parallel", "arbitrary"), MOE_VMEM_LIMIT),
        name="attn_lat",
    )(q, kc, vc, k, v)


def _cmul(ar, ai, br, bi):
    return ar * br - ai * bi, ar * bi + ai * br


def _outer_rows(pw, m):
    return (pw[:, None, :] * m[None, :, :]).reshape(CW, SSM_STATE)


def _ssm_prep_kernel(are_ref, aim_ref, ldt_ref, bre_ref, bim_ref, cre_ref, cim_ref,
                     t_ref, et_ref, ft_ref, a16_ref):
    et_ref[...] = jnp.zeros(et_ref.shape, et_ref.dtype)
    ft_ref[...] = jnp.zeros(ft_ref.shape, ft_ref.dtype)
    row_id = lax.broadcasted_iota(jnp.int32, (CHUNK, SSM_STATE), 0)
    strip_lane = lax.broadcasted_iota(jnp.int32, (SSM_GROUP, CW), 1)
    nt = (((1,), (1,)), ((), ()))
    for gi in range(2):
        g = 2 * pl.program_id(1) + gi
        steps = _slot_step(row_id, g)
        lanes = slice(gi * SSM_STATE, (gi + 1) * SSM_STATE)
        rows = slice(gi * CW, (gi + 1) * CW)
        cre = cre_ref[gi]
        cim = cim_ref[gi]
        m_dir = []
        for d in range(2):
            a_re = are_ref[d, gi]
            a_im = aim_ref[d, gi]
            dt = jnp.exp(ldt_ref[d, gi])
            mag = jnp.exp(a_re * dt)
            abr = mag * jnp.cos(a_im * dt)
            abi = mag * jnp.sin(a_im * dt)
            den = a_re * a_re + a_im * a_im
            nr = abr - 1.0
            coef_re = (nr * a_re + abi * a_im) / den
            coef_im = (abi * a_re - nr * a_im) / den
            bbr, bbi = _cmul(coef_re, coef_im, bre_ref[d, gi], bim_ref[d, gi])

            squares = [(abr, abi)]
            for _ in range(4):
                squares.append(_cmul(*squares[-1], *squares[-1]))

            def power(n):
                pr = jnp.ones(n.shape, F32)
                pi = jnp.zeros(n.shape, F32)
                for b, (sr, si) in enumerate(squares):
                    nr_, ni_ = _cmul(pr, pi, sr, si)
                    hit = (n & (1 << b)) != 0
                    pr = jnp.where(hit, nr_, pr)
                    pi = jnp.where(hit, ni_, pi)
                return pr, pi

            to_end = (CHUNK - 1 - steps) if d == 0 else steps
            from_start = (steps + 1) if d == 0 else (CHUNK - steps)
            lag = row_id if d == 0 else (CHUNK - 1 - row_id)
            pr, pi = power(to_end)
            er = _outer_rows(pr, bbr) - _outer_rows(pi, bbi)
            ei = _outer_rows(pr, bbi) + _outer_rows(pi, bbr)
            et_ref[2 * d, rows, lanes] = er.astype(et_ref.dtype)
            et_ref[2 * d + 1, rows, lanes] = ei.astype(et_ref.dtype)
            pr, pi = power(from_start)
            ft_ref[2 * d, rows, lanes] = (_outer_rows(pr, cre) - _outer_rows(pi, cim)).astype(ft_ref.dtype)
            ft_ref[2 * d + 1, rows, lanes] = (-(_outer_rows(pr, cim) + _outer_rows(pi, cre))).astype(ft_ref.dtype)
            pr, pi = power(lag)
            zr = _outer_rows(pr, cre) - _outer_rows(pi, cim)
            zi = _outer_rows(pi, cre) + _outer_rows(pr, cim)
            m_dir.append(
                lax.dot_general(bbr, zr, nt, preferred_element_type=F32, precision=lax.Precision.HIGHEST)
                - lax.dot_general(bbi, zi, nt, preferred_element_type=F32, precision=lax.Precision.HIGHEST))
            p16r, p16i = squares[4]
            a16_ref[2 * d:2 * d + 1, lanes] = p16r
            a16_ref[2 * d + 1:2 * d + 2, lanes] = p16i
        mf, mb = m_dir
        edge = CW - SSM_GROUP
        low = mb + pltpu.roll(jnp.where(strip_lane < SSM_GROUP, mf, 0.0), edge, 1)
        high = jnp.where(strip_lane < edge, pltpu.roll(mf, edge, 1), 0.0)
        strip = jnp.concatenate([low, high], axis=1)
        turn = SSM_GROUP * (g & 7)
        for j in range(CHUNK):
            first = SSM_GROUP * (CHUNK - 1 - j)
            win = strip[:, first:first + CW]
            win = jnp.concatenate([pltpu.roll(win[:, LANES * h:LANES * (h + 1)], turn, 1)
                                   for h in range(CW // LANES)], axis=1)
            slot = (j & 8) + ((j + g) & 7)
            t_ref[gi, pl.ds(pl.multiple_of(SSM_GROUP * slot, SSM_GROUP), SSM_GROUP), :] = win.astype(t_ref.dtype)


def _ssm_prep(a_re, a_im, log_dt, b_re, b_im, c_re, c_im):
    g2 = SSM_GROUPS // 2
    a_spec = pl.BlockSpec((None, 2, 2, 1, SSM_STATE), lambda l, g: (l, 0, g, 0, 0))
    b_spec = pl.BlockSpec((None, 2, 2, SSM_GROUP, SSM_STATE), lambda l, g: (l, 0, g, 0, 0))
    c_spec = pl.BlockSpec((None, 2, SSM_GROUP, SSM_STATE), lambda l, g: (l, g, 0, 0))
    out_shapes = (
        jax.ShapeDtypeStruct((DEPTH, SSM_GROUPS, CW, CW), BF16),
        jax.ShapeDtypeStruct((DEPTH, g2, 4, 2 * CW, 2 * SSM_STATE), BF16),
        jax.ShapeDtypeStruct((DEPTH, g2, 4, 2 * CW, 2 * SSM_STATE), BF16),
        jax.ShapeDtypeStruct((DEPTH, g2, 4, 2 * SSM_STATE), F32),
    )
    return pl.pallas_call(
        _ssm_prep_kernel,
        grid=(DEPTH, g2),
        in_specs=[a_spec, a_spec,
                  pl.BlockSpec((None, 2, 2, 1, 1), lambda l, g: (l, 0, g, 0, 0)),
                  b_spec, b_spec, c_spec, c_spec],
        out_specs=[
            pl.BlockSpec((None, 2, CW, CW), lambda l, g: (l, g, 0, 0)),
            pl.BlockSpec((None, None, 4, 2 * CW, 2 * SSM_STATE), lambda l, g: (l, g, 0, 0, 0)),
            pl.BlockSpec((None, None, 4, 2 * CW, 2 * SSM_STATE), lambda l, g: (l, g, 0, 0, 0)),
            pl.BlockSpec((None, None, 4, 2 * SSM_STATE), lambda l, g: (l, g, 0, 0)),
        ],
        out_shape=out_shapes,
        compiler_params=_params(("arbitrary", "arbitrary")),
        name="ssm_prep",
    )(a_re.reshape(DEPTH, 2, SSM_GROUPS, 1, SSM_STATE), a_im.reshape(DEPTH, 2, SSM_GROUPS, 1, SSM_STATE),
      log_dt.reshape(DEPTH, 2, SSM_GROUPS, 1, 1),
      jnp.swapaxes(b_re, -1, -2), jnp.swapaxes(b_im, -1, -2), c_re, c_im)


def _ssm_kernel(nchunk, nb, u_ref, t_ref, et_ref, ft_ref, a16_ref, h0_ref, y_ref, fin_ref, e_ref, h_ref):
    u0 = u_ref[0]
    u1 = u_ref[1]
    e_all = jnp.dot(jnp.concatenate([u0, u1], axis=1), jnp.concatenate([et_ref[x] for x in range(4)], axis=1),
                    preferred_element_type=F32)
    for x in range(4):
        e_ref[x] = e_all[:, LANES * x:LANES * (x + 1)]
    afr, afi, abr, abi = (a16_ref[i:i + 1, :] for i in range(4))

    def body(i, carry):
        fr, fi, br, bi = carry
        kf = pl.ds(i, nb, stride=nchunk)
        kb = pl.ds(nchunk - 1 - i, nb, stride=nchunk)
        h_ref[0, kf, :] = fr
        h_ref[1, kf, :] = fi
        h_ref[2, kb, :] = br
        h_ref[3, kb, :] = bi
        return (afr * fr - afi * fi + e_ref[0, kf, :], afr * fi + afi * fr + e_ref[1, kf, :],
                abr * br - abi * bi + e_ref[2, kb, :], abr * bi + abi * br + e_ref[3, kb, :])

    fin = lax.fori_loop(0, nchunk, body, tuple(h0_ref[i] for i in range(4)), unroll=4)
    for i in range(4):
        fin_ref[i] = fin[i]
    nt = (((1,), (1,)), ((), ()))
    acc = lax.dot_general(jnp.concatenate([h_ref[x].astype(BF16) for x in range(4)], axis=1),
                          jnp.concatenate([ft_ref[x] for x in range(4)], axis=1), nt,
                          preferred_element_type=F32)
    y_ref[0] = jnp.dot(u0, t_ref[0], preferred_element_type=F32) + acc[:, :CW]
    y_ref[1] = jnp.dot(u1, t_ref[1], preferred_element_type=F32) + acc[:, CW:]


def _ssm_mix(ug, nb, seq, h0_re, h0_im, prep, l):
    tmat, et, ft, a16 = prep
    nchunk = seq // CHUNK
    r = ug.shape[1]
    g2 = SSM_GROUPS // 2
    width = SSM_GROUPS * SSM_STATE
    h0 = jnp.stack([h[:, d].reshape(nb, width).astype(F32) for d in range(2) for h in (h0_re, h0_im)], axis=0)
    pair = pl.BlockSpec((2, r, CW), lambda g: (g, 0, 0))
    mats = pl.BlockSpec((None, None, 4, 2 * CW, 2 * SSM_STATE), lambda g: (l, g, 0, 0, 0))
    state = pl.BlockSpec((4, nb, 2 * SSM_STATE), lambda g: (0, 0, g))
    y, fin = pl.pallas_call(
        functools.partial(_ssm_kernel, nchunk, nb),
        grid=(g2,),
        in_specs=[pair, pl.BlockSpec((None, 2, CW, CW), lambda g: (l, g, 0, 0)), mats, mats,
                  pl.BlockSpec((None, None, 4, 2 * SSM_STATE), lambda g: (l, g, 0, 0)), state],
        out_specs=[pair, state],
        out_shape=(jax.ShapeDtypeStruct((SSM_GROUPS, r, CW), F32), jax.ShapeDtypeStruct((4, nb, width), F32)),
        scratch_shapes=[pltpu.VMEM((4, r, 2 * SSM_STATE), F32), pltpu.VMEM((4, r, 2 * SSM_STATE), F32)],
        compiler_params=_params(("parallel",)),
        name="ssm",
    )(ug, tmat, et, ft, a16, h0)
    fin = fin.reshape(2, 2, nb, SSM_GROUPS, SSM_STATE)
    return y, fin


def _gelu_tanh(x):
    return 0.5 * x * (1.0 + jnp.tanh(math.sqrt(2.0 / math.pi) * (x + 0.044715 * (x * x * x))))


def _route(aff, bias):
    a = [aff[e:e + 1, :] for e in range(N_EXPERTS)]
    s = [a[e] + bias[e:e + 1, :] for e in range(N_EXPERTS)]
    keep = []
    for e in range(N_EXPERTS):
        g0 = (e // EXPERTS_PER_GROUP) * EXPERTS_PER_GROUP
        rank = None
        for j in range(g0, g0 + EXPERTS_PER_GROUP):
            if j == e:
                continue
            beats = (s[j] >= s[e]) if j < e else (s[j] > s[e])
            r = jnp.where(beats, 1.0, 0.0)
            rank = r if rank is None else rank + r
        keep.append(rank < 1.5)
    score = []
    for g in range(N_GROUPS):
        tot = None
        for e in range(g * EXPERTS_PER_GROUP, (g + 1) * EXPERTS_PER_GROUP):
            v = jnp.where(keep[e], s[e], 0.0)
            tot = v if tot is None else tot + v
        score.append(tot)
    gates = []
    picked = []
    for g in range(N_GROUPS):
        lost = None
        for j in range(N_GROUPS):
            if j == g:
                continue
            beats = (score[j] >= score[g]) if j < g else (score[j] > score[g])
            r = jnp.where(beats, 1.0, 0.0)
            lost = r if lost is None else lost + r
        chosen = lost < 0.5
        picked.append(jnp.where(chosen, 1.0, 0.0))
        w = [jnp.where(keep[e], a[e], 0.0) for e in range(g * EXPERTS_PER_GROUP, (g + 1) * EXPERTS_PER_GROUP)]
        tot = w[0] + w[1] + w[2] + w[3]
        for v in w:
            gates.append(jnp.where(chosen, v / tot, 0.0))
    return gates, picked


def _merge_kernel(x_ref, yg_ref, u_ref, at_ref, sga_ref, sgb_ref, mod_ref, d_ref, wglu_ref, bglu_ref, wsp_ref,
                  wap_ref, wo_ref, gffn_ref, wr_ref, rb_ref, tri_ref, ramp_ref,
                  xo_ref, hs_ref, mt_ref, ms_ref, of_ref, tok_ref):
    mod = mod_ref[...]
    gate1 = mod[:, 2 * D_MODEL:3 * D_MODEL]
    shift2 = mod[:, 3 * D_MODEL:4 * D_MODEL]
    scale2 = mod[:, 4 * D_MODEL:5 * D_MODEL]
    _from_chunk_rows(yg_ref, tok_ref)
    ys = jnp.concatenate([tok_ref[j] for j in range(LANE_TILES)], axis=1)
    y = _gelu_tanh(ys + d_ref[...] * u_ref[...].astype(F32))
    glu = jnp.dot(y.astype(BF16), wglu_ref[...], preferred_element_type=F32) + bglu_ref[...]
    y = (y * _sigmoid(glu)).astype(BF16)
    a_out = jnp.dot(y, wsp_ref[...], preferred_element_type=F32)
    b_out = jnp.dot(at_ref[...], wap_ref[...], preferred_element_type=F32)
    m = (sga_ref[...].astype(F32) * a_out + sgb_ref[...].astype(F32) * b_out).astype(BF16)
    x = x_ref[...] + gate1 * jnp.dot(m, wo_ref[...], preferred_element_type=F32)
    xo_ref[...] = x
    h2 = _rms(x, D_MODEL) * (gffn_ref[...] * (1.0 + scale2)) + shift2
    nt = (((1,), (1,)), ((), ()))
    h2_hi = h2.astype(BF16)
    h2_lo = (h2 - h2_hi.astype(F32)).astype(BF16)
    wr = wr_ref[...]
    wr_hi = wr.astype(BF16)
    wr_lo = (wr - wr_hi.astype(F32)).astype(BF16)
    logits = (lax.dot_general(wr_hi, h2_hi, nt, preferred_element_type=F32)
              + lax.dot_general(wr_hi, h2_lo, nt, preferred_element_type=F32)
              + lax.dot_general(wr_lo, h2_hi, nt, preferred_element_type=F32))
    gates, picked = _route(_sigmoid(logits), rb_ref[...])
    n = x.shape[0]
    onehot = jnp.concatenate(picked + [jnp.zeros((8 - N_GROUPS, n), F32)], axis=0)
    before = jnp.dot(onehot.astype(BF16), tri_ref[...], preferred_element_type=F32)
    count = [jnp.sum(p, axis=-1, keepdims=True) for p in picked]
    start = [jnp.zeros((1, 1), F32)]
    for g in range(N_GROUPS - 1):
        start.append(start[-1] + count[g])
    pos = sum(picked[g] * (start[g] + before[g:g + 1, :]) for g in range(N_GROUPS))
    gid = sum(float(g) * picked[g] for g in range(1, N_GROUPS))
    perm = jnp.where(ramp_ref[...] == pos, 1.0, 0.0).astype(BF16)
    hs_ref[...] = jnp.dot(perm, h2_hi, preferred_element_type=F32).astype(BF16)
    own = [sum(gates[EXPERTS_PER_GROUP * g + e] for g in range(N_GROUPS)) for e in range(EXPERTS_PER_GROUP)]
    meta = jnp.concatenate(own + [gid, pos, jnp.zeros((LANES - EXPERTS_PER_GROUP - 2, n), F32)], axis=0).T
    mt_ref[...] = meta
    hi = meta.astype(BF16)
    lo = (meta - hi.astype(F32)).astype(BF16)
    ms_ref[...] = (jnp.dot(perm, hi, preferred_element_type=F32)
                   + jnp.dot(perm, lo, preferred_element_type=F32))
    lane = lax.broadcasted_iota(jnp.int32, (8, LANES), 1)
    offs = sum(jnp.where(lane == g, start[g], 0.0) for g in range(1, N_GROUPS)) + jnp.where(lane == N_GROUPS, n, 0.0)
    of_ref[...] = offs.astype(jnp.int32)


def _merge(x, yg, u, attn, sga, sgb, mods, l, lat, wts):
    n = x.shape[0]
    tt = TOK_TILE
    per_seq = 2048 // tt
    mod_map = (lambda i: (l, 1 + i // per_seq, 0, 0)) if lat else (lambda i: (l, 0, 0, 0))
    row = lambda i: (i, 0)
    lw = lambda *shape: pl.BlockSpec((None,) + shape, lambda i: (l,) + (0,) * len(shape))
    return pl.pallas_call(
        _merge_kernel,
        grid=(n // tt,),
        in_specs=[
            pl.BlockSpec((tt, D_MODEL), row),
            pl.BlockSpec((SSM_GROUPS, TILE_CHUNKS, CW), lambda i: (0, i, 0)),
            pl.BlockSpec((tt, SSM_WIDTH), row),
            pl.BlockSpec((tt, ATTN_WIDTH), row), pl.BlockSpec((tt, D_MODEL), row), pl.BlockSpec((tt, D_MODEL), row),
            pl.BlockSpec((None, None, 1, 6 * D_MODEL), mod_map),
            lw(1, SSM_WIDTH), lw(SSM_WIDTH, SSM_WIDTH), lw(1, SSM_WIDTH), lw(SSM_WIDTH, D_MODEL),
            lw(ATTN_WIDTH, D_MODEL), lw(D_MODEL, D_MODEL), lw(1, D_MODEL),
            pl.BlockSpec((N_EXPERTS, D_MODEL), lambda i: (0, 0)),
            pl.BlockSpec((N_EXPERTS, 1), lambda i: (0, 0)),
            pl.BlockSpec((tt, tt), lambda i: (0, 0)), pl.BlockSpec((tt, tt), lambda i: (0, 0)),
        ],
        out_specs=[pl.BlockSpec((tt, D_MODEL), row), pl.BlockSpec((tt, D_MODEL), row),
                   pl.BlockSpec((tt, LANES), row), pl.BlockSpec((tt, LANES), row),
                   pl.BlockSpec((None, 8, LANES), lambda i: (i, 0, 0))],
        out_shape=(jax.ShapeDtypeStruct((n, D_MODEL), F32), jax.ShapeDtypeStruct((n, D_MODEL), BF16),
                   jax.ShapeDtypeStruct((n, LANES), F32), jax.ShapeDtypeStruct((n, LANES), F32),
                   jax.ShapeDtypeStruct((n // tt, 8, LANES), jnp.int32)),
        scratch_shapes=[pltpu.VMEM((LANE_TILES, tt, LANES), F32)],
        compiler_params=_params(("parallel",)),
        name="merge_lat" if lat else "merge_ctx",
    )(x, yg, u, attn, sga, sgb, mods, wts["ssm_d"], wts["w_glu"], wts["b_glu"], wts["w_ssm_proj"],
      wts["w_attn_proj"], wts["w_o"], wts["g_ffn"], wts["w_router_t"], wts["router_bias"],
      jnp.asarray(np.triu(np.ones((tt, tt), np.float32), 1), BF16),
      jnp.asarray(np.broadcast_to(np.arange(tt, dtype=np.float32)[:, None], (tt, tt))))


def _moe_kernel(offs_ref, x_ref, hs_ref, ms_ref, mt_ref, mod_ref, wg_ref, wu_ref, wd_ref, o_ref, acc_ref):
    tile = pl.program_id(0)
    acc_ref[...] = jnp.zeros(acc_ref.shape, F32)

    def run_passes(height, g, first, span):
        row_id = lax.broadcasted_iota(jnp.int32, (height, 1), 0)
        n_pass = sum((span > k * height).astype(jnp.int32) for k in range(pl.cdiv(TOK_TILE + 16, height)))

        def pass_body(j, c):
            lo = first + height * j
            s = pl.multiple_of(jnp.minimum(lo, TOK_TILE - height), 16)
            rows = pl.ds(s, height)
            meta = ms_ref[rows, :]
            group_id = lax.convert_element_type(g, F32)
            mine = (meta[:, EXPERTS_PER_GROUP:EXPERTS_PER_GROUP + 1] == group_id) & (row_id + s >= lo)
            own = jnp.where(mine, 1.0, 0.0)
            hs = hs_ref[rows, :]
            acts = []
            for e in range(EXPERTS_PER_GROUP):
                expert = EXPERTS_PER_GROUP * g + e
                hg = jnp.dot(hs, wg_ref[expert], preferred_element_type=F32)
                hu = jnp.dot(hs, wu_ref[expert], preferred_element_type=F32)
                acts.append(((hg * _sigmoid(hg)) * hu * (meta[:, e:e + 1] * own)).astype(BF16))
            acc_ref[rows, :] += jnp.dot(jnp.concatenate(acts, axis=1), wd_ref[g], preferred_element_type=F32)
            return c

        lax.fori_loop(0, n_pass, pass_body, 0)

    def group_body(g, carry):
        seg_start = offs_ref[(N_GROUPS + 1) * tile + g]
        seg_end = offs_ref[(N_GROUPS + 1) * tile + g + 1]
        first = (seg_start >> 4) << 4
        span = jnp.where(seg_end > seg_start, seg_end - first, 0)
        choice = sum((span > height).astype(jnp.int32) for height in MOE_PASS_ROWS[:-1])
        lax.switch(choice, [functools.partial(run_passes, height, g, first, span) for height in MOE_PASS_ROWS])
        return carry

    lax.fori_loop(0, N_GROUPS, group_body, 0)

    slot = lax.broadcasted_iota(jnp.int32, (TOK_TILE, TOK_TILE), 1).astype(F32)
    unsort = jnp.where(mt_ref[:, EXPERTS_PER_GROUP + 1:EXPERTS_PER_GROUP + 2] == slot, 1.0, 0.0).astype(BF16)
    y = jnp.dot(unsort, acc_ref[...].astype(BF16), preferred_element_type=F32)
    o_ref[...] = x_ref[...] + mod_ref[:, 5 * D_MODEL:6 * D_MODEL] * y


def _moe(x, hs, meta_tok, meta_sorted, offs, mods, l, lat, wts):
    n = x.shape[0]
    tt = TOK_TILE
    per_seq = 2048 // tt
    mod_map = (lambda i, o: (l, 1 + i // per_seq, 0, 0)) if lat else (lambda i, o: (l, 0, 0, 0))
    row = lambda i, o: (i, 0)
    width = EXPERTS_PER_GROUP * D_EXPERT
    resident = lambda *shape: pl.BlockSpec((None,) + shape, lambda i, o: (l,) + (0,) * len(shape),
                                           pipeline_mode=pl.Buffered(1))
    grid_spec = pltpu.PrefetchScalarGridSpec(
        num_scalar_prefetch=1,
        grid=(n // tt,),
        in_specs=[
            pl.BlockSpec((tt, D_MODEL), row), pl.BlockSpec((tt, D_MODEL), row), pl.BlockSpec((tt, LANES), row),
            pl.BlockSpec((tt, LANES), row),
            pl.BlockSpec((None, None, 1, 6 * D_MODEL), mod_map),
            resident(N_EXPERTS, D_MODEL, D_EXPERT), resident(N_EXPERTS, D_MODEL, D_EXPERT),
            resident(N_GROUPS, width, D_MODEL),
        ],
        out_specs=pl.BlockSpec((tt, D_MODEL), row),
        scratch_shapes=[pltpu.VMEM((tt, D_MODEL), F32)],
    )
    return pl.pallas_call(
        _moe_kernel,
        grid_spec=grid_spec,
        out_shape=jax.ShapeDtypeStruct((n, D_MODEL), F32),
        compiler_params=_params(("arbitrary",), MOE_VMEM_LIMIT),
        name="moe_lat" if lat else "moe_ctx",
    )(offs[:, 0, :N_GROUPS + 1].reshape(-1), x, hs, meta_sorted, meta_tok, mods, wts["w_gate"], wts["w_up"],
      wts["w_down"])


def _pad_heads(w, width):
    lead = w.shape[:-1]
    w = w.reshape(lead + (N_HEADS, width))
    w = jnp.pad(w, [(0, 0)] * len(lead) + [(0, 0), (0, HEAD_PAD - width)])
    return w.reshape(lead + (N_HEADS * HEAD_PAD,))


def _swap_partners(w):
    lead = w.shape[:-1]
    w = w.reshape(lead + (-1, HEAD_PAD))
    half = ROPE_AXIS // 2
    a, b = QK_NOPE, QK_NOPE + ROPE_AXIS
    parts = [w[..., :a], w[..., a + half:b], w[..., a:a + half], w[..., b + half:b + 2 * half], w[..., b:b + half],
             w[..., b + 2 * half:]]
    return jnp.concatenate(parts, axis=-1).reshape(lead + (-1,))


def _layout_weights(w_in, g_qa, w_qb, g_q, g_kva, w_kvb, g_k, g_mix, w_attn_proj, ssm_d, ssm_w_glu, ssm_b_glu,
                    ssm_w_proj, w_o, g_ffn, w_router, router_bias, w_gate, w_up, w_down):
    n_a = SSM_WIDTH + Q_LORA + KV_LORA + QK_ROPE
    w_a = jnp.pad(w_in[:, :, :n_a], ((0, 0), (0, 0), (0, 1024 - n_a)))
    kv = w_kvb.reshape(DEPTH, KV_LORA, N_HEADS, QK_NOPE + V_HEAD)
    k_cols = _pad_heads(kv[..., :QK_NOPE].reshape(DEPTH, KV_LORA, N_HEADS * QK_NOPE), QK_NOPE)
    v_cols = _pad_heads(kv[..., QK_NOPE:].reshape(DEPTH, KV_LORA, ATTN_WIDTH), V_HEAD)
    place = np.zeros((KV_LORA, 2 * QK_PAD), np.float32)
    for h in range(N_HEADS):
        for r in range(QK_ROPE):
            place[r, h * HEAD_PAD + QK_NOPE + r] = 1.0
    w_k = jnp.concatenate([jnp.concatenate([k_cols, v_cols], axis=-1),
                           jnp.broadcast_to(jnp.asarray(place), (DEPTH,) + place.shape)], axis=1)
    w_qb_pad = _pad_heads(w_qb, QK_HEAD).astype(BF16)
    w_k = w_k.astype(BF16)
    pad_gain = lambda g: jnp.pad(g, ((0, 0), (0, HEAD_PAD - QK_HEAD))).reshape(DEPTH, 1, HEAD_PAD)
    vec = lambda g: g.reshape(DEPTH, 1, -1)
    return {
        "g_mix": vec(g_mix), "w_a": w_a.astype(BF16),
        "w_ga": w_in[:, :, n_a:n_a + D_MODEL].astype(BF16), "w_gb": w_in[:, :, n_a + D_MODEL:].astype(BF16),
        "g_qa": vec(g_qa), "w_qb": w_qb_pad, "w_qb_sw": _swap_partners(w_qb_pad),
        "g_q": pad_gain(g_q) * Q_SCALE, "g_q_sw": _swap_partners(pad_gain(g_q)) * Q_SCALE,
        "g_kva": vec(g_kva), "w_k": w_k, "w_k_sw": _swap_partners(w_k[:, :, :QK_PAD]),
        "g_k": pad_gain(g_k), "g_k_sw": _swap_partners(pad_gain(g_k)),
        "ssm_d": vec(ssm_d), "w_glu": ssm_w_glu.astype(BF16), "b_glu": vec(ssm_b_glu),
        "w_ssm_proj": ssm_w_proj.astype(BF16), "w_attn_proj": w_attn_proj.astype(BF16), "w_o": w_o.astype(BF16),
        "g_ffn": vec(g_ffn), "w_router_t": w_router.T, "router_bias": router_bias.reshape(N_EXPERTS, 1),
        "w_gate": w_gate.astype(BF16), "w_up": w_up.astype(BF16),
        "w_down": w_down.astype(BF16).reshape(DEPTH, N_GROUPS, EXPERTS_PER_GROUP * D_EXPERT, D_MODEL),
    }


def _rope_tables(n_tokens):
    pos = np.arange(n_tokens)
    inv = 1.0 / (ROPE_BASE ** (np.arange(ROPE_AXIS // 2, dtype=np.float32) * 2.0 / ROPE_AXIS))
    ang = np.zeros((n_tokens, HEAD_PAD), np.float32)
    half = ROPE_AXIS // 2
    row = (pos // GRID_W).astype(np.float32)[:, None] * inv
    col = (pos % GRID_W).astype(np.float32)[:, None] * inv
    first = np.zeros((HEAD_PAD,), bool)
    second = np.zeros((HEAD_PAD,), bool)
    for base, a in ((QK_NOPE, row), (QK_NOPE + ROPE_AXIS, col)):
        ang[:, base:base + half] = a
        ang[:, base + half:base + 2 * half] = a
        first[base:base + half] = True
        second[base + half:base + 2 * half] = True
    ang = jnp.asarray(ang)
    cos = jnp.where(jnp.asarray(first | second), jnp.cos(ang), 1.0)
    sin = jnp.sin(ang)
    return cos, jnp.where(jnp.asarray(first), -sin, jnp.where(jnp.asarray(second), sin, 0.0))


def kernel(x_prompt, x_sample, c, cache_ckv, cache_krope, state_ssm_re, state_ssm_im, c_ctx, w_ada, b_ada, g_mix, w_in, g_qa, w_qb, g_q, g_kva, w_kvb, g_k, w_attn_proj, ssm_a_re, ssm_a_im, ssm_log_dt, ssm_b_re, ssm_b_im, ssm_c_re, ssm_c_im, ssm_d, ssm_w_glu, ssm_b_glu, ssm_w_proj, w_o, g_ffn, w_router, router_bias, w_gate, w_up, w_down):
    nb_c, seq_c, _ = x_prompt.shape
    nb_l, seq_l, _ = x_sample.shape
    wts = _layout_weights(w_in, g_qa, w_qb, g_q, g_kva, w_kvb, g_k, g_mix, w_attn_proj, ssm_d, ssm_w_glu,
                          ssm_b_glu, ssm_w_proj, w_o, g_ffn, w_router, router_bias, w_gate, w_up, w_down)
    cvec = jnp.concatenate([c_ctx[None, :], c, jnp.zeros((8 - 1 - nb_l, D_MODEL), F32)], axis=0)
    mods = _mods(cvec, w_ada, b_ada).reshape(DEPTH, 8, 1, 6 * D_MODEL)
    prep = _ssm_prep(ssm_a_re, ssm_a_im, ssm_log_dt, ssm_b_re, ssm_b_im, ssm_c_re, ssm_c_im)
    tabs = _rope_tables(seq_l)
    kc, vc = _cachekv(cache_ckv, cache_krope, wts)

    xp = x_prompt.reshape(nb_c * seq_c, D_MODEL)
    xs = x_sample.reshape(nb_l * seq_l, D_MODEL)
    zeros_c = jnp.zeros((nb_c, 2, SSM_GROUPS, SSM_STATE), F32)
    ckvs, kropes, fins = [], [], []
    for l in range(DEPTH):
        u, ug, q, k, v, sga, sgb, ckv, krope = _inproj(xp, mods, l, False, wts, tabs)
        ckvs.append(ckv)
        kropes.append(krope)
        yg, fin = _ssm_mix(ug, nb_c, seq_c, zeros_c, zeros_c, prep, l)
        fins.append(fin)
        attn = _attn_ctx(q, k, v, seq_c)
        xp, *routed = _merge(xp, yg, u, attn, sga, sgb, mods, l, False, wts)
        xp = _moe(xp, *routed, mods, l, False, wts)

        u, ug, q, k, v, sga, sgb, _, _ = _inproj(xs, mods, l, True, wts, tabs)
        yg, _ = _ssm_mix(ug, nb_l, seq_l, state_ssm_re[:, l], state_ssm_im[:, l], prep, l)
        attn = _attn_lat(q, k, v, kc, vc, l, seq_l, TOK_TILE)
        xs, *routed = _merge(xs, yg, u, attn, sga, sgb, mods, l, True, wts)
        xs = _moe(xs, *routed, mods, l, True, wts)

    new_ckv = jnp.stack(ckvs, axis=0).reshape(DEPTH, nb_c, seq_c, KV_LORA).transpose(1, 0, 2, 3)
    new_krope = jnp.stack(kropes, axis=0).reshape(DEPTH, nb_c, seq_c, QK_ROPE).transpose(1, 0, 2, 3)
    fin = jnp.stack(fins, axis=0)
    new_re = jnp.transpose(fin[:, :, 0], (2, 0, 1, 3, 4))
    new_im = jnp.transpose(fin[:, :, 1], (2, 0, 1, 3, 4))
    return (xp.reshape(nb_c, seq_c, D_MODEL), xs.reshape(nb_l, seq_l, D_MODEL), new_ckv, new_krope, new_re, new_im)
```

```python
import functools
import math

import jax
import jax.numpy as jnp
import numpy as np
from jax import lax
from jax.experimental import pallas as pl
from jax.experimental.pallas import tpu as pltpu

F32 = jnp.float32
BF16 = jnp.bfloat16

D_MODEL = 1024
DEPTH = 4
GRID_W = 64
SSM_WIDTH = 512
SSM_GROUP = 16
SSM_GROUPS = 32
SSM_STATE = 64
N_HEADS = 8
QK_NOPE = 64
QK_ROPE = 32
QK_HEAD = 96
V_HEAD = 64
Q_LORA = 256
KV_LORA = 128
ATTN_WIDTH = 512
ROPE_AXIS = 16
ROPE_BASE = 10000.0
N_EXPERTS = 16
N_GROUPS = 4
EXPERTS_PER_GROUP = 4
D_EXPERT = 256
EPS = 1e-6

LANES = 128
HEAD_PAD = LANES
QK_PAD = N_HEADS * HEAD_PAD
CHUNK = 16
CW = CHUNK * SSM_GROUP
TOK_TILE = 512
TILE_CHUNKS = TOK_TILE // CHUNK
LANE_TILES = SSM_WIDTH // LANES
GROUPS_PER_TILE = LANES // SSM_GROUP
Q_SCALE = math.log2(math.e) / math.sqrt(QK_HEAD)
MOE_PASS_ROWS = (128, 160, 192, 224, 256)
VMEM_LIMIT = 48 * 1024 * 1024
MOE_VMEM_LIMIT = 56 * 1024 * 1024
ATTN_VMEM_LIMIT = 60 * 1024 * 1024


def _params(sem, vmem=VMEM_LIMIT):
    return pltpu.CompilerParams(dimension_semantics=sem, vmem_limit_bytes=vmem)


def _sigmoid(x):
    return 0.5 * jnp.tanh(0.5 * x) + 0.5


def _rms(x, n):
    return x * lax.rsqrt(jnp.sum(x * x, axis=-1, keepdims=True) * (1.0 / n) + EPS)


def _mods_kernel(c_ref, w_ref, b_ref, o_ref):
    c = c_ref[...]
    cs = (c * _sigmoid(c)).astype(BF16)
    o_ref[...] = jnp.dot(cs, w_ref[...].astype(BF16), preferred_element_type=F32) + b_ref[...]


def _mods(cvec, w_ada, b_ada):
    nb = 1536
    return pl.pallas_call(
        _mods_kernel,
        grid=(DEPTH, 6 * D_MODEL // nb),
        in_specs=[
            pl.BlockSpec((8, D_MODEL), lambda l, j: (0, 0)),
            pl.BlockSpec((None, D_MODEL, nb), lambda l, j: (l, 0, j)),
            pl.BlockSpec((None, 1, nb), lambda l, j: (l, 0, j)),
        ],
        out_specs=pl.BlockSpec((None, 8, nb), lambda l, j: (l, 0, j)),
        out_shape=jax.ShapeDtypeStruct((DEPTH, 8, 6 * D_MODEL), F32),
        compiler_params=_params(("arbitrary", "arbitrary")),
        name="mods",
    )(cvec, w_ada, b_ada.reshape(DEPTH, 1, 6 * D_MODEL))


def _slot_step(slot, g):
    return (slot & 8) + ((slot - g) & 7)


def _lane_segment():
    return lax.broadcasted_iota(jnp.int32, (TILE_CHUNKS, LANES), 1) // SSM_GROUP


def _to_chunk_rows(tok_ref, ug_ref):
    seg = _lane_segment()
    for j in range(LANE_TILES):
        for half in range(CHUNK // 8):
            rolled = []
            for r in range(8):
                v = tok_ref[j, pl.ds(8 * half + r, TILE_CHUNKS, stride=CHUNK), :]
                rolled.append(pltpu.roll(v, SSM_GROUP * r, 1) if r else v)
            for gg in range(GROUPS_PER_TILE):
                acc = rolled[(0 - gg) % 8]
                for m in range(1, 8):
                    acc = jnp.where(seg == m, rolled[(m - gg) % 8], acc)
                ug_ref[GROUPS_PER_TILE * j + gg, :, LANES * half:LANES * (half + 1)] = acc.astype(ug_ref.dtype)


def _from_chunk_rows(yg_ref, tok_ref):
    seg = _lane_segment()
    for j in range(LANE_TILES):
        for half in range(CHUNK // 8):
            ys = [yg_ref[GROUPS_PER_TILE * j + gg, :, LANES * half:LANES * (half + 1)]
                  for gg in range(GROUPS_PER_TILE)]
            for r in range(8):
                acc = ys[(0 - r) % 8]
                for m in range(1, 8):
                    acc = jnp.where(seg == m, ys[(m - r) % 8], acc)
                out = pltpu.roll(acc, LANES - SSM_GROUP * r, 1) if r else acc
                tok_ref[j, pl.ds(8 * half + r, TILE_CHUNKS, stride=CHUNK), :] = out


def _head_norm_rope(t, gain, swapped):
    outs = []
    for h in range(N_HEADS):
        lanes = slice(h * HEAD_PAD, (h + 1) * HEAD_PAD)
        th = t[:, lanes]
        scale = lax.rsqrt(jnp.sum(th * th, axis=-1, keepdims=True) * (1.0 / QK_HEAD) + EPS)
        y = th * gain
        if swapped is not None:
            y = y + swapped[0][:, lanes] * swapped[1]
        outs.append(y * scale)
    return jnp.concatenate(outs, axis=1)


def _with_ones(v):
    lane = lax.broadcasted_iota(jnp.int32, (1, v.shape[1]), 1)
    return jnp.where((lane & (HEAD_PAD - 1)) == V_HEAD, 1.0, v)


def _kv_dot(ckvn, krp, wk_ref):
    return (jnp.dot(ckvn, wk_ref[0:KV_LORA, :], preferred_element_type=F32)
            + jnp.dot(krp, wk_ref[KV_LORA:KV_LORA + krp.shape[1], :], preferred_element_type=F32))


def _inproj_kernel(use_rope, x_ref, mod_ref, gmix_ref, wa_ref, wga_ref, wgb_ref, gqa_ref, wqb_ref, wqbs_ref, gq_ref,
                   gqs_ref, gkva_ref, wk_ref, wks_ref, gk_ref, gks_ref, cos_ref, sin_ref,
                   u_ref, ug_ref, q_ref, k_ref, v_ref, sga_ref, sgb_ref, ckv_ref, kr_ref, tok_ref):
    mod = mod_ref[...]
    shift1 = mod[:, 0:D_MODEL]
    gain1 = gmix_ref[...] * (1.0 + mod[:, D_MODEL:2 * D_MODEL])
    h = (_rms(x_ref[...], D_MODEL) * gain1 + shift1).astype(BF16)
    za = jnp.dot(h, wa_ref[...], preferred_element_type=F32)
    sga_ref[...] = _sigmoid(jnp.dot(h, wga_ref[...], preferred_element_type=F32)).astype(BF16)
    sgb_ref[...] = _sigmoid(jnp.dot(h, wgb_ref[...], preferred_element_type=F32)).astype(BF16)
    u_ref[...] = za[:, :SSM_WIDTH].astype(BF16)
    for j in range(LANE_TILES):
        tok_ref[j] = za[:, LANES * j:LANES * (j + 1)]
    _to_chunk_rows(tok_ref, ug_ref)
    qa = za[:, SSM_WIDTH:SSM_WIDTH + Q_LORA]
    ckv = za[:, SSM_WIDTH + Q_LORA:SSM_WIDTH + Q_LORA + KV_LORA]
    krp = za[:, SSM_WIDTH + Q_LORA + KV_LORA:]
    qn = (_rms(qa, Q_LORA) * gqa_ref[...]).astype(BF16)
    qr = jnp.dot(qn, wqb_ref[...], preferred_element_type=F32)
    ckvn = _rms(ckv, KV_LORA) * gkva_ref[...]
    ckv_ref[...] = ckvn
    kr_ref[...] = krp[:, :QK_ROPE]
    ckvn = ckvn.astype(BF16)
    krp = krp.astype(BF16)
    kv = _kv_dot(ckvn, krp, wk_ref)
    gq = gq_ref[...]
    gk = gk_ref[...]
    q_sw = k_sw = None
    if use_rope:
        cos = cos_ref[...]
        sin = sin_ref[...]
        q_sw = (jnp.dot(qn, wqbs_ref[...], preferred_element_type=F32), gqs_ref[...] * sin)
        k_sw = (_kv_dot(ckvn, krp, wks_ref), gks_ref[...] * sin)
        gq = gq * cos
        gk = gk * cos
    q_ref[...] = _head_norm_rope(qr, gq, q_sw).astype(BF16)
    k_ref[...] = _head_norm_rope(kv[:, :QK_PAD], gk, k_sw).astype(BF16)
    v_ref[...] = _with_ones(kv[:, QK_PAD:]).astype(BF16)


def _inproj(x, mods, l, lat, wts, tabs):
    n = x.shape[0]
    tt = TOK_TILE
    per_seq = 2048 // tt
    if lat:
        mod_map = lambda i: (l, 1 + i // per_seq, 0, 0)
        tab_map = lambda i: (i % per_seq, 0)
    else:
        mod_map = lambda i: (l, 0, 0, 0)
        tab_map = lambda i: (0, 0)
    row = lambda i: (i, 0)
    lw = lambda *shape: pl.BlockSpec((None,) + shape, lambda i: (l,) + (0,) * len(shape))
    tab = pl.BlockSpec((tt, HEAD_PAD), tab_map)
    row_shapes = (
        jax.ShapeDtypeStruct((n, SSM_WIDTH), BF16),
        jax.ShapeDtypeStruct((n, QK_PAD), BF16),
        jax.ShapeDtypeStruct((n, QK_PAD), BF16),
        jax.ShapeDtypeStruct((n, QK_PAD), BF16),
        jax.ShapeDtypeStruct((n, D_MODEL), BF16),
        jax.ShapeDtypeStruct((n, D_MODEL), BF16),
        jax.ShapeDtypeStruct((n, KV_LORA), F32),
        jax.ShapeDtypeStruct((n, QK_ROPE), F32),
    )
    row_specs = [pl.BlockSpec((tt, s.shape[1]), row) for s in row_shapes]
    ug_shape = jax.ShapeDtypeStruct((SSM_GROUPS, n // CHUNK, CW), BF16)
    ug_spec = pl.BlockSpec((SSM_GROUPS, TILE_CHUNKS, CW), lambda i: (0, i, 0))
    outs = pl.pallas_call(
        functools.partial(_inproj_kernel, lat),
        grid=(n // tt,),
        in_specs=[
            pl.BlockSpec((tt, D_MODEL), row),
            pl.BlockSpec((None, None, 1, 6 * D_MODEL), mod_map),
            lw(1, D_MODEL), lw(D_MODEL, 1024), lw(D_MODEL, D_MODEL), lw(D_MODEL, D_MODEL),
            lw(1, Q_LORA), lw(Q_LORA, QK_PAD), lw(Q_LORA, QK_PAD), lw(1, HEAD_PAD), lw(1, HEAD_PAD), lw(1, KV_LORA),
            lw(2 * KV_LORA, 2 * QK_PAD), lw(2 * KV_LORA, QK_PAD), lw(1, HEAD_PAD), lw(1, HEAD_PAD),
            tab, tab,
        ],
        out_specs=row_specs[:1] + [ug_spec] + row_specs[1:],
        out_shape=row_shapes[:1] + (ug_shape,) + row_shapes[1:],
        scratch_shapes=[pltpu.VMEM((LANE_TILES, tt, LANES), F32)],
        compiler_params=_params(("parallel",)),
        name="inproj_lat" if lat else "inproj_ctx",
    )(x, mods, wts["g_mix"], wts["w_a"], wts["w_ga"], wts["w_gb"], wts["g_qa"], wts["w_qb"], wts["w_qb_sw"],
      wts["g_q"], wts["g_q_sw"], wts["g_kva"], wts["w_k"], wts["w_k_sw"], wts["g_k"], wts["g_k_sw"], *tabs)
    return outs


def _cachekv_kernel(ckv_ref, kr_ref, wk_ref, gk_ref, k_ref, v_ref):
    kv = _kv_dot(ckv_ref[...].astype(BF16), kr_ref[...].astype(BF16), wk_ref)
    k_ref[...] = _head_norm_rope(kv[:, :QK_PAD], gk_ref[...], None).astype(BF16)
    v_ref[...] = _with_ones(kv[:, QK_PAD:]).astype(BF16)


def _cachekv(cache_ckv, cache_krope, wts):
    nb, _, past, _ = cache_ckv.shape
    return pl.pallas_call(
        _cachekv_kernel,
        grid=(DEPTH, nb),
        in_specs=[
            pl.BlockSpec((None, None, past, KV_LORA), lambda l, b: (b, l, 0, 0)),
            pl.BlockSpec((None, None, past, QK_ROPE), lambda l, b: (b, l, 0, 0)),
            pl.BlockSpec((None, 2 * KV_LORA, 2 * QK_PAD), lambda l, b: (l, 0, 0)),
            pl.BlockSpec((None, 1, HEAD_PAD), lambda l, b: (l, 0, 0)),
        ],
        out_specs=[
            pl.BlockSpec((None, None, past, QK_PAD), lambda l, b: (l, b, 0, 0)),
            pl.BlockSpec((None, None, past, QK_PAD), lambda l, b: (l, b, 0, 0)),
        ],
        out_shape=(jax.ShapeDtypeStruct((DEPTH, nb, past, QK_PAD), BF16),
                   jax.ShapeDtypeStruct((DEPTH, nb, past, QK_PAD), BF16)),
        compiler_params=_params(("arbitrary", "arbitrary")),
        name="cache_kv",
    )(cache_ckv, cache_krope, wts["w_k"], wts["g_k"])


def _attn_heads(q_ref, segs, o_ref, den_on_mxu):
    for h in range(N_HEADS):
        qh = q_ref[:, h * HEAD_PAD:(h + 1) * HEAD_PAD]
        scores = [lax.dot_general(qh, k_ref[:, h * HEAD_PAD:(h + 1) * HEAD_PAD], (((1,), (1,)), ((), ())),
                                  preferred_element_type=F32) for k_ref, _ in segs]
        m = scores[0].max(axis=-1, keepdims=True)
        for s in scores[1:]:
            m = jnp.maximum(m, s.max(axis=-1, keepdims=True))
        acc = None
        den = None
        for s, (_, v_ref) in zip(scores, segs):
            if den_on_mxu:
                p = jnp.exp2((s - m).astype(BF16))
                a = jnp.dot(p, v_ref[:, h * HEAD_PAD:(h + 1) * HEAD_PAD], preferred_element_type=F32)
            else:
                p = jnp.exp2(s - m)
                d = p.sum(axis=-1, keepdims=True)
                den = d if den is None else den + d
                a = jnp.dot(p.astype(BF16), v_ref[:, h * HEAD_PAD:h * HEAD_PAD + V_HEAD],
                            preferred_element_type=F32)
            acc = a if acc is None else acc + a
        if den_on_mxu:
            den = acc[:, V_HEAD:V_HEAD + 1]
        o_ref[:, h * V_HEAD:(h + 1) * V_HEAD] = (acc[:, :V_HEAD] / den).astype(o_ref.dtype)


def _attn_ctx_kernel(q_ref, k_ref, v_ref, o_ref):
    for b in range(q_ref.shape[0]):
        _attn_heads(q_ref.at[b], [(k_ref.at[b], v_ref.at[b])], o_ref.at[b], False)


def _attn_lat_kernel(q_ref, kc_ref, vc_ref, kl_ref, vl_ref, o_ref):
    _attn_heads(q_ref, [(kc_ref, vc_ref), (kl_ref, vl_ref)], o_ref, True)


def _attn_ctx(q, k, v, seq):
    n = q.shape[0]
    nb = n // seq
    per_step = 2
    blk = lambda w: pl.BlockSpec((per_step, seq, w), lambda b: (b, 0, 0))
    seqs = lambda a: a.reshape(nb, seq, a.shape[1])
    return pl.pallas_call(
        _attn_ctx_kernel,
        grid=(nb // per_step,),
        in_specs=[blk(QK_PAD), blk(QK_PAD), blk(QK_PAD)],
        out_specs=blk(ATTN_WIDTH),
        out_shape=jax.ShapeDtypeStruct((nb, seq, ATTN_WIDTH), BF16),
        compiler_params=_params(("parallel",)),
        name="attn_ctx",
    )(seqs(q), seqs(k), seqs(v)).reshape(n, ATTN_WIDTH)


def _attn_lat(q, k, v, kc, vc, l, seq, tq):
    n = q.shape[0]
    per = seq // tq
    past = kc.shape[2]
    return pl.pallas_call(
        _attn_lat_kernel,
        grid=(n // seq, per),
        in_specs=[
            pl.BlockSpec((tq, QK_PAD), lambda b, j: (b * per + j, 0)),
            pl.BlockSpec((None, None, past, QK_PAD), lambda b, j: (l, b, 0, 0)),
            pl.BlockSpec((None, None, past, QK_PAD), lambda b, j: (l, b, 0, 0)),
            pl.BlockSpec((seq, QK_PAD), lambda b, j: (b, 0)),
            pl.BlockSpec((seq, QK_PAD), lambda b, j: (b, 0)),
        ],
        out_specs=pl.BlockSpec((tq, ATTN_WIDTH), lambda b, j: (b * per + j, 0)),
        out_shape=jax.ShapeDtypeStruct((n, ATTN_WIDTH), BF16),
        compiler_params=_params(("parallel", "arbitrary"), ATTN_VMEM_LIMIT),
        name="attn_lat",
    )(q, kc, vc, k, v)


def _cmul(ar, ai, br, bi):
    return ar * br - ai * bi, ar * bi + ai * br


def _outer_rows(pw, m):
    return (pw[:, None, :] * m[None, :, :]).reshape(CW, SSM_STATE)


def _ssm_prep_kernel(are_ref, aim_ref, ldt_ref, bre_ref, bim_ref, cre_ref, cim_ref,
                     t_ref, et_ref, ft_ref, a16_ref):
    et_ref[...] = jnp.zeros(et_ref.shape, et_ref.dtype)
    ft_ref[...] = jnp.zeros(ft_ref.shape, ft_ref.dtype)
    row_id = lax.broadcasted_iota(jnp.int32, (CHUNK, SSM_STATE), 0)
    strip_lane = lax.broadcasted_iota(jnp.int32, (SSM_GROUP, CW), 1)
    nt = (((1,), (1,)), ((), ()))
    for gi in range(2):
        g = 2 * pl.program_id(1) + gi
        steps = _slot_step(row_id, g)
        lanes = slice(gi * SSM_STATE, (gi + 1) * SSM_STATE)
        rows = slice(gi * CW, (gi + 1) * CW)
        cre = cre_ref[gi]
        cim = cim_ref[gi]
        m_dir = []
        for d in range(2):
            a_re = are_ref[d, gi]
            a_im = aim_ref[d, gi]
            dt = jnp.exp(ldt_ref[d, gi])
            mag = jnp.exp(a_re * dt)
            abr = mag * jnp.cos(a_im * dt)
            abi = mag * jnp.sin(a_im * dt)
            den = a_re * a_re + a_im * a_im
            nr = abr - 1.0
            coef_re = (nr * a_re + abi * a_im) / den
            coef_im = (abi * a_re - nr * a_im) / den
            bbr, bbi = _cmul(coef_re, coef_im, bre_ref[d, gi], bim_ref[d, gi])

            squares = [(abr, abi)]
            for _ in range(4):
                squares.append(_cmul(*squares[-1], *squares[-1]))

            def power(n):
                pr = jnp.ones(n.shape, F32)
                pi = jnp.zeros(n.shape, F32)
                for b, (sr, si) in enumerate(squares):
                    nr_, ni_ = _cmul(pr, pi, sr, si)
                    hit = (n & (1 << b)) != 0
                    pr = jnp.where(hit, nr_, pr)
                    pi = jnp.where(hit, ni_, pi)
                return pr, pi

            to_end = (CHUNK - 1 - steps) if d == 0 else steps
            from_start = (steps + 1) if d == 0 else (CHUNK - steps)
            lag = row_id if d == 0 else (CHUNK - 1 - row_id)
            pr, pi = power(to_end)
            er = _outer_rows(pr, bbr) - _outer_rows(pi, bbi)
            ei = _outer_rows(pr, bbi) + _outer_rows(pi, bbr)
            et_ref[2 * d, rows, lanes] = er.astype(et_ref.dtype)
            et_ref[2 * d + 1, rows, lanes] = ei.astype(et_ref.dtype)
            pr, pi = power(from_start)
            ft_ref[2 * d, rows, lanes] = (_outer_rows(pr, cre) - _outer_rows(pi, cim)).astype(ft_ref.dtype)
            ft_ref[2 * d + 1, rows, lanes] = (-(_outer_rows(pr, cim) + _outer_rows(pi, cre))).astype(ft_ref.dtype)
            pr, pi = power(lag)
            zr = _outer_rows(pr, cre) - _outer_rows(pi, cim)
            zi = _outer_rows(pi, cre) + _outer_rows(pr, cim)
            m_dir.append(
                lax.dot_general(bbr, zr, nt, preferred_element_type=F32, precision=lax.Precision.HIGHEST)
                - lax.dot_general(bbi, zi, nt, preferred_element_type=F32, precision=lax.Precision.HIGHEST))
            p16r, p16i = squares[4]
            a16_ref[2 * d:2 * d + 1, lanes] = p16r
            a16_ref[2 * d + 1:2 * d + 2, lanes] = p16i
        mf, mb = m_dir
        edge = CW - SSM_GROUP
        low = mb + pltpu.roll(jnp.where(strip_lane < SSM_GROUP, mf, 0.0), edge, 1)
        high = jnp.where(strip_lane < edge, pltpu.roll(mf, edge, 1), 0.0)
        strip = jnp.concatenate([low, high], axis=1)
        turn = SSM_GROUP * (g & 7)
        for j in range(CHUNK):
            first = SSM_GROUP * (CHUNK - 1 - j)
            win = strip[:, first:first + CW]
            win = jnp.concatenate([pltpu.roll(win[:, LANES * h:LANES * (h + 1)], turn, 1)
                                   for h in range(CW // LANES)], axis=1)
            slot = (j & 8) + ((j + g) & 7)
            t_ref[gi, pl.ds(pl.multiple_of(SSM_GROUP * slot, SSM_GROUP), SSM_GROUP), :] = win.astype(t_ref.dtype)


def _ssm_prep(a_re, a_im, log_dt, b_re, b_im, c_re, c_im):
    g2 = SSM_GROUPS // 2
    a_spec = pl.BlockSpec((None, 2, 2, 1, SSM_STATE), lambda l, g: (l, 0, g, 0, 0))
    b_spec = pl.BlockSpec((None, 2, 2, SSM_GROUP, SSM_STATE), lambda l, g: (l, 0, g, 0, 0))
    c_spec = pl.BlockSpec((None, 2, SSM_GROUP, SSM_STATE), lambda l, g: (l, g, 0, 0))
    out_shapes = (
        jax.ShapeDtypeStruct((DEPTH, SSM_GROUPS, CW, CW), BF16),
        jax.ShapeDtypeStruct((DEPTH, g2, 4, 2 * CW, 2 * SSM_STATE), BF16),
        jax.ShapeDtypeStruct((DEPTH, g2, 4, 2 * CW, 2 * SSM_STATE), BF16),
        jax.ShapeDtypeStruct((DEPTH, g2, 4, 2 * SSM_STATE), F32),
    )
    return pl.pallas_call(
        _ssm_prep_kernel,
        grid=(DEPTH, g2),
        in_specs=[a_spec, a_spec,
                  pl.BlockSpec((None, 2, 2, 1, 1), lambda l, g: (l, 0, g, 0, 0)),
                  b_spec, b_spec, c_spec, c_spec],
        out_specs=[
            pl.BlockSpec((None, 2, CW, CW), lambda l, g: (l, g, 0, 0)),
            pl.BlockSpec((None, None, 4, 2 * CW, 2 * SSM_STATE), lambda l, g: (l, g, 0, 0, 0)),
            pl.BlockSpec((None, None, 4, 2 * CW, 2 * SSM_STATE), lambda l, g: (l, g, 0, 0, 0)),
            pl.BlockSpec((None, None, 4, 2 * SSM_STATE), lambda l, g: (l, g, 0, 0)),
        ],
        out_shape=out_shapes,
        compiler_params=_params(("arbitrary", "arbitrary")),
        name="ssm_prep",
    )(a_re.reshape(DEPTH, 2, SSM_GROUPS, 1, SSM_STATE), a_im.reshape(DEPTH, 2, SSM_GROUPS, 1, SSM_STATE),
      log_dt.reshape(DEPTH, 2, SSM_GROUPS, 1, 1),
      jnp.swapaxes(b_re, -1, -2), jnp.swapaxes(b_im, -1, -2), c_re, c_im)


def _ssm_kernel(nchunk, nb, u_ref, t_ref, et_ref, ft_ref, a16_ref, h0_ref, y_ref, fin_ref, e_ref, h_ref):
    u0 = u_ref[0]
    u1 = u_ref[1]
    e_all = jnp.dot(jnp.concatenate([u0, u1], axis=1), jnp.concatenate([et_ref[x] for x in range(4)], axis=1),
                    preferred_element_type=F32)
    for x in range(4):
        e_ref[x] = e_all[:, LANES * x:LANES * (x + 1)]
    afr, afi, abr, abi = (a16_ref[i:i + 1, :] for i in range(4))

    def body(i, carry):
        fr, fi, br, bi = carry
        kf = pl.ds(i, nb, stride=nchunk)
        kb = pl.ds(nchunk - 1 - i, nb, stride=nchunk)
        h_ref[0, kf, :] = fr
        h_ref[1, kf, :] = fi
        h_ref[2, kb, :] = br
        h_ref[3, kb, :] = bi
        return (afr * fr - afi * fi + e_ref[0, kf, :], afr * fi + afi * fr + e_ref[1, kf, :],
                abr * br - abi * bi + e_ref[2, kb, :], abr * bi + abi * br + e_ref[3, kb, :])

    fin = lax.fori_loop(0, nchunk, body, tuple(h0_ref[i] for i in range(4)), unroll=4)
    for i in range(4):
        fin_ref[i] = fin[i]
    nt = (((1,), (1,)), ((), ()))
    acc = lax.dot_general(jnp.concatenate([h_ref[x].astype(BF16) for x in range(4)], axis=1),
                          jnp.concatenate([ft_ref[x] for x in range(4)], axis=1), nt,
                          preferred_element_type=F32)
    y_ref[0] = jnp.dot(u0, t_ref[0], preferred_element_type=F32) + acc[:, :CW]
    y_ref[1] = jnp.dot(u1, t_ref[1], preferred_element_type=F32) + acc[:, CW:]


def _ssm_mix(ug, nb, seq, h0_re, h0_im, prep, l):
    tmat, et, ft, a16 = prep
    nchunk = seq // CHUNK
    r = ug.shape[1]
    g2 = SSM_GROUPS // 2
    width = SSM_GROUPS * SSM_STATE
    h0 = jnp.stack([h[:, d].reshape(nb, width).astype(F32) for d in range(2) for h in (h0_re, h0_im)], axis=0)
    pair = pl.BlockSpec((2, r, CW), lambda g: (g, 0, 0))
    mats = pl.BlockSpec((None, None, 4, 2 * CW, 2 * SSM_STATE), lambda g: (l, g, 0, 0, 0))
    state = pl.BlockSpec((4, nb, 2 * SSM_STATE), lambda g: (0, 0, g))
    y, fin = pl.pallas_call(
        functools.partial(_ssm_kernel, nchunk, nb),
        grid=(g2,),
        in_specs=[pair, pl.BlockSpec((None, 2, CW, CW), lambda g: (l, g, 0, 0)), mats, mats,
                  pl.BlockSpec((None, None, 4, 2 * SSM_STATE), lambda g: (l, g, 0, 0)), state],
        out_specs=[pair, state],
        out_shape=(jax.ShapeDtypeStruct((SSM_GROUPS, r, CW), F32), jax.ShapeDtypeStruct((4, nb, width), F32)),
        scratch_shapes=[pltpu.VMEM((4, r, 2 * SSM_STATE), F32), pltpu.VMEM((4, r, 2 * SSM_STATE), F32)],
        compiler_params=_params(("parallel",)),
        name="ssm",
    )(ug, tmat, et, ft, a16, h0)
    fin = fin.reshape(2, 2, nb, SSM_GROUPS, SSM_STATE)
    return y, fin


def _gelu_tanh(x):
    return 0.5 * x * (1.0 + jnp.tanh(math.sqrt(2.0 / math.pi) * (x + 0.044715 * (x * x * x))))


def _route(aff, bias):
    a = [aff[e:e + 1, :] for e in range(N_EXPERTS)]
    s = [a[e] + bias[e:e + 1, :] for e in range(N_EXPERTS)]
    keep = []
    for e in range(N_EXPERTS):
        g0 = (e // EXPERTS_PER_GROUP) * EXPERTS_PER_GROUP
        rank = None
        for j in range(g0, g0 + EXPERTS_PER_GROUP):
            if j == e:
                continue
            beats = (s[j] >= s[e]) if j < e else (s[j] > s[e])
            r = jnp.where(beats, 1.0, 0.0)
            rank = r if rank is None else rank + r
        keep.append(rank < 1.5)
    score = []
    for g in range(N_GROUPS):
        tot = None
        for e in range(g * EXPERTS_PER_GROUP, (g + 1) * EXPERTS_PER_GROUP):
            v = jnp.where(keep[e], s[e], 0.0)
            tot = v if tot is None else tot + v
        score.append(tot)
    gates = []
    picked = []
    for g in range(N_GROUPS):
        lost = None
        for j in range(N_GROUPS):
            if j == g:
                continue
            beats = (score[j] >= score[g]) if j < g else (score[j] > score[g])
            r = jnp.where(beats, 1.0, 0.0)
            lost = r if lost is None else lost + r
        chosen = lost < 0.5
        picked.append(jnp.where(chosen, 1.0, 0.0))
        w = [jnp.where(keep[e], a[e], 0.0) for e in range(g * EXPERTS_PER_GROUP, (g + 1) * EXPERTS_PER_GROUP)]
        tot = w[0] + w[1] + w[2] + w[3]
        for v in w:
            gates.append(jnp.where(chosen, v / tot, 0.0))
    return gates, picked


def _merge_kernel(x_ref, yg_ref, u_ref, at_ref, sga_ref, sgb_ref, mod_ref, d_ref, wglu_ref, bglu_ref, wsp_ref,
                  wap_ref, wo_ref, gffn_ref, wr_ref, rb_ref, tri_ref, ramp_ref,
                  xo_ref, hs_ref, mt_ref, ms_ref, of_ref, tok_ref):
    mod = mod_ref[...]
    gate1 = mod[:, 2 * D_MODEL:3 * D_MODEL]
    shift2 = mod[:, 3 * D_MODEL:4 * D_MODEL]
    scale2 = mod[:, 4 * D_MODEL:5 * D_MODEL]
    _from_chunk_rows(yg_ref, tok_ref)
    ys = jnp.concatenate([tok_ref[j] for j in range(LANE_TILES)], axis=1)
    y = _gelu_tanh(ys + d_ref[...] * u_ref[...].astype(F32))
    glu = jnp.dot(y.astype(BF16), wglu_ref[...], preferred_element_type=F32) + bglu_ref[...]
    y = (y * _sigmoid(glu)).astype(BF16)
    a_out = jnp.dot(y, wsp_ref[...], preferred_element_type=F32)
    b_out = jnp.dot(at_ref[...], wap_ref[...], preferred_element_type=F32)
    m = (sga_ref[...].astype(F32) * a_out + sgb_ref[...].astype(F32) * b_out).astype(BF16)
    x = x_ref[...] + gate1 * jnp.dot(m, wo_ref[...], preferred_element_type=F32)
    xo_ref[...] = x
    h2 = _rms(x, D_MODEL) * (gffn_ref[...] * (1.0 + scale2)) + shift2
    nt = (((1,), (1,)), ((), ()))
    h2_hi = h2.astype(BF16)
    h2_lo = (h2 - h2_hi.astype(F32)).astype(BF16)
    wr = wr_ref[...]
    wr_hi = wr.astype(BF16)
    wr_lo = (wr - wr_hi.astype(F32)).astype(BF16)
    logits = (lax.dot_general(wr_hi, h2_hi, nt, preferred_element_type=F32)
              + lax.dot_general(wr_hi, h2_lo, nt, preferred_element_type=F32)
              + lax.dot_general(wr_lo, h2_hi, nt, preferred_element_type=F32))
    gates, picked = _route(_sigmoid(logits), rb_ref[...])
    n = x.shape[0]
    onehot = jnp.concatenate(picked + [jnp.zeros((8 - N_GROUPS, n), F32)], axis=0)
    before = jnp.dot(onehot.astype(BF16), tri_ref[...], preferred_element_type=F32)
    count = [jnp.sum(p, axis=-1, keepdims=True) for p in picked]
    start = [jnp.zeros((1, 1), F32)]
    for g in range(N_GROUPS - 1):
        start.append(start[-1] + count[g])
    pos = sum(picked[g] * (start[g] + before[g:g + 1, :]) for g in range(N_GROUPS))
    gid = sum(float(g) * picked[g] for g in range(1, N_GROUPS))
    perm = jnp.where(ramp_ref[...] == pos, 1.0, 0.0).astype(BF16)
    hs_ref[...] = jnp.dot(perm, h2_hi, preferred_element_type=F32).astype(BF16)
    own = [sum(gates[EXPERTS_PER_GROUP * g + e] for g in range(N_GROUPS)) for e in range(EXPERTS_PER_GROUP)]
    meta = jnp.concatenate(own + [gid, pos, jnp.zeros((LANES - EXPERTS_PER_GROUP - 2, n), F32)], axis=0).T
    mt_ref[...] = meta
    hi = meta.astype(BF16)
    lo = (meta - hi.astype(F32)).astype(BF16)
    ms_ref[...] = (jnp.dot(perm, hi, preferred_element_type=F32)
                   + jnp.dot(perm, lo, preferred_element_type=F32))
    lane = lax.broadcasted_iota(jnp.int32, (8, LANES), 1)
    offs = sum(jnp.where(lane == g, start[g], 0.0) for g in range(1, N_GROUPS)) + jnp.where(lane == N_GROUPS, n, 0.0)
    of_ref[...] = offs.astype(jnp.int32)


def _merge(x, yg, u, attn, sga, sgb, mods, l, lat, wts):
    n = x.shape[0]
    tt = TOK_TILE
    per_seq = 2048 // tt
    mod_map = (lambda i: (l, 1 + i // per_seq, 0, 0)) if lat else (lambda i: (l, 0, 0, 0))
    row = lambda i: (i, 0)
    lw = lambda *shape: pl.BlockSpec((None,) + shape, lambda i: (l,) + (0,) * len(shape))
    return pl.pallas_call(
        _merge_kernel,
        grid=(n // tt,),
        in_specs=[
            pl.BlockSpec((tt, D_MODEL), row),
            pl.BlockSpec((SSM_GROUPS, TILE_CHUNKS, CW), lambda i: (0, i, 0)),
            pl.BlockSpec((tt, SSM_WIDTH), row),
            pl.BlockSpec((tt, ATTN_WIDTH), row), pl.BlockSpec((tt, D_MODEL), row), pl.BlockSpec((tt, D_MODEL), row),
            pl.BlockSpec((None, None, 1, 6 * D_MODEL), mod_map),
            lw(1, SSM_WIDTH), lw(SSM_WIDTH, SSM_WIDTH), lw(1, SSM_WIDTH), lw(SSM_WIDTH, D_MODEL),
            lw(ATTN_WIDTH, D_MODEL), lw(D_MODEL, D_MODEL), lw(1, D_MODEL),
            pl.BlockSpec((N_EXPERTS, D_MODEL), lambda i: (0, 0)),
            pl.BlockSpec((N_EXPERTS, 1), lambda i: (0, 0)),
            pl.BlockSpec((tt, tt), lambda i: (0, 0)), pl.BlockSpec((tt, tt), lambda i: (0, 0)),
        ],
        out_specs=[pl.BlockSpec((tt, D_MODEL), row), pl.BlockSpec((tt, D_MODEL), row),
                   pl.BlockSpec((tt, LANES), row), pl.BlockSpec((tt, LANES), row),
                   pl.BlockSpec((None, 8, LANES), lambda i: (i, 0, 0))],
        out_shape=(jax.ShapeDtypeStruct((n, D_MODEL), F32), jax.ShapeDtypeStruct((n, D_MODEL), BF16),
                   jax.ShapeDtypeStruct((n, LANES), F32), jax.ShapeDtypeStruct((n, LANES), F32),
                   jax.ShapeDtypeStruct((n // tt, 8, LANES), jnp.int32)),
        scratch_shapes=[pltpu.VMEM((LANE_TILES, tt, LANES), F32)],
        compiler_params=_params(("parallel",)),
        name="merge_lat" if lat else "merge_ctx",
    )(x, yg, u, attn, sga, sgb, mods, wts["ssm_d"], wts["w_glu"], wts["b_glu"], wts["w_ssm_proj"],
      wts["w_attn_proj"], wts["w_o"], wts["g_ffn"], wts["w_router_t"], wts["router_bias"],
      jnp.asarray(np.triu(np.ones((tt, tt), np.float32), 1), BF16),
      jnp.asarray(np.broadcast_to(np.arange(tt, dtype=np.float32)[:, None], (tt, tt))))


def _moe_kernel(offs_ref, x_ref, hs_ref, ms_ref, mt_ref, mod_ref, wg_ref, wu_ref, wd_ref, o_ref, acc_ref):
    tile = pl.program_id(0)
    acc_ref[...] = jnp.zeros(acc_ref.shape, F32)

    def run_passes(height, g, first, span):
        row_id = lax.broadcasted_iota(jnp.int32, (height, 1), 0)
        n_pass = sum((span > k * height).astype(jnp.int32) for k in range(pl.cdiv(TOK_TILE + 16, height)))

        def pass_body(j, c):
            lo = first + height * j
            s = pl.multiple_of(jnp.minimum(lo, TOK_TILE - height), 16)
            rows = pl.ds(s, height)
            meta = ms_ref[rows, :]
            group_id = lax.convert_element_type(g, F32)
            mine = (meta[:, EXPERTS_PER_GROUP:EXPERTS_PER_GROUP + 1] == group_id) & (row_id + s >= lo)
            own = jnp.where(mine, 1.0, 0.0)
            hs = hs_ref[rows, :]
            acts = []
            for e in range(EXPERTS_PER_GROUP):
                expert = EXPERTS_PER_GROUP * g + e
                hg = jnp.dot(hs, wg_ref[expert], preferred_element_type=F32)
                hu = jnp.dot(hs, wu_ref[expert], preferred_element_type=F32)
                acts.append(((hg * _sigmoid(hg)) * hu * (meta[:, e:e + 1] * own)).astype(BF16))
            acc_ref[rows, :] += jnp.dot(jnp.concatenate(acts, axis=1), wd_ref[g], preferred_element_type=F32)
            return c

        lax.fori_loop(0, n_pass, pass_body, 0)

    def group_body(g, carry):
        seg_start = offs_ref[(N_GROUPS + 1) * tile + g]
        seg_end = offs_ref[(N_GROUPS + 1) * tile + g + 1]
        first = (seg_start >> 4) << 4
        span = jnp.where(seg_end > seg_start, seg_end - first, 0)
        choice = sum((span > height).astype(jnp.int32) for height in MOE_PASS_ROWS[:-1])
        lax.switch(choice, [functools.partial(run_passes, height, g, first, span) for height in MOE_PASS_ROWS])
        return carry

    lax.fori_loop(0, N_GROUPS, group_body, 0)

    slot = lax.broadcasted_iota(jnp.int32, (TOK_TILE, TOK_TILE), 1).astype(F32)
    unsort = jnp.where(mt_ref[:, EXPERTS_PER_GROUP + 1:EXPERTS_PER_GROUP + 2] == slot, 1.0, 0.0).astype(BF16)
    y = jnp.dot(unsort, acc_ref[...].astype(BF16), preferred_element_type=F32)
    o_ref[...] = x_ref[...] + mod_ref[:, 5 * D_MODEL:6 * D_MODEL] * y


def _moe(x, hs, meta_tok, meta_sorted, offs, mods, l, lat, wts):
    n = x.shape[0]
    tt = TOK_TILE
    per_seq = 2048 // tt
    mod_map = (lambda i, o: (l, 1 + i // per_seq, 0, 0)) if lat else (lambda i, o: (l, 0, 0, 0))
    row = lambda i, o: (i, 0)
    width = EXPERTS_PER_GROUP * D_EXPERT
    resident = lambda *shape: pl.BlockSpec((None,) + shape, lambda i, o: (l,) + (0,) * len(shape),
                                           pipeline_mode=pl.Buffered(1))
    grid_spec = pltpu.PrefetchScalarGridSpec(
        num_scalar_prefetch=1,
        grid=(n // tt,),
        in_specs=[
            pl.BlockSpec((tt, D_MODEL), row), pl.BlockSpec((tt, D_MODEL), row), pl.BlockSpec((tt, LANES), row),
            pl.BlockSpec((tt, LANES), row),
            pl.BlockSpec((None, None, 1, 6 * D_MODEL), mod_map),
            resident(N_EXPERTS, D_MODEL, D_EXPERT), resident(N_EXPERTS, D_MODEL, D_EXPERT),
            resident(N_GROUPS, width, D_MODEL),
        ],
        out_specs=pl.BlockSpec((tt, D_MODEL), row),
        scratch_shapes=[pltpu.VMEM((tt, D_MODEL), F32)],
    )
    return pl.pallas_call(
        _moe_kernel,
        grid_spec=grid_spec,
        out_shape=jax.ShapeDtypeStruct((n, D_MODEL), F32),
        compiler_params=_params(("arbitrary",), MOE_VMEM_LIMIT),
        name="moe_lat" if lat else "moe_ctx",
    )(offs[:, 0, :N_GROUPS + 1].reshape(-1), x, hs, meta_sorted, meta_tok, mods, wts["w_gate"], wts["w_up"],
      wts["w_down"])


def _pad_heads(w, width):
    lead = w.shape[:-1]
    w = w.reshape(lead + (N_HEADS, width))
    w = jnp.pad(w, [(0, 0)] * len(lead) + [(0, 0), (0, HEAD_PAD - width)])
    return w.reshape(lead + (N_HEADS * HEAD_PAD,))


def _swap_partners(w):
    lead = w.shape[:-1]
    w = w.reshape(lead + (-1, HEAD_PAD))
    half = ROPE_AXIS // 2
    a, b = QK_NOPE, QK_NOPE + ROPE_AXIS
    parts = [w[..., :a], w[..., a + half:b], w[..., a:a + half], w[..., b + half:b + 2 * half], w[..., b:b + half],
             w[..., b + 2 * half:]]
    return jnp.concatenate(parts, axis=-1).reshape(lead + (-1,))


def _layout_weights(w_in, g_qa, w_qb, g_q, g_kva, w_kvb, g_k, g_mix, w_attn_proj, ssm_d, ssm_w_glu, ssm_b_glu,
                    ssm_w_proj, w_o, g_ffn, w_router, router_bias, w_gate, w_up, w_down):
    n_a = SSM_WIDTH + Q_LORA + KV_LORA + QK_ROPE
    w_a = jnp.pad(w_in[:, :, :n_a], ((0, 0), (0, 0), (0, 1024 - n_a)))
    kv = w_kvb.reshape(DEPTH, KV_LORA, N_HEADS, QK_NOPE + V_HEAD)
    k_cols = _pad_heads(kv[..., :QK_NOPE].reshape(DEPTH, KV_LORA, N_HEADS * QK_NOPE), QK_NOPE)
    v_cols = _pad_heads(kv[..., QK_NOPE:].reshape(DEPTH, KV_LORA, ATTN_WIDTH), V_HEAD)
    place = np.zeros((KV_LORA, 2 * QK_PAD), np.float32)
    for h in range(N_HEADS):
        for r in range(QK_ROPE):
            place[r, h * HEAD_PAD + QK_NOPE + r] = 1.0
    w_k = jnp.concatenate([jnp.concatenate([k_cols, v_cols], axis=-1),
                           jnp.broadcast_to(jnp.asarray(place), (DEPTH,) + place.shape)], axis=1)
    w_qb_pad = _pad_heads(w_qb, QK_HEAD).astype(BF16)
    w_k = w_k.astype(BF16)
    pad_gain = lambda g: jnp.pad(g, ((0, 0), (0, HEAD_PAD - QK_HEAD))).reshape(DEPTH, 1, HEAD_PAD)
    vec = lambda g: g.reshape(DEPTH, 1, -1)
    return {
        "g_mix": vec(g_mix), "w_a": w_a.astype(BF16),
        "w_ga": w_in[:, :, n_a:n_a + D_MODEL].astype(BF16), "w_gb": w_in[:, :, n_a + D_MODEL:].astype(BF16),
        "g_qa": vec(g_qa), "w_qb": w_qb_pad, "w_qb_sw": _swap_partners(w_qb_pad),
        "g_q": pad_gain(g_q) * Q_SCALE, "g_q_sw": _swap_partners(pad_gain(g_q)) * Q_SCALE,
        "g_kva": vec(g_kva), "w_k": w_k, "w_k_sw": _swap_partners(w_k[:, :, :QK_PAD]),
        "g_k": pad_gain(g_k), "g_k_sw": _swap_partners(pad_gain(g_k)),
        "ssm_d": vec(ssm_d), "w_glu": ssm_w_glu.astype(BF16), "b_glu": vec(ssm_b_glu),
        "w_ssm_proj": ssm_w_proj.astype(BF16), "w_attn_proj": w_attn_proj.astype(BF16), "w_o": w_o.astype(BF16),
        "g_ffn": vec(g_ffn), "w_router_t": w_router.T, "router_bias": router_bias.reshape(N_EXPERTS, 1),
        "w_gate": w_gate.astype(BF16), "w_up": w_up.astype(BF16),
        "w_down": w_down.astype(BF16).reshape(DEPTH, N_GROUPS, EXPERTS_PER_GROUP * D_EXPERT, D_MODEL),
    }


def _rope_tables(n_tokens):
    pos = np.arange(n_tokens)
    inv = 1.0 / (ROPE_BASE ** (np.arange(ROPE_AXIS // 2, dtype=np.float32) * 2.0 / ROPE_AXIS))
    ang = np.zeros((n_tokens, HEAD_PAD), np.float32)
    half = ROPE_AXIS // 2
    row = (pos // GRID_W).astype(np.float32)[:, None] * inv
    col = (pos % GRID_W).astype(np.float32)[:, None] * inv
    first = np.zeros((HEAD_PAD,), bool)
    second = np.zeros((HEAD_PAD,), bool)
    for base, a in ((QK_NOPE, row), (QK_NOPE + ROPE_AXIS, col)):
        ang[:, base:base + half] = a
        ang[:, base + half:base + 2 * half] = a
        first[base:base + half] = True
        second[base + half:base + 2 * half] = True
    ang = jnp.asarray(ang)
    cos = jnp.where(jnp.asarray(first | second), jnp.cos(ang), 1.0)
    sin = jnp.sin(ang)
    return cos, jnp.where(jnp.asarray(first), -sin, jnp.where(jnp.asarray(second), sin, 0.0))


def kernel(x_prompt, x_sample, c, cache_ckv, cache_krope, state_ssm_re, state_ssm_im, c_ctx, w_ada, b_ada, g_mix, w_in, g_qa, w_qb, g_q, g_kva, w_kvb, g_k, w_attn_proj, ssm_a_re, ssm_a_im, ssm_log_dt, ssm_b_re, ssm_b_im, ssm_c_re, ssm_c_im, ssm_d, ssm_w_glu, ssm_b_glu, ssm_w_proj, w_o, g_ffn, w_router, router_bias, w_gate, w_up, w_down):
    nb_c, seq_c, _ = x_prompt.shape
    nb_l, seq_l, _ = x_sample.shape
    wts = _layout_weights(w_in, g_qa, w_qb, g_q, g_kva, w_kvb, g_k, g_mix, w_attn_proj, ssm_d, ssm_w_glu,
                          ssm_b_glu, ssm_w_proj, w_o, g_ffn, w_router, router_bias, w_gate, w_up, w_down)
    cvec = jnp.concatenate([c_ctx[None, :], c, jnp.zeros((8 - 1 - nb_l, D_MODEL), F32)], axis=0)
    mods = _mods(cvec, w_ada, b_ada).reshape(DEPTH, 8, 1, 6 * D_MODEL)
    prep = _ssm_prep(ssm_a_re, ssm_a_im, ssm_log_dt, ssm_b_re, ssm_b_im, ssm_c_re, ssm_c_im)
    tabs = _rope_tables(seq_l)
    kc, vc = _cachekv(cache_ckv, cache_krope, wts)

    xp = x_prompt.reshape(nb_c * seq_c, D_MODEL)
    xs = x_sample.reshape(nb_l * seq_l, D_MODEL)
    zeros_c = jnp.zeros((nb_c, 2, SSM_GROUPS, SSM_STATE), F32)
    ckvs, kropes, fins = [], [], []
    for l in range(DEPTH):
        u, ug, q, k, v, sga, sgb, ckv, krope = _inproj(xp, mods, l, False, wts, tabs)
        ckvs.append(ckv)
        kropes.append(krope)
        yg, fin = _ssm_mix(ug, nb_c, seq_c, zeros_c, zeros_c, prep, l)
        fins.append(fin)
        attn = _attn_ctx(q, k, v, seq_c)
        xp, *routed = _merge(xp, yg, u, attn, sga, sgb, mods, l, False, wts)
        xp = _moe(xp, *routed, mods, l, False, wts)

        u, ug, q, k, v, sga, sgb, _, _ = _inproj(xs, mods, l, True, wts, tabs)
        yg, _ = _ssm_mix(ug, nb_l, seq_l, state_ssm_re[:, l], state_ssm_im[:, l], prep, l)
        attn = _attn_lat(q, k, v, kc, vc, l, seq_l, TOK_TILE)
        xs, *routed = _merge(xs, yg, u, attn, sga, sgb, mods, l, True, wts)
        xs = _moe(xs, *routed, mods, l, True, wts)

    new_ckv = jnp.stack(ckvs, axis=0).reshape(DEPTH, nb_c, seq_c, KV_LORA).transpose(1, 0, 2, 3)
    new_krope = jnp.stack(kropes, axis=0).reshape(DEPTH, nb_c, seq_c, QK_ROPE).transpose(1, 0, 2, 3)
    fin = jnp.stack(fins, axis=0)
    new_re = jnp.transpose(fin[:, :, 0], (2, 0, 1, 3, 4))
    new_im = jnp.transpose(fin[:, :, 1], (2, 0, 1, 3, 4))
    return (xp.reshape(nb_c, seq_c, D_MODEL), xs.reshape(nb_l, seq_l, D_MODEL), new_ckv, new_krope, new_re, new_im)
```

```python
import functools
import math

import jax
import jax.numpy as jnp
import numpy as np
from jax import lax
from jax.experimental import pallas as pl
from jax.experimental.pallas import tpu as pltpu

F32 = jnp.float32
BF16 = jnp.bfloat16

D_MODEL = 1024
DEPTH = 4
GRID_W = 64
SSM_WIDTH = 512
SSM_GROUP = 16
SSM_GROUPS = 32
SSM_STATE = 64
N_HEADS = 8
QK_NOPE = 64
QK_ROPE = 32
QK_HEAD = 96
V_HEAD = 64
Q_LORA = 256
KV_LORA = 128
ATTN_WIDTH = 512
ROPE_AXIS = 16
ROPE_BASE = 10000.0
N_EXPERTS = 16
N_GROUPS = 4
EXPERTS_PER_GROUP = 4
D_EXPERT = 256
EPS = 1e-6

LANES = 128
HEAD_PAD = LANES
QK_PAD = N_HEADS * HEAD_PAD
CHUNK = 16
CW = CHUNK * SSM_GROUP
TOK_TILE = 512
TILE_CHUNKS = TOK_TILE // CHUNK
LANE_TILES = SSM_WIDTH // LANES
GROUPS_PER_TILE = LANES // SSM_GROUP
Q_SCALE = math.log2(math.e) / math.sqrt(QK_HEAD)
MOE_PASS_ROWS = (128, 160, 192, 224, 256)
VMEM_LIMIT = 48 * 1024 * 1024
MOE_VMEM_LIMIT = 56 * 1024 * 1024
ATTN_VMEM_LIMIT = 60 * 1024 * 1024


def _params(sem, vmem=VMEM_LIMIT):
    return pltpu.CompilerParams(dimension_semantics=sem, vmem_limit_bytes=vmem)


def _sigmoid(x):
    return 0.5 * jnp.tanh(0.5 * x) + 0.5


def _rms(x, n):
    return x * lax.rsqrt(jnp.sum(x * x, axis=-1, keepdims=True) * (1.0 / n) + EPS)


def _mods_kernel(c_ref, w_ref, b_ref, o_ref):
    c = c_ref[...]
    cs = (c * _sigmoid(c)).astype(BF16)
    o_ref[...] = jnp.dot(cs, w_ref[...].astype(BF16), preferred_element_type=F32) + b_ref[...]


def _mods(cvec, w_ada, b_ada):
    nb = 1536
    return pl.pallas_call(
        _mods_kernel,
        grid=(DEPTH, 6 * D_MODEL // nb),
        in_specs=[
            pl.BlockSpec((8, D_MODEL), lambda l, j: (0, 0)),
            pl.BlockSpec((None, D_MODEL, nb), lambda l, j: (l, 0, j)),
            pl.BlockSpec((None, 1, nb), lambda l, j: (l, 0, j)),
        ],
        out_specs=pl.BlockSpec((None, 8, nb), lambda l, j: (l, 0, j)),
        out_shape=jax.ShapeDtypeStruct((DEPTH, 8, 6 * D_MODEL), F32),
        compiler_params=_params(("arbitrary", "arbitrary")),
        name="mods",
    )(cvec, w_ada, b_ada.reshape(DEPTH, 1, 6 * D_MODEL))


def _slot_step(slot, g):
    return (slot & 8) + ((slot - g) & 7)


def _lane_segment():
    return lax.broadcasted_iota(jnp.int32, (TILE_CHUNKS, LANES), 1) // SSM_GROUP


def _to_chunk_rows(tok_ref, ug_ref):
    seg = _lane_segment()
    for j in range(LANE_TILES):
        for half in range(CHUNK // 8):
            rolled = []
            for r in range(8):
                v = tok_ref[j, pl.ds(8 * half + r, TILE_CHUNKS, stride=CHUNK), :]
                rolled.append(pltpu.roll(v, SSM_GROUP * r, 1) if r else v)
            for gg in range(GROUPS_PER_TILE):
                acc = rolled[(0 - gg) % 8]
                for m in range(1, 8):
                    acc = jnp.where(seg == m, rolled[(m - gg) % 8], acc)
                ug_ref[GROUPS_PER_TILE * j + gg, :, LANES * half:LANES * (half + 1)] = acc.astype(ug_ref.dtype)


def _from_chunk_rows(yg_ref, tok_ref):
    seg = _lane_segment()
    for j in range(LANE_TILES):
        for half in range(CHUNK // 8):
            ys = [yg_ref[GROUPS_PER_TILE * j + gg, :, LANES * half:LANES * (half + 1)]
                  for gg in range(GROUPS_PER_TILE)]
            for r in range(8):
                acc = ys[(0 - r) % 8]
                for m in range(1, 8):
                    acc = jnp.where(seg == m, ys[(m - r) % 8], acc)
                out = pltpu.roll(acc, LANES - SSM_GROUP * r, 1) if r else acc
                tok_ref[j, pl.ds(8 * half + r, TILE_CHUNKS, stride=CHUNK), :] = out


def _head_norm_rope(t, gain, swapped):
    outs = []
    for h in range(N_HEADS):
        lanes = slice(h * HEAD_PAD, (h + 1) * HEAD_PAD)
        th = t[:, lanes]
        scale = lax.rsqrt(jnp.sum(th * th, axis=-1, keepdims=True) * (1.0 / QK_HEAD) + EPS)
        y = th * gain
        if swapped is not None:
            y = y + swapped[0][:, lanes] * swapped[1]
        outs.append(y * scale)
    return jnp.concatenate(outs, axis=1)


def _with_ones(v):
    lane = lax.broadcasted_iota(jnp.int32, (1, v.shape[1]), 1)
    return jnp.where((lane & (HEAD_PAD - 1)) == V_HEAD, 1.0, v)


def _kv_dot(ckvn, krp, wk_ref):
    return (jnp.dot(ckvn, wk_ref[0:KV_LORA, :], preferred_element_type=F32)
            + jnp.dot(krp, wk_ref[KV_LORA:KV_LORA + krp.shape[1], :], preferred_element_type=F32))


def _inproj_kernel(use_rope, x_ref, mod_ref, gmix_ref, wa_ref, wga_ref, wgb_ref, gqa_ref, wqb_ref, wqbs_ref, gq_ref,
                   gqs_ref, gkva_ref, wk_ref, wks_ref, gk_ref, gks_ref, cos_ref, sin_ref,
                   ug_ref, q_ref, k_ref, v_ref, sga_ref, sgb_ref, ckv_ref, kr_ref, tok_ref):
    mod = mod_ref[...]
    shift1 = mod[:, 0:D_MODEL]
    gain1 = gmix_ref[...] * (1.0 + mod[:, D_MODEL:2 * D_MODEL])
    h = (_rms(x_ref[...], D_MODEL) * gain1 + shift1).astype(BF16)
    za = jnp.dot(h, wa_ref[...], preferred_element_type=F32)
    sga_ref[...] = _sigmoid(jnp.dot(h, wga_ref[...], preferred_element_type=F32)).astype(BF16)
    sgb_ref[...] = _sigmoid(jnp.dot(h, wgb_ref[...], preferred_element_type=F32)).astype(BF16)
    for j in range(LANE_TILES):
        tok_ref[j] = za[:, LANES * j:LANES * (j + 1)]
    _to_chunk_rows(tok_ref, ug_ref)
    qa = za[:, SSM_WIDTH:SSM_WIDTH + Q_LORA]
    ckv = za[:, SSM_WIDTH + Q_LORA:SSM_WIDTH + Q_LORA + KV_LORA]
    krp = za[:, SSM_WIDTH + Q_LORA + KV_LORA:]
    qn = (_rms(qa, Q_LORA) * gqa_ref[...]).astype(BF16)
    qr = jnp.dot(qn, wqb_ref[...], preferred_element_type=F32)
    ckvn = _rms(ckv, KV_LORA) * gkva_ref[...]
    ckv_ref[...] = ckvn
    kr_ref[...] = krp[:, :QK_ROPE]
    ckvn = ckvn.astype(BF16)
    krp = krp.astype(BF16)
    kv = _kv_dot(ckvn, krp, wk_ref)
    gq = gq_ref[...]
    gk = gk_ref[...]
    q_sw = k_sw = None
    if use_rope:
        cos = cos_ref[...]
        sin = sin_ref[...]
        q_sw = (jnp.dot(qn, wqbs_ref[...], preferred_element_type=F32), gqs_ref[...] * sin)
        k_sw = (_kv_dot(ckvn, krp, wks_ref), gks_ref[...] * sin)
        gq = gq * cos
        gk = gk * cos
    q_ref[...] = _head_norm_rope(qr, gq, q_sw).astype(BF16)
    k_ref[...] = _head_norm_rope(kv[:, :QK_PAD], gk, k_sw).astype(BF16)
    v_ref[...] = _with_ones(kv[:, QK_PAD:]).astype(BF16)


def _inproj(x, mods, l, lat, wts, tabs):
    n = x.shape[0]
    tt = TOK_TILE
    per_seq = 2048 // tt
    if lat:
        mod_map = lambda i: (l, 1 + i // per_seq, 0, 0)
        tab_map = lambda i: (i % per_seq, 0)
    else:
        mod_map = lambda i: (l, 0, 0, 0)
        tab_map = lambda i: (0, 0)
    row = lambda i: (i, 0)
    lw = lambda *shape: pl.BlockSpec((None,) + shape, lambda i: (l,) + (0,) * len(shape))
    tab = pl.BlockSpec((tt, HEAD_PAD), tab_map)
    row_shapes = (
        jax.ShapeDtypeStruct((n, QK_PAD), BF16),
        jax.ShapeDtypeStruct((n, QK_PAD), BF16),
        jax.ShapeDtypeStruct((n, QK_PAD), BF16),
        jax.ShapeDtypeStruct((n, D_MODEL), BF16),
        jax.ShapeDtypeStruct((n, D_MODEL), BF16),
        jax.ShapeDtypeStruct((n, KV_LORA), F32),
        jax.ShapeDtypeStruct((n, QK_ROPE), F32),
    )
    row_specs = [pl.BlockSpec((tt, s.shape[1]), row) for s in row_shapes]
    ug_shape = jax.ShapeDtypeStruct((SSM_GROUPS, n // CHUNK, CW), BF16)
    ug_spec = pl.BlockSpec((SSM_GROUPS, TILE_CHUNKS, CW), lambda i: (0, i, 0))
    outs = pl.pallas_call(
        functools.partial(_inproj_kernel, lat),
        grid=(n // tt,),
        in_specs=[
            pl.BlockSpec((tt, D_MODEL), row),
            pl.BlockSpec((None, None, 1, 6 * D_MODEL), mod_map),
            lw(1, D_MODEL), lw(D_MODEL, 1024), lw(D_MODEL, D_MODEL), lw(D_MODEL, D_MODEL),
            lw(1, Q_LORA), lw(Q_LORA, QK_PAD), lw(Q_LORA, QK_PAD), lw(1, HEAD_PAD), lw(1, HEAD_PAD), lw(1, KV_LORA),
            lw(2 * KV_LORA, 2 * QK_PAD), lw(2 * KV_LORA, QK_PAD), lw(1, HEAD_PAD), lw(1, HEAD_PAD),
            tab, tab,
        ],
        out_specs=[ug_spec] + row_specs,
        out_shape=(ug_shape,) + row_shapes,
        scratch_shapes=[pltpu.VMEM((LANE_TILES, tt, LANES), F32)],
        compiler_params=_params(("parallel",)),
        name="inproj_lat" if lat else "inproj_ctx",
    )(x, mods, wts["g_mix"], wts["w_a"], wts["w_ga"], wts["w_gb"], wts["g_qa"], wts["w_qb"], wts["w_qb_sw"],
      wts["g_q"], wts["g_q_sw"], wts["g_kva"], wts["w_k"], wts["w_k_sw"], wts["g_k"], wts["g_k_sw"], *tabs)
    return outs


def _cachekv_kernel(ckv_ref, kr_ref, wk_ref, gk_ref, k_ref, v_ref):
    kv = _kv_dot(ckv_ref[...].astype(BF16), kr_ref[...].astype(BF16), wk_ref)
    k_ref[...] = _head_norm_rope(kv[:, :QK_PAD], gk_ref[...], None).astype(BF16)
    v_ref[...] = _with_ones(kv[:, QK_PAD:]).astype(BF16)


def _cachekv(cache_ckv, cache_krope, wts):
    nb, _, past, _ = cache_ckv.shape
    return pl.pallas_call(
        _cachekv_kernel,
        grid=(DEPTH, nb),
        in_specs=[
            pl.BlockSpec((None, None, past, KV_LORA), lambda l, b: (b, l, 0, 0)),
            pl.BlockSpec((None, None, past, QK_ROPE), lambda l, b: (b, l, 0, 0)),
            pl.BlockSpec((None, 2 * KV_LORA, 2 * QK_PAD), lambda l, b: (l, 0, 0)),
            pl.BlockSpec((None, 1, HEAD_PAD), lambda l, b: (l, 0, 0)),
        ],
        out_specs=[
            pl.BlockSpec((None, None, past, QK_PAD), lambda l, b: (l, b, 0, 0)),
            pl.BlockSpec((None, None, past, QK_PAD), lambda l, b: (l, b, 0, 0)),
        ],
        out_shape=(jax.ShapeDtypeStruct((DEPTH, nb, past, QK_PAD), BF16),
                   jax.ShapeDtypeStruct((DEPTH, nb, past, QK_PAD), BF16)),
        compiler_params=_params(("arbitrary", "arbitrary")),
        name="cache_kv",
    )(cache_ckv, cache_krope, wts["w_k"], wts["g_k"])


def _attn_heads(q_ref, segs, o_ref, den_on_mxu):
    for h in range(N_HEADS):
        qh = q_ref[:, h * HEAD_PAD:(h + 1) * HEAD_PAD]
        scores = [lax.dot_general(qh, k_ref[:, h * HEAD_PAD:(h + 1) * HEAD_PAD], (((1,), (1,)), ((), ())),
                                  preferred_element_type=F32) for k_ref, _ in segs]
        m = scores[0].max(axis=-1, keepdims=True)
        for s in scores[1:]:
            m = jnp.maximum(m, s.max(axis=-1, keepdims=True))
        acc = None
        den = None
        for s, (_, v_ref) in zip(scores, segs):
            if den_on_mxu:
                p = jnp.exp2((s - m).astype(BF16))
                a = jnp.dot(p, v_ref[:, h * HEAD_PAD:(h + 1) * HEAD_PAD], preferred_element_type=F32)
            else:
                p = jnp.exp2(s - m)
                d = p.sum(axis=-1, keepdims=True)
                den = d if den is None else den + d
                a = jnp.dot(p.astype(BF16), v_ref[:, h * HEAD_PAD:h * HEAD_PAD + V_HEAD],
                            preferred_element_type=F32)
            acc = a if acc is None else acc + a
        if den_on_mxu:
            den = acc[:, V_HEAD:V_HEAD + 1]
        o_ref[:, h * V_HEAD:(h + 1) * V_HEAD] = (acc[:, :V_HEAD] / den).astype(o_ref.dtype)


def _attn_ctx_kernel(q_ref, k_ref, v_ref, o_ref):
    for b in range(q_ref.shape[0]):
        _attn_heads(q_ref.at[b], [(k_ref.at[b], v_ref.at[b])], o_ref.at[b], False)


def _attn_lat_kernel(q_ref, kc_ref, vc_ref, kl_ref, vl_ref, o_ref):
    _attn_heads(q_ref, [(kc_ref, vc_ref), (kl_ref, vl_ref)], o_ref, True)


def _attn_ctx(q, k, v, seq):
    n = q.shape[0]
    nb = n // seq
    per_step = 2
    blk = lambda w: pl.BlockSpec((per_step, seq, w), lambda b: (b, 0, 0))
    seqs = lambda a: a.reshape(nb, seq, a.shape[1])
    return pl.pallas_call(
        _attn_ctx_kernel,
        grid=(nb // per_step,),
        in_specs=[blk(QK_PAD), blk(QK_PAD), blk(QK_PAD)],
        out_specs=blk(ATTN_WIDTH),
        out_shape=jax.ShapeDtypeStruct((nb, seq, ATTN_WIDTH), BF16),
        compiler_params=_params(("parallel",)),
        name="attn_ctx",
    )(seqs(q), seqs(k), seqs(v)).reshape(n, ATTN_WIDTH)


def _attn_lat(q, k, v, kc, vc, l, seq, tq):
    n = q.shape[0]
    per = seq // tq
    past = kc.shape[2]
    return pl.pallas_call(
        _attn_lat_kernel,
        grid=(n // seq, per),
        in_specs=[
            pl.BlockSpec((tq, QK_PAD), lambda b, j: (b * per + j, 0)),
            pl.BlockSpec((None, None, past, QK_PAD), lambda b, j: (l, b, 0, 0)),
            pl.BlockSpec((None, None, past, QK_PAD), lambda b, j: (l, b, 0, 0)),
            pl.BlockSpec((seq, QK_PAD), lambda b, j: (b, 0)),
            pl.BlockSpec((seq, QK_PAD), lambda b, j: (b, 0)),
        ],
        out_specs=pl.BlockSpec((tq, ATTN_WIDTH), lambda b, j: (b * per + j, 0)),
        out_shape=jax.ShapeDtypeStruct((n, ATTN_WIDTH), BF16),
        compiler_params=_params(("parallel", "arbitrary"), ATTN_VMEM_LIMIT),
        name="attn_lat",
    )(q, kc, vc, k, v)


def _cmul(ar, ai, br, bi):
    return ar * br - ai * bi, ar * bi + ai * br


def _outer_rows(pw, m):
    return (pw[:, None, :] * m[None, :, :]).reshape(CW, SSM_STATE)


def _ssm_prep_kernel(are_ref, aim_ref, ldt_ref, bre_ref, bim_ref, cre_ref, cim_ref, d_ref,
                     t_ref, et_ref, ft_ref, a16_ref):
    et_ref[...] = jnp.zeros(et_ref.shape, et_ref.dtype)
    ft_ref[...] = jnp.zeros(ft_ref.shape, ft_ref.dtype)
    row_id = lax.broadcasted_iota(jnp.int32, (CHUNK, SSM_STATE), 0)
    strip_lane = lax.broadcasted_iota(jnp.int32, (SSM_GROUP, CW), 1)
    nt = (((1,), (1,)), ((), ()))
    for gi in range(2):
        g = 2 * pl.program_id(1) + gi
        steps = _slot_step(row_id, g)
        lanes = slice(gi * SSM_STATE, (gi + 1) * SSM_STATE)
        rows = slice(gi * CW, (gi + 1) * CW)
        cre = cre_ref[gi]
        cim = cim_ref[gi]
        m_dir = []
        for d in range(2):
            a_re = are_ref[d, gi]
            a_im = aim_ref[d, gi]
            dt = jnp.exp(ldt_ref[d, gi])
            mag = jnp.exp(a_re * dt)
            abr = mag * jnp.cos(a_im * dt)
            abi = mag * jnp.sin(a_im * dt)
            den = a_re * a_re + a_im * a_im
            nr = abr - 1.0
            coef_re = (nr * a_re + abi * a_im) / den
            coef_im = (abi * a_re - nr * a_im) / den
            bbr, bbi = _cmul(coef_re, coef_im, bre_ref[d, gi], bim_ref[d, gi])

            squares = [(abr, abi)]
            for _ in range(4):
                squares.append(_cmul(*squares[-1], *squares[-1]))

            def power(n):
                pr = jnp.ones(n.shape, F32)
                pi = jnp.zeros(n.shape, F32)
                for b, (sr, si) in enumerate(squares):
                    nr_, ni_ = _cmul(pr, pi, sr, si)
                    hit = (n & (1 << b)) != 0
                    pr = jnp.where(hit, nr_, pr)
                    pi = jnp.where(hit, ni_, pi)
                return pr, pi

            to_end = (CHUNK - 1 - steps) if d == 0 else steps
            from_start = (steps + 1) if d == 0 else (CHUNK - steps)
            lag = row_id if d == 0 else (CHUNK - 1 - row_id)
            pr, pi = power(to_end)
            er = _outer_rows(pr, bbr) - _outer_rows(pi, bbi)
            ei = _outer_rows(pr, bbi) + _outer_rows(pi, bbr)
            et_ref[2 * d, rows, lanes] = er.astype(et_ref.dtype)
            et_ref[2 * d + 1, rows, lanes] = ei.astype(et_ref.dtype)
            pr, pi = power(from_start)
            ft_ref[2 * d, rows, lanes] = (_outer_rows(pr, cre) - _outer_rows(pi, cim)).astype(ft_ref.dtype)
            ft_ref[2 * d + 1, rows, lanes] = (-(_outer_rows(pr, cim) + _outer_rows(pi, cre))).astype(ft_ref.dtype)
            pr, pi = power(lag)
            zr = _outer_rows(pr, cre) - _outer_rows(pi, cim)
            zi = _outer_rows(pi, cre) + _outer_rows(pr, cim)
            m_dir.append(
                lax.dot_general(bbr, zr, nt, preferred_element_type=F32, precision=lax.Precision.HIGHEST)
                - lax.dot_general(bbi, zi, nt, preferred_element_type=F32, precision=lax.Precision.HIGHEST))
            p16r, p16i = squares[4]
            a16_ref[2 * d:2 * d + 1, lanes] = p16r
            a16_ref[2 * d + 1:2 * d + 2, lanes] = p16i
        mf, mb = m_dir
        edge = CW - SSM_GROUP
        low = mb + pltpu.roll(jnp.where(strip_lane < SSM_GROUP, mf, 0.0), edge, 1)
        strip_row = lax.broadcasted_iota(jnp.int32, (SSM_GROUP, CW), 0)
        low = low + jnp.where(strip_lane - edge == strip_row, d_ref[gi], 0.0)
        high = jnp.where(strip_lane < edge, pltpu.roll(mf, edge, 1), 0.0)
        strip = jnp.concatenate([low, high], axis=1)
        turn = SSM_GROUP * (g & 7)
        for j in range(CHUNK):
            first = SSM_GROUP * (CHUNK - 1 - j)
            win = strip[:, first:first + CW]
            win = jnp.concatenate([pltpu.roll(win[:, LANES * h:LANES * (h + 1)], turn, 1)
                                   for h in range(CW // LANES)], axis=1)
            slot = (j & 8) + ((j + g) & 7)
            t_ref[gi, pl.ds(pl.multiple_of(SSM_GROUP * slot, SSM_GROUP), SSM_GROUP), :] = win.astype(t_ref.dtype)


def _ssm_prep(a_re, a_im, log_dt, b_re, b_im, c_re, c_im, d):
    g2 = SSM_GROUPS // 2
    a_spec = pl.BlockSpec((None, 2, 2, 1, SSM_STATE), lambda l, g: (l, 0, g, 0, 0))
    b_spec = pl.BlockSpec((None, 2, 2, SSM_GROUP, SSM_STATE), lambda l, g: (l, 0, g, 0, 0))
    c_spec = pl.BlockSpec((None, 2, SSM_GROUP, SSM_STATE), lambda l, g: (l, g, 0, 0))
    out_shapes = (
        jax.ShapeDtypeStruct((DEPTH, SSM_GROUPS, CW, CW), BF16),
        jax.ShapeDtypeStruct((DEPTH, g2, 4, 2 * CW, 2 * SSM_STATE), BF16),
        jax.ShapeDtypeStruct((DEPTH, g2, 4, 2 * CW, 2 * SSM_STATE), BF16),
        jax.ShapeDtypeStruct((DEPTH, g2, 4, 2 * SSM_STATE), F32),
    )
    return pl.pallas_call(
        _ssm_prep_kernel,
        grid=(DEPTH, g2),
        in_specs=[a_spec, a_spec,
                  pl.BlockSpec((None, 2, 2, 1, 1), lambda l, g: (l, 0, g, 0, 0)),
                  b_spec, b_spec, c_spec, c_spec,
                  pl.BlockSpec((None, 2, SSM_GROUP, 1), lambda l, g: (l, g, 0, 0))],
        out_specs=[
            pl.BlockSpec((None, 2, CW, CW), lambda l, g: (l, g, 0, 0)),
            pl.BlockSpec((None, None, 4, 2 * CW, 2 * SSM_STATE), lambda l, g: (l, g, 0, 0, 0)),
            pl.BlockSpec((None, None, 4, 2 * CW, 2 * SSM_STATE), lambda l, g: (l, g, 0, 0, 0)),
            pl.BlockSpec((None, None, 4, 2 * SSM_STATE), lambda l, g: (l, g, 0, 0)),
        ],
        out_shape=out_shapes,
        compiler_params=_params(("arbitrary", "arbitrary")),
        name="ssm_prep",
    )(a_re.reshape(DEPTH, 2, SSM_GROUPS, 1, SSM_STATE), a_im.reshape(DEPTH, 2, SSM_GROUPS, 1, SSM_STATE),
      log_dt.reshape(DEPTH, 2, SSM_GROUPS, 1, 1),
      jnp.swapaxes(b_re, -1, -2), jnp.swapaxes(b_im, -1, -2), c_re, c_im,
      d.reshape(DEPTH, SSM_GROUPS, SSM_GROUP, 1))


def _ssm_kernel(nchunk, nb, u_ref, t_ref, et_ref, ft_ref, a16_ref, h0_ref, y_ref, fin_ref, e_ref, h_ref):
    u0 = u_ref[0]
    u1 = u_ref[1]
    e_all = jnp.dot(jnp.concatenate([u0, u1], axis=1), jnp.concatenate([et_ref[x] for x in range(4)], axis=1),
                    preferred_element_type=F32)
    for x in range(4):
        e_ref[x] = e_all[:, LANES * x:LANES * (x + 1)]
    afr, afi, abr, abi = (a16_ref[i:i + 1, :] for i in range(4))

    def body(i, carry):
        fr, fi, br, bi = carry
        kf = pl.ds(i, nb, stride=nchunk)
        kb = pl.ds(nchunk - 1 - i, nb, stride=nchunk)
        h_ref[0, kf, :] = fr
        h_ref[1, kf, :] = fi
        h_ref[2, kb, :] = br
        h_ref[3, kb, :] = bi
        return (afr * fr - afi * fi + e_ref[0, kf, :], afr * fi + afi * fr + e_ref[1, kf, :],
                abr * br - abi * bi + e_ref[2, kb, :], abr * bi + abi * br + e_ref[3, kb, :])

    fin = lax.fori_loop(0, nchunk, body, tuple(h0_ref[i] for i in range(4)), unroll=4)
    for i in range(4):
        fin_ref[i] = fin[i]
    nt = (((1,), (1,)), ((), ()))
    acc = lax.dot_general(jnp.concatenate([h_ref[x].astype(BF16) for x in range(4)], axis=1),
                          jnp.concatenate([ft_ref[x] for x in range(4)], axis=1), nt,
                          preferred_element_type=F32)
    y_ref[0] = jnp.dot(u0, t_ref[0], preferred_element_type=F32) + acc[:, :CW]
    y_ref[1] = jnp.dot(u1, t_ref[1], preferred_element_type=F32) + acc[:, CW:]


def _ssm_mix(ug, nb, seq, h0_re, h0_im, prep, l):
    tmat, et, ft, a16 = prep
    nchunk = seq // CHUNK
    r = ug.shape[1]
    g2 = SSM_GROUPS // 2
    width = SSM_GROUPS * SSM_STATE
    h0 = jnp.stack([h[:, d].reshape(nb, width).astype(F32) for d in range(2) for h in (h0_re, h0_im)], axis=0)
    pair = pl.BlockSpec((2, r, CW), lambda g: (g, 0, 0))
    mats = pl.BlockSpec((None, None, 4, 2 * CW, 2 * SSM_STATE), lambda g: (l, g, 0, 0, 0))
    state = pl.BlockSpec((4, nb, 2 * SSM_STATE), lambda g: (0, 0, g))
    y, fin = pl.pallas_call(
        functools.partial(_ssm_kernel, nchunk, nb),
        grid=(g2,),
        in_specs=[pair, pl.BlockSpec((None, 2, CW, CW), lambda g: (l, g, 0, 0)), mats, mats,
                  pl.BlockSpec((None, None, 4, 2 * SSM_STATE), lambda g: (l, g, 0, 0)), state],
        out_specs=[pair, state],
        out_shape=(jax.ShapeDtypeStruct((SSM_GROUPS, r, CW), F32), jax.ShapeDtypeStruct((4, nb, width), F32)),
        scratch_shapes=[pltpu.VMEM((4, r, 2 * SSM_STATE), F32), pltpu.VMEM((4, r, 2 * SSM_STATE), F32)],
        compiler_params=_params(("parallel",)),
        name="ssm",
    )(ug, tmat, et, ft, a16, h0)
    fin = fin.reshape(2, 2, nb, SSM_GROUPS, SSM_STATE)
    return y, fin


def _gelu_tanh(x):
    return 0.5 * x * (1.0 + jnp.tanh(math.sqrt(2.0 / math.pi) * (x + 0.044715 * (x * x * x))))


def _route(aff, bias):
    a = [aff[e:e + 1, :] for e in range(N_EXPERTS)]
    s = [a[e] + bias[e:e + 1, :] for e in range(N_EXPERTS)]
    keep = []
    for e in range(N_EXPERTS):
        g0 = (e // EXPERTS_PER_GROUP) * EXPERTS_PER_GROUP
        rank = None
        for j in range(g0, g0 + EXPERTS_PER_GROUP):
            if j == e:
                continue
            beats = (s[j] >= s[e]) if j < e else (s[j] > s[e])
            r = jnp.where(beats, 1.0, 0.0)
            rank = r if rank is None else rank + r
        keep.append(rank < 1.5)
    score = []
    for g in range(N_GROUPS):
        tot = None
        for e in range(g * EXPERTS_PER_GROUP, (g + 1) * EXPERTS_PER_GROUP):
            v = jnp.where(keep[e], s[e], 0.0)
            tot = v if tot is None else tot + v
        score.append(tot)
    gates = []
    picked = []
    for g in range(N_GROUPS):
        lost = None
        for j in range(N_GROUPS):
            if j == g:
                continue
            beats = (score[j] >= score[g]) if j < g else (score[j] > score[g])
            r = jnp.where(beats, 1.0, 0.0)
            lost = r if lost is None else lost + r
        chosen = lost < 0.5
        picked.append(jnp.where(chosen, 1.0, 0.0))
        w = [jnp.where(keep[e], a[e], 0.0) for e in range(g * EXPERTS_PER_GROUP, (g + 1) * EXPERTS_PER_GROUP)]
        tot = w[0] + w[1] + w[2] + w[3]
        for v in w:
            gates.append(jnp.where(chosen, v / tot, 0.0))
    return gates, picked


def _merge_kernel(x_ref, yg_ref, at_ref, sga_ref, sgb_ref, mod_ref, wglu_ref, bglu_ref, wsp_ref,
                  wap_ref, wo_ref, gffn_ref, wr_ref, rb_ref, tri_ref, ramp_ref,
                  xo_ref, hs_ref, mt_ref, ms_ref, of_ref, tok_ref):
    mod = mod_ref[...]
    gate1 = mod[:, 2 * D_MODEL:3 * D_MODEL]
    shift2 = mod[:, 3 * D_MODEL:4 * D_MODEL]
    scale2 = mod[:, 4 * D_MODEL:5 * D_MODEL]
    _from_chunk_rows(yg_ref, tok_ref)
    ys = jnp.concatenate([tok_ref[j] for j in range(LANE_TILES)], axis=1)
    y = _gelu_tanh(ys)
    glu = jnp.dot(y.astype(BF16), wglu_ref[...], preferred_element_type=F32) + bglu_ref[...]
    y = (y * _sigmoid(glu)).astype(BF16)
    a_out = jnp.dot(y, wsp_ref[...], preferred_element_type=F32)
    b_out = jnp.dot(at_ref[...], wap_ref[...], preferred_element_type=F32)
    m = (sga_ref[...].astype(F32) * a_out + sgb_ref[...].astype(F32) * b_out).astype(BF16)
    x = x_ref[...] + gate1 * jnp.dot(m, wo_ref[...], preferred_element_type=F32)
    xo_ref[...] = x
    h2 = _rms(x, D_MODEL) * (gffn_ref[...] * (1.0 + scale2)) + shift2
    nt = (((1,), (1,)), ((), ()))
    h2_hi = h2.astype(BF16)
    h2_lo = (h2 - h2_hi.astype(F32)).astype(BF16)
    wr = wr_ref[...]
    wr_hi = wr.astype(BF16)
    wr_lo = (wr - wr_hi.astype(F32)).astype(BF16)
    logits = (lax.dot_general(wr_hi, h2_hi, nt, preferred_element_type=F32)
              + lax.dot_general(wr_hi, h2_lo, nt, preferred_element_type=F32)
              + lax.dot_general(wr_lo, h2_hi, nt, preferred_element_type=F32))
    gates, picked = _route(_sigmoid(logits), rb_ref[...])
    n = x.shape[0]
    onehot = jnp.concatenate(picked + [jnp.zeros((8 - N_GROUPS, n), F32)], axis=0)
    before = jnp.dot(onehot.astype(BF16), tri_ref[...], preferred_element_type=F32)
    count = [jnp.sum(p, axis=-1, keepdims=True) for p in picked]
    start = [jnp.zeros((1, 1), F32)]
    for g in range(N_GROUPS - 1):
        start.append(start[-1] + count[g])
    pos = sum(picked[g] * (start[g] + before[g:g + 1, :]) for g in range(N_GROUPS))
    gid = sum(float(g) * picked[g] for g in range(1, N_GROUPS))
    perm = jnp.where(ramp_ref[...] == pos, 1.0, 0.0).astype(BF16)
    hs_ref[...] = jnp.dot(perm, h2_hi, preferred_element_type=F32).astype(BF16)
    own = [sum(gates[EXPERTS_PER_GROUP * g + e] for g in range(N_GROUPS)) for e in range(EXPERTS_PER_GROUP)]
    meta = jnp.concatenate(own + [gid, pos, jnp.zeros((LANES - EXPERTS_PER_GROUP - 2, n), F32)], axis=0).T
    mt_ref[...] = meta
    hi = meta.astype(BF16)
    lo = (meta - hi.astype(F32)).astype(BF16)
    ms_ref[...] = (jnp.dot(perm, hi, preferred_element_type=F32)
                   + jnp.dot(perm, lo, preferred_element_type=F32))
    lane = lax.broadcasted_iota(jnp.int32, (8, LANES), 1)
    offs = sum(jnp.where(lane == g, start[g], 0.0) for g in range(1, N_GROUPS)) + jnp.where(lane == N_GROUPS, n, 0.0)
    of_ref[...] = offs.astype(jnp.int32)


def _merge(x, yg, attn, sga, sgb, mods, l, lat, wts):
    n = x.shape[0]
    tt = TOK_TILE
    per_seq = 2048 // tt
    mod_map = (lambda i: (l, 1 + i // per_seq, 0, 0)) if lat else (lambda i: (l, 0, 0, 0))
    row = lambda i: (i, 0)
    lw = lambda *shape: pl.BlockSpec((None,) + shape, lambda i: (l,) + (0,) * len(shape))
    return pl.pallas_call(
        _merge_kernel,
        grid=(n // tt,),
        in_specs=[
            pl.BlockSpec((tt, D_MODEL), row),
            pl.BlockSpec((SSM_GROUPS, TILE_CHUNKS, CW), lambda i: (0, i, 0)),
            pl.BlockSpec((tt, ATTN_WIDTH), row), pl.BlockSpec((tt, D_MODEL), row), pl.BlockSpec((tt, D_MODEL), row),
            pl.BlockSpec((None, None, 1, 6 * D_MODEL), mod_map),
            lw(SSM_WIDTH, SSM_WIDTH), lw(1, SSM_WIDTH), lw(SSM_WIDTH, D_MODEL),
            lw(ATTN_WIDTH, D_MODEL), lw(D_MODEL, D_MODEL), lw(1, D_MODEL),
            pl.BlockSpec((N_EXPERTS, D_MODEL), lambda i: (0, 0)),
            pl.BlockSpec((N_EXPERTS, 1), lambda i: (0, 0)),
            pl.BlockSpec((tt, tt), lambda i: (0, 0)), pl.BlockSpec((tt, tt), lambda i: (0, 0)),
        ],
        out_specs=[pl.BlockSpec((tt, D_MODEL), row), pl.BlockSpec((tt, D_MODEL), row),
                   pl.BlockSpec((tt, LANES), row), pl.BlockSpec((tt, LANES), row),
                   pl.BlockSpec((None, 8, LANES), lambda i: (i, 0, 0))],
        out_shape=(jax.ShapeDtypeStruct((n, D_MODEL), F32), jax.ShapeDtypeStruct((n, D_MODEL), BF16),
                   jax.ShapeDtypeStruct((n, LANES), F32), jax.ShapeDtypeStruct((n, LANES), F32),
                   jax.ShapeDtypeStruct((n // tt, 8, LANES), jnp.int32)),
        scratch_shapes=[pltpu.VMEM((LANE_TILES, tt, LANES), F32)],
        compiler_params=_params(("parallel",)),
        name="merge_lat" if lat else "merge_ctx",
    )(x, yg, attn, sga, sgb, mods, wts["w_glu"], wts["b_glu"], wts["w_ssm_proj"],
      wts["w_attn_proj"], wts["w_o"], wts["g_ffn"], wts["w_router_t"], wts["router_bias"],
      jnp.asarray(np.triu(np.ones((tt, tt), np.float32), 1), BF16),
      jnp.asarray(np.broadcast_to(np.arange(tt, dtype=np.float32)[:, None], (tt, tt))))


def _moe_kernel(offs_ref, x_ref, hs_ref, ms_ref, mt_ref, mod_ref, wg_ref, wu_ref, wd_ref, o_ref, acc_ref):
    tile = pl.program_id(0)
    acc_ref[...] = jnp.zeros(acc_ref.shape, F32)

    def run_passes(height, g, first, span):
        row_id = lax.broadcasted_iota(jnp.int32, (height, 1), 0)
        n_pass = sum((span > k * height).astype(jnp.int32) for k in range(pl.cdiv(TOK_TILE + 16, height)))

        def pass_body(j, c):
            lo = first + height * j
            s = pl.multiple_of(jnp.minimum(lo, TOK_TILE - height), 16)
            rows = pl.ds(s, height)
            meta = ms_ref[rows, :]
            group_id = lax.convert_element_type(g, F32)
            mine = (meta[:, EXPERTS_PER_GROUP:EXPERTS_PER_GROUP + 1] == group_id) & (row_id + s >= lo)
            own = jnp.where(mine, 1.0, 0.0)
            hs = hs_ref[rows, :]
            acts = []
            for e in range(EXPERTS_PER_GROUP):
                expert = EXPERTS_PER_GROUP * g + e
                hg = jnp.dot(hs, wg_ref[expert], preferred_element_type=F32)
                hu = jnp.dot(hs, wu_ref[expert], preferred_element_type=F32)
                acts.append(((hg * _sigmoid(hg)) * hu * (meta[:, e:e + 1] * own)).astype(BF16))
            acc_ref[rows, :] += jnp.dot(jnp.concatenate(acts, axis=1), wd_ref[g], preferred_element_type=F32)
            return c

        lax.fori_loop(0, n_pass, pass_body, 0)

    def group_body(g, carry):
        seg_start = offs_ref[(N_GROUPS + 1) * tile + g]
        seg_end = offs_ref[(N_GROUPS + 1) * tile + g + 1]
        first = (seg_start >> 4) << 4
        span = jnp.where(seg_end > seg_start, seg_end - first, 0)
        choice = sum((span > height).astype(jnp.int32) for height in MOE_PASS_ROWS[:-1])
        lax.switch(choice, [functools.partial(run_passes, height, g, first, span) for height in MOE_PASS_ROWS])
        return carry

    lax.fori_loop(0, N_GROUPS, group_body, 0)

    slot = lax.broadcasted_iota(jnp.int32, (TOK_TILE, TOK_TILE), 1).astype(F32)
    unsort = jnp.where(mt_ref[:, EXPERTS_PER_GROUP + 1:EXPERTS_PER_GROUP + 2] == slot, 1.0, 0.0).astype(BF16)
    y = jnp.dot(unsort, acc_ref[...].astype(BF16), preferred_element_type=F32)
    o_ref[...] = x_ref[...] + mod_ref[:, 5 * D_MODEL:6 * D_MODEL] * y


def _moe(x, hs, meta_tok, meta_sorted, offs, mods, l, lat, wts):
    n = x.shape[0]
    tt = TOK_TILE
    per_seq = 2048 // tt
    mod_map = (lambda i, o: (l, 1 + i // per_seq, 0, 0)) if lat else (lambda i, o: (l, 0, 0, 0))
    row = lambda i, o: (i, 0)
    width = EXPERTS_PER_GROUP * D_EXPERT
    resident = lambda *shape: pl.BlockSpec((None,) + shape, lambda i, o: (l,) + (0,) * len(shape),
                                           pipeline_mode=pl.Buffered(1))
    grid_spec = pltpu.PrefetchScalarGridSpec(
        num_scalar_prefetch=1,
        grid=(n // tt,),
        in_specs=[
            pl.BlockSpec((tt, D_MODEL), row), pl.BlockSpec((tt, D_MODEL), row), pl.BlockSpec((tt, LANES), row),
            pl.BlockSpec((tt, LANES), row),
            pl.BlockSpec((None, None, 1, 6 * D_MODEL), mod_map),
            resident(N_EXPERTS, D_MODEL, D_EXPERT), resident(N_EXPERTS, D_MODEL, D_EXPERT),
            resident(N_GROUPS, width, D_MODEL),
        ],
        out_specs=pl.BlockSpec((tt, D_MODEL), row),
        scratch_shapes=[pltpu.VMEM((tt, D_MODEL), F32)],
    )
    return pl.pallas_call(
        _moe_kernel,
        grid_spec=grid_spec,
        out_shape=jax.ShapeDtypeStruct((n, D_MODEL), F32),
        compiler_params=_params(("arbitrary",), MOE_VMEM_LIMIT),
        name="moe_lat" if lat else "moe_ctx",
    )(offs[:, 0, :N_GROUPS + 1].reshape(-1), x, hs, meta_sorted, meta_tok, mods, wts["w_gate"], wts["w_up"],
      wts["w_down"])


def _pad_heads(w, width):
    lead = w.shape[:-1]
    w = w.reshape(lead + (N_HEADS, width))
    w = jnp.pad(w, [(0, 0)] * len(lead) + [(0, 0), (0, HEAD_PAD - width)])
    return w.reshape(lead + (N_HEADS * HEAD_PAD,))


def _swap_partners(w):
    lead = w.shape[:-1]
    w = w.reshape(lead + (-1, HEAD_PAD))
    half = ROPE_AXIS // 2
    a, b = QK_NOPE, QK_NOPE + ROPE_AXIS
    parts = [w[..., :a], w[..., a + half:b], w[..., a:a + half], w[..., b + half:b + 2 * half], w[..., b:b + half],
             w[..., b + 2 * half:]]
    return jnp.concatenate(parts, axis=-1).reshape(lead + (-1,))


def _layout_weights(w_in, g_qa, w_qb, g_q, g_kva, w_kvb, g_k, g_mix, w_attn_proj, ssm_w_glu, ssm_b_glu,
                    ssm_w_proj, w_o, g_ffn, w_router, router_bias, w_gate, w_up, w_down):
    n_a = SSM_WIDTH + Q_LORA + KV_LORA + QK_ROPE
    w_a = jnp.pad(w_in[:, :, :n_a], ((0, 0), (0, 0), (0, 1024 - n_a)))
    kv = w_kvb.reshape(DEPTH, KV_LORA, N_HEADS, QK_NOPE + V_HEAD)
    k_cols = _pad_heads(kv[..., :QK_NOPE].reshape(DEPTH, KV_LORA, N_HEADS * QK_NOPE), QK_NOPE)
    v_cols = _pad_heads(kv[..., QK_NOPE:].reshape(DEPTH, KV_LORA, ATTN_WIDTH), V_HEAD)
    place = np.zeros((KV_LORA, 2 * QK_PAD), np.float32)
    for h in range(N_HEADS):
        for r in range(QK_ROPE):
            place[r, h * HEAD_PAD + QK_NOPE + r] = 1.0
    w_k = jnp.concatenate([jnp.concatenate([k_cols, v_cols], axis=-1),
                           jnp.broadcast_to(jnp.asarray(place), (DEPTH,) + place.shape)], axis=1)
    w_qb_pad = _pad_heads(w_qb, QK_HEAD).astype(BF16)
    w_k = w_k.astype(BF16)
    pad_gain = lambda g: jnp.pad(g, ((0, 0), (0, HEAD_PAD - QK_HEAD))).reshape(DEPTH, 1, HEAD_PAD)
    vec = lambda g: g.reshape(DEPTH, 1, -1)
    return {
        "g_mix": vec(g_mix), "w_a": w_a.astype(BF16),
        "w_ga": w_in[:, :, n_a:n_a + D_MODEL].astype(BF16), "w_gb": w_in[:, :, n_a + D_MODEL:].astype(BF16),
        "g_qa": vec(g_qa), "w_qb": w_qb_pad, "w_qb_sw": _swap_partners(w_qb_pad),
        "g_q": pad_gain(g_q) * Q_SCALE, "g_q_sw": _swap_partners(pad_gain(g_q)) * Q_SCALE,
        "g_kva": vec(g_kva), "w_k": w_k, "w_k_sw": _swap_partners(w_k[:, :, :QK_PAD]),
        "g_k": pad_gain(g_k), "g_k_sw": _swap_partners(pad_gain(g_k)),
        "w_glu": ssm_w_glu.astype(BF16), "b_glu": vec(ssm_b_glu),
        "w_ssm_proj": ssm_w_proj.astype(BF16), "w_attn_proj": w_attn_proj.astype(BF16), "w_o": w_o.astype(BF16),
        "g_ffn": vec(g_ffn), "w_router_t": w_router.T, "router_bias": router_bias.reshape(N_EXPERTS, 1),
        "w_gate": w_gate.astype(BF16), "w_up": w_up.astype(BF16),
        "w_down": w_down.astype(BF16).reshape(DEPTH, N_GROUPS, EXPERTS_PER_GROUP * D_EXPERT, D_MODEL),
    }


def _rope_tables(n_tokens):
    pos = np.arange(n_tokens)
    inv = 1.0 / (ROPE_BASE ** (np.arange(ROPE_AXIS // 2, dtype=np.float32) * 2.0 / ROPE_AXIS))
    ang = np.zeros((n_tokens, HEAD_PAD), np.float32)
    half = ROPE_AXIS // 2
    row = (pos // GRID_W).astype(np.float32)[:, None] * inv
    col = (pos % GRID_W).astype(np.float32)[:, None] * inv
    first = np.zeros((HEAD_PAD,), bool)
    second = np.zeros((HEAD_PAD,), bool)
    for base, a in ((QK_NOPE, row), (QK_NOPE + ROPE_AXIS, col)):
        ang[:, base:base + half] = a
        ang[:, base + half:base + 2 * half] = a
        first[base:base + half] = True
        second[base + half:base + 2 * half] = True
    ang = jnp.asarray(ang)
    cos = jnp.where(jnp.asarray(first | second), jnp.cos(ang), 1.0)
    sin = jnp.sin(ang)
    return cos, jnp.where(jnp.asarray(first), -sin, jnp.where(jnp.asarray(second), sin, 0.0))


def kernel(x_prompt, x_sample, c, cache_ckv, cache_krope, state_ssm_re, state_ssm_im, c_ctx, w_ada, b_ada, g_mix, w_in, g_qa, w_qb, g_q, g_kva, w_kvb, g_k, w_attn_proj, ssm_a_re, ssm_a_im, ssm_log_dt, ssm_b_re, ssm_b_im, ssm_c_re, ssm_c_im, ssm_d, ssm_w_glu, ssm_b_glu, ssm_w_proj, w_o, g_ffn, w_router, router_bias, w_gate, w_up, w_down):
    nb_c, seq_c, _ = x_prompt.shape
    nb_l, seq_l, _ = x_sample.shape
    wts = _layout_weights(w_in, g_qa, w_qb, g_q, g_kva, w_kvb, g_k, g_mix, w_attn_proj, ssm_w_glu,
                          ssm_b_glu, ssm_w_proj, w_o, g_ffn, w_router, router_bias, w_gate, w_up, w_down)
    cvec = jnp.concatenate([c_ctx[None, :], c, jnp.zeros((8 - 1 - nb_l, D_MODEL), F32)], axis=0)
    mods = _mods(cvec, w_ada, b_ada).reshape(DEPTH, 8, 1, 6 * D_MODEL)
    prep = _ssm_prep(ssm_a_re, ssm_a_im, ssm_log_dt, ssm_b_re, ssm_b_im, ssm_c_re, ssm_c_im, ssm_d)
    tabs = _rope_tables(seq_l)
    kc, vc = _cachekv(cache_ckv, cache_krope, wts)

    xp = x_prompt.reshape(nb_c * seq_c, D_MODEL)
    xs = x_sample.reshape(nb_l * seq_l, D_MODEL)
    zeros_c = jnp.zeros((nb_c, 2, SSM_GROUPS, SSM_STATE), F32)
    ckvs, kropes, fins = [], [], []
    for l in range(DEPTH):
        ug, q, k, v, sga, sgb, ckv, krope = _inproj(xp, mods, l, False, wts, tabs)
        ckvs.append(ckv)
        kropes.append(krope)
        yg, fin = _ssm_mix(ug, nb_c, seq_c, zeros_c, zeros_c, prep, l)
        fins.append(fin)
        attn = _attn_ctx(q, k, v, seq_c)
        xp, *routed = _merge(xp, yg, attn, sga, sgb, mods, l, False, wts)
        xp = _moe(xp, *routed, mods, l, False, wts)

        ug, q, k, v, sga, sgb, _, _ = _inproj(xs, mods, l, True, wts, tabs)
        yg, _ = _ssm_mix(ug, nb_l, seq_l, state_ssm_re[:, l], state_ssm_im[:, l], prep, l)
        attn = _attn_lat(q, k, v, kc, vc, l, seq_l, TOK_TILE)
        xs, *routed = _merge(xs, yg, attn, sga, sgb, mods, l, True, wts)
        xs = _moe(xs, *routed, mods, l, True, wts)

    new_ckv = jnp.stack(ckvs, axis=0).reshape(DEPTH, nb_c, seq_c, KV_LORA).transpose(1, 0, 2, 3)
    new_krope = jnp.stack(kropes, axis=0).reshape(DEPTH, nb_c, seq_c, QK_ROPE).transpose(1, 0, 2, 3)
    fin = jnp.stack(fins, axis=0)
    new_re = jnp.transpose(fin[:, :, 0], (2, 0, 1, 3, 4))
    new_im = jnp.transpose(fin[:, :, 1], (2, 0, 1, 3, 4))
    return (xp.reshape(nb_c, seq_c, D_MODEL), xs.reshape(nb_l, seq_l, D_MODEL), new_ckv, new_krope, new_re, new_im)
```

```python
import functools
import math

import jax
import jax.numpy as jnp
import numpy as np
from jax import lax
from jax.experimental import pallas as pl
from jax.experimental.pallas import tpu as pltpu

F32 = jnp.float32
BF16 = jnp.bfloat16

D_MODEL = 1024
DEPTH = 4
GRID_W = 64
SSM_WIDTH = 512
SSM_GROUP = 16
SSM_GROUPS = 32
SSM_STATE = 64
N_HEADS = 8
QK_NOPE = 64
QK_ROPE = 32
QK_HEAD = 96
V_HEAD = 64
Q_LORA = 256
KV_LORA = 128
ATTN_WIDTH = 512
ROPE_AXIS = 16
ROPE_BASE = 10000.0
N_EXPERTS = 16
N_GROUPS = 4
EXPERTS_PER_GROUP = 4
D_EXPERT = 256
EPS = 1e-6

LANES = 128
HEAD_PAD = LANES
QK_PAD = N_HEADS * HEAD_PAD
CHUNK = 16
CW = CHUNK * SSM_GROUP
TOK_TILE = 512
TILE_CHUNKS = TOK_TILE // CHUNK
LANE_TILES = SSM_WIDTH // LANES
GROUPS_PER_TILE = LANES // SSM_GROUP
Q_SCALE = math.log2(math.e) / math.sqrt(QK_HEAD)
MOE_PASS_ROWS = tuple(range(128, 257, 16))
VMEM_LIMIT = 48 * 1024 * 1024
MOE_VMEM_LIMIT = 56 * 1024 * 1024
ATTN_VMEM_LIMIT = 60 * 1024 * 1024


def _params(sem, vmem=VMEM_LIMIT):
    return pltpu.CompilerParams(dimension_semantics=sem, vmem_limit_bytes=vmem)


def _sigmoid(x):
    return 0.5 * jnp.tanh(0.5 * x) + 0.5


def _rms(x, n):
    return x * lax.rsqrt(jnp.sum(x * x, axis=-1, keepdims=True) * (1.0 / n) + EPS)


def _mods_kernel(c_ref, w_ref, b_ref, o_ref):
    c = c_ref[...]
    cs = (c * _sigmoid(c)).astype(BF16)
    o_ref[...] = jnp.dot(cs, w_ref[...].astype(BF16), preferred_element_type=F32) + b_ref[...]


def _mods(cvec, w_ada, b_ada):
    nb = 1536
    return pl.pallas_call(
        _mods_kernel,
        grid=(DEPTH, 6 * D_MODEL // nb),
        in_specs=[
            pl.BlockSpec((8, D_MODEL), lambda l, j: (0, 0)),
            pl.BlockSpec((None, D_MODEL, nb), lambda l, j: (l, 0, j)),
            pl.BlockSpec((None, 1, nb), lambda l, j: (l, 0, j)),
        ],
        out_specs=pl.BlockSpec((None, 8, nb), lambda l, j: (l, 0, j)),
        out_shape=jax.ShapeDtypeStruct((DEPTH, 8, 6 * D_MODEL), F32),
        compiler_params=_params(("arbitrary", "arbitrary")),
        name="mods",
    )(cvec, w_ada, b_ada.reshape(DEPTH, 1, 6 * D_MODEL))


def _slot_step(slot, g):
    return (slot & 8) + ((slot - g) & 7)


def _lane_segment():
    return lax.broadcasted_iota(jnp.int32, (TILE_CHUNKS, LANES), 1) // SSM_GROUP


def _to_chunk_rows(tok_ref, ug_ref):
    seg = _lane_segment()
    for j in range(LANE_TILES):
        for half in range(CHUNK // 8):
            rolled = []
            for r in range(8):
                v = tok_ref[j, pl.ds(8 * half + r, TILE_CHUNKS, stride=CHUNK), :]
                rolled.append(pltpu.roll(v, SSM_GROUP * r, 1) if r else v)
            for gg in range(GROUPS_PER_TILE):
                acc = rolled[(0 - gg) % 8]
                for m in range(1, 8):
                    acc = jnp.where(seg == m, rolled[(m - gg) % 8], acc)
                ug_ref[GROUPS_PER_TILE * j + gg, :, LANES * half:LANES * (half + 1)] = acc.astype(ug_ref.dtype)


def _from_chunk_rows(yg_ref, tok_ref):
    seg = _lane_segment()
    for j in range(LANE_TILES):
        for half in range(CHUNK // 8):
            ys = [yg_ref[GROUPS_PER_TILE * j + gg, :, LANES * half:LANES * (half + 1)]
                  for gg in range(GROUPS_PER_TILE)]
            for r in range(8):
                acc = ys[(0 - r) % 8]
                for m in range(1, 8):
                    acc = jnp.where(seg == m, ys[(m - r) % 8], acc)
                out = pltpu.roll(acc, LANES - SSM_GROUP * r, 1) if r else acc
                tok_ref[j, pl.ds(8 * half + r, TILE_CHUNKS, stride=CHUNK), :] = out


def _head_norm_rope(t, gain, swapped):
    outs = []
    for h in range(N_HEADS):
        lanes = slice(h * HEAD_PAD, (h + 1) * HEAD_PAD)
        th = t[:, lanes]
        scale = lax.rsqrt(jnp.sum(th * th, axis=-1, keepdims=True) * (1.0 / QK_HEAD) + EPS)
        y = th * gain
        if swapped is not None:
            y = y + swapped[0][:, lanes] * swapped[1]
        outs.append(y * scale)
    return jnp.concatenate(outs, axis=1)


def _with_ones(v):
    lane = lax.broadcasted_iota(jnp.int32, (1, v.shape[1]), 1)
    return jnp.where((lane & (HEAD_PAD - 1)) == V_HEAD, 1.0, v)


def _kv_dot(ckvn, krp, wk_ref):
    return (jnp.dot(ckvn, wk_ref[0:KV_LORA, :], preferred_element_type=F32)
            + jnp.dot(krp, wk_ref[KV_LORA:KV_LORA + krp.shape[1], :], preferred_element_type=F32))


def _inproj_kernel(use_rope, x_ref, mod_ref, gmix_ref, wa_ref, wga_ref, wgb_ref, gqa_ref, wqb_ref, wqbs_ref, gq_ref,
                   gqs_ref, gkva_ref, wk_ref, wks_ref, gk_ref, gks_ref, cos_ref, sin_ref,
                   ug_ref, q_ref, k_ref, v_ref, sga_ref, sgb_ref, ckv_ref, kr_ref, tok_ref):
    mod = mod_ref[...]
    shift1 = mod[:, 0:D_MODEL]
    gain1 = gmix_ref[...] * (1.0 + mod[:, D_MODEL:2 * D_MODEL])
    h = (_rms(x_ref[...], D_MODEL) * gain1 + shift1).astype(BF16)
    za = jnp.dot(h, wa_ref[...], preferred_element_type=F32)
    sga_ref[...] = _sigmoid(jnp.dot(h, wga_ref[...], preferred_element_type=F32)).astype(BF16)
    sgb_ref[...] = _sigmoid(jnp.dot(h, wgb_ref[...], preferred_element_type=F32)).astype(BF16)
    for j in range(LANE_TILES):
        tok_ref[j] = za[:, LANES * j:LANES * (j + 1)]
    _to_chunk_rows(tok_ref, ug_ref)
    qa = za[:, SSM_WIDTH:SSM_WIDTH + Q_LORA]
    ckv = za[:, SSM_WIDTH + Q_LORA:SSM_WIDTH + Q_LORA + KV_LORA]
    krp = za[:, SSM_WIDTH + Q_LORA + KV_LORA:]
    qn = (_rms(qa, Q_LORA) * gqa_ref[...]).astype(BF16)
    qr = jnp.dot(qn, wqb_ref[...], preferred_element_type=F32)
    ckvn = _rms(ckv, KV_LORA) * gkva_ref[...]
    ckv_ref[...] = ckvn
    kr_ref[...] = krp[:, :QK_ROPE]
    ckvn = ckvn.astype(BF16)
    krp = krp.astype(BF16)
    kv = _kv_dot(ckvn, krp, wk_ref)
    gq = gq_ref[...]
    gk = gk_ref[...]
    q_sw = k_sw = None
    if use_rope:
        cos = cos_ref[...]
        sin = sin_ref[...]
        q_sw = (jnp.dot(qn, wqbs_ref[...], preferred_element_type=F32), gqs_ref[...] * sin)
        k_sw = (_kv_dot(ckvn, krp, wks_ref), gks_ref[...] * sin)
        gq = gq * cos
        gk = gk * cos
    q_ref[...] = _head_norm_rope(qr, gq, q_sw).astype(BF16)
    k_ref[...] = _head_norm_rope(kv[:, :QK_PAD], gk, k_sw).astype(BF16)
    v_ref[...] = _with_ones(kv[:, QK_PAD:]).astype(BF16)


def _inproj(x, mods, l, lat, wts, tabs):
    n = x.shape[0]
    tt = TOK_TILE
    per_seq = 2048 // tt
    if lat:
        mod_map = lambda i: (l, 1 + i // per_seq, 0, 0)
        tab_map = lambda i: (i % per_seq, 0)
    else:
        mod_map = lambda i: (l, 0, 0, 0)
        tab_map = lambda i: (0, 0)
    row = lambda i: (i, 0)
    lw = lambda *shape: pl.BlockSpec((None,) + shape, lambda i: (l,) + (0,) * len(shape))
    tab = pl.BlockSpec((tt, HEAD_PAD), tab_map)
    row_shapes = (
        jax.ShapeDtypeStruct((n, QK_PAD), BF16),
        jax.ShapeDtypeStruct((n, QK_PAD), BF16),
        jax.ShapeDtypeStruct((n, QK_PAD), BF16),
        jax.ShapeDtypeStruct((n, D_MODEL), BF16),
        jax.ShapeDtypeStruct((n, D_MODEL), BF16),
        jax.ShapeDtypeStruct((n, KV_LORA), F32),
        jax.ShapeDtypeStruct((n, QK_ROPE), F32),
    )
    row_specs = [pl.BlockSpec((tt, s.shape[1]), row) for s in row_shapes]
    ug_shape = jax.ShapeDtypeStruct((SSM_GROUPS, n // CHUNK, CW), BF16)
    ug_spec = pl.BlockSpec((SSM_GROUPS, TILE_CHUNKS, CW), lambda i: (0, i, 0))
    outs = pl.pallas_call(
        functools.partial(_inproj_kernel, lat),
        grid=(n // tt,),
        in_specs=[
            pl.BlockSpec((tt, D_MODEL), row),
            pl.BlockSpec((None, None, 1, 6 * D_MODEL), mod_map),
            lw(1, D_MODEL), lw(D_MODEL, 1024), lw(D_MODEL, D_MODEL), lw(D_MODEL, D_MODEL),
            lw(1, Q_LORA), lw(Q_LORA, QK_PAD), lw(Q_LORA, QK_PAD), lw(1, HEAD_PAD), lw(1, HEAD_PAD), lw(1, KV_LORA),
            lw(2 * KV_LORA, 2 * QK_PAD), lw(2 * KV_LORA, QK_PAD), lw(1, HEAD_PAD), lw(1, HEAD_PAD),
            tab, tab,
        ],
        out_specs=[ug_spec] + row_specs,
        out_shape=(ug_shape,) + row_shapes,
        scratch_shapes=[pltpu.VMEM((LANE_TILES, tt, LANES), F32)],
        compiler_params=_params(("parallel",)),
        name="inproj_lat" if lat else "inproj_ctx",
    )(x, mods, wts["g_mix"], wts["w_a"], wts["w_ga"], wts["w_gb"], wts["g_qa"], wts["w_qb"], wts["w_qb_sw"],
      wts["g_q"], wts["g_q_sw"], wts["g_kva"], wts["w_k"], wts["w_k_sw"], wts["g_k"], wts["g_k_sw"], *tabs)
    return outs


def _cachekv_kernel(ckv_ref, kr_ref, wk_ref, gk_ref, k_ref, v_ref):
    kv = _kv_dot(ckv_ref[...].astype(BF16), kr_ref[...].astype(BF16), wk_ref)
    k_ref[...] = _head_norm_rope(kv[:, :QK_PAD], gk_ref[...], None).astype(BF16)
    v_ref[...] = _with_ones(kv[:, QK_PAD:]).astype(BF16)


def _cachekv(cache_ckv, cache_krope, wts):
    nb, _, past, _ = cache_ckv.shape
    return pl.pallas_call(
        _cachekv_kernel,
        grid=(DEPTH, nb),
        in_specs=[
            pl.BlockSpec((None, None, past, KV_LORA), lambda l, b: (b, l, 0, 0)),
            pl.BlockSpec((None, None, past, QK_ROPE), lambda l, b: (b, l, 0, 0)),
            pl.BlockSpec((None, 2 * KV_LORA, 2 * QK_PAD), lambda l, b: (l, 0, 0)),
            pl.BlockSpec((None, 1, HEAD_PAD), lambda l, b: (l, 0, 0)),
        ],
        out_specs=[
            pl.BlockSpec((None, None, past, QK_PAD), lambda l, b: (l, b, 0, 0)),
            pl.BlockSpec((None, None, past, QK_PAD), lambda l, b: (l, b, 0, 0)),
        ],
        out_shape=(jax.ShapeDtypeStruct((DEPTH, nb, past, QK_PAD), BF16),
                   jax.ShapeDtypeStruct((DEPTH, nb, past, QK_PAD), BF16)),
        compiler_params=_params(("arbitrary", "arbitrary")),
        name="cache_kv",
    )(cache_ckv, cache_krope, wts["w_k"], wts["g_k"])


def _attn_heads(q_ref, segs, o_ref, den_on_mxu):
    for h in range(N_HEADS):
        qh = q_ref[:, h * HEAD_PAD:(h + 1) * HEAD_PAD]
        scores = [lax.dot_general(qh, k_ref[:, h * HEAD_PAD:(h + 1) * HEAD_PAD], (((1,), (1,)), ((), ())),
                                  preferred_element_type=F32) for k_ref, _ in segs]
        m = scores[0].max(axis=-1, keepdims=True)
        for s in scores[1:]:
            m = jnp.maximum(m, s.max(axis=-1, keepdims=True))
        acc = None
        den = None
        for s, (_, v_ref) in zip(scores, segs):
            if den_on_mxu:
                p = jnp.exp2((s - m).astype(BF16))
                a = jnp.dot(p, v_ref[:, h * HEAD_PAD:(h + 1) * HEAD_PAD], preferred_element_type=F32)
            else:
                p = jnp.exp2(s - m)
                d = p.sum(axis=-1, keepdims=True)
                den = d if den is None else den + d
                a = jnp.dot(p.astype(BF16), v_ref[:, h * HEAD_PAD:h * HEAD_PAD + V_HEAD],
                            preferred_element_type=F32)
            acc = a if acc is None else acc + a
        if den_on_mxu:
            den = acc[:, V_HEAD:V_HEAD + 1]
        o_ref[:, h * V_HEAD:(h + 1) * V_HEAD] = (acc[:, :V_HEAD] / den).astype(o_ref.dtype)


def _attn_ctx_kernel(q_ref, k_ref, v_ref, o_ref):
    for b in range(q_ref.shape[0]):
        _attn_heads(q_ref.at[b], [(k_ref.at[b], v_ref.at[b])], o_ref.at[b], False)


def _attn_lat_kernel(q_ref, kc_ref, vc_ref, kl_ref, vl_ref, o_ref):
    _attn_heads(q_ref, [(kc_ref, vc_ref), (kl_ref, vl_ref)], o_ref, True)


def _attn_ctx(q, k, v, seq):
    n = q.shape[0]
    nb = n // seq
    per_step = 2
    blk = lambda w: pl.BlockSpec((per_step, seq, w), lambda b: (b, 0, 0))
    seqs = lambda a: a.reshape(nb, seq, a.shape[1])
    return pl.pallas_call(
        _attn_ctx_kernel,
        grid=(nb // per_step,),
        in_specs=[blk(QK_PAD), blk(QK_PAD), blk(QK_PAD)],
        out_specs=blk(ATTN_WIDTH),
        out_shape=jax.ShapeDtypeStruct((nb, seq, ATTN_WIDTH), BF16),
        compiler_params=_params(("parallel",)),
        name="attn_ctx",
    )(seqs(q), seqs(k), seqs(v)).reshape(n, ATTN_WIDTH)


def _attn_lat(q, k, v, kc, vc, l, seq, tq):
    n = q.shape[0]
    per = seq // tq
    past = kc.shape[2]
    return pl.pallas_call(
        _attn_lat_kernel,
        grid=(n // seq, per),
        in_specs=[
            pl.BlockSpec((tq, QK_PAD), lambda b, j: (b * per + j, 0)),
            pl.BlockSpec((None, None, past, QK_PAD), lambda b, j: (l, b, 0, 0)),
            pl.BlockSpec((None, None, past, QK_PAD), lambda b, j: (l, b, 0, 0)),
            pl.BlockSpec((seq, QK_PAD), lambda b, j: (b, 0)),
            pl.BlockSpec((seq, QK_PAD), lambda b, j: (b, 0)),
        ],
        out_specs=pl.BlockSpec((tq, ATTN_WIDTH), lambda b, j: (b * per + j, 0)),
        out_shape=jax.ShapeDtypeStruct((n, ATTN_WIDTH), BF16),
        compiler_params=_params(("parallel", "arbitrary"), ATTN_VMEM_LIMIT),
        name="attn_lat",
    )(q, kc, vc, k, v)


def _cmul(ar, ai, br, bi):
    return ar * br - ai * bi, ar * bi + ai * br


def _outer_rows(pw, m):
    return (pw[:, None, :] * m[None, :, :]).reshape(CW, SSM_STATE)


def _ssm_prep_kernel(are_ref, aim_ref, ldt_ref, bre_ref, bim_ref, cre_ref, cim_ref, d_ref,
                     t_ref, et_ref, ft_ref, a16_ref):
    et_ref[...] = jnp.zeros(et_ref.shape, et_ref.dtype)
    ft_ref[...] = jnp.zeros(ft_ref.shape, ft_ref.dtype)
    row_id = lax.broadcasted_iota(jnp.int32, (CHUNK, SSM_STATE), 0)
    strip_lane = lax.broadcasted_iota(jnp.int32, (SSM_GROUP, CW), 1)
    nt = (((1,), (1,)), ((), ()))
    for gi in range(2):
        g = 2 * pl.program_id(1) + gi
        steps = _slot_step(row_id, g)
        lanes = slice(gi * SSM_STATE, (gi + 1) * SSM_STATE)
        rows = slice(gi * CW, (gi + 1) * CW)
        cre = cre_ref[gi]
        cim = cim_ref[gi]
        m_dir = []
        for d in range(2):
            a_re = are_ref[d, gi]
            a_im = aim_ref[d, gi]
            dt = jnp.exp(ldt_ref[d, gi])
            mag = jnp.exp(a_re * dt)
            abr = mag * jnp.cos(a_im * dt)
            abi = mag * jnp.sin(a_im * dt)
            den = a_re * a_re + a_im * a_im
            nr = abr - 1.0
            coef_re = (nr * a_re + abi * a_im) / den
            coef_im = (abi * a_re - nr * a_im) / den
            bbr, bbi = _cmul(coef_re, coef_im, bre_ref[d, gi], bim_ref[d, gi])

            squares = [(abr, abi)]
            for _ in range(4):
                squares.append(_cmul(*squares[-1], *squares[-1]))

            def power(n):
                pr = jnp.ones(n.shape, F32)
                pi = jnp.zeros(n.shape, F32)
                for b, (sr, si) in enumerate(squares):
                    nr_, ni_ = _cmul(pr, pi, sr, si)
                    hit = (n & (1 << b)) != 0
                    pr = jnp.where(hit, nr_, pr)
                    pi = jnp.where(hit, ni_, pi)
                return pr, pi

            to_end = (CHUNK - 1 - steps) if d == 0 else steps
            from_start = (steps + 1) if d == 0 else (CHUNK - steps)
            lag = row_id if d == 0 else (CHUNK - 1 - row_id)
            pr, pi = power(to_end)
            er = _outer_rows(pr, bbr) - _outer_rows(pi, bbi)
            ei = _outer_rows(pr, bbi) + _outer_rows(pi, bbr)
            et_ref[2 * d, rows, lanes] = er.astype(et_ref.dtype)
            et_ref[2 * d + 1, rows, lanes] = ei.astype(et_ref.dtype)
            pr, pi = power(from_start)
            ft_ref[2 * d, rows, lanes] = (_outer_rows(pr, cre) - _outer_rows(pi, cim)).astype(ft_ref.dtype)
            ft_ref[2 * d + 1, rows, lanes] = (-(_outer_rows(pr, cim) + _outer_rows(pi, cre))).astype(ft_ref.dtype)
            pr, pi = power(lag)
            zr = _outer_rows(pr, cre) - _outer_rows(pi, cim)
            zi = _outer_rows(pi, cre) + _outer_rows(pr, cim)
            m_dir.append(
                lax.dot_general(bbr, zr, nt, preferred_element_type=F32, precision=lax.Precision.HIGHEST)
                - lax.dot_general(bbi, zi, nt, preferred_element_type=F32, precision=lax.Precision.HIGHEST))
            p16r, p16i = squares[4]
            a16_ref[2 * d:2 * d + 1, lanes] = p16r
            a16_ref[2 * d + 1:2 * d + 2, lanes] = p16i
        mf, mb = m_dir
        edge = CW - SSM_GROUP
        low = mb + pltpu.roll(jnp.where(strip_lane < SSM_GROUP, mf, 0.0), edge, 1)
        strip_row = lax.broadcasted_iota(jnp.int32, (SSM_GROUP, CW), 0)
        low = low + jnp.where(strip_lane - edge == strip_row, d_ref[gi], 0.0)
        high = jnp.where(strip_lane < edge, pltpu.roll(mf, edge, 1), 0.0)
        strip = jnp.concatenate([low, high], axis=1)
        turn = SSM_GROUP * (g & 7)
        for j in range(CHUNK):
            first = SSM_GROUP * (CHUNK - 1 - j)
            win = strip[:, first:first + CW]
            win = jnp.concatenate([pltpu.roll(win[:, LANES * h:LANES * (h + 1)], turn, 1)
                                   for h in range(CW // LANES)], axis=1)
            slot = (j & 8) + ((j + g) & 7)
            t_ref[gi, pl.ds(pl.multiple_of(SSM_GROUP * slot, SSM_GROUP), SSM_GROUP), :] = win.astype(t_ref.dtype)


def _ssm_prep(a_re, a_im, log_dt, b_re, b_im, c_re, c_im, d):
    g2 = SSM_GROUPS // 2
    a_spec = pl.BlockSpec((None, 2, 2, 1, SSM_STATE), lambda l, g: (l, 0, g, 0, 0))
    b_spec = pl.BlockSpec((None, 2, 2, SSM_GROUP, SSM_STATE), lambda l, g: (l, 0, g, 0, 0))
    c_spec = pl.BlockSpec((None, 2, SSM_GROUP, SSM_STATE), lambda l, g: (l, g, 0, 0))
    out_shapes = (
        jax.ShapeDtypeStruct((DEPTH, SSM_GROUPS, CW, CW), BF16),
        jax.ShapeDtypeStruct((DEPTH, g2, 4, 2 * CW, 2 * SSM_STATE), BF16),
        jax.ShapeDtypeStruct((DEPTH, g2, 4, 2 * CW, 2 * SSM_STATE), BF16),
        jax.ShapeDtypeStruct((DEPTH, g2, 4, 2 * SSM_STATE), F32),
    )
    return pl.pallas_call(
        _ssm_prep_kernel,
        grid=(DEPTH, g2),
        in_specs=[a_spec, a_spec,
                  pl.BlockSpec((None, 2, 2, 1, 1), lambda l, g: (l, 0, g, 0, 0)),
                  b_spec, b_spec, c_spec, c_spec,
                  pl.BlockSpec((None, 2, SSM_GROUP, 1), lambda l, g: (l, g, 0, 0))],
        out_specs=[
            pl.BlockSpec((None, 2, CW, CW), lambda l, g: (l, g, 0, 0)),
            pl.BlockSpec((None, None, 4, 2 * CW, 2 * SSM_STATE), lambda l, g: (l, g, 0, 0, 0)),
            pl.BlockSpec((None, None, 4, 2 * CW, 2 * SSM_STATE), lambda l, g: (l, g, 0, 0, 0)),
            pl.BlockSpec((None, None, 4, 2 * SSM_STATE), lambda l, g: (l, g, 0, 0)),
        ],
        out_shape=out_shapes,
        compiler_params=_params(("arbitrary", "arbitrary")),
        name="ssm_prep",
    )(a_re.reshape(DEPTH, 2, SSM_GROUPS, 1, SSM_STATE), a_im.reshape(DEPTH, 2, SSM_GROUPS, 1, SSM_STATE),
      log_dt.reshape(DEPTH, 2, SSM_GROUPS, 1, 1),
      jnp.swapaxes(b_re, -1, -2), jnp.swapaxes(b_im, -1, -2), c_re, c_im,
      d.reshape(DEPTH, SSM_GROUPS, SSM_GROUP, 1))


def _ssm_kernel(nchunk, nb, u_ref, t_ref, et_ref, ft_ref, a16_ref, h0_ref, y_ref, fin_ref, e_ref, h_ref):
    u0 = u_ref[0]
    u1 = u_ref[1]
    e_all = jnp.dot(jnp.concatenate([u0, u1], axis=1), jnp.concatenate([et_ref[x] for x in range(4)], axis=1),
                    preferred_element_type=F32)
    for x in range(4):
        e_ref[x] = e_all[:, LANES * x:LANES * (x + 1)]
    afr, afi, abr, abi = (a16_ref[i:i + 1, :] for i in range(4))

    def body(i, carry):
        fr, fi, br, bi = carry
        kf = pl.ds(i, nb, stride=nchunk)
        kb = pl.ds(nchunk - 1 - i, nb, stride=nchunk)
        h_ref[0, kf, :] = fr
        h_ref[1, kf, :] = fi
        h_ref[2, kb, :] = br
        h_ref[3, kb, :] = bi
        return (afr * fr - afi * fi + e_ref[0, kf, :], afr * fi + afi * fr + e_ref[1, kf, :],
                abr * br - abi * bi + e_ref[2, kb, :], abr * bi + abi * br + e_ref[3, kb, :])

    fin = lax.fori_loop(0, nchunk, body, tuple(h0_ref[i] for i in range(4)), unroll=4)
    for i in range(4):
        fin_ref[i] = fin[i]
    nt = (((1,), (1,)), ((), ()))
    acc = lax.dot_general(jnp.concatenate([h_ref[x].astype(BF16) for x in range(4)], axis=1),
                          jnp.concatenate([ft_ref[x] for x in range(4)], axis=1), nt,
                          preferred_element_type=F32)
    y_ref[0] = jnp.dot(u0, t_ref[0], preferred_element_type=F32) + acc[:, :CW]
    y_ref[1] = jnp.dot(u1, t_ref[1], preferred_element_type=F32) + acc[:, CW:]


def _ssm_mix(ug, nb, seq, h0_re, h0_im, prep, l):
    tmat, et, ft, a16 = prep
    nchunk = seq // CHUNK
    r = ug.shape[1]
    g2 = SSM_GROUPS // 2
    width = SSM_GROUPS * SSM_STATE
    h0 = jnp.stack([h[:, d].reshape(nb, width).astype(F32) for d in range(2) for h in (h0_re, h0_im)], axis=0)
    pair = pl.BlockSpec((2, r, CW), lambda g: (g, 0, 0))
    mats = pl.BlockSpec((None, None, 4, 2 * CW, 2 * SSM_STATE), lambda g: (l, g, 0, 0, 0))
    state = pl.BlockSpec((4, nb, 2 * SSM_STATE), lambda g: (0, 0, g))
    y, fin = pl.pallas_call(
        functools.partial(_ssm_kernel, nchunk, nb),
        grid=(g2,),
        in_specs=[pair, pl.BlockSpec((None, 2, CW, CW), lambda g: (l, g, 0, 0)), mats, mats,
                  pl.BlockSpec((None, None, 4, 2 * SSM_STATE), lambda g: (l, g, 0, 0)), state],
        out_specs=[pair, state],
        out_shape=(jax.ShapeDtypeStruct((SSM_GROUPS, r, CW), F32), jax.ShapeDtypeStruct((4, nb, width), F32)),
        scratch_shapes=[pltpu.VMEM((4, r, 2 * SSM_STATE), F32), pltpu.VMEM((4, r, 2 * SSM_STATE), F32)],
        compiler_params=_params(("parallel",)),
        name="ssm",
    )(ug, tmat, et, ft, a16, h0)
    fin = fin.reshape(2, 2, nb, SSM_GROUPS, SSM_STATE)
    return y, fin


def _gelu_tanh(x):
    return 0.5 * x * (1.0 + jnp.tanh(math.sqrt(2.0 / math.pi) * (x + 0.044715 * (x * x * x))))


def _route(aff, bias):
    a = [aff[e:e + 1, :] for e in range(N_EXPERTS)]
    s = [a[e] + bias[e:e + 1, :] for e in range(N_EXPERTS)]
    keep = []
    for e in range(N_EXPERTS):
        g0 = (e // EXPERTS_PER_GROUP) * EXPERTS_PER_GROUP
        rank = None
        for j in range(g0, g0 + EXPERTS_PER_GROUP):
            if j == e:
                continue
            beats = (s[j] >= s[e]) if j < e else (s[j] > s[e])
            r = jnp.where(beats, 1.0, 0.0)
            rank = r if rank is None else rank + r
        keep.append(rank < 1.5)
    score = []
    for g in range(N_GROUPS):
        tot = None
        for e in range(g * EXPERTS_PER_GROUP, (g + 1) * EXPERTS_PER_GROUP):
            v = jnp.where(keep[e], s[e], 0.0)
            tot = v if tot is None else tot + v
        score.append(tot)
    gates = []
    picked = []
    for g in range(N_GROUPS):
        lost = None
        for j in range(N_GROUPS):
            if j == g:
                continue
            beats = (score[j] >= score[g]) if j < g else (score[j] > score[g])
            r = jnp.where(beats, 1.0, 0.0)
            lost = r if lost is None else lost + r
        chosen = lost < 0.5
        picked.append(jnp.where(chosen, 1.0, 0.0))
        w = [jnp.where(keep[e], a[e], 0.0) for e in range(g * EXPERTS_PER_GROUP, (g + 1) * EXPERTS_PER_GROUP)]
        tot = w[0] + w[1] + w[2] + w[3]
        for v in w:
            gates.append(jnp.where(chosen, v / tot, 0.0))
    return gates, picked


def _merge_kernel(x_ref, yg_ref, at_ref, sga_ref, sgb_ref, mod_ref, wglu_ref, bglu_ref, wsp_ref,
                  wap_ref, wo_ref, gffn_ref, wr_ref, rb_ref, tri_ref, ramp_ref,
                  xo_ref, hs_ref, mt_ref, ms_ref, of_ref, tok_ref):
    mod = mod_ref[...]
    gate1 = mod[:, 2 * D_MODEL:3 * D_MODEL]
    shift2 = mod[:, 3 * D_MODEL:4 * D_MODEL]
    scale2 = mod[:, 4 * D_MODEL:5 * D_MODEL]
    _from_chunk_rows(yg_ref, tok_ref)
    ys = jnp.concatenate([tok_ref[j] for j in range(LANE_TILES)], axis=1)
    y = _gelu_tanh(ys)
    glu = jnp.dot(y.astype(BF16), wglu_ref[...], preferred_element_type=F32) + bglu_ref[...]
    y = (y * _sigmoid(glu)).astype(BF16)
    a_out = jnp.dot(y, wsp_ref[...], preferred_element_type=F32)
    b_out = jnp.dot(at_ref[...], wap_ref[...], preferred_element_type=F32)
    m = (sga_ref[...].astype(F32) * a_out + sgb_ref[...].astype(F32) * b_out).astype(BF16)
    x = x_ref[...] + gate1 * jnp.dot(m, wo_ref[...], preferred_element_type=F32)
    xo_ref[...] = x
    h2 = _rms(x, D_MODEL) * (gffn_ref[...] * (1.0 + scale2)) + shift2
    nt = (((1,), (1,)), ((), ()))
    h2_hi = h2.astype(BF16)
    h2_lo = (h2 - h2_hi.astype(F32)).astype(BF16)
    wr = wr_ref[...]
    wr_hi = wr.astype(BF16)
    wr_lo = (wr - wr_hi.astype(F32)).astype(BF16)
    logits = (lax.dot_general(wr_hi, h2_hi, nt, preferred_element_type=F32)
              + lax.dot_general(wr_hi, h2_lo, nt, preferred_element_type=F32)
              + lax.dot_general(wr_lo, h2_hi, nt, preferred_element_type=F32))
    gates, picked = _route(_sigmoid(logits), rb_ref[...])
    n = x.shape[0]
    onehot = jnp.concatenate(picked + [jnp.zeros((8 - N_GROUPS, n), F32)], axis=0)
    before = jnp.dot(onehot.astype(BF16), tri_ref[...], preferred_element_type=F32)
    count = [jnp.sum(p, axis=-1, keepdims=True) for p in picked]
    start = [jnp.zeros((1, 1), F32)]
    for g in range(N_GROUPS - 1):
        start.append(start[-1] + count[g])
    pos = sum(picked[g] * (start[g] + before[g:g + 1, :]) for g in range(N_GROUPS))
    gid = sum(float(g) * picked[g] for g in range(1, N_GROUPS))
    perm = jnp.where(ramp_ref[...] == pos, 1.0, 0.0).astype(BF16)
    hs_ref[...] = jnp.dot(perm, h2_hi, preferred_element_type=F32).astype(BF16)
    own = [sum(gates[EXPERTS_PER_GROUP * g + e] for g in range(N_GROUPS)) for e in range(EXPERTS_PER_GROUP)]
    meta = jnp.concatenate(own + [gid, pos, jnp.zeros((LANES - EXPERTS_PER_GROUP - 2, n), F32)], axis=0).T
    mt_ref[...] = meta
    hi = meta.astype(BF16)
    lo = (meta - hi.astype(F32)).astype(BF16)
    ms_ref[...] = (jnp.dot(perm, hi, preferred_element_type=F32)
                   + jnp.dot(perm, lo, preferred_element_type=F32))
    lane = lax.broadcasted_iota(jnp.int32, (8, LANES), 1)
    offs = sum(jnp.where(lane == g, start[g], 0.0) for g in range(1, N_GROUPS)) + jnp.where(lane == N_GROUPS, n, 0.0)
    of_ref[...] = offs.astype(jnp.int32)


def _merge(x, yg, attn, sga, sgb, mods, l, lat, wts):
    n = x.shape[0]
    tt = TOK_TILE
    per_seq = 2048 // tt
    mod_map = (lambda i: (l, 1 + i // per_seq, 0, 0)) if lat else (lambda i: (l, 0, 0, 0))
    row = lambda i: (i, 0)
    lw = lambda *shape: pl.BlockSpec((None,) + shape, lambda i: (l,) + (0,) * len(shape))
    return pl.pallas_call(
        _merge_kernel,
        grid=(n // tt,),
        in_specs=[
            pl.BlockSpec((tt, D_MODEL), row),
            pl.BlockSpec((SSM_GROUPS, TILE_CHUNKS, CW), lambda i: (0, i, 0)),
            pl.BlockSpec((tt, ATTN_WIDTH), row), pl.BlockSpec((tt, D_MODEL), row), pl.BlockSpec((tt, D_MODEL), row),
            pl.BlockSpec((None, None, 1, 6 * D_MODEL), mod_map),
            lw(SSM_WIDTH, SSM_WIDTH), lw(1, SSM_WIDTH), lw(SSM_WIDTH, D_MODEL),
            lw(ATTN_WIDTH, D_MODEL), lw(D_MODEL, D_MODEL), lw(1, D_MODEL),
            pl.BlockSpec((N_EXPERTS, D_MODEL), lambda i: (0, 0)),
            pl.BlockSpec((N_EXPERTS, 1), lambda i: (0, 0)),
            pl.BlockSpec((tt, tt), lambda i: (0, 0)), pl.BlockSpec((tt, tt), lambda i: (0, 0)),
        ],
        out_specs=[pl.BlockSpec((tt, D_MODEL), row), pl.BlockSpec((tt, D_MODEL), row),
                   pl.BlockSpec((tt, LANES), row), pl.BlockSpec((tt, LANES), row),
                   pl.BlockSpec((None, 8, LANES), lambda i: (i, 0, 0))],
        out_shape=(jax.ShapeDtypeStruct((n, D_MODEL), F32), jax.ShapeDtypeStruct((n, D_MODEL), BF16),
                   jax.ShapeDtypeStruct((n, LANES), F32), jax.ShapeDtypeStruct((n, LANES), F32),
                   jax.ShapeDtypeStruct((n // tt, 8, LANES), jnp.int32)),
        scratch_shapes=[pltpu.VMEM((LANE_TILES, tt, LANES), F32)],
        compiler_params=_params(("parallel",)),
        name="merge_lat" if lat else "merge_ctx",
    )(x, yg, attn, sga, sgb, mods, wts["w_glu"], wts["b_glu"], wts["w_ssm_proj"],
      wts["w_attn_proj"], wts["w_o"], wts["g_ffn"], wts["w_router_t"], wts["router_bias"],
      jnp.asarray(np.triu(np.ones((tt, tt), np.float32), 1), BF16),
      jnp.asarray(np.broadcast_to(np.arange(tt, dtype=np.float32)[:, None], (tt, tt))))


def _moe_kernel(offs_ref, x_ref, hs_ref, ms_ref, mt_ref, mod_ref, wg_ref, wu_ref, wd_ref, o_ref, acc_ref):
    tile = pl.program_id(0)
    acc_ref[...] = jnp.zeros(acc_ref.shape, F32)

    def run_passes(height, g, first, span):
        row_id = lax.broadcasted_iota(jnp.int32, (height, 1), 0)
        n_pass = sum((span > k * height).astype(jnp.int32) for k in range(pl.cdiv(TOK_TILE + 16, height)))

        def pass_body(j, c):
            lo = first + height * j
            s = pl.multiple_of(jnp.minimum(lo, TOK_TILE - height), 16)
            rows = pl.ds(s, height)
            meta = ms_ref[rows, :]
            group_id = lax.convert_element_type(g, F32)
            mine = (meta[:, EXPERTS_PER_GROUP:EXPERTS_PER_GROUP + 1] == group_id) & (row_id + s >= lo)
            own = jnp.where(mine, 1.0, 0.0)
            hs = hs_ref[rows, :]
            acts = []
            for e in range(EXPERTS_PER_GROUP):
                expert = EXPERTS_PER_GROUP * g + e
                hg = jnp.dot(hs, wg_ref[expert], preferred_element_type=F32)
                hu = jnp.dot(hs, wu_ref[expert], preferred_element_type=F32)
                acts.append(((hg * _sigmoid(hg)) * hu * (meta[:, e:e + 1] * own)).astype(BF16))
            acc_ref[rows, :] += jnp.dot(jnp.concatenate(acts, axis=1), wd_ref[g], preferred_element_type=F32)
            return c

        lax.fori_loop(0, n_pass, pass_body, 0)

    def group_body(g, carry):
        seg_start = offs_ref[(N_GROUPS + 1) * tile + g]
        seg_end = offs_ref[(N_GROUPS + 1) * tile + g + 1]
        first = (seg_start >> 4) << 4
        span = jnp.where(seg_end > seg_start, seg_end - first, 0)
        choice = sum((span > height).astype(jnp.int32) for height in MOE_PASS_ROWS[:-1])
        lax.switch(choice, [functools.partial(run_passes, height, g, first, span) for height in MOE_PASS_ROWS])
        return carry

    lax.fori_loop(0, N_GROUPS, group_body, 0)

    slot = lax.broadcasted_iota(jnp.int32, (TOK_TILE, TOK_TILE), 1).astype(F32)
    unsort = jnp.where(mt_ref[:, EXPERTS_PER_GROUP + 1:EXPERTS_PER_GROUP + 2] == slot, 1.0, 0.0).astype(BF16)
    y = jnp.dot(unsort, acc_ref[...].astype(BF16), preferred_element_type=F32)
    o_ref[...] = x_ref[...] + mod_ref[:, 5 * D_MODEL:6 * D_MODEL] * y


def _moe(x, hs, meta_tok, meta_sorted, offs, mods, l, lat, wts):
    n = x.shape[0]
    tt = TOK_TILE
    per_seq = 2048 // tt
    mod_map = (lambda i, o: (l, 1 + i // per_seq, 0, 0)) if lat else (lambda i, o: (l, 0, 0, 0))
    row = lambda i, o: (i, 0)
    width = EXPERTS_PER_GROUP * D_EXPERT
    resident = lambda *shape: pl.BlockSpec((None,) + shape, lambda i, o: (l,) + (0,) * len(shape),
                                           pipeline_mode=pl.Buffered(1))
    grid_spec = pltpu.PrefetchScalarGridSpec(
        num_scalar_prefetch=1,
        grid=(n // tt,),
        in_specs=[
            pl.BlockSpec((tt, D_MODEL), row), pl.BlockSpec((tt, D_MODEL), row), pl.BlockSpec((tt, LANES), row),
            pl.BlockSpec((tt, LANES), row),
            pl.BlockSpec((None, None, 1, 6 * D_MODEL), mod_map),
            resident(N_EXPERTS, D_MODEL, D_EXPERT), resident(N_EXPERTS, D_MODEL, D_EXPERT),
            resident(N_GROUPS, width, D_MODEL),
        ],
        out_specs=pl.BlockSpec((tt, D_MODEL), row),
        scratch_shapes=[pltpu.VMEM((tt, D_MODEL), F32)],
    )
    return pl.pallas_call(
        _moe_kernel,
        grid_spec=grid_spec,
        out_shape=jax.ShapeDtypeStruct((n, D_MODEL), F32),
        compiler_params=_params(("arbitrary",), MOE_VMEM_LIMIT),
        name="moe_lat" if lat else "moe_ctx",
    )(offs[:, 0, :N_GROUPS + 1].reshape(-1), x, hs, meta_sorted, meta_tok, mods, wts["w_gate"], wts["w_up"],
      wts["w_down"])


def _pad_heads(w, width):
    lead = w.shape[:-1]
    w = w.reshape(lead + (N_HEADS, width))
    w = jnp.pad(w, [(0, 0)] * len(lead) + [(0, 0), (0, HEAD_PAD - width)])
    return w.reshape(lead + (N_HEADS * HEAD_PAD,))


def _swap_partners(w):
    lead = w.shape[:-1]
    w = w.reshape(lead + (-1, HEAD_PAD))
    half = ROPE_AXIS // 2
    a, b = QK_NOPE, QK_NOPE + ROPE_AXIS
    parts = [w[..., :a], w[..., a + half:b], w[..., a:a + half], w[..., b + half:b + 2 * half], w[..., b:b + half],
             w[..., b + 2 * half:]]
    return jnp.concatenate(parts, axis=-1).reshape(lead + (-1,))


def _layout_weights(w_in, g_qa, w_qb, g_q, g_kva, w_kvb, g_k, g_mix, w_attn_proj, ssm_w_glu, ssm_b_glu,
                    ssm_w_proj, w_o, g_ffn, w_router, router_bias, w_gate, w_up, w_down):
    n_a = SSM_WIDTH + Q_LORA + KV_LORA + QK_ROPE
    w_a = jnp.pad(w_in[:, :, :n_a], ((0, 0), (0, 0), (0, 1024 - n_a)))
    kv = w_kvb.reshape(DEPTH, KV_LORA, N_HEADS, QK_NOPE + V_HEAD)
    k_cols = _pad_heads(kv[..., :QK_NOPE].reshape(DEPTH, KV_LORA, N_HEADS * QK_NOPE), QK_NOPE)
    v_cols = _pad_heads(kv[..., QK_NOPE:].reshape(DEPTH, KV_LORA, ATTN_WIDTH), V_HEAD)
    place = np.zeros((KV_LORA, 2 * QK_PAD), np.float32)
    for h in range(N_HEADS):
        for r in range(QK_ROPE):
            place[r, h * HEAD_PAD + QK_NOPE + r] = 1.0
    w_k = jnp.concatenate([jnp.concatenate([k_cols, v_cols], axis=-1),
                           jnp.broadcast_to(jnp.asarray(place), (DEPTH,) + place.shape)], axis=1)
    w_qb_pad = _pad_heads(w_qb, QK_HEAD).astype(BF16)
    w_k = w_k.astype(BF16)
    pad_gain = lambda g: jnp.pad(g, ((0, 0), (0, HEAD_PAD - QK_HEAD))).reshape(DEPTH, 1, HEAD_PAD)
    vec = lambda g: g.reshape(DEPTH, 1, -1)
    return {
        "g_mix": vec(g_mix), "w_a": w_a.astype(BF16),
        "w_ga": w_in[:, :, n_a:n_a + D_MODEL].astype(BF16), "w_gb": w_in[:, :, n_a + D_MODEL:].astype(BF16),
        "g_qa": vec(g_qa), "w_qb": w_qb_pad, "w_qb_sw": _swap_partners(w_qb_pad),
        "g_q": pad_gain(g_q) * Q_SCALE, "g_q_sw": _swap_partners(pad_gain(g_q)) * Q_SCALE,
        "g_kva": vec(g_kva), "w_k": w_k, "w_k_sw": _swap_partners(w_k[:, :, :QK_PAD]),
        "g_k": pad_gain(g_k), "g_k_sw": _swap_partners(pad_gain(g_k)),
        "w_glu": ssm_w_glu.astype(BF16), "b_glu": vec(ssm_b_glu),
        "w_ssm_proj": ssm_w_proj.astype(BF16), "w_attn_proj": w_attn_proj.astype(BF16), "w_o": w_o.astype(BF16),
        "g_ffn": vec(g_ffn), "w_router_t": w_router.T, "router_bias": router_bias.reshape(N_EXPERTS, 1),
        "w_gate": w_gate.astype(BF16), "w_up": w_up.astype(BF16),
        "w_down": w_down.astype(BF16).reshape(DEPTH, N_GROUPS, EXPERTS_PER_GROUP * D_EXPERT, D_MODEL),
    }


def _rope_tables(n_tokens):
    pos = np.arange(n_tokens)
    inv = 1.0 / (ROPE_BASE ** (np.arange(ROPE_AXIS // 2, dtype=np.float32) * 2.0 / ROPE_AXIS))
    ang = np.zeros((n_tokens, HEAD_PAD), np.float32)
    half = ROPE_AXIS // 2
    row = (pos // GRID_W).astype(np.float32)[:, None] * inv
    col = (pos % GRID_W).astype(np.float32)[:, None] * inv
    first = np.zeros((HEAD_PAD,), bool)
    second = np.zeros((HEAD_PAD,), bool)
    for base, a in ((QK_NOPE, row), (QK_NOPE + ROPE_AXIS, col)):
        ang[:, base:base + half] = a
        ang[:, base + half:base + 2 * half] = a
        first[base:base + half] = True
        second[base + half:base + 2 * half] = True
    ang = jnp.asarray(ang)
    cos = jnp.where(jnp.asarray(first | second), jnp.cos(ang), 1.0)
    sin = jnp.sin(ang)
    return cos, jnp.where(jnp.asarray(first), -sin, jnp.where(jnp.asarray(second), sin, 0.0))


def kernel(x_prompt, x_sample, c, cache_ckv, cache_krope, state_ssm_re, state_ssm_im, c_ctx, w_ada, b_ada, g_mix, w_in, g_qa, w_qb, g_q, g_kva, w_kvb, g_k, w_attn_proj, ssm_a_re, ssm_a_im, ssm_log_dt, ssm_b_re, ssm_b_im, ssm_c_re, ssm_c_im, ssm_d, ssm_w_glu, ssm_b_glu, ssm_w_proj, w_o, g_ffn, w_router, router_bias, w_gate, w_up, w_down):
    nb_c, seq_c, _ = x_prompt.shape
    nb_l, seq_l, _ = x_sample.shape
    wts = _layout_weights(w_in, g_qa, w_qb, g_q, g_kva, w_kvb, g_k, g_mix, w_attn_proj, ssm_w_glu,
                          ssm_b_glu, ssm_w_proj, w_o, g_ffn, w_router, router_bias, w_gate, w_up, w_down)
    cvec = jnp.concatenate([c_ctx[None, :], c, jnp.zeros((8 - 1 - nb_l, D_MODEL), F32)], axis=0)
    mods = _mods(cvec, w_ada, b_ada).reshape(DEPTH, 8, 1, 6 * D_MODEL)
    prep = _ssm_prep(ssm_a_re, ssm_a_im, ssm_log_dt, ssm_b_re, ssm_b_im, ssm_c_re, ssm_c_im, ssm_d)
    tabs = _rope_tables(seq_l)
    kc, vc = _cachekv(cache_ckv, cache_krope, wts)

    xp = x_prompt.reshape(nb_c * seq_c, D_MODEL)
    xs = x_sample.reshape(nb_l * seq_l, D_MODEL)
    zeros_c = jnp.zeros((nb_c, 2, SSM_GROUPS, SSM_STATE), F32)
    ckvs, kropes, fins = [], [], []
    for l in range(DEPTH):
        ug, q, k, v, sga, sgb, ckv, krope = _inproj(xp, mods, l, False, wts, tabs)
        ckvs.append(ckv)
        kropes.append(krope)
        yg, fin = _ssm_mix(ug, nb_c, seq_c, zeros_c, zeros_c, prep, l)
        fins.append(fin)
        attn = _attn_ctx(q, k, v, seq_c)
        xp, *routed = _merge(xp, yg, attn, sga, sgb, mods, l, False, wts)
        xp = _moe(xp, *routed, mods, l, False, wts)

        ug, q, k, v, sga, sgb, _, _ = _inproj(xs, mods, l, True, wts, tabs)
        yg, _ = _ssm_mix(ug, nb_l, seq_l, state_ssm_re[:, l], state_ssm_im[:, l], prep, l)
        attn = _attn_lat(q, k, v, kc, vc, l, seq_l, TOK_TILE)
        xs, *routed = _merge(xs, yg, attn, sga, sgb, mods, l, True, wts)
        xs = _moe(xs, *routed, mods, l, True, wts)

    new_ckv = jnp.stack(ckvs, axis=0).reshape(DEPTH, nb_c, seq_c, KV_LORA).transpose(1, 0, 2, 3)
    new_krope = jnp.stack(kropes, axis=0).reshape(DEPTH, nb_c, seq_c, QK_ROPE).transpose(1, 0, 2, 3)
    fin = jnp.stack(fins, axis=0)
    new_re = jnp.transpose(fin[:, :, 0], (2, 0, 1, 3, 4))
    new_im = jnp.transpose(fin[:, :, 1], (2, 0, 1, 3, 4))
    return (xp.reshape(nb_c, seq_c, D_MODEL), xs.reshape(nb_l, seq_l, D_MODEL), new_ckv, new_krope, new_re, new_im)
```
